```python
import math
import jax, jax.numpy as jnp
from jax import lax
import numpy as np

D_MODEL = 1024
BATCH = 8
SEQ = 2048
DEPTH = 2

HEAD_DIM = 64
A_HEADS = 8
A_KV_HEADS = 2
WINDOW = 128
A_BLOCK = 128
B_HEADS = 4
B_BLOCK = 128
C_HEADS = 8
GRID_W = 64
NA_ROWS = 8
NA_COLS = 16
NA_COL_BLOCK = 16
NA_KEY_COLS = NA_COL_BLOCK + NA_COLS
MIX_W = 512
N_BRANCH = 3
N_EXPERTS = 32
TOP_K = 4
D_EXPERT = 1024
MOE_BLOCK = 128
SWIGLU_LIMIT = 7.0
SWIGLU_ALPHA = 1.702
LN_EPS = 1e-5
NEG_INF = -1e30
DEEPNORM_ALPHA = (2 * DEPTH) ** 0.25
DEEPNORM_BETA = (8 * DEPTH) ** -0.25

A_Q_W = A_HEADS * HEAD_DIM
A_KV_W = A_KV_HEADS * HEAD_DIM
B_QK_W = B_HEADS * 2 * HEAD_DIM
B_V_W = B_HEADS * 2 * HEAD_DIM
C_W = C_HEADS * HEAD_DIM
GATE_W = N_BRANCH * D_MODEL
IN_SPLITS = (A_Q_W, A_KV_W, A_KV_W, B_QK_W, B_QK_W, B_V_W, C_W, C_W, C_W, GATE_W)
IN_W = sum(IN_SPLITS)

kernel_name = "hybrid_gated_swa_diff_na_moe_encoder"


def layer_norm(x, g, b):
    xf = x.astype(jnp.float32)
    mu = jnp.mean(xf, axis=-1, keepdims=True)
    var = jnp.mean(jnp.square(xf - mu), axis=-1, keepdims=True)
    y = (xf - mu) * lax.rsqrt(var + LN_EPS) * g.astype(jnp.float32) + b.astype(jnp.float32)
    return y.astype(x.dtype)


def alibi_slopes():
    n = A_HEADS + B_HEADS
    return jnp.asarray(2.0 ** (-8.0 * np.arange(1, n + 1) / n), dtype=jnp.float32)


def windowed_gqa_sink(q, k, v, sink, slopes):
    bsz, seq = q.shape[0], q.shape[1]
    nb = seq // A_BLOCK
    grp = A_HEADS // A_KV_HEADS

    def band(t):
        tp = jnp.pad(t, ((0, 0), (A_BLOCK, A_BLOCK), (0, 0), (0, 0)))
        tp = tp.reshape(bsz, nb + 2, A_BLOCK, A_KV_HEADS, HEAD_DIM)
        return jnp.concatenate([tp[:, :-2], tp[:, 1:-1], tp[:, 2:]], axis=2)

    kb, vb = band(k), band(v)
    qb = q.reshape(bsz, nb, A_BLOCK, A_KV_HEADS, grp, HEAD_DIM)
    s = jnp.einsum('bnqgrd,bnkgd->bngrqk', qb, kb).astype(jnp.float32) * (HEAD_DIM ** -0.5)
    qi = np.arange(A_BLOCK)[:, None]
    ki = np.arange(3 * A_BLOCK)[None, :]
    rel = ki - A_BLOCK - qi
    kpos = (np.arange(nb)[:, None, None] - 1) * A_BLOCK + ki[None]
    valid = (np.abs(rel) <= WINDOW)[None] & (kpos >= 0) & (kpos < seq)
    dist = jnp.asarray(np.abs(rel), dtype=jnp.float32)
    bias = -slopes.reshape(A_KV_HEADS, grp, 1, 1) * dist
    s = jnp.where(valid[None, :, None, None], s + bias, NEG_INF)
    sink_logit = jnp.broadcast_to(sink.astype(jnp.float32).reshape(1, 1, A_KV_HEADS, grp, 1, 1),
                                  s.shape[:-1] + (1,))
    p = jax.nn.softmax(jnp.concatenate([s, sink_logit], axis=-1), axis=-1)[..., :-1]
    o = jnp.einsum('bngrqk,bnkgd->bnqgrd', p.astype(v.dtype), vb)
    return o.reshape(bsz, seq, A_Q_W)


def diff_attention(q, k, v, lam, lam_init, gain, slopes):
    bsz, seq = q.shape[0], q.shape[1]
    nb = seq // B_BLOCK
    kpos = jnp.arange(seq, dtype=jnp.float32)
    qblocks = q.reshape(bsz, nb, B_BLOCK, B_HEADS, 2, HEAD_DIM).swapaxes(0, 1)

    def block(args):
        qb, i = args
        qpos = i.astype(jnp.float32) * B_BLOCK + jnp.arange(B_BLOCK, dtype=jnp.float32)
        s = jnp.einsum('bqhmd,bkhmd->bhmqk', qb, k).astype(jnp.float32) * (HEAD_DIM ** -0.5)
        s = s - slopes[:, None, None, None] * jnp.abs(qpos[:, None] - kpos[None, :])
        p = jax.nn.softmax(s, axis=-1)
        a = p[:, :, 0] - lam * p[:, :, 1]
        return jnp.einsum('bhqk,bkhe->bqhe', a.astype(v.dtype), v)

    o = lax.map(block, (qblocks, jnp.arange(nb)))
    o = o.swapaxes(0, 1).reshape(bsz, seq, B_HEADS, 2 * HEAD_DIM).astype(jnp.float32)
    o = o * lax.rsqrt(jnp.mean(jnp.square(o), axis=-1, keepdims=True) + LN_EPS)
    o = o * gain.astype(jnp.float32) * (1.0 - lam_init)
    return o.astype(v.dtype).reshape(bsz, seq, B_V_W)


def neighbourhood_attention(q, k, v, rpb):
    bsz, seq = q.shape[0], q.shape[1]
    rows = seq // GRID_W
    kr = min(NA_ROWS, rows)
    ncb = GRID_W // NA_COL_BLOCK
    r = np.arange(rows)
    rs = np.clip(r - kr // 2, 0, rows - kr)
    rows_idx = rs[:, None] + np.arange(kr)[None, :]
    cbs = np.clip(np.arange(ncb) * NA_COL_BLOCK - NA_COLS // 2, 0, GRID_W - NA_KEY_COLS)
    cols_idx = cbs[:, None] + np.arange(NA_KEY_COLS)[None, :]
    qc = np.arange(ncb)[:, None] * NA_COL_BLOCK + np.arange(NA_COL_BLOCK)[None, :]
    cs = np.clip(qc - NA_COLS // 2, 0, GRID_W - NA_COLS)
    kc = cols_idx[:, None, :]
    valid = (kc >= cs[:, :, None]) & (kc < cs[:, :, None] + NA_COLS)
    dri = rows_idx - r[:, None] + NA_ROWS - 1
    dci = np.clip(kc - qc[:, :, None] + NA_COLS - 1, 0, 2 * NA_COLS - 2)

    qg = q.reshape(bsz, rows, ncb, NA_COL_BLOCK, C_HEADS, HEAD_DIM)
    kg = k.reshape(bsz, rows, GRID_W, C_HEADS, HEAD_DIM)
    vg = v.reshape(bsz, rows, GRID_W, C_HEADS, HEAD_DIM)
    ridx = rows_idx[:, None, :, None]
    cidx = cols_idx[None, :, None, :]
    kn = kg[:, ridx, cidx]
    vn = vg[:, ridx, cidx]
    s = jnp.einsum('brjqhd,brjuvhd->brjhquv', qg, kn).astype(jnp.float32) * (HEAD_DIM ** -0.5)
    bias = rpb.astype(jnp.float32)[:, dri[:, :, None, None, None], dci[None, None]]
    bias = bias.transpose(1, 3, 0, 4, 2, 5)
    s = jnp.where(valid[None, None, :, None, :, None, :], s + bias[None], NEG_INF)
    shp = s.shape
    p = jax.nn.softmax(s.reshape(shp[:-2] + (kr * NA_KEY_COLS,)), axis=-1).reshape(shp)
    o = jnp.einsum('brjhquv,brjuvhd->brjqhd', p.astype(v.dtype), vn)
    return o.reshape(bsz, seq, C_W)


def hybrid_mixer(h, layer, w_in, a_sink, lambda_q1, lambda_k1, lambda_q2, lambda_k2,
                 diff_norm_g, na_rpb, w_branch, w_out):
    bsz, seq, _ = h.shape
    proj = h @ w_in
    offs = np.cumsum(IN_SPLITS)[:-1].tolist()
    aq, ak, av, bq, bk, bv, cq, ck, cv, g = jnp.split(proj, offs, axis=-1)
    slopes = alibi_slopes()

    oa = windowed_gqa_sink(aq.reshape(bsz, seq, A_HEADS, HEAD_DIM),
                           ak.reshape(bsz, seq, A_KV_HEADS, HEAD_DIM),
                           av.reshape(bsz, seq, A_KV_HEADS, HEAD_DIM),
                           a_sink, slopes[:A_HEADS])

    lam_init = 0.8 - 0.6 * math.exp(-0.3 * layer)
    f32 = jnp.float32
    lam = (jnp.exp(jnp.sum(lambda_q1.astype(f32) * lambda_k1.astype(f32)))
           - jnp.exp(jnp.sum(lambda_q2.astype(f32) * lambda_k2.astype(f32))) + lam_init)
    ob = diff_attention(bq.reshape(bsz, seq, B_HEADS, 2, HEAD_DIM),
                        bk.reshape(bsz, seq, B_HEADS, 2, HEAD_DIM),
                        bv.reshape(bsz, seq, B_HEADS, 2 * HEAD_DIM),
                        lam, lam_init, diff_norm_g, slopes[A_HEADS:])

    oc = neighbourhood_attention(cq.reshape(bsz, seq, C_HEADS, HEAD_DIM),
                                 ck.reshape(bsz, seq, C_HEADS, HEAD_DIM),
                                 cv.reshape(bsz, seq, C_HEADS, HEAD_DIM), na_rpb)

    branches = jnp.stack([oa, ob, oc], axis=2)
    br = jnp.einsum('bsmc,mcd->bsmd', branches, w_branch)
    gates = jax.nn.sigmoid(g.reshape(bsz, seq, N_BRANCH, D_MODEL))
    merged = jnp.sum(gates * br, axis=2)
    return merged @ w_out


def moe_ffn(x, w_router, b_router, w_up, b_up, w_down, b_down):
    bsz, seq, d = x.shape
    n = bsz * seq
    xf = x.reshape(n, d)
    logits = (xf @ w_router + b_router).astype(jnp.float32)
    top_vals, top_idx = lax.top_k(logits, TOP_K)
    gate_w = jax.nn.softmax(top_vals, axis=-1)
    n_assign = n * TOP_K
    flat_e = top_idx.reshape(-1).astype(jnp.int32)
    flat_tok = (jnp.arange(n_assign, dtype=jnp.int32) // TOP_K)
    flat_w = gate_w.reshape(-1)
    order = jnp.argsort(flat_e)
    sorted_e = flat_e[order]
    counts = jnp.bincount(flat_e, length=N_EXPERTS)
    padded = (counts + MOE_BLOCK - 1) // MOE_BLOCK * MOE_BLOCK
    start = jnp.cumsum(counts) - counts
    pend = jnp.cumsum(padded)
    pstart = pend - padded
    dest = pstart[sorted_e] + (jnp.arange(n_assign, dtype=jnp.int32) - start[sorted_e])
    n_pad = n_assign + N_EXPERTS * MOE_BLOCK
    n_blk = n_pad // MOE_BLOCK
    tok_pad = jnp.full((n_pad,), n, dtype=jnp.int32).at[dest].set(flat_tok[order])
    w_pad = jnp.zeros((n_pad,), jnp.float32).at[dest].set(flat_w[order])
    blk_e = jnp.minimum(jnp.searchsorted(pend, jnp.arange(n_blk) * MOE_BLOCK, side='right'),
                        N_EXPERTS - 1).astype(jnp.int32)
    x_pad = jnp.concatenate([xf, jnp.zeros((1, d), xf.dtype)], axis=0)
    xs = x_pad[tok_pad].reshape(n_blk, MOE_BLOCK, d)

    def expert_block(args):
        xb, e = args
        hh = xb @ w_up[e] + b_up[e]
        hg, hl = hh[:, :D_EXPERT], hh[:, D_EXPERT:]
        hg = jnp.minimum(hg, SWIGLU_LIMIT)
        hl = jnp.clip(hl, -SWIGLU_LIMIT, SWIGLU_LIMIT)
        act = hg * jax.nn.sigmoid(SWIGLU_ALPHA * hg) * (hl + 1.0)
        return act @ w_down[e] + b_down[e]

    ys = lax.map(expert_block, (xs, blk_e)).reshape(n_pad, d)
    out = jnp.zeros((n + 1, d), ys.dtype).at[tok_pad].add(ys * w_pad[:, None].astype(ys.dtype))
    return out[:n].reshape(bsz, seq, d)


def setup_inputs(seed: int = 0) -> dict:
    key = jax.random.key(seed)
    ks = jax.random.split(key, 20)
    L, D, E, F = DEPTH, D_MODEL, N_EXPERTS, D_EXPERT
    nrm = jax.random.normal
    val_scale = [1.0, 1.0, DEEPNORM_BETA, 1.0, 1.0, DEEPNORM_BETA, 1.0, 1.0, DEEPNORM_BETA, 1.0]
    col_scale = jnp.asarray(np.concatenate([np.full(w, s, np.float32)
                                            for w, s in zip(IN_SPLITS, val_scale)]))
    return {
        "x": nrm(ks[0], (BATCH, SEQ, D), jnp.float32),
        "w_in": nrm(ks[1], (L, D, IN_W), jnp.float32) * (D ** -0.5) * col_scale,
        "a_sink": nrm(ks[2], (L, A_HEADS), jnp.float32) * 0.5,
        "lambda_q1": nrm(ks[3], (L, HEAD_DIM), jnp.float32) * 0.1,
        "lambda_k1": nrm(ks[4], (L, HEAD_DIM), jnp.float32) * 0.1,
        "lambda_q2": nrm(ks[5], (L, HEAD_DIM), jnp.float32) * 0.1,
        "lambda_k2": nrm(ks[6], (L, HEAD_DIM), jnp.float32) * 0.1,
        "diff_norm_g": 1.0 + 0.02 * nrm(ks[7], (L, 2 * HEAD_DIM), jnp.float32),
        "na_rpb": 0.05 * nrm(ks[8], (L, C_HEADS, 2 * NA_ROWS - 1, 2 * NA_COLS - 1), jnp.float32),
        "w_branch": nrm(ks[9], (L, N_BRANCH, MIX_W, D), jnp.float32) * (MIX_W ** -0.5),
        "w_out": nrm(ks[10], (L, D, D), jnp.float32) * (D ** -0.5) * DEEPNORM_BETA,
        "ln1_g": 1.0 + 0.02 * nrm(ks[11], (L, D), jnp.float32),
        "ln1_b": 0.02 * nrm(ks[12], (L, D), jnp.float32),
        "w_router": nrm(ks[13], (L, D, E), jnp.float32) * (D ** -0.5),
        "b_router": 0.01 * nrm(ks[14], (L, E), jnp.float32),
        "w_up": nrm(ks[15], (L, E, D, 2 * F), jnp.float32) * (D ** -0.5),
        "b_up": 0.01 * nrm(ks[16], (L, E, 2 * F), jnp.float32),
        "w_down": nrm(ks[17], (L, E, F, D), jnp.float32) * (F ** -0.5) * DEEPNORM_BETA,
        "b_down": 0.01 * nrm(ks[18], (L, E, D), jnp.float32),
        "ln2_g": 1.0 + 0.02 * nrm(ks[19], (L, D), jnp.float32),
        "ln2_b": 0.02 * nrm(jax.random.fold_in(ks[19], 1), (L, D), jnp.float32),
    }


def reference(x, w_in, a_sink, lambda_q1, lambda_k1, lambda_q2, lambda_k2, diff_norm_g, na_rpb,
              w_branch, w_out, ln1_g, ln1_b, w_router, b_router, w_up, b_up, w_down, b_down,
              ln2_g, ln2_b):
    for l in range(DEPTH):
        mix = hybrid_mixer(x, l, w_in[l], a_sink[l], lambda_q1[l], lambda_k1[l], lambda_q2[l],
                           lambda_k2[l], diff_norm_g[l], na_rpb[l], w_branch[l], w_out[l])
        x = layer_norm(DEEPNORM_ALPHA * x + mix, ln1_g[l], ln1_b[l])
        ffn = moe_ffn(x, w_router[l], b_router[l], w_up[l], b_up[l], w_down[l], b_down[l])
        x = layer_norm(DEEPNORM_ALPHA * x + ffn, ln2_g[l], ln2_b[l])
    return x
```

```python
import functools
import math

import numpy as np
import jax
import jax.numpy as jnp
from jax import lax
from jax.experimental import pallas as pl
from jax.experimental.pallas import tpu as pltpu

F32 = jnp.float32
BF16 = jnp.bfloat16

D_MODEL = 1024
DEPTH = 2
HEAD_DIM = 64
A_HEADS = 8
A_KV_HEADS = 2
WINDOW = 128
B_HEADS = 4
C_HEADS = 8
GRID_W = 64
NA_ROWS = 8
NA_COLS = 16
MIX_W = 512
N_BRANCH = 3
N_EXPERTS = 32
TOP_K = 4
D_EXPERT = 1024
SWIGLU_LIMIT = 7.0
SWIGLU_ALPHA = 1.702
LN_EPS = 1e-5
NEG_INF = -1e30
DEEPNORM_ALPHA = (2 * DEPTH) ** 0.25

LANES = 128
QKV_W = 3840
GATE_W = N_BRANCH * D_MODEL
A_Q_BLK, A_K_BLK, A_V_BLK = 0, 4, 5
B_Q_BLK, B_K_BLK, B_V_BLK = 6, 10, 14
C_Q_BLK, C_K_BLK, C_V_BLK = 18, 22, 26

A_BAND = 3 * WINDOW
B_TQ = 256
MOE_BM = 512
ROUTE_T = 256
VMEM_LIMIT = 56 * 1024 * 1024

_ALIBI = [float(2.0 ** (-8.0 * (i + 1) / (A_HEADS + B_HEADS))) for i in range(A_HEADS + B_HEADS)]


def _cparams(sem):
    return pltpu.CompilerParams(dimension_semantics=sem, vmem_limit_bytes=VMEM_LIMIT)


def _layer_norm(y, g, b):
    mu = jnp.mean(y, axis=-1, keepdims=True)
    yc = y - mu
    var = jnp.mean(yc * yc, axis=-1, keepdims=True)
    return yc * lax.rsqrt(var + LN_EPS) * g + b


def _inproj_kernel(x_ref, w_ref, o_ref, *, chunk):
    xb = x_ref[...].astype(BF16)
    for c in range(QKV_W // chunk):
        sl = slice(c * chunk, (c + 1) * chunk)
        o_ref[:, sl] = jnp.dot(xb, w_ref[:, sl], preferred_element_type=F32).astype(BF16)


def _inproj(x2d, w_qkv):
    n = x2d.shape[0]
    tm = 512
    return pl.pallas_call(
        functools.partial(_inproj_kernel, chunk=768),
        grid=(n // tm,),
        in_specs=[pl.BlockSpec((tm, D_MODEL), lambda i: (i, 0)),
                  pl.BlockSpec((D_MODEL, QKV_W), lambda i: (0, 0))],
        out_specs=pl.BlockSpec((tm, QKV_W), lambda i: (i, 0)),
        out_shape=jax.ShapeDtypeStruct((n, QKV_W), BF16),
        compiler_params=_cparams(("arbitrary",)),
        name="inproj",
    )(x2d, w_qkv)


def _attn_a_kernel(sink_ref, q_ref, k_ref, v_ref, o_ref, *, seq):
    lane = lax.broadcasted_iota(jnp.int32, (WINDOW, LANES), 1)
    low = lane < HEAD_DIM
    qi = lax.broadcasted_iota(jnp.int32, (WINDOW, A_BAND), 0)
    kj = lax.broadcasted_iota(jnp.int32, (WINDOW, A_BAND), 1)
    kq = kj - qi

    def body(n, carry):
        q0 = pl.multiple_of(n * WINDOW, WINDOW)
        start = pl.multiple_of(jnp.clip((n - 1) * WINDOW, 0, seq - A_BAND), WINDOW)
        kb = k_ref[pl.ds(start, A_BAND), :]
        vb = v_ref[pl.ds(start, A_BAND), :]
        dist = jnp.abs(kq + (start - q0)).astype(F32)
        valid = dist <= float(WINDOW)
        for j in range(4):
            qt = q_ref[pl.ds(q0, WINDOW), j * LANES:(j + 1) * LANES]
            halves = []
            for hf in range(2):
                h = j + 4 * hf
                qm = jnp.where(low if hf == 0 else jnp.logical_not(low), qt, jnp.zeros_like(qt))
                s = lax.dot_general(qm, kb, (((1,), (1,)), ((), ())), preferred_element_type=F32)
                s = jnp.where(valid, s - _ALIBI[h] * dist, NEG_INF)
                sink = sink_ref[h]
                m = jnp.maximum(jnp.max(s, axis=-1, keepdims=True), sink)
                e = jnp.exp(s - m)
                den = jnp.sum(e, axis=-1, keepdims=True) + jnp.exp(sink - m)
                p = (e / den).astype(BF16)
                halves.append(jnp.dot(p, vb, preferred_element_type=F32))
            o = jnp.where(low, halves[0], halves[1])
            o_ref[pl.ds(q0, WINDOW), j * LANES:(j + 1) * LANES] = o.astype(BF16)
        return carry

    lax.fori_loop(0, seq // WINDOW, body, 0)


def _attn_a(proj, sink, bsz, seq):
    return pl.pallas_call(
        functools.partial(_attn_a_kernel, seq=seq),
        grid=(bsz,),
        in_specs=[pl.BlockSpec(memory_space=pltpu.SMEM),
                  pl.BlockSpec((seq, 4 * LANES), lambda b: (b, A_Q_BLK // 4)),
                  pl.BlockSpec((seq, LANES), lambda b: (b, A_K_BLK)),
                  pl.BlockSpec((seq, LANES), lambda b: (b, A_V_BLK))],
        out_specs=pl.BlockSpec((seq, MIX_W), lambda b: (b, 0)),
        out_shape=jax.ShapeDtypeStruct((bsz * seq, MIX_W), BF16),
        compiler_params=_cparams(("arbitrary",)),
        name="attn_a",
    )(sink, proj, proj, proj)


def _attn_b_kernel(slope_ref, lamv_ref, gain_ref, q_ref, k_ref, v_ref, o_ref, *, seq, lam_init):
    h = pl.program_id(1)
    slope = slope_ref[h]
    lv = lamv_ref[...]
    lam = (jnp.exp(jnp.sum(lv[0:1] * lv[1:2], axis=-1, keepdims=True))
           - jnp.exp(jnp.sum(lv[2:3] * lv[3:4], axis=-1, keepdims=True)) + lam_init)
    scale = gain_ref[...] * (1.0 - lam_init)
    lane = lax.broadcasted_iota(jnp.int32, (B_TQ, LANES), 1)
    low = lane < HEAD_DIM
    qi = lax.broadcasted_iota(jnp.int32, (B_TQ, seq), 0)
    kj = lax.broadcasted_iota(jnp.int32, (B_TQ, seq), 1)
    kq = kj - qi
    kall = k_ref[...]
    vall = v_ref[...]
    dn = (((1,), (1,)), ((), ()))

    def body(n, carry):
        q0 = pl.multiple_of(n * B_TQ, B_TQ)
        qt = q_ref[pl.ds(q0, B_TQ), :]
        bias = slope * jnp.abs(kq - q0).astype(F32)
        probs = []
        for mp in range(2):
            qm = jnp.where(low if mp == 0 else jnp.logical_not(low), qt, jnp.zeros_like(qt))
            s = lax.dot_general(qm, kall, dn, preferred_element_type=F32) - bias
            m = jnp.max(s, axis=-1, keepdims=True)
            e = jnp.exp(s - m)
            probs.append(e / jnp.sum(e, axis=-1, keepdims=True))
        a = (probs[0] - lam * probs[1]).astype(BF16)
        o = jnp.dot(a, vall, preferred_element_type=F32)
        o = o * lax.rsqrt(jnp.mean(o * o, axis=-1, keepdims=True) + LN_EPS)
        o_ref[pl.ds(q0, B_TQ), :] = (o * scale).astype(BF16)
        return carry

    lax.fori_loop(0, seq // B_TQ, body, 0)


def _attn_b(proj, slopes_b, lamv, gain, bsz, seq, lam_init):
    return pl.pallas_call(
        functools.partial(_attn_b_kernel, seq=seq, lam_init=lam_init),
        grid=(bsz, B_HEADS),
        in_specs=[pl.BlockSpec(memory_space=pltpu.SMEM),
                  pl.BlockSpec((4, HEAD_DIM), lambda b, h: (0, 0)),
                  pl.BlockSpec((1, 2 * HEAD_DIM), lambda b, h: (0, 0)),
                  pl.BlockSpec((seq, LANES), lambda b, h: (b, B_Q_BLK + h)),
                  pl.BlockSpec((seq, LANES), lambda b, h: (b, B_K_BLK + h)),
                  pl.BlockSpec((seq, LANES), lambda b, h: (b, B_V_BLK + h))],
        out_specs=pl.BlockSpec((seq, LANES), lambda b, h: (b, h)),
        out_shape=jax.ShapeDtypeStruct((bsz * seq, MIX_W), BF16),
        compiler_params=_cparams(("arbitrary", "arbitrary")),
        name="attn_b",
    )(slopes_b, lamv, gain, proj, proj, proj)


def _attn_c_kernel(bias_ref, q_ref, k_ref, v_ref, o_ref, *, rows):
    kr = min(NA_ROWS, rows)
    slab = kr * GRID_W
    lane = lax.broadcasted_iota(jnp.int32, (GRID_W, LANES), 1)
    low = lane < HEAD_DIM
    dn = (((1,), (1,)), ((), ()))

    def body(r, carry):
        rs = jnp.clip(r - kr // 2, 0, rows - kr)
        k0 = pl.multiple_of(rs * GRID_W, GRID_W)
        q0 = pl.multiple_of(r * GRID_W, GRID_W)
        qr = q_ref[pl.ds(q0, GRID_W), :]
        zero = jnp.zeros_like(qr)
        lhs = jnp.concatenate([jnp.where(low, qr, zero), jnp.where(low, zero, qr)], axis=0)
        ks = k_ref[pl.ds(k0, slab), :]
        vs = v_ref[pl.ds(k0, slab), :]
        s = lax.dot_general(lhs, ks, dn, preferred_element_type=F32) + bias_ref[0, r - rs]
        m = jnp.max(s, axis=-1, keepdims=True)
        e = jnp.exp(s - m)
        p = (e / jnp.sum(e, axis=-1, keepdims=True)).astype(BF16)
        pv = jnp.dot(p, vs, preferred_element_type=F32)
        o = jnp.where(low, pv[:GRID_W], pv[GRID_W:])
        o_ref[pl.ds(q0, GRID_W), :] = o.astype(BF16)
        return carry

    lax.fori_loop(0, rows, body, 0)


def _attn_c(proj, bias_tab, bsz, seq):
    rows = seq // GRID_W
    npair = C_HEADS // 2
    return pl.pallas_call(
        functools.partial(_attn_c_kernel, rows=rows),
        grid=(npair, bsz),
        in_specs=[pl.BlockSpec((1,) + bias_tab.shape[1:], lambda p, b: (p, 0, 0, 0)),
                  pl.BlockSpec((seq, LANES), lambda p, b: (b, C_Q_BLK + p)),
                  pl.BlockSpec((seq, LANES), lambda p, b: (b, C_K_BLK + p)),
                  pl.BlockSpec((seq, LANES), lambda p, b: (b, C_V_BLK + p))],
        out_specs=pl.BlockSpec((seq, LANES), lambda p, b: (b, p)),
        out_shape=jax.ShapeDtypeStruct((bsz * seq, MIX_W), BF16),
        compiler_params=_cparams(("arbitrary", "arbitrary")),
        name="attn_c",
    )(bias_tab, proj, proj, proj)


def _na_bias_table(rpb, rows):
    kr = min(NA_ROWS, rows)
    qc = np.arange(GRID_W)[:, None]
    kc = np.arange(GRID_W)[None, :]
    cs = np.clip(qc - NA_COLS // 2, 0, GRID_W - NA_COLS)
    valid = (kc >= cs) & (kc < cs + NA_COLS)
    dci = np.clip(kc - qc + NA_COLS - 1, 0, 2 * NA_COLS - 2)
    delta = np.arange(kr)[:, None]
    u = np.arange(kr)[None, :]
    dri = np.clip(u - delta + NA_ROWS - 1, 0, 2 * NA_ROWS - 2)
    tab = rpb.astype(F32)[:, dri[:, None, :, None], dci[None, :, None, :]]
    tab = jnp.where(valid[None, None, :, None, :], tab, NEG_INF)
    tab = tab.reshape(C_HEADS // 2, 2, kr, GRID_W, kr * GRID_W)
    return tab.transpose(0, 2, 1, 3, 4).reshape(C_HEADS // 2, kr, 2 * GRID_W, kr * GRID_W)


def _merge_kernel(x_ref, oa_ref, ob_ref, oc_ref, wg_ref, wbr_ref, wout_ref, g_ref, b_ref, o_ref):
    x = x_ref[...]
    xb = x.astype(BF16)
    merged = None
    for i, br_ref in enumerate((oa_ref, ob_ref, oc_ref)):
        gate = jax.nn.sigmoid(jnp.dot(xb, wg_ref[:, i * D_MODEL:(i + 1) * D_MODEL],
                                      preferred_element_type=F32))
        br = jnp.dot(br_ref[...], wbr_ref[i], preferred_element_type=F32)
        merged = gate * br if merged is None else merged + gate * br
    mix = jnp.dot(merged.astype(BF16), wout_ref[...], preferred_element_type=F32)
    o_ref[...] = _layer_norm(DEEPNORM_ALPHA * x + mix, g_ref[...], b_ref[...])


def _merge(x2d, oa, ob, oc, w_gate, w_br, w_out, ln_g, ln_b):
    n = x2d.shape[0]
    tm = 256
    const2 = lambda i: (0, 0)
    return pl.pallas_call(
        _merge_kernel,
        grid=(n // tm,),
        in_specs=[pl.BlockSpec((tm, D_MODEL), lambda i: (i, 0)),
                  pl.BlockSpec((tm, MIX_W), lambda i: (i, 0)),
                  pl.BlockSpec((tm, MIX_W), lambda i: (i, 0)),
                  pl.BlockSpec((tm, MIX_W), lambda i: (i, 0)),
                  pl.BlockSpec((D_MODEL, GATE_W), const2),
                  pl.BlockSpec((N_BRANCH, MIX_W, D_MODEL), lambda i: (0, 0, 0)),
                  pl.BlockSpec((D_MODEL, D_MODEL), const2),
                  pl.BlockSpec((1, D_MODEL), const2),
                  pl.BlockSpec((1, D_MODEL), const2)],
        out_specs=pl.BlockSpec((tm, D_MODEL), lambda i: (i, 0)),
        out_shape=jax.ShapeDtypeStruct((n, D_MODEL), F32),
        compiler_params=_cparams(("arbitrary",)),
        name="merge_ln1",
    )(x2d, oa, ob, oc, w_gate, w_br, w_out, ln_g, ln_b)


def _route_kernel(x_ref, w_ref, b_ref, eidx_ref, gate_ref, rank_ref, cnt_ref, carry_ref):
    t = x_ref.shape[0]

    @pl.when(pl.program_id(0) == 0)
    def _():
        carry_ref[...] = jnp.zeros_like(carry_ref)

    logits = jnp.dot(x_ref[...], w_ref[...], preferred_element_type=F32,
                     precision=lax.Precision.HIGHEST) + b_ref[...]
    lane = lax.broadcasted_iota(jnp.int32, (t, LANES), 1).astype(F32)
    work = logits
    sels, vals, idxs = [], [], []
    for _ in range(TOP_K):
        m = jnp.max(work, axis=-1, keepdims=True)
        idx = jnp.min(jnp.where(work == m, lane, float(LANES)), axis=-1, keepdims=True)
        sel = lane == idx
        work = jnp.where(sel, -jnp.inf, work)
        sels.append(sel)
        vals.append(m)
        idxs.append(idx)
    ex = [jnp.exp(v - vals[0]) for v in vals]
    den = ex[0] + ex[1] + ex[2] + ex[3]
    onehot = jnp.zeros((t, LANES), F32)
    for sel in sels:
        onehot = jnp.where(sel, 1.0, onehot)
    ri = lax.broadcasted_iota(jnp.int32, (t, t), 0)
    ci = lax.broadcasted_iota(jnp.int32, (t, t), 1)
    tri = jnp.where(ci < ri, 1.0, 0.0).astype(BF16)
    before = jnp.dot(tri, onehot.astype(BF16), preferred_element_type=F32) + carry_ref[...]
    eidx = jnp.zeros((t, LANES), F32)
    gate = jnp.zeros((t, LANES), F32)
    rank = jnp.zeros((t, LANES), F32)
    for k in range(TOP_K):
        rk = jnp.sum(jnp.where(sels[k], before, 0.0), axis=-1, keepdims=True)
        eidx = jnp.where(lane == float(k), idxs[k], eidx)
        gate = jnp.where(lane == float(k), ex[k] / den, gate)
        rank = jnp.where(lane == float(k), rk, rank)
    eidx_ref[...] = eidx.astype(jnp.int32)
    gate_ref[...] = gate
    rank_ref[...] = rank.astype(jnp.int32)
    carry_ref[...] += jnp.sum(onehot, axis=0, keepdims=True)
    cnt_ref[...] = carry_ref[...].astype(jnp.int32)


def _route(x2d, w_router_pad, b_router_pad):
    n = x2d.shape[0]
    t = ROUTE_T
    tile = pl.BlockSpec((t, LANES), lambda i: (i, 0))
    return pl.pallas_call(
        _route_kernel,
        grid=(n // t,),
        in_specs=[pl.BlockSpec((t, D_MODEL), lambda i: (i, 0)),
                  pl.BlockSpec((D_MODEL, LANES), lambda i: (0, 0)),
                  pl.BlockSpec((1, LANES), lambda i: (0, 0))],
        out_specs=[tile, tile, tile, pl.BlockSpec((1, LANES), lambda i: (0, 0))],
        out_shape=[jax.ShapeDtypeStruct((n, LANES), jnp.int32),
                   jax.ShapeDtypeStruct((n, LANES), F32),
                   jax.ShapeDtypeStruct((n, LANES), jnp.int32),
                   jax.ShapeDtypeStruct((1, LANES), jnp.int32)],
        scratch_shapes=[pltpu.VMEM((1, LANES), F32)],
        compiler_params=_cparams(("arbitrary",)),
        name="route",
    )(x2d, w_router_pad, b_router_pad)


def _row_copy(src_ref, src_row, dst_ref, dst_row, sem):
    return pltpu.make_async_copy(src_ref.at[pl.ds(src_row, 1), :],
                                 dst_ref.at[pl.ds(dst_row, 1), :], sem)


def _dispatch_kernel(dest_ref, x_ref, xs_in_ref, xs_ref, sem):
    del xs_in_ref
    t = x_ref.shape[0]

    def issue(i, carry):
        for k in range(TOP_K):
            _row_copy(x_ref, i, xs_ref, dest_ref[0, 0, i * TOP_K + k], sem).start()
        return carry

    lax.fori_loop(0, t, issue, 0)
    for _ in range(TOP_K):
        pltpu.make_async_copy(x_ref, xs_ref.at[pl.ds(0, t), :], sem).wait()


def _dispatch(x2d, dest3, xs_init):
    n = x2d.shape[0]
    t = ROUTE_T
    return pl.pallas_call(
        _dispatch_kernel,
        grid=(n // t,),
        in_specs=[pl.BlockSpec((1, 1, t * TOP_K), lambda i: (i, 0, 0), memory_space=pltpu.SMEM),
                  pl.BlockSpec((t, D_MODEL), lambda i: (i, 0)),
                  pl.BlockSpec(memory_space=pl.ANY)],
        out_specs=pl.BlockSpec(memory_space=pl.ANY),
        out_shape=jax.ShapeDtypeStruct(xs_init.shape, xs_init.dtype),
        scratch_shapes=[pltpu.SemaphoreType.DMA(())],
        input_output_aliases={2: 0},
        compiler_params=_cparams(("arbitrary",)),
        name="dispatch",
    )(dest3, x2d, xs_init)


def _expert_kernel(blk_e_ref, n_used_ref, xs_ref, wu_ref, bu_ref, wd_ref, bd_ref, ys_ref,
                   wu_bf, wd_bf):
    i = pl.program_id(0)

    @pl.when(i < n_used_ref[0])
    def _():
        prev = blk_e_ref[jnp.maximum(i - 1, 0)]

        @pl.when(jnp.logical_or(i == 0, blk_e_ref[i] != prev))
        def _():
            wu_bf[...] = wu_ref[0].astype(BF16)
            wd_bf[...] = wd_ref[0].astype(BF16)

        xb = xs_ref[...].astype(BF16)
        bu = bu_ref[0]
        hg = jnp.dot(xb, wu_bf[:, :D_EXPERT], preferred_element_type=F32) + bu[:, :D_EXPERT]
        hl = jnp.dot(xb, wu_bf[:, D_EXPERT:], preferred_element_type=F32) + bu[:, D_EXPERT:]
        hg = jnp.minimum(hg, SWIGLU_LIMIT)
        hl = jnp.clip(hl, -SWIGLU_LIMIT, SWIGLU_LIMIT)
        act = hg * jax.nn.sigmoid(SWIGLU_ALPHA * hg) * (hl + 1.0)
        ys_ref[...] = jnp.dot(act.astype(BF16), wd_bf[...], preferred_element_type=F32) + bd_ref[0]

    @pl.when(i >= n_used_ref[0])
    def _():
        ys_ref[...] = jnp.zeros_like(ys_ref)


def _experts(blk_e, n_used, xs, w_up, b_up3, w_down, b_down3):
    n_pad = xs.shape[0]
    n_blk = n_pad // MOE_BM

    def blk(i, be, nu):
        return jnp.minimum(i, nu[0] - 1)

    return pl.pallas_call(
        _expert_kernel,
        grid_spec=pltpu.PrefetchScalarGridSpec(
            num_scalar_prefetch=2,
            grid=(n_blk,),
            in_specs=[pl.BlockSpec((MOE_BM, D_MODEL), lambda i, be, nu: (blk(i, be, nu), 0)),
                      pl.BlockSpec((1, D_MODEL, 2 * D_EXPERT), lambda i, be, nu: (be[blk(i, be, nu)], 0, 0)),
                      pl.BlockSpec((1, 1, 2 * D_EXPERT), lambda i, be, nu: (be[blk(i, be, nu)], 0, 0)),
                      pl.BlockSpec((1, D_EXPERT, D_MODEL), lambda i, be, nu: (be[blk(i, be, nu)], 0, 0)),
                      pl.BlockSpec((1, 1, D_MODEL), lambda i, be, nu: (be[blk(i, be, nu)], 0, 0))],
            out_specs=pl.BlockSpec((MOE_BM, D_MODEL), lambda i, be, nu: (i, 0)),
            scratch_shapes=[pltpu.VMEM((D_MODEL, 2 * D_EXPERT), BF16),
                            pltpu.VMEM((D_EXPERT, D_MODEL), BF16)]),
        out_shape=jax.ShapeDtypeStruct((n_pad, D_MODEL), F32),
        compiler_params=_cparams(("arbitrary",)),
        name="experts",
    )(blk_e, n_used, xs, w_up, b_up3, w_down, b_down3)


def _combine_kernel(dest_ref, gate_ref, x_ref, ys_ref, g_ref, b_ref, o_ref, buf, sem):
    t = x_ref.shape[0]

    def issue(i, carry):
        for k in range(TOP_K):
            pltpu.make_async_copy(ys_ref.at[pl.ds(dest_ref[0, 0, i * TOP_K + k], 1), :],
                                  buf.at[k, pl.ds(i, 1), :], sem).start()
        return carry

    lax.fori_loop(0, t, issue, 0)
    for k in range(TOP_K):
        pltpu.make_async_copy(ys_ref.at[pl.ds(0, t), :], buf.at[k], sem).wait()
    gate = gate_ref[...]
    ffn = gate[:, 0:1] * buf[0]
    for k in range(1, TOP_K):
        ffn = ffn + gate[:, k:k + 1] * buf[k]
    o_ref[...] = _layer_norm(DEEPNORM_ALPHA * x_ref[...] + ffn, g_ref[...], b_ref[...])


def _combine(dest3, gate, x2d, ys, ln_g, ln_b):
    n = x2d.shape[0]
    t = ROUTE_T
    const2 = lambda i: (0, 0)
    return pl.pallas_call(
        _combine_kernel,
        grid=(n // t,),
        in_specs=[pl.BlockSpec((1, 1, t * TOP_K), lambda i: (i, 0, 0), memory_space=pltpu.SMEM),
                  pl.BlockSpec((t, LANES), lambda i: (i, 0)),
                  pl.BlockSpec((t, D_MODEL), lambda i: (i, 0)),
                  pl.BlockSpec(memory_space=pl.ANY),
                  pl.BlockSpec((1, D_MODEL), const2),
                  pl.BlockSpec((1, D_MODEL), const2)],
        out_specs=pl.BlockSpec((t, D_MODEL), lambda i: (i, 0)),
        out_shape=jax.ShapeDtypeStruct((n, D_MODEL), F32),
        scratch_shapes=[pltpu.VMEM((TOP_K, t, D_MODEL), F32), pltpu.SemaphoreType.DMA(())],
        compiler_params=_cparams(("arbitrary",)),
        name="combine_ln2",
    )(dest3, gate, x2d, ys, ln_g, ln_b)


def _a_head_perm():
    grp = A_HEADS // A_KV_HEADS
    order = []
    for j in range(grp):
        order += [j, grp + j]
    return np.concatenate([np.arange(h * HEAD_DIM, (h + 1) * HEAD_DIM) for h in order])


def _moe(x1, w_router, b_router, w_up, b_up, w_down, b_down, ln_g, ln_b):
    n = x1.shape[0]
    wr = jnp.pad(w_router, ((0, 0), (0, LANES - N_EXPERTS)))
    br = jnp.pad(b_router, (0, LANES - N_EXPERTS), constant_values=NEG_INF).reshape(1, LANES)
    eidx, gate, rank, cnt = _route(x1, wr, br)
    counts = cnt[0, :N_EXPERTS]
    padded = (counts + MOE_BM - 1) // MOE_BM * MOE_BM
    pend = jnp.cumsum(padded)
    pstart = pend - padded
    e4 = eidx[:, :TOP_K]
    dest = pstart[e4] + rank[:, :TOP_K]
    dest3 = dest.reshape(n // ROUTE_T, 1, ROUTE_T * TOP_K).astype(jnp.int32)
    n_pad = n * TOP_K + N_EXPERTS * MOE_BM
    n_blk = n_pad // MOE_BM
    blk_e = jnp.minimum(jnp.searchsorted(pend, jnp.arange(n_blk) * MOE_BM, side='right'),
                        N_EXPERTS - 1).astype(jnp.int32)
    n_used = (pend[-1] // MOE_BM).astype(jnp.int32).reshape(1)
    xs = _dispatch(x1, dest3, jnp.zeros((n_pad, D_MODEL), F32))
    ys = _experts(blk_e, n_used, xs, w_up, b_up.reshape(N_EXPERTS, 1, -1),
                  w_down, b_down.reshape(N_EXPERTS, 1, -1))
    return _combine(dest3, gate, x1, ys, ln_g.reshape(1, -1), ln_b.reshape(1, -1))


def kernel(x, w_in, a_sink, lambda_q1, lambda_k1, lambda_q2, lambda_k2, diff_norm_g, na_rpb,
           w_branch, w_out, ln1_g, ln1_b, w_router, b_router, w_up, b_up, w_down, b_down,
           ln2_g, ln2_b):
    bsz, seq, d = x.shape
    n = bsz * seq
    rows = seq // GRID_W
    perm = _a_head_perm()
    slopes_b = jnp.asarray(_ALIBI[A_HEADS:], F32)
    xcur = x.reshape(n, d)
    for l in range(DEPTH):
        w = w_in[l]
        qscale = HEAD_DIM ** -0.5
        w_qkv = jnp.concatenate([
            w[:, :512][:, perm] * qscale, w[:, 512:768],
            w[:, 768:1280] * qscale, w[:, 1280:2304],
            w[:, 2304:2816] * qscale, w[:, 2816:QKV_W]], axis=1).astype(BF16)
        w_gate = w[:, QKV_W:].astype(BF16)
        w_br = jnp.stack([w_branch[l, 0][perm], w_branch[l, 1], w_branch[l, 2]]).astype(BF16)
        lam_init = 0.8 - 0.6 * math.exp(-0.3 * l)
        lamv = jnp.stack([lambda_q1[l], lambda_k1[l], lambda_q2[l], lambda_k2[l]]).astype(F32)

        proj = _inproj(xcur, w_qkv)
        oa = _attn_a(proj, a_sink[l].astype(F32), bsz, seq)
        ob = _attn_b(proj, slopes_b, lamv, diff_norm_g[l].reshape(1, -1).astype(F32), bsz, seq, lam_init)
        oc = _attn_c(proj, _na_bias_table(na_rpb[l], rows), bsz, seq)
        x1 = _merge(xcur, oa, ob, oc, w_gate, w_br, w_out[l].astype(BF16),
                    ln1_g[l].reshape(1, -1), ln1_b[l].reshape(1, -1))
        xcur = _moe(x1, w_router[l], b_router[l], w_up[l], b_up[l], w_down[l], b_down[l],
                    ln2_g[l], ln2_b[l])
    return xcur.reshape(bsz, seq, d)
```

```python
import functools
import math

import numpy as np
import jax
import jax.numpy as jnp
from jax import lax
from jax.experimental import pallas as pl
from jax.experimental.pallas import tpu as pltpu

F32 = jnp.float32
BF16 = jnp.bfloat16

D_MODEL = 1024
DEPTH = 2
HEAD_DIM = 64
A_HEADS = 8
A_KV_HEADS = 2
WINDOW = 128
B_HEADS = 4
C_HEADS = 8
GRID_W = 64
NA_ROWS = 8
NA_COLS = 16
MIX_W = 512
N_BRANCH = 3
N_EXPERTS = 32
TOP_K = 4
D_EXPERT = 1024
SWIGLU_LIMIT = 7.0
SWIGLU_ALPHA = 1.702
LN_EPS = 1e-5
NEG_INF = -1e30
DEEPNORM_ALPHA = (2 * DEPTH) ** 0.25

LANES = 128
QKV_W = 3840
GATE_W = N_BRANCH * D_MODEL
A_Q_BLK, A_K_BLK, A_V_BLK = 0, 4, 5
B_Q_BLK, B_K_BLK, B_V_BLK = 6, 10, 14
C_Q_BLK, C_K_BLK, C_V_BLK = 18, 22, 26

A_BAND = 3 * WINDOW
B_TQ = 256
MOE_BM = 512
ROUTE_T = 256
VMEM_LIMIT = 56 * 1024 * 1024

_ALIBI = [float(2.0 ** (-8.0 * (i + 1) / (A_HEADS + B_HEADS))) for i in range(A_HEADS + B_HEADS)]


def _cparams(sem):
    return pltpu.CompilerParams(dimension_semantics=sem, vmem_limit_bytes=VMEM_LIMIT)


def _layer_norm(y, g, b):
    mu = jnp.mean(y, axis=-1, keepdims=True)
    yc = y - mu
    var = jnp.mean(yc * yc, axis=-1, keepdims=True)
    return yc * lax.rsqrt(var + LN_EPS) * g + b


def _inproj_kernel(x_ref, w_ref, o_ref, *, chunk):
    xb = x_ref[...].astype(BF16)
    for c in range(QKV_W // chunk):
        sl = slice(c * chunk, (c + 1) * chunk)
        o_ref[:, sl] = jnp.dot(xb, w_ref[:, sl], preferred_element_type=F32).astype(BF16)


def _inproj(x2d, w_qkv):
    n = x2d.shape[0]
    tm = 512
    return pl.pallas_call(
        functools.partial(_inproj_kernel, chunk=768),
        grid=(n // tm,),
        in_specs=[pl.BlockSpec((tm, D_MODEL), lambda i: (i, 0)),
                  pl.BlockSpec((D_MODEL, QKV_W), lambda i: (0, 0))],
        out_specs=pl.BlockSpec((tm, QKV_W), lambda i: (i, 0)),
        out_shape=jax.ShapeDtypeStruct((n, QKV_W), BF16),
        compiler_params=_cparams(("arbitrary",)),
        name="inproj",
    )(x2d, w_qkv)


def _attn_a_kernel(sink_ref, q_ref, k_ref, v_ref, o_ref, *, seq):
    lane = lax.broadcasted_iota(jnp.int32, (WINDOW, LANES), 1)
    low = lane < HEAD_DIM
    qi = lax.broadcasted_iota(jnp.int32, (WINDOW, A_BAND), 0)
    kj = lax.broadcasted_iota(jnp.int32, (WINDOW, A_BAND), 1)
    kq = kj - qi

    def body(n, carry):
        q0 = pl.multiple_of(n * WINDOW, WINDOW)
        start = pl.multiple_of(jnp.clip((n - 1) * WINDOW, 0, seq - A_BAND), WINDOW)
        kb = k_ref[pl.ds(start, A_BAND), :]
        vb = v_ref[pl.ds(start, A_BAND), :]
        dist = jnp.abs(kq + (start - q0)).astype(F32)
        valid = dist <= float(WINDOW)
        for j in range(4):
            qt = q_ref[pl.ds(q0, WINDOW), j * LANES:(j + 1) * LANES]
            halves = []
            for hf in range(2):
                h = j + 4 * hf
                qm = jnp.where(low if hf == 0 else jnp.logical_not(low), qt, jnp.zeros_like(qt))
                s = lax.dot_general(qm, kb, (((1,), (1,)), ((), ())), preferred_element_type=F32)
                s = jnp.where(valid, s - _ALIBI[h] * dist, NEG_INF)
                sink = sink_ref[h]
                m = jnp.maximum(jnp.max(s, axis=-1, keepdims=True), sink)
                e = jnp.exp(s - m)
                den = jnp.sum(e, axis=-1, keepdims=True) + jnp.exp(sink - m)
                p = (e / den).astype(BF16)
                halves.append(jnp.dot(p, vb, preferred_element_type=F32))
            o = jnp.where(low, halves[0], halves[1])
            o_ref[pl.ds(q0, WINDOW), j * LANES:(j + 1) * LANES] = o.astype(BF16)
        return carry

    lax.fori_loop(0, seq // WINDOW, body, 0)


def _attn_a(proj, sink, bsz, seq):
    return pl.pallas_call(
        functools.partial(_attn_a_kernel, seq=seq),
        grid=(bsz,),
        in_specs=[pl.BlockSpec(memory_space=pltpu.SMEM),
                  pl.BlockSpec((seq, 4 * LANES), lambda b: (b, A_Q_BLK // 4)),
                  pl.BlockSpec((seq, LANES), lambda b: (b, A_K_BLK)),
                  pl.BlockSpec((seq, LANES), lambda b: (b, A_V_BLK))],
        out_specs=pl.BlockSpec((seq, MIX_W), lambda b: (b, 0)),
        out_shape=jax.ShapeDtypeStruct((bsz * seq, MIX_W), BF16),
        compiler_params=_cparams(("arbitrary",)),
        name="attn_a",
    )(sink, proj, proj, proj)


def _attn_b_kernel(slope_ref, lamv_ref, gain_ref, q_ref, k_ref, v_ref, o_ref, *, seq, lam_init):
    h = pl.program_id(1)
    slope = slope_ref[h]
    lv = lamv_ref[...]
    lam = (jnp.exp(jnp.sum(lv[0:1] * lv[1:2], axis=-1, keepdims=True))
           - jnp.exp(jnp.sum(lv[2:3] * lv[3:4], axis=-1, keepdims=True)) + lam_init)
    scale = gain_ref[...] * (1.0 - lam_init)
    lane = lax.broadcasted_iota(jnp.int32, (B_TQ, LANES), 1)
    low = lane < HEAD_DIM
    qi = lax.broadcasted_iota(jnp.int32, (B_TQ, seq), 0)
    kj = lax.broadcasted_iota(jnp.int32, (B_TQ, seq), 1)
    kq = kj - qi
    kall = k_ref[...]
    vall = v_ref[...]
    dn = (((1,), (1,)), ((), ()))

    def body(n, carry):
        q0 = pl.multiple_of(n * B_TQ, B_TQ)
        qt = q_ref[pl.ds(q0, B_TQ), :]
        bias = slope * jnp.abs(kq - q0).astype(F32)
        probs = []
        for mp in range(2):
            qm = jnp.where(low if mp == 0 else jnp.logical_not(low), qt, jnp.zeros_like(qt))
            s = lax.dot_general(qm, kall, dn, preferred_element_type=F32) - bias
            m = jnp.max(s, axis=-1, keepdims=True)
            e = jnp.exp(s - m)
            probs.append(e / jnp.sum(e, axis=-1, keepdims=True))
        a = (probs[0] - lam * probs[1]).astype(BF16)
        o = jnp.dot(a, vall, preferred_element_type=F32)
        o = o * lax.rsqrt(jnp.mean(o * o, axis=-1, keepdims=True) + LN_EPS)
        o_ref[pl.ds(q0, B_TQ), :] = (o * scale).astype(BF16)
        return carry

    lax.fori_loop(0, seq // B_TQ, body, 0)


def _attn_b(proj, slopes_b, lamv, gain, bsz, seq, lam_init):
    return pl.pallas_call(
        functools.partial(_attn_b_kernel, seq=seq, lam_init=lam_init),
        grid=(bsz, B_HEADS),
        in_specs=[pl.BlockSpec(memory_space=pltpu.SMEM),
                  pl.BlockSpec((4, HEAD_DIM), lambda b, h: (0, 0)),
                  pl.BlockSpec((1, 2 * HEAD_DIM), lambda b, h: (0, 0)),
                  pl.BlockSpec((seq, LANES), lambda b, h: (b, B_Q_BLK + h)),
                  pl.BlockSpec((seq, LANES), lambda b, h: (b, B_K_BLK + h)),
                  pl.BlockSpec((seq, LANES), lambda b, h: (b, B_V_BLK + h))],
        out_specs=pl.BlockSpec((seq, LANES), lambda b, h: (b, h)),
        out_shape=jax.ShapeDtypeStruct((bsz * seq, MIX_W), BF16),
        compiler_params=_cparams(("arbitrary", "arbitrary")),
        name="attn_b",
    )(slopes_b, lamv, gain, proj, proj, proj)


def _attn_c_kernel(bias_ref, q_ref, k_ref, v_ref, o_ref, *, rows):
    kr = min(NA_ROWS, rows)
    slab = kr * GRID_W
    lane = lax.broadcasted_iota(jnp.int32, (GRID_W, LANES), 1)
    low = lane < HEAD_DIM
    dn = (((1,), (1,)), ((), ()))

    def body(r, carry):
        rs = jnp.clip(r - kr // 2, 0, rows - kr)
        k0 = pl.multiple_of(rs * GRID_W, GRID_W)
        q0 = pl.multiple_of(r * GRID_W, GRID_W)
        qr = q_ref[pl.ds(q0, GRID_W), :]
        zero = jnp.zeros_like(qr)
        lhs = jnp.concatenate([jnp.where(low, qr, zero), jnp.where(low, zero, qr)], axis=0)
        ks = k_ref[pl.ds(k0, slab), :]
        vs = v_ref[pl.ds(k0, slab), :]
        s = lax.dot_general(lhs, ks, dn, preferred_element_type=F32) + bias_ref[0, r - rs]
        m = jnp.max(s, axis=-1, keepdims=True)
        e = jnp.exp(s - m)
        p = (e / jnp.sum(e, axis=-1, keepdims=True)).astype(BF16)
        pv = jnp.dot(p, vs, preferred_element_type=F32)
        o = jnp.where(low, pv[:GRID_W], pv[GRID_W:])
        o_ref[pl.ds(q0, GRID_W), :] = o.astype(BF16)
        return carry

    lax.fori_loop(0, rows, body, 0)


def _attn_c(proj, bias_tab, bsz, seq):
    rows = seq // GRID_W
    npair = C_HEADS // 2
    return pl.pallas_call(
        functools.partial(_attn_c_kernel, rows=rows),
        grid=(npair, bsz),
        in_specs=[pl.BlockSpec((1,) + bias_tab.shape[1:], lambda p, b: (p, 0, 0, 0)),
                  pl.BlockSpec((seq, LANES), lambda p, b: (b, C_Q_BLK + p)),
                  pl.BlockSpec((seq, LANES), lambda p, b: (b, C_K_BLK + p)),
                  pl.BlockSpec((seq, LANES), lambda p, b: (b, C_V_BLK + p))],
        out_specs=pl.BlockSpec((seq, LANES), lambda p, b: (b, p)),
        out_shape=jax.ShapeDtypeStruct((bsz * seq, MIX_W), BF16),
        compiler_params=_cparams(("arbitrary", "arbitrary")),
        name="attn_c",
    )(bias_tab, proj, proj, proj)


def _na_bias_table(rpb, rows):
    kr = min(NA_ROWS, rows)
    qc = np.arange(GRID_W)[:, None]
    kc = np.arange(GRID_W)[None, :]
    cs = np.clip(qc - NA_COLS // 2, 0, GRID_W - NA_COLS)
    valid = (kc >= cs) & (kc < cs + NA_COLS)
    off = GRID_W - NA_COLS
    rp = jnp.pad(rpb.astype(F32), ((0, 0), (0, 0), (off, off)))
    toep = jnp.stack([rp[:, :, off + NA_COLS - 1 - c:off + NA_COLS - 1 - c + GRID_W]
                      for c in range(GRID_W)], axis=2)
    toep = jnp.where(valid[None, None], toep, NEG_INF)
    tab = jnp.stack([toep[:, NA_ROWS - 1 - dl:NA_ROWS - 1 - dl + kr] for dl in range(kr)], axis=1)
    tab = tab.transpose(0, 1, 3, 2, 4)
    tab = tab.reshape(C_HEADS // 2, 2, kr, GRID_W, kr * GRID_W)
    return tab.transpose(0, 2, 1, 3, 4).reshape(C_HEADS // 2, kr, 2 * GRID_W, kr * GRID_W)


def _merge_kernel(x_ref, oa_ref, ob_ref, oc_ref, wg_ref, wbr_ref, wout_ref, g_ref, b_ref, o_ref):
    x = x_ref[...]
    xb = x.astype(BF16)
    merged = None
    for i, br_ref in enumerate((oa_ref, ob_ref, oc_ref)):
        gate = jax.nn.sigmoid(jnp.dot(xb, wg_ref[:, i * D_MODEL:(i + 1) * D_MODEL],
                                      preferred_element_type=F32))
        br = jnp.dot(br_ref[...], wbr_ref[i], preferred_element_type=F32)
        merged = gate * br if merged is None else merged + gate * br
    mix = jnp.dot(merged.astype(BF16), wout_ref[...], preferred_element_type=F32)
    o_ref[...] = _layer_norm(DEEPNORM_ALPHA * x + mix, g_ref[...], b_ref[...])


def _merge(x2d, oa, ob, oc, w_gate, w_br, w_out, ln_g, ln_b):
    n = x2d.shape[0]
    tm = 256
    const2 = lambda i: (0, 0)
    return pl.pallas_call(
        _merge_kernel,
        grid=(n // tm,),
        in_specs=[pl.BlockSpec((tm, D_MODEL), lambda i: (i, 0)),
                  pl.BlockSpec((tm, MIX_W), lambda i: (i, 0)),
                  pl.BlockSpec((tm, MIX_W), lambda i: (i, 0)),
                  pl.BlockSpec((tm, MIX_W), lambda i: (i, 0)),
                  pl.BlockSpec((D_MODEL, GATE_W), const2),
                  pl.BlockSpec((N_BRANCH, MIX_W, D_MODEL), lambda i: (0, 0, 0)),
                  pl.BlockSpec((D_MODEL, D_MODEL), const2),
                  pl.BlockSpec((1, D_MODEL), const2),
                  pl.BlockSpec((1, D_MODEL), const2)],
        out_specs=pl.BlockSpec((tm, D_MODEL), lambda i: (i, 0)),
        out_shape=jax.ShapeDtypeStruct((n, D_MODEL), F32),
        compiler_params=_cparams(("arbitrary",)),
        name="merge_ln1",
    )(x2d, oa, ob, oc, w_gate, w_br, w_out, ln_g, ln_b)


def _route_kernel(x_ref, w_ref, b_ref, eidx_ref, gate_ref, rank_ref, cnt_ref, carry_ref):
    t = x_ref.shape[0]

    @pl.when(pl.program_id(0) == 0)
    def _():
        carry_ref[...] = jnp.zeros_like(carry_ref)

    logits = jnp.dot(x_ref[...], w_ref[...], preferred_element_type=F32,
                     precision=lax.Precision.HIGHEST) + b_ref[...]
    lane = lax.broadcasted_iota(jnp.int32, (t, LANES), 1).astype(F32)
    work = logits
    sels, vals, idxs = [], [], []
    for _ in range(TOP_K):
        m = jnp.max(work, axis=-1, keepdims=True)
        idx = jnp.min(jnp.where(work == m, lane, float(LANES)), axis=-1, keepdims=True)
        sel = lane == idx
        work = jnp.where(sel, -jnp.inf, work)
        sels.append(sel)
        vals.append(m)
        idxs.append(idx)
    ex = [jnp.exp(v - vals[0]) for v in vals]
    den = ex[0] + ex[1] + ex[2] + ex[3]
    onehot = jnp.zeros((t, LANES), F32)
    for sel in sels:
        onehot = jnp.where(sel, 1.0, onehot)
    ri = lax.broadcasted_iota(jnp.int32, (t, t), 0)
    ci = lax.broadcasted_iota(jnp.int32, (t, t), 1)
    tri = jnp.where(ci < ri, 1.0, 0.0).astype(BF16)
    before = jnp.dot(tri, onehot.astype(BF16), preferred_element_type=F32) + carry_ref[...]
    eidx = jnp.zeros((t, LANES), F32)
    gate = jnp.zeros((t, LANES), F32)
    rank = jnp.zeros((t, LANES), F32)
    for k in range(TOP_K):
        rk = jnp.sum(jnp.where(sels[k], before, 0.0), axis=-1, keepdims=True)
        eidx = jnp.where(lane == float(k), idxs[k], eidx)
        gate = jnp.where(lane == float(k), ex[k] / den, gate)
        rank = jnp.where(lane == float(k), rk, rank)
    eidx_ref[...] = eidx.astype(jnp.int32)
    gate_ref[...] = gate
    rank_ref[...] = rank.astype(jnp.int32)
    carry_ref[...] += jnp.sum(onehot, axis=0, keepdims=True)
    cnt_ref[...] = carry_ref[...].astype(jnp.int32)


def _route(x2d, w_router_pad, b_router_pad):
    n = x2d.shape[0]
    t = ROUTE_T
    tile = pl.BlockSpec((t, LANES), lambda i: (i, 0))
    return pl.pallas_call(
        _route_kernel,
        grid=(n // t,),
        in_specs=[pl.BlockSpec((t, D_MODEL), lambda i: (i, 0)),
                  pl.BlockSpec((D_MODEL, LANES), lambda i: (0, 0)),
                  pl.BlockSpec((1, LANES), lambda i: (0, 0))],
        out_specs=[tile, tile, tile, pl.BlockSpec((1, LANES), lambda i: (0, 0))],
        out_shape=[jax.ShapeDtypeStruct((n, LANES), jnp.int32),
                   jax.ShapeDtypeStruct((n, LANES), F32),
                   jax.ShapeDtypeStruct((n, LANES), jnp.int32),
                   jax.ShapeDtypeStruct((1, LANES), jnp.int32)],
        scratch_shapes=[pltpu.VMEM((1, LANES), F32)],
        compiler_params=_cparams(("arbitrary",)),
        name="route",
    )(x2d, w_router_pad, b_router_pad)


def _row_copy(src_ref, src_row, dst_ref, dst_row, sem):
    return pltpu.make_async_copy(src_ref.at[pl.ds(src_row, 1), :],
                                 dst_ref.at[pl.ds(dst_row, 1), :], sem)


def _dispatch_kernel(dest_ref, x_ref, xs_ref, sem):
    t = x_ref.shape[0]

    def issue(i, carry):
        for k in range(TOP_K):
            _row_copy(x_ref, i, xs_ref, dest_ref[0, 0, i * TOP_K + k], sem).start()
        return carry

    lax.fori_loop(0, t, issue, 0)
    for _ in range(TOP_K):
        pltpu.make_async_copy(x_ref, xs_ref.at[pl.ds(0, t), :], sem).wait()


def _dispatch(x2d, dest3):
    n = x2d.shape[0]
    t = ROUTE_T
    return pl.pallas_call(
        _dispatch_kernel,
        grid=(n // t,),
        in_specs=[pl.BlockSpec((1, 1, t * TOP_K), lambda i: (i, 0, 0), memory_space=pltpu.SMEM),
                  pl.BlockSpec((t, D_MODEL), lambda i: (i, 0))],
        out_specs=pl.BlockSpec(memory_space=pl.ANY),
        out_shape=jax.ShapeDtypeStruct((n * TOP_K, D_MODEL), F32),
        scratch_shapes=[pltpu.SemaphoreType.DMA(())],
        compiler_params=_cparams(("arbitrary",)),
        name="dispatch",
    )(dest3, x2d)


def _expert_kernel(it_e, it_b, it_lo, it_hi, it_first, n_items, xs_ref, wu_ref, bu_ref, wd_ref,
                   bd_ref, ys_ref, wu_bf, wd_bf):
    del it_b
    j = pl.program_id(0)

    @pl.when(j < n_items[0])
    def _():
        @pl.when(jnp.logical_or(j == 0, it_e[j] != it_e[jnp.maximum(j - 1, 0)]))
        def _():
            wu_bf[...] = wu_ref[0, 0].astype(BF16)
            wd_bf[...] = wd_ref[0, 0].astype(BF16)

        xb = xs_ref[...].astype(BF16)
        bu = bu_ref[0, 0]
        hg = jnp.dot(xb, wu_bf[:, :D_EXPERT], preferred_element_type=F32) + bu[:, :D_EXPERT]
        hl = jnp.dot(xb, wu_bf[:, D_EXPERT:], preferred_element_type=F32) + bu[:, D_EXPERT:]
        hg = jnp.minimum(hg, SWIGLU_LIMIT)
        hl = jnp.clip(hl, -SWIGLU_LIMIT, SWIGLU_LIMIT)
        act = hg * jax.nn.sigmoid(SWIGLU_ALPHA * hg) * (hl + 1.0)
        y = jnp.dot(act.astype(BF16), wd_bf[...], preferred_element_type=F32) + bd_ref[0, 0]
        row = lax.broadcasted_iota(jnp.int32, (MOE_BM, 1), 0)
        mine = jnp.logical_and(row >= it_lo[j], row < it_hi[j])

        @pl.when(it_first[j] == 1)
        def _():
            ys_ref[...] = jnp.where(mine, y, 0.0)

        @pl.when(it_first[j] == 0)
        def _():
            ys_ref[...] = jnp.where(mine, y, ys_ref[...])


def _experts(items, xs, w_up, b_up4, w_down, b_down4, layer):
    n_rows = xs.shape[0]
    n_items = items[0].shape[0]

    def cur(j, ni):
        return jnp.minimum(j, ni[0] - 1)

    def w_map(j, ie, ib, lo, hi, fi, ni):
        return (layer, ie[cur(j, ni)], 0, 0)

    def row_map(j, ie, ib, lo, hi, fi, ni):
        return (ib[cur(j, ni)], 0)

    return pl.pallas_call(
        _expert_kernel,
        grid_spec=pltpu.PrefetchScalarGridSpec(
            num_scalar_prefetch=6,
            grid=(n_items,),
            in_specs=[pl.BlockSpec((MOE_BM, D_MODEL), row_map),
                      pl.BlockSpec((1, 1, D_MODEL, 2 * D_EXPERT), w_map),
                      pl.BlockSpec((1, 1, 1, 2 * D_EXPERT), w_map),
                      pl.BlockSpec((1, 1, D_EXPERT, D_MODEL), w_map),
                      pl.BlockSpec((1, 1, 1, D_MODEL), w_map)],
            out_specs=pl.BlockSpec((MOE_BM, D_MODEL), row_map),
            scratch_shapes=[pltpu.VMEM((D_MODEL, 2 * D_EXPERT), BF16),
                            pltpu.VMEM((D_EXPERT, D_MODEL), BF16)]),
        out_shape=jax.ShapeDtypeStruct((n_rows, D_MODEL), F32),
        compiler_params=_cparams(("arbitrary",)),
        name="experts",
    )(*items, xs, w_up, b_up4, w_down, b_down4)


def _expert_items(counts, n_rows):
    n_items = n_rows // MOE_BM + N_EXPERTS
    end = jnp.cumsum(counts)
    start = end - counts
    first_b = start // MOE_BM
    nb = jnp.where(counts > 0, (end - 1) // MOE_BM - first_b + 1, 0)
    item_end = jnp.cumsum(nb)
    item_start = item_end - nb
    total = item_end[-1]
    jc = jnp.minimum(jnp.arange(n_items, dtype=jnp.int32), total - 1)
    it_e = jnp.sum((item_end[None, :] <= jc[:, None]).astype(jnp.int32), axis=1)
    sel = it_e[:, None] == jnp.arange(N_EXPERTS, dtype=jnp.int32)[None, :]
    pick = lambda v: jnp.sum(jnp.where(sel, v[None, :], 0), axis=1)
    it_b = pick(first_b) + jc - pick(item_start)
    it_lo = jnp.maximum(pick(start), it_b * MOE_BM) - it_b * MOE_BM
    it_hi = jnp.minimum(pick(end), (it_b + 1) * MOE_BM) - it_b * MOE_BM
    prev_b = jnp.concatenate([jnp.full((1,), -1, jnp.int32), it_b[:-1]])
    it_first = (it_b != prev_b).astype(jnp.int32)
    i32 = lambda v: v.astype(jnp.int32)
    return (i32(it_e), i32(it_b), i32(it_lo), i32(it_hi), it_first, i32(total).reshape(1)), start


def _combine_kernel(dest_ref, gate_ref, x_ref, ys_ref, g_ref, b_ref, o_ref, buf, sem):
    t = x_ref.shape[0]

    def issue(i, carry):
        for k in range(TOP_K):
            pltpu.make_async_copy(ys_ref.at[pl.ds(dest_ref[0, 0, i * TOP_K + k], 1), :],
                                  buf.at[k, pl.ds(i, 1), :], sem).start()
        return carry

    lax.fori_loop(0, t, issue, 0)
    for k in range(TOP_K):
        pltpu.make_async_copy(ys_ref.at[pl.ds(0, t), :], buf.at[k], sem).wait()
    gate = gate_ref[...]
    ffn = gate[:, 0:1] * buf[0]
    for k in range(1, TOP_K):
        ffn = ffn + gate[:, k:k + 1] * buf[k]
    o_ref[...] = _layer_norm(DEEPNORM_ALPHA * x_ref[...] + ffn, g_ref[...], b_ref[...])


def _combine(dest3, gate, x2d, ys, ln_g, ln_b):
    n = x2d.shape[0]
    t = ROUTE_T
    const2 = lambda i: (0, 0)
    return pl.pallas_call(
        _combine_kernel,
        grid=(n // t,),
        in_specs=[pl.BlockSpec((1, 1, t * TOP_K), lambda i: (i, 0, 0), memory_space=pltpu.SMEM),
                  pl.BlockSpec((t, LANES), lambda i: (i, 0)),
                  pl.BlockSpec((t, D_MODEL), lambda i: (i, 0)),
                  pl.BlockSpec(memory_space=pl.ANY),
                  pl.BlockSpec((1, D_MODEL), const2),
                  pl.BlockSpec((1, D_MODEL), const2)],
        out_specs=pl.BlockSpec((t, D_MODEL), lambda i: (i, 0)),
        out_shape=jax.ShapeDtypeStruct((n, D_MODEL), F32),
        scratch_shapes=[pltpu.VMEM((TOP_K, t, D_MODEL), F32), pltpu.SemaphoreType.DMA(())],
        compiler_params=_cparams(("arbitrary",)),
        name="combine_ln2",
    )(dest3, gate, x2d, ys, ln_g, ln_b)


def _a_head_perm():
    grp = A_HEADS // A_KV_HEADS
    order = []
    for j in range(grp):
        order += [j, grp + j]
    return np.concatenate([np.arange(h * HEAD_DIM, (h + 1) * HEAD_DIM) for h in order])


def _moe(x1, layer, w_router, b_router, w_up, b_up, w_down, b_down, ln_g, ln_b):
    n = x1.shape[0]
    wr = jnp.pad(w_router, ((0, 0), (0, LANES - N_EXPERTS)))
    br = jnp.pad(b_router, (0, LANES - N_EXPERTS), constant_values=NEG_INF).reshape(1, LANES)
    eidx, gate, rank, cnt = _route(x1, wr, br)
    items, start = _expert_items(cnt[0, :N_EXPERTS], n * TOP_K)
    e4 = eidx[:, :TOP_K]
    sel = e4[:, :, None] == jnp.arange(N_EXPERTS, dtype=jnp.int32)
    dest = jnp.sum(jnp.where(sel, start, 0), axis=-1) + rank[:, :TOP_K]
    dest3 = dest.reshape(n // ROUTE_T, 1, ROUTE_T * TOP_K).astype(jnp.int32)
    xs = _dispatch(x1, dest3)
    lead = (DEPTH, N_EXPERTS, 1)
    ys = _experts(items, xs, w_up, b_up.reshape(lead + (-1,)), w_down, b_down.reshape(lead + (-1,)), layer)
    return _combine(dest3, gate, x1, ys, ln_g.reshape(1, -1), ln_b.reshape(1, -1))


def kernel(x, w_in, a_sink, lambda_q1, lambda_k1, lambda_q2, lambda_k2, diff_norm_g, na_rpb,
           w_branch, w_out, ln1_g, ln1_b, w_router, b_router, w_up, b_up, w_down, b_down,
           ln2_g, ln2_b):
    bsz, seq, d = x.shape
    n = bsz * seq
    rows = seq // GRID_W
    perm = _a_head_perm()
    slopes_b = jnp.asarray(_ALIBI[A_HEADS:], F32)
    xcur = x.reshape(n, d)
    for l in range(DEPTH):
        w = w_in[l]
        qscale = HEAD_DIM ** -0.5
        w_qkv = jnp.concatenate([
            w[:, :512][:, perm] * qscale, w[:, 512:768],
            w[:, 768:1280] * qscale, w[:, 1280:2304],
            w[:, 2304:2816] * qscale, w[:, 2816:QKV_W]], axis=1).astype(BF16)
        w_gate = w[:, QKV_W:].astype(BF16)
        w_br = jnp.stack([w_branch[l, 0][perm], w_branch[l, 1], w_branch[l, 2]]).astype(BF16)
        lam_init = 0.8 - 0.6 * math.exp(-0.3 * l)
        lamv = jnp.stack([lambda_q1[l], lambda_k1[l], lambda_q2[l], lambda_k2[l]]).astype(F32)

        proj = _inproj(xcur, w_qkv)
        oa = _attn_a(proj, a_sink[l].astype(F32), bsz, seq)
        ob = _attn_b(proj, slopes_b, lamv, diff_norm_g[l].reshape(1, -1).astype(F32), bsz, seq, lam_init)
        oc = _attn_c(proj, _na_bias_table(na_rpb[l], rows), bsz, seq)
        x1 = _merge(xcur, oa, ob, oc, w_gate, w_br, w_out[l].astype(BF16),
                    ln1_g[l].reshape(1, -1), ln1_b[l].reshape(1, -1))
        xcur = _moe(x1, l, w_router[l], b_router[l], w_up, b_up, w_down, b_down,
                    ln2_g[l], ln2_b[l])
    return xcur.reshape(bsz, seq, d)
```

```python
import functools
import math

import numpy as np
import jax
import jax.numpy as jnp
from jax import lax
from jax.experimental import pallas as pl
from jax.experimental.pallas import tpu as pltpu

F32 = jnp.float32
BF16 = jnp.bfloat16

D_MODEL = 1024
DEPTH = 2
HEAD_DIM = 64
A_HEADS = 8
A_KV_HEADS = 2
WINDOW = 128
B_HEADS = 4
C_HEADS = 8
GRID_W = 64
NA_ROWS = 8
NA_COLS = 16
MIX_W = 512
N_BRANCH = 3
N_EXPERTS = 32
TOP_K = 4
D_EXPERT = 1024
SWIGLU_LIMIT = 7.0
SWIGLU_ALPHA = 1.702
LN_EPS = 1e-5
NEG_INF = -1e30
DEEPNORM_ALPHA = (2 * DEPTH) ** 0.25

LANES = 128
QKV_W = 3840
GATE_W = N_BRANCH * D_MODEL
A_Q_BLK, A_K_BLK, A_V_BLK = 0, 4, 5
B_Q_BLK, B_K_BLK, B_V_BLK = 6, 10, 14
C_Q_BLK, C_K_BLK, C_V_BLK = 18, 22, 26

A_BAND = 3 * WINDOW
B_TQ = 256
NA_QROWS = 4
NA_SLAB = NA_QROWS + NA_ROWS
MOE_BM = 512
ROUTE_T = 256
VMEM_LIMIT = 56 * 1024 * 1024

_ALIBI = [float(2.0 ** (-8.0 * (i + 1) / (A_HEADS + B_HEADS))) for i in range(A_HEADS + B_HEADS)]


def _cparams(sem):
    return pltpu.CompilerParams(dimension_semantics=sem, vmem_limit_bytes=VMEM_LIMIT)


def _layer_norm(y, g, b):
    mu = jnp.mean(y, axis=-1, keepdims=True)
    yc = y - mu
    var = jnp.mean(yc * yc, axis=-1, keepdims=True)
    return yc * lax.rsqrt(var + LN_EPS) * g + b


def _inproj_kernel(x_ref, w_ref, o_ref, *, chunk):
    xb = x_ref[...].astype(BF16)
    for c in range(QKV_W // chunk):
        sl = slice(c * chunk, (c + 1) * chunk)
        o_ref[:, sl] = jnp.dot(xb, w_ref[:, sl], preferred_element_type=F32).astype(BF16)


def _inproj(x2d, w_qkv):
    n = x2d.shape[0]
    tm = 512
    return pl.pallas_call(
        functools.partial(_inproj_kernel, chunk=768),
        grid=(n // tm,),
        in_specs=[pl.BlockSpec((tm, D_MODEL), lambda i: (i, 0)),
                  pl.BlockSpec((D_MODEL, QKV_W), lambda i: (0, 0))],
        out_specs=pl.BlockSpec((tm, QKV_W), lambda i: (i, 0)),
        out_shape=jax.ShapeDtypeStruct((n, QKV_W), BF16),
        compiler_params=_cparams(("arbitrary",)),
        name="inproj",
    )(x2d, w_qkv)


def _attn_a_kernel(sink_ref, q_ref, k_ref, v_ref, o_ref, bias_ref, sink_tab, *, seq):
    nblk = seq // WINDOW
    heads = [j + 4 * hf for j in range(4) for hf in range(2)]
    lane = lax.broadcasted_iota(jnp.int32, (WINDOW, LANES), 1)
    low = lane < HEAD_DIM
    dn = (((1,), (1,)), ((), ()))

    @pl.when(pl.program_id(0) == 0)
    def _():
        kq = (lax.broadcasted_iota(jnp.int32, (WINDOW, A_BAND), 1)
              - lax.broadcasted_iota(jnp.int32, (WINDOW, A_BAND), 0))
        for c in range(3):
            dist = jnp.abs(kq - c * WINDOW).astype(F32)
            for i, h in enumerate(heads):
                bias_ref[c, i * WINDOW:(i + 1) * WINDOW, :] = jnp.where(
                    dist <= float(WINDOW), _ALIBI[h] * dist, -NEG_INF)
        for i, h in enumerate(heads):
            sink_tab[i * WINDOW:(i + 1) * WINDOW, :] = jnp.full((WINDOW, LANES), sink_ref[h], F32)

    sink_col = jnp.max(sink_tab[...], axis=-1, keepdims=True)

    def body(n, carry):
        q0 = pl.multiple_of(n * WINDOW, WINDOW)
        start = pl.multiple_of(jnp.clip((n - 1) * WINDOW, 0, seq - A_BAND), WINDOW)
        cfg = jnp.where(n == 0, 0, jnp.where(n == nblk - 1, 2, 1))
        kb = k_ref[pl.ds(start, A_BAND), :]
        vb = v_ref[pl.ds(start, A_BAND), :]
        pieces = []
        for j in range(4):
            qt = q_ref[pl.ds(q0, WINDOW), j * LANES:(j + 1) * LANES]
            zero = jnp.zeros_like(qt)
            lhs = jnp.concatenate([jnp.where(low, qt, zero), jnp.where(low, zero, qt)], axis=0)
            pieces.append(lax.dot_general(lhs, kb, dn, preferred_element_type=F32))
        t = jnp.concatenate(pieces, axis=0) - bias_ref[cfg]
        m = jnp.maximum(jnp.max(t, axis=-1, keepdims=True), sink_col)
        e = jnp.exp(t - m)
        den = jnp.sum(e, axis=-1, keepdims=True) + jnp.exp(sink_col - m)
        p = (e * (1.0 / den)).astype(BF16)
        pv = jnp.dot(p, vb, preferred_element_type=F32)
        for j in range(4):
            o = jnp.where(low, pv[2 * j * WINDOW:(2 * j + 1) * WINDOW],
                          pv[(2 * j + 1) * WINDOW:(2 * j + 2) * WINDOW])
            o_ref[pl.ds(q0, WINDOW), j * LANES:(j + 1) * LANES] = o.astype(BF16)
        return carry

    lax.fori_loop(0, nblk, body, 0, unroll=4)


def _attn_a(proj, sink, bsz, seq):
    return pl.pallas_call(
        functools.partial(_attn_a_kernel, seq=seq),
        grid=(bsz,),
        in_specs=[pl.BlockSpec(memory_space=pltpu.SMEM),
                  pl.BlockSpec((seq, 4 * LANES), lambda b: (b, A_Q_BLK // 4)),
                  pl.BlockSpec((seq, LANES), lambda b: (b, A_K_BLK)),
                  pl.BlockSpec((seq, LANES), lambda b: (b, A_V_BLK))],
        out_specs=pl.BlockSpec((seq, MIX_W), lambda b: (b, 0)),
        out_shape=jax.ShapeDtypeStruct((bsz * seq, MIX_W), BF16),
        scratch_shapes=[pltpu.VMEM((3, A_HEADS * WINDOW, A_BAND), F32),
                        pltpu.VMEM((A_HEADS * WINDOW, LANES), F32)],
        compiler_params=_cparams(("arbitrary",)),
        name="attn_a",
    )(sink, proj, proj, proj)


def _attn_b_kernel(slope_ref, lamv_ref, gain_ref, q_ref, k_ref, v_ref, o_ref, bias_ref, *,
                   seq, lam_init):
    nblk = seq // B_TQ
    slope = slope_ref[pl.program_id(0)]

    @pl.when(pl.program_id(1) == 0)
    def _():
        ji = (lax.broadcasted_iota(jnp.int32, (B_TQ, B_TQ), 1)
              - lax.broadcasted_iota(jnp.int32, (B_TQ, B_TQ), 0))
        for dd in range(2 * nblk - 1):
            bias_ref[dd] = slope * jnp.abs(ji + (dd - (nblk - 1)) * B_TQ).astype(F32)

    lv = lamv_ref[...]
    lam = (jnp.exp(jnp.sum(lv[0:1] * lv[1:2], axis=-1, keepdims=True))
           - jnp.exp(jnp.sum(lv[2:3] * lv[3:4], axis=-1, keepdims=True)) + lam_init)
    scale = gain_ref[...] * (1.0 - lam_init)
    lane = lax.broadcasted_iota(jnp.int32, (B_TQ, LANES), 1)
    low = lane < HEAD_DIM
    kall = k_ref[...]
    vall = v_ref[...]
    dn = (((1,), (1,)), ((), ()))

    def body(n, carry):
        q0 = pl.multiple_of(n * B_TQ, B_TQ)
        qt = q_ref[pl.ds(q0, B_TQ), :]
        zero = jnp.zeros_like(qt)
        bias = jnp.concatenate([bias_ref[kc - n + (nblk - 1)] for kc in range(nblk)], axis=1)
        es, ls = [], []
        for mp in range(2):
            qm = jnp.where(low, qt, zero) if mp == 0 else jnp.where(low, zero, qt)
            s = lax.dot_general(qm, kall, dn, preferred_element_type=F32) - bias
            m = jnp.max(s, axis=-1, keepdims=True)
            e = jnp.exp(s - m)
            ls.append(jnp.sum(e, axis=-1, keepdims=True))
            es.append(e.astype(BF16))
        a = es[0] * (1.0 / ls[0]).astype(BF16) - es[1] * (lam / ls[1]).astype(BF16)
        o = jnp.dot(a, vall, preferred_element_type=F32)
        o = o * lax.rsqrt(jnp.mean(o * o, axis=-1, keepdims=True) + LN_EPS)
        o_ref[pl.ds(q0, B_TQ), :] = (o * scale).astype(BF16)
        return carry

    lax.fori_loop(0, nblk, body, 0, unroll=2)


def _attn_b(proj, slopes_b, lamv, gain, bsz, seq, lam_init):
    nblk = seq // B_TQ
    return pl.pallas_call(
        functools.partial(_attn_b_kernel, seq=seq, lam_init=lam_init),
        grid=(B_HEADS, bsz),
        in_specs=[pl.BlockSpec(memory_space=pltpu.SMEM),
                  pl.BlockSpec((4, HEAD_DIM), lambda h, b: (0, 0)),
                  pl.BlockSpec((1, 2 * HEAD_DIM), lambda h, b: (0, 0)),
                  pl.BlockSpec((seq, LANES), lambda h, b: (b, B_Q_BLK + h)),
                  pl.BlockSpec((seq, LANES), lambda h, b: (b, B_K_BLK + h)),
                  pl.BlockSpec((seq, LANES), lambda h, b: (b, B_V_BLK + h))],
        out_specs=pl.BlockSpec((seq, LANES), lambda h, b: (b, h)),
        out_shape=jax.ShapeDtypeStruct((bsz * seq, MIX_W), BF16),
        scratch_shapes=[pltpu.VMEM((2 * nblk - 1, B_TQ, B_TQ), F32)],
        compiler_params=_cparams(("arbitrary", "arbitrary")),
        name="attn_b",
    )(slopes_b, lamv, gain, proj, proj, proj)


def _attn_c_kernel(bias_ref, q_ref, k_ref, v_ref, o_ref, *, rows):
    qtok = NA_QROWS * GRID_W
    ktok = NA_SLAB * GRID_W
    ngrp = rows // NA_QROWS
    lane = lax.broadcasted_iota(jnp.int32, (qtok, LANES), 1)
    low = lane < HEAD_DIM
    dn = (((1,), (1,)), ((), ()))

    def body(g, carry):
        slab0 = jnp.clip(g * NA_QROWS - NA_ROWS // 2, 0, rows - NA_SLAB)
        k0 = pl.multiple_of(slab0 * GRID_W, GRID_W)
        q0 = pl.multiple_of(g * qtok, qtok)
        cfg = jnp.where(g == 0, 0, jnp.where(g == ngrp - 1, 2, 1))
        qr = q_ref[pl.ds(q0, qtok), :]
        zero = jnp.zeros_like(qr)
        lhs = jnp.concatenate([jnp.where(low, qr, zero), jnp.where(low, zero, qr)], axis=0)
        ks = k_ref[pl.ds(k0, ktok), :]
        vs = v_ref[pl.ds(k0, ktok), :]
        s = lax.dot_general(lhs, ks, dn, preferred_element_type=F32) + bias_ref[0, cfg]
        m = jnp.max(s, axis=-1, keepdims=True)
        e = jnp.exp(s - m)
        p = (e * (1.0 / jnp.sum(e, axis=-1, keepdims=True))).astype(BF16)
        pv = jnp.dot(p, vs, preferred_element_type=F32)
        o = jnp.where(low, pv[:qtok], pv[qtok:])
        o_ref[pl.ds(q0, qtok), :] = o.astype(BF16)
        return carry

    lax.fori_loop(0, ngrp, body, 0, unroll=True)


def _attn_c(proj, bias_tab, bsz, seq):
    rows = seq // GRID_W
    npair = C_HEADS // 2
    return pl.pallas_call(
        functools.partial(_attn_c_kernel, rows=rows),
        grid=(npair, bsz),
        in_specs=[pl.BlockSpec((1,) + bias_tab.shape[1:], lambda p, b: (p, 0, 0, 0)),
                  pl.BlockSpec((seq, LANES), lambda p, b: (b, C_Q_BLK + p)),
                  pl.BlockSpec((seq, LANES), lambda p, b: (b, C_K_BLK + p)),
                  pl.BlockSpec((seq, LANES), lambda p, b: (b, C_V_BLK + p))],
        out_specs=pl.BlockSpec((seq, LANES), lambda p, b: (b, p)),
        out_shape=jax.ShapeDtypeStruct((bsz * seq, MIX_W), BF16),
        compiler_params=_cparams(("arbitrary", "arbitrary")),
        name="attn_c",
    )(bias_tab, proj, proj, proj)


def _na_bias_table(rpb, rows):
    assert rows % NA_QROWS == 0 and rows >= NA_SLAB + NA_QROWS and rows >= NA_ROWS
    qc = np.arange(GRID_W)[:, None]
    kc = np.arange(GRID_W)[None, :]
    cs = np.clip(qc - NA_COLS // 2, 0, GRID_W - NA_COLS)
    valid = (kc >= cs) & (kc < cs + NA_COLS)
    off = GRID_W - NA_COLS
    rp = jnp.pad(rpb.astype(F32), ((0, 0), (0, 0), (off, off)))
    toep = jnp.stack([rp[:, :, off + NA_COLS - 1 - c:off + NA_COLS - 1 - c + GRID_W]
                      for c in range(GRID_W)], axis=2)
    toep = jnp.where(valid[None, None], toep, NEG_INF)
    masked = jnp.full((C_HEADS, GRID_W, GRID_W), NEG_INF, F32)
    ngrp = rows // NA_QROWS
    cfgs = []
    for g in (0, 1, ngrp - 1):
        slab0 = int(np.clip(g * NA_QROWS - NA_ROWS // 2, 0, rows - NA_SLAB))
        qrows = []
        for a in range(NA_QROWS):
            r = g * NA_QROWS + a
            rs = int(np.clip(r - NA_ROWS // 2, 0, rows - NA_ROWS))
            blocks = []
            for u in range(NA_SLAB):
                krow = slab0 + u
                inside = rs <= krow < rs + NA_ROWS
                blocks.append(toep[:, krow - r + NA_ROWS - 1] if inside else masked)
            qrows.append(jnp.concatenate(blocks, axis=-1))
        cfgs.append(jnp.concatenate(qrows, axis=1))
    tab = jnp.stack(cfgs, axis=1)
    tab = tab.reshape(C_HEADS // 2, 2, 3, NA_QROWS * GRID_W, NA_SLAB * GRID_W)
    return tab.transpose(0, 2, 1, 3, 4).reshape(C_HEADS // 2, 3, 2 * NA_QROWS * GRID_W,
                                                NA_SLAB * GRID_W)


def _merge_kernel(x_ref, oa_ref, ob_ref, oc_ref, wg_ref, wbr_ref, wout_ref, g_ref, b_ref, o_ref):
    x = x_ref[...]
    xb = x.astype(BF16)
    merged = None
    for i, br_ref in enumerate((oa_ref, ob_ref, oc_ref)):
        gate = jax.nn.sigmoid(jnp.dot(xb, wg_ref[:, i * D_MODEL:(i + 1) * D_MODEL],
                                      preferred_element_type=F32))
        br = jnp.dot(br_ref[...], wbr_ref[i], preferred_element_type=F32)
        merged = gate * br if merged is None else merged + gate * br
    mix = jnp.dot(merged.astype(BF16), wout_ref[...], preferred_element_type=F32)
    o_ref[...] = _layer_norm(DEEPNORM_ALPHA * x + mix, g_ref[...], b_ref[...])


def _merge(x2d, oa, ob, oc, w_gate, w_br, w_out, ln_g, ln_b):
    n = x2d.shape[0]
    tm = 256
    const2 = lambda i: (0, 0)
    return pl.pallas_call(
        _merge_kernel,
        grid=(n // tm,),
        in_specs=[pl.BlockSpec((tm, D_MODEL), lambda i: (i, 0)),
                  pl.BlockSpec((tm, MIX_W), lambda i: (i, 0)),
                  pl.BlockSpec((tm, MIX_W), lambda i: (i, 0)),
                  pl.BlockSpec((tm, MIX_W), lambda i: (i, 0)),
                  pl.BlockSpec((D_MODEL, GATE_W), const2),
                  pl.BlockSpec((N_BRANCH, MIX_W, D_MODEL), lambda i: (0, 0, 0)),
                  pl.BlockSpec((D_MODEL, D_MODEL), const2),
                  pl.BlockSpec((1, D_MODEL), const2),
                  pl.BlockSpec((1, D_MODEL), const2)],
        out_specs=pl.BlockSpec((tm, D_MODEL), lambda i: (i, 0)),
        out_shape=jax.ShapeDtypeStruct((n, D_MODEL), F32),
        compiler_params=_cparams(("arbitrary",)),
        name="merge_ln1",
    )(x2d, oa, ob, oc, w_gate, w_br, w_out, ln_g, ln_b)


def _route_kernel(x_ref, w_ref, b_ref, eidx_ref, gate_ref, rank_ref, cnt_ref, carry_ref):
    t = x_ref.shape[0]

    @pl.when(pl.program_id(0) == 0)
    def _():
        carry_ref[...] = jnp.zeros_like(carry_ref)

    logits = jnp.dot(x_ref[...], w_ref[...], preferred_element_type=F32,
                     precision=lax.Precision.HIGHEST) + b_ref[...]
    lane = lax.broadcasted_iota(jnp.int32, (t, LANES), 1).astype(F32)
    work = logits
    sels, vals, idxs = [], [], []
    for _ in range(TOP_K):
        m = jnp.max(work, axis=-1, keepdims=True)
        idx = jnp.min(jnp.where(work == m, lane, float(LANES)), axis=-1, keepdims=True)
        sel = lane == idx
        work = jnp.where(sel, -jnp.inf, work)
        sels.append(sel)
        vals.append(m)
        idxs.append(idx)
    ex = [jnp.exp(v - vals[0]) for v in vals]
    den = ex[0] + ex[1] + ex[2] + ex[3]
    onehot = jnp.zeros((t, LANES), F32)
    for sel in sels:
        onehot = jnp.where(sel, 1.0, onehot)
    ri = lax.broadcasted_iota(jnp.int32, (t, t), 0)
    ci = lax.broadcasted_iota(jnp.int32, (t, t), 1)
    tri = jnp.where(ci < ri, 1.0, 0.0).astype(BF16)
    before = jnp.dot(tri, onehot.astype(BF16), preferred_element_type=F32) + carry_ref[...]
    eidx = jnp.zeros((t, LANES), F32)
    gate = jnp.zeros((t, LANES), F32)
    rank = jnp.zeros((t, LANES), F32)
    for k in range(TOP_K):
        rk = jnp.sum(jnp.where(sels[k], before, 0.0), axis=-1, keepdims=True)
        eidx = jnp.where(lane == float(k), idxs[k], eidx)
        gate = jnp.where(lane == float(k), ex[k] / den, gate)
        rank = jnp.where(lane == float(k), rk, rank)
    eidx_ref[...] = eidx.astype(jnp.int32)
    gate_ref[...] = gate
    rank_ref[...] = rank.astype(jnp.int32)
    carry_ref[...] += jnp.sum(onehot, axis=0, keepdims=True)
    cnt_ref[...] = carry_ref[...].astype(jnp.int32)


def _route(x2d, w_router_pad, b_router_pad):
    n = x2d.shape[0]
    t = ROUTE_T
    tile = pl.BlockSpec((t, LANES), lambda i: (i, 0))
    return pl.pallas_call(
        _route_kernel,
        grid=(n // t,),
        in_specs=[pl.BlockSpec((t, D_MODEL), lambda i: (i, 0)),
                  pl.BlockSpec((D_MODEL, LANES), lambda i: (0, 0)),
                  pl.BlockSpec((1, LANES), lambda i: (0, 0))],
        out_specs=[tile, tile, tile, pl.BlockSpec((1, LANES), lambda i: (0, 0))],
        out_shape=[jax.ShapeDtypeStruct((n, LANES), jnp.int32),
                   jax.ShapeDtypeStruct((n, LANES), F32),
                   jax.ShapeDtypeStruct((n, LANES), jnp.int32),
                   jax.ShapeDtypeStruct((1, LANES), jnp.int32)],
        scratch_shapes=[pltpu.VMEM((1, LANES), F32)],
        compiler_params=_cparams(("arbitrary",)),
        name="route",
    )(x2d, w_router_pad, b_router_pad)


def _row_copy(src_ref, src_row, dst_ref, dst_row, sem):
    return pltpu.make_async_copy(src_ref.at[pl.ds(src_row, 1), :],
                                 dst_ref.at[pl.ds(dst_row, 1), :], sem)


def _dispatch_kernel(dest_ref, x_ref, xs_ref, sem):
    t = x_ref.shape[0]

    def issue(i, carry):
        for k in range(TOP_K):
            _row_copy(x_ref, i, xs_ref, dest_ref[0, 0, i * TOP_K + k], sem).start()
        return carry

    lax.fori_loop(0, t, issue, 0)
    for _ in range(TOP_K):
        pltpu.make_async_copy(x_ref, xs_ref.at[pl.ds(0, t), :], sem).wait()


def _dispatch(x2d, dest3):
    n = x2d.shape[0]
    t = ROUTE_T
    return pl.pallas_call(
        _dispatch_kernel,
        grid=(n // t,),
        in_specs=[pl.BlockSpec((1, 1, t * TOP_K), lambda i: (i, 0, 0), memory_space=pltpu.SMEM),
                  pl.BlockSpec((t, D_MODEL), lambda i: (i, 0))],
        out_specs=pl.BlockSpec(memory_space=pl.ANY),
        out_shape=jax.ShapeDtypeStruct((n * TOP_K, D_MODEL), F32),
        scratch_shapes=[pltpu.SemaphoreType.DMA(())],
        compiler_params=_cparams(("arbitrary",)),
        name="dispatch",
    )(dest3, x2d)


def _expert_kernel(it_e, it_b, it_lo, it_hi, it_first, n_items, xs_ref, wu_ref, bu_ref, wd_ref,
                   bd_ref, ys_ref, wu_bf, wd_bf):
    del it_b
    j = pl.program_id(0)

    @pl.when(j < n_items[0])
    def _():
        @pl.when(jnp.logical_or(j == 0, it_e[j] != it_e[jnp.maximum(j - 1, 0)]))
        def _():
            wu_bf[...] = wu_ref[0, 0].astype(BF16)
            wd_bf[...] = wd_ref[0, 0].astype(BF16)

        xb = xs_ref[...].astype(BF16)
        bu = bu_ref[0, 0]
        hg = jnp.dot(xb, wu_bf[:, :D_EXPERT], preferred_element_type=F32) + bu[:, :D_EXPERT]
        hl = jnp.dot(xb, wu_bf[:, D_EXPERT:], preferred_element_type=F32) + bu[:, D_EXPERT:]
        hg = jnp.minimum(hg, SWIGLU_LIMIT)
        hl = jnp.clip(hl, -SWIGLU_LIMIT, SWIGLU_LIMIT)
        act = hg * jax.nn.sigmoid(SWIGLU_ALPHA * hg) * (hl + 1.0)
        y = jnp.dot(act.astype(BF16), wd_bf[...], preferred_element_type=F32) + bd_ref[0, 0]
        row = lax.broadcasted_iota(jnp.int32, (MOE_BM, 1), 0)
        mine = jnp.logical_and(row >= it_lo[j], row < it_hi[j])

        @pl.when(it_first[j] == 1)
        def _():
            ys_ref[...] = jnp.where(mine, y, 0.0)

        @pl.when(it_first[j] == 0)
        def _():
            ys_ref[...] = jnp.where(mine, y, ys_ref[...])


def _experts(items, xs, w_up, b_up4, w_down, b_down4, layer):
    n_rows = xs.shape[0]
    n_items = items[0].shape[0]

    def cur(j, ni):
        return jnp.minimum(j, ni[0] - 1)

    def w_map(j, ie, ib, lo, hi, fi, ni):
        return (layer, ie[cur(j, ni)], 0, 0)

    def row_map(j, ie, ib, lo, hi, fi, ni):
        return (ib[cur(j, ni)], 0)

    return pl.pallas_call(
        _expert_kernel,
        grid_spec=pltpu.PrefetchScalarGridSpec(
            num_scalar_prefetch=6,
            grid=(n_items,),
            in_specs=[pl.BlockSpec((MOE_BM, D_MODEL), row_map),
                      pl.BlockSpec((1, 1, D_MODEL, 2 * D_EXPERT), w_map),
                      pl.BlockSpec((1, 1, 1, 2 * D_EXPERT), w_map),
                      pl.BlockSpec((1, 1, D_EXPERT, D_MODEL), w_map),
                      pl.BlockSpec((1, 1, 1, D_MODEL), w_map)],
            out_specs=pl.BlockSpec((MOE_BM, D_MODEL), row_map),
            scratch_shapes=[pltpu.VMEM((D_MODEL, 2 * D_EXPERT), BF16),
                            pltpu.VMEM((D_EXPERT, D_MODEL), BF16)]),
        out_shape=jax.ShapeDtypeStruct((n_rows, D_MODEL), F32),
        compiler_params=_cparams(("arbitrary",)),
        name="experts",
    )(*items, xs, w_up, b_up4, w_down, b_down4)


def _expert_items(counts, n_rows):
    n_items = n_rows // MOE_BM + N_EXPERTS
    end = jnp.cumsum(counts)
    start = end - counts
    first_b = start // MOE_BM
    nb = jnp.where(counts > 0, (end - 1) // MOE_BM - first_b + 1, 0)
    item_end = jnp.cumsum(nb)
    item_start = item_end - nb
    total = item_end[-1]
    jc = jnp.minimum(jnp.arange(n_items, dtype=jnp.int32), total - 1)
    it_e = jnp.sum((item_end[None, :] <= jc[:, None]).astype(jnp.int32), axis=1)
    sel = it_e[:, None] == jnp.arange(N_EXPERTS, dtype=jnp.int32)[None, :]
    pick = lambda v: jnp.sum(jnp.where(sel, v[None, :], 0), axis=1)
    it_b = pick(first_b) + jc - pick(item_start)
    it_lo = jnp.maximum(pick(start), it_b * MOE_BM) - it_b * MOE_BM
    it_hi = jnp.minimum(pick(end), (it_b + 1) * MOE_BM) - it_b * MOE_BM
    prev_b = jnp.concatenate([jnp.full((1,), -1, jnp.int32), it_b[:-1]])
    it_first = (it_b != prev_b).astype(jnp.int32)
    i32 = lambda v: v.astype(jnp.int32)
    return (i32(it_e), i32(it_b), i32(it_lo), i32(it_hi), it_first, i32(total).reshape(1)), start


def _combine_kernel(dest_ref, gate_ref, x_ref, ys_ref, g_ref, b_ref, o_ref, buf, sem):
    t = x_ref.shape[0]

    def issue(i, carry):
        for k in range(TOP_K):
            pltpu.make_async_copy(ys_ref.at[pl.ds(dest_ref[0, 0, i * TOP_K + k], 1), :],
                                  buf.at[k, pl.ds(i, 1), :], sem).start()
        return carry

    lax.fori_loop(0, t, issue, 0)
    for k in range(TOP_K):
        pltpu.make_async_copy(ys_ref.at[pl.ds(0, t), :], buf.at[k], sem).wait()
    gate = gate_ref[...]
    ffn = gate[:, 0:1] * buf[0]
    for k in range(1, TOP_K):
        ffn = ffn + gate[:, k:k + 1] * buf[k]
    o_ref[...] = _layer_norm(DEEPNORM_ALPHA * x_ref[...] + ffn, g_ref[...], b_ref[...])


def _combine(dest3, gate, x2d, ys, ln_g, ln_b):
    n = x2d.shape[0]
    t = ROUTE_T
    const2 = lambda i: (0, 0)
    return pl.pallas_call(
        _combine_kernel,
        grid=(n // t,),
        in_specs=[pl.BlockSpec((1, 1, t * TOP_K), lambda i: (i, 0, 0), memory_space=pltpu.SMEM),
                  pl.BlockSpec((t, LANES), lambda i: (i, 0)),
                  pl.BlockSpec((t, D_MODEL), lambda i: (i, 0)),
                  pl.BlockSpec(memory_space=pl.ANY),
                  pl.BlockSpec((1, D_MODEL), const2),
                  pl.BlockSpec((1, D_MODEL), const2)],
        out_specs=pl.BlockSpec((t, D_MODEL), lambda i: (i, 0)),
        out_shape=jax.ShapeDtypeStruct((n, D_MODEL), F32),
        scratch_shapes=[pltpu.VMEM((TOP_K, t, D_MODEL), F32), pltpu.SemaphoreType.DMA(())],
        compiler_params=_cparams(("arbitrary",)),
        name="combine_ln2",
    )(dest3, gate, x2d, ys, ln_g, ln_b)


def _a_head_perm():
    grp = A_HEADS // A_KV_HEADS
    order = []
    for j in range(grp):
        order += [j, grp + j]
    return np.concatenate([np.arange(h * HEAD_DIM, (h + 1) * HEAD_DIM) for h in order])


def _moe(x1, layer, w_router, b_router, w_up, b_up, w_down, b_down, ln_g, ln_b):
    n = x1.shape[0]
    wr = jnp.pad(w_router, ((0, 0), (0, LANES - N_EXPERTS)))
    br = jnp.pad(b_router, (0, LANES - N_EXPERTS), constant_values=NEG_INF).reshape(1, LANES)
    eidx, gate, rank, cnt = _route(x1, wr, br)
    items, start = _expert_items(cnt[0, :N_EXPERTS], n * TOP_K)
    e4 = eidx[:, :TOP_K]
    sel = e4[:, :, None] == jnp.arange(N_EXPERTS, dtype=jnp.int32)
    dest = jnp.sum(jnp.where(sel, start, 0), axis=-1) + rank[:, :TOP_K]
    dest3 = dest.reshape(n // ROUTE_T, 1, ROUTE_T * TOP_K).astype(jnp.int32)
    xs = _dispatch(x1, dest3)
    lead = (DEPTH, N_EXPERTS, 1)
    ys = _experts(items, xs, w_up, b_up.reshape(lead + (-1,)), w_down, b_down.reshape(lead + (-1,)), layer)
    return _combine(dest3, gate, x1, ys, ln_g.reshape(1, -1), ln_b.reshape(1, -1))


def kernel(x, w_in, a_sink, lambda_q1, lambda_k1, lambda_q2, lambda_k2, diff_norm_g, na_rpb,
           w_branch, w_out, ln1_g, ln1_b, w_router, b_router, w_up, b_up, w_down, b_down,
           ln2_g, ln2_b):
    bsz, seq, d = x.shape
    n = bsz * seq
    rows = seq // GRID_W
    perm = _a_head_perm()
    slopes_b = jnp.asarray(_ALIBI[A_HEADS:], F32)
    xcur = x.reshape(n, d)
    for l in range(DEPTH):
        w = w_in[l]
        qscale = HEAD_DIM ** -0.5
        w_qkv = jnp.concatenate([
            w[:, :512][:, perm] * qscale, w[:, 512:768],
            w[:, 768:1280] * qscale, w[:, 1280:2304],
            w[:, 2304:2816] * qscale, w[:, 2816:QKV_W]], axis=1).astype(BF16)
        w_gate = w[:, QKV_W:].astype(BF16)
        w_br = jnp.stack([w_branch[l, 0][perm], w_branch[l, 1], w_branch[l, 2]]).astype(BF16)
        lam_init = 0.8 - 0.6 * math.exp(-0.3 * l)
        lamv = jnp.stack([lambda_q1[l], lambda_k1[l], lambda_q2[l], lambda_k2[l]]).astype(F32)

        proj = _inproj(xcur, w_qkv)
        oa = _attn_a(proj, a_sink[l].astype(F32), bsz, seq)
        ob = _attn_b(proj, slopes_b, lamv, diff_norm_g[l].reshape(1, -1).astype(F32), bsz, seq, lam_init)
        oc = _attn_c(proj, _na_bias_table(na_rpb[l], rows), bsz, seq)
        x1 = _merge(xcur, oa, ob, oc, w_gate, w_br, w_out[l].astype(BF16),
                    ln1_g[l].reshape(1, -1), ln1_b[l].reshape(1, -1))
        xcur = _moe(x1, l, w_router[l], b_router[l], w_up, b_up, w_down, b_down,
                    ln2_g[l], ln2_b[l])
    return xcur.reshape(bsz, seq, d)
```

```python
import functools
import math

import numpy as np
import jax
import jax.numpy as jnp
from jax import lax
from jax.experimental import pallas as pl
from jax.experimental.pallas import tpu as pltpu

F32 = jnp.float32
BF16 = jnp.bfloat16

D_MODEL = 1024
DEPTH = 2
HEAD_DIM = 64
A_HEADS = 8
A_KV_HEADS = 2
WINDOW = 128
B_HEADS = 4
C_HEADS = 8
GRID_W = 64
NA_ROWS = 8
NA_COLS = 16
MIX_W = 512
N_BRANCH = 3
N_EXPERTS = 32
TOP_K = 4
D_EXPERT = 1024
SWIGLU_LIMIT = 7.0
SWIGLU_ALPHA = 1.702
LN_EPS = 1e-5
NEG_INF = -1e30
DEEPNORM_ALPHA = (2 * DEPTH) ** 0.25

LANES = 128
QKV_W = 3840
GATE_W = N_BRANCH * D_MODEL
A_Q_BLK, A_K_BLK, A_V_BLK = 0, 4, 5
B_Q_BLK, B_K_BLK, B_V_BLK = 6, 10, 14
C_Q_BLK, C_K_BLK, C_V_BLK = 18, 22, 26

A_BAND = 3 * WINDOW
B_TQ = 256
NA_QROWS = 4
NA_SLAB = NA_QROWS + NA_ROWS
MOE_BM = 512
ROUTE_T = 256
VMEM_LIMIT = 56 * 1024 * 1024

_ALIBI = [float(2.0 ** (-8.0 * (i + 1) / (A_HEADS + B_HEADS))) for i in range(A_HEADS + B_HEADS)]


def _cparams(sem):
    return pltpu.CompilerParams(dimension_semantics=sem, vmem_limit_bytes=VMEM_LIMIT)


def _layer_norm(y, g, b):
    mu = jnp.mean(y, axis=-1, keepdims=True)
    yc = y - mu
    var = jnp.mean(yc * yc, axis=-1, keepdims=True)
    return yc * lax.rsqrt(var + LN_EPS) * g + b


def _inproj_kernel(x_ref, w_ref, o_ref, *, chunk):
    xb = x_ref[...].astype(BF16)
    for c in range(QKV_W // chunk):
        sl = slice(c * chunk, (c + 1) * chunk)
        o_ref[:, sl] = jnp.dot(xb, w_ref[:, sl], preferred_element_type=F32).astype(BF16)


def _inproj(x2d, w_qkv):
    n = x2d.shape[0]
    tm = 512
    return pl.pallas_call(
        functools.partial(_inproj_kernel, chunk=768),
        grid=(n // tm,),
        in_specs=[pl.BlockSpec((tm, D_MODEL), lambda i: (i, 0)),
                  pl.BlockSpec((D_MODEL, QKV_W), lambda i: (0, 0))],
        out_specs=pl.BlockSpec((tm, QKV_W), lambda i: (i, 0)),
        out_shape=jax.ShapeDtypeStruct((n, QKV_W), BF16),
        compiler_params=_cparams(("arbitrary",)),
        name="inproj",
    )(x2d, w_qkv)


def _attn_a_kernel(sink_ref, q_ref, k_ref, v_ref, o_ref, bias_ref, sink_tab, *, seq):
    nblk = seq // WINDOW
    heads = [j + 4 * hf for j in range(4) for hf in range(2)]
    lane = lax.broadcasted_iota(jnp.int32, (WINDOW, LANES), 1)
    low = lane < HEAD_DIM
    dn = (((1,), (1,)), ((), ()))

    @pl.when(pl.program_id(0) == 0)
    def _():
        kq = (lax.broadcasted_iota(jnp.int32, (WINDOW, A_BAND), 1)
              - lax.broadcasted_iota(jnp.int32, (WINDOW, A_BAND), 0))
        for c in range(3):
            dist = jnp.abs(kq - c * WINDOW).astype(F32)
            for i, h in enumerate(heads):
                bias_ref[c, i * WINDOW:(i + 1) * WINDOW, :] = jnp.where(
                    dist <= float(WINDOW), _ALIBI[h] * dist, -NEG_INF)
        for i, h in enumerate(heads):
            sink_tab[i * WINDOW:(i + 1) * WINDOW, :] = jnp.full((WINDOW, LANES), sink_ref[h], F32)

    sink_col = jnp.max(sink_tab[...], axis=-1, keepdims=True)

    def body(n, carry):
        q0 = pl.multiple_of(n * WINDOW, WINDOW)
        start = pl.multiple_of(jnp.clip((n - 1) * WINDOW, 0, seq - A_BAND), WINDOW)
        cfg = jnp.where(n == 0, 0, jnp.where(n == nblk - 1, 2, 1))
        kb = k_ref[pl.ds(start, A_BAND), :]
        vb = v_ref[pl.ds(start, A_BAND), :]
        pieces = []
        for j in range(4):
            qt = q_ref[pl.ds(q0, WINDOW), j * LANES:(j + 1) * LANES]
            zero = jnp.zeros_like(qt)
            lhs = jnp.concatenate([jnp.where(low, qt, zero), jnp.where(low, zero, qt)], axis=0)
            pieces.append(lax.dot_general(lhs, kb, dn, preferred_element_type=F32))
        t = jnp.concatenate(pieces, axis=0) - bias_ref[cfg]
        m = jnp.maximum(jnp.max(t, axis=-1, keepdims=True), sink_col)
        e = jnp.exp(t - m)
        den = jnp.sum(e, axis=-1, keepdims=True) + jnp.exp(sink_col - m)
        p = (e * (1.0 / den)).astype(BF16)
        pv = jnp.dot(p, vb, preferred_element_type=F32)
        for j in range(4):
            o = jnp.where(low, pv[2 * j * WINDOW:(2 * j + 1) * WINDOW],
                          pv[(2 * j + 1) * WINDOW:(2 * j + 2) * WINDOW])
            o_ref[pl.ds(q0, WINDOW), j * LANES:(j + 1) * LANES] = o.astype(BF16)
        return carry

    lax.fori_loop(0, nblk, body, 0, unroll=4)


def _attn_a(proj, sink, bsz, seq):
    return pl.pallas_call(
        functools.partial(_attn_a_kernel, seq=seq),
        grid=(bsz,),
        in_specs=[pl.BlockSpec(memory_space=pltpu.SMEM),
                  pl.BlockSpec((seq, 4 * LANES), lambda b: (b, A_Q_BLK // 4)),
                  pl.BlockSpec((seq, LANES), lambda b: (b, A_K_BLK)),
                  pl.BlockSpec((seq, LANES), lambda b: (b, A_V_BLK))],
        out_specs=pl.BlockSpec((seq, MIX_W), lambda b: (b, 0)),
        out_shape=jax.ShapeDtypeStruct((bsz * seq, MIX_W), BF16),
        scratch_shapes=[pltpu.VMEM((3, A_HEADS * WINDOW, A_BAND), F32),
                        pltpu.VMEM((A_HEADS * WINDOW, LANES), F32)],
        compiler_params=_cparams(("arbitrary",)),
        name="attn_a",
    )(sink, proj, proj, proj)


def _attn_b_kernel(slope_ref, lamv_ref, gain_ref, q_ref, k_ref, v_ref, o_ref, bias_ref, *,
                   seq, lam_init):
    nblk = seq // B_TQ
    slope = slope_ref[pl.program_id(0)]

    @pl.when(pl.program_id(1) == 0)
    def _():
        ji = (lax.broadcasted_iota(jnp.int32, (B_TQ, B_TQ), 1)
              - lax.broadcasted_iota(jnp.int32, (B_TQ, B_TQ), 0))
        for dd in range(2 * nblk - 1):
            bias_ref[dd] = slope * jnp.abs(ji + (dd - (nblk - 1)) * B_TQ).astype(F32)

    lv = lamv_ref[...]
    lam = (jnp.exp(jnp.sum(lv[0:1] * lv[1:2], axis=-1, keepdims=True))
           - jnp.exp(jnp.sum(lv[2:3] * lv[3:4], axis=-1, keepdims=True)) + lam_init)
    scale = gain_ref[...] * (1.0 - lam_init)
    lane = lax.broadcasted_iota(jnp.int32, (B_TQ, LANES), 1)
    low = lane < HEAD_DIM
    kall = k_ref[...]
    vall = v_ref[...]
    dn = (((1,), (1,)), ((), ()))

    def body(n, carry):
        q0 = pl.multiple_of(n * B_TQ, B_TQ)
        qt = q_ref[pl.ds(q0, B_TQ), :]
        zero = jnp.zeros_like(qt)
        bias = jnp.concatenate([bias_ref[kc - n + (nblk - 1)] for kc in range(nblk)], axis=1)
        es, ls = [], []
        for mp in range(2):
            qm = jnp.where(low, qt, zero) if mp == 0 else jnp.where(low, zero, qt)
            s = lax.dot_general(qm, kall, dn, preferred_element_type=F32) - bias
            m = jnp.max(s, axis=-1, keepdims=True)
            e = jnp.exp(s - m)
            ls.append(jnp.sum(e, axis=-1, keepdims=True))
            es.append(e.astype(BF16))
        a = es[0] * (1.0 / ls[0]).astype(BF16) - es[1] * (lam / ls[1]).astype(BF16)
        o = jnp.dot(a, vall, preferred_element_type=F32)
        o = o * lax.rsqrt(jnp.mean(o * o, axis=-1, keepdims=True) + LN_EPS)
        o_ref[pl.ds(q0, B_TQ), :] = (o * scale).astype(BF16)
        return carry

    lax.fori_loop(0, nblk, body, 0, unroll=2)


def _attn_b(proj, slopes_b, lamv, gain, bsz, seq, lam_init):
    nblk = seq // B_TQ
    return pl.pallas_call(
        functools.partial(_attn_b_kernel, seq=seq, lam_init=lam_init),
        grid=(B_HEADS, bsz),
        in_specs=[pl.BlockSpec(memory_space=pltpu.SMEM),
                  pl.BlockSpec((4, HEAD_DIM), lambda h, b: (0, 0)),
                  pl.BlockSpec((1, 2 * HEAD_DIM), lambda h, b: (0, 0)),
                  pl.BlockSpec((seq, LANES), lambda h, b: (b, B_Q_BLK + h)),
                  pl.BlockSpec((seq, LANES), lambda h, b: (b, B_K_BLK + h)),
                  pl.BlockSpec((seq, LANES), lambda h, b: (b, B_V_BLK + h))],
        out_specs=pl.BlockSpec((seq, LANES), lambda h, b: (b, h)),
        out_shape=jax.ShapeDtypeStruct((bsz * seq, MIX_W), BF16),
        scratch_shapes=[pltpu.VMEM((2 * nblk - 1, B_TQ, B_TQ), F32)],
        compiler_params=_cparams(("arbitrary", "arbitrary")),
        name="attn_b",
    )(slopes_b, lamv, gain, proj, proj, proj)


def _attn_c_kernel(bias_ref, q_ref, k_ref, v_ref, o_ref, *, rows):
    qtok = NA_QROWS * GRID_W
    ktok = NA_SLAB * GRID_W
    ngrp = rows // NA_QROWS
    lane = lax.broadcasted_iota(jnp.int32, (qtok, LANES), 1)
    low = lane < HEAD_DIM
    dn = (((1,), (1,)), ((), ()))

    def body(g, carry):
        slab0 = jnp.clip(g * NA_QROWS - NA_ROWS // 2, 0, rows - NA_SLAB)
        k0 = pl.multiple_of(slab0 * GRID_W, GRID_W)
        q0 = pl.multiple_of(g * qtok, qtok)
        cfg = jnp.where(g == 0, 0, jnp.where(g == ngrp - 1, 2, 1))
        qr = q_ref[pl.ds(q0, qtok), :]
        zero = jnp.zeros_like(qr)
        lhs = jnp.concatenate([jnp.where(low, qr, zero), jnp.where(low, zero, qr)], axis=0)
        ks = k_ref[pl.ds(k0, ktok), :]
        vs = v_ref[pl.ds(k0, ktok), :]
        s = lax.dot_general(lhs, ks, dn, preferred_element_type=F32) + bias_ref[0, cfg]
        m = jnp.max(s, axis=-1, keepdims=True)
        e = jnp.exp(s - m)
        p = (e * (1.0 / jnp.sum(e, axis=-1, keepdims=True))).astype(BF16)
        pv = jnp.dot(p, vs, preferred_element_type=F32)
        o = jnp.where(low, pv[:qtok], pv[qtok:])
        o_ref[pl.ds(q0, qtok), :] = o.astype(BF16)
        return carry

    lax.fori_loop(0, ngrp, body, 0, unroll=True)


def _attn_c(proj, bias_tab, bsz, seq):
    rows = seq // GRID_W
    npair = C_HEADS // 2
    return pl.pallas_call(
        functools.partial(_attn_c_kernel, rows=rows),
        grid=(npair, bsz),
        in_specs=[pl.BlockSpec((1,) + bias_tab.shape[1:], lambda p, b: (p, 0, 0, 0)),
                  pl.BlockSpec((seq, LANES), lambda p, b: (b, C_Q_BLK + p)),
                  pl.BlockSpec((seq, LANES), lambda p, b: (b, C_K_BLK + p)),
                  pl.BlockSpec((seq, LANES), lambda p, b: (b, C_V_BLK + p))],
        out_specs=pl.BlockSpec((seq, LANES), lambda p, b: (b, p)),
        out_shape=jax.ShapeDtypeStruct((bsz * seq, MIX_W), BF16),
        compiler_params=_cparams(("arbitrary", "arbitrary")),
        name="attn_c",
    )(bias_tab, proj, proj, proj)


def _na_bias_table(rpb, rows):
    assert rows % NA_QROWS == 0 and rows >= NA_SLAB + NA_QROWS and rows >= NA_ROWS
    qc = np.arange(GRID_W)[:, None]
    kc = np.arange(GRID_W)[None, :]
    cs = np.clip(qc - NA_COLS // 2, 0, GRID_W - NA_COLS)
    valid = (kc >= cs) & (kc < cs + NA_COLS)
    off = GRID_W - NA_COLS
    rp = jnp.pad(rpb.astype(F32), ((0, 0), (0, 0), (off, off)))
    toep = jnp.stack([rp[:, :, off + NA_COLS - 1 - c:off + NA_COLS - 1 - c + GRID_W]
                      for c in range(GRID_W)], axis=2)
    toep = jnp.where(valid[None, None], toep, NEG_INF)
    masked = jnp.full((C_HEADS, GRID_W, GRID_W), NEG_INF, F32)
    ngrp = rows // NA_QROWS
    cfgs = []
    for g in (0, 1, ngrp - 1):
        slab0 = int(np.clip(g * NA_QROWS - NA_ROWS // 2, 0, rows - NA_SLAB))
        qrows = []
        for a in range(NA_QROWS):
            r = g * NA_QROWS + a
            rs = int(np.clip(r - NA_ROWS // 2, 0, rows - NA_ROWS))
            blocks = []
            for u in range(NA_SLAB):
                krow = slab0 + u
                inside = rs <= krow < rs + NA_ROWS
                blocks.append(toep[:, krow - r + NA_ROWS - 1] if inside else masked)
            qrows.append(jnp.concatenate(blocks, axis=-1))
        cfgs.append(jnp.concatenate(qrows, axis=1))
    tab = jnp.stack(cfgs, axis=1)
    tab = tab.reshape(C_HEADS // 2, 2, 3, NA_QROWS * GRID_W, NA_SLAB * GRID_W)
    return tab.transpose(0, 2, 1, 3, 4).reshape(C_HEADS // 2, 3, 2 * NA_QROWS * GRID_W,
                                                NA_SLAB * GRID_W)


def _merge_kernel(x_ref, oa_ref, ob_ref, oc_ref, wg_ref, wbr_ref, wout_ref, g_ref, b_ref, o_ref):
    x = x_ref[...]
    xb = x.astype(BF16)
    merged = None
    for i, br_ref in enumerate((oa_ref, ob_ref, oc_ref)):
        gate = jax.nn.sigmoid(jnp.dot(xb, wg_ref[:, i * D_MODEL:(i + 1) * D_MODEL],
                                      preferred_element_type=F32))
        br = jnp.dot(br_ref[...], wbr_ref[i], preferred_element_type=F32)
        merged = gate * br if merged is None else merged + gate * br
    mix = jnp.dot(merged.astype(BF16), wout_ref[...], preferred_element_type=F32)
    o_ref[...] = _layer_norm(DEEPNORM_ALPHA * x + mix, g_ref[...], b_ref[...])


def _merge(x2d, oa, ob, oc, w_gate, w_br, w_out, ln_g, ln_b):
    n = x2d.shape[0]
    tm = 512
    const2 = lambda i: (0, 0)
    return pl.pallas_call(
        _merge_kernel,
        grid=(n // tm,),
        in_specs=[pl.BlockSpec((tm, D_MODEL), lambda i: (i, 0)),
                  pl.BlockSpec((tm, MIX_W), lambda i: (i, 0)),
                  pl.BlockSpec((tm, MIX_W), lambda i: (i, 0)),
                  pl.BlockSpec((tm, MIX_W), lambda i: (i, 0)),
                  pl.BlockSpec((D_MODEL, GATE_W), const2),
                  pl.BlockSpec((N_BRANCH, MIX_W, D_MODEL), lambda i: (0, 0, 0)),
                  pl.BlockSpec((D_MODEL, D_MODEL), const2),
                  pl.BlockSpec((1, D_MODEL), const2),
                  pl.BlockSpec((1, D_MODEL), const2)],
        out_specs=pl.BlockSpec((tm, D_MODEL), lambda i: (i, 0)),
        out_shape=jax.ShapeDtypeStruct((n, D_MODEL), F32),
        compiler_params=_cparams(("arbitrary",)),
        name="merge_ln1",
    )(x2d, oa, ob, oc, w_gate, w_br, w_out, ln_g, ln_b)


def _route_kernel(x_ref, w_ref, b_ref, eidx_ref, gate_ref, rank_ref, cnt_ref, carry_ref):
    t = x_ref.shape[0]

    @pl.when(pl.program_id(0) == 0)
    def _():
        carry_ref[...] = jnp.zeros_like(carry_ref)

    logits = jnp.dot(x_ref[...], w_ref[...], preferred_element_type=F32,
                     precision=lax.Precision.HIGHEST) + b_ref[...]
    lane = lax.broadcasted_iota(jnp.int32, (t, LANES), 1).astype(F32)
    work = logits
    sels, vals, idxs = [], [], []
    for _ in range(TOP_K):
        m = jnp.max(work, axis=-1, keepdims=True)
        idx = jnp.min(jnp.where(work == m, lane, float(LANES)), axis=-1, keepdims=True)
        sel = lane == idx
        work = jnp.where(sel, -jnp.inf, work)
        sels.append(sel)
        vals.append(m)
        idxs.append(idx)
    ex = [jnp.exp(v - vals[0]) for v in vals]
    den = ex[0] + ex[1] + ex[2] + ex[3]
    onehot = jnp.zeros((t, LANES), F32)
    for sel in sels:
        onehot = jnp.where(sel, 1.0, onehot)
    ri = lax.broadcasted_iota(jnp.int32, (t, t), 0)
    ci = lax.broadcasted_iota(jnp.int32, (t, t), 1)
    tri = jnp.where(ci < ri, 1.0, 0.0).astype(BF16)
    before = jnp.dot(tri, onehot.astype(BF16), preferred_element_type=F32) + carry_ref[...]
    eidx = jnp.zeros((t, LANES), F32)
    gate = jnp.zeros((t, LANES), F32)
    rank = jnp.zeros((t, LANES), F32)
    for k in range(TOP_K):
        rk = jnp.sum(jnp.where(sels[k], before, 0.0), axis=-1, keepdims=True)
        eidx = jnp.where(lane == float(k), idxs[k], eidx)
        gate = jnp.where(lane == float(k), ex[k] / den, gate)
        rank = jnp.where(lane == float(k), rk, rank)
    eidx_ref[...] = eidx.astype(jnp.int32)
    gate_ref[...] = gate
    rank_ref[...] = rank.astype(jnp.int32)
    carry_ref[...] += jnp.sum(onehot, axis=0, keepdims=True)
    cnt_ref[...] = carry_ref[...].astype(jnp.int32)


def _route(x2d, w_router_pad, b_router_pad):
    n = x2d.shape[0]
    t = ROUTE_T
    tile = pl.BlockSpec((t, LANES), lambda i: (i, 0))
    return pl.pallas_call(
        _route_kernel,
        grid=(n // t,),
        in_specs=[pl.BlockSpec((t, D_MODEL), lambda i: (i, 0)),
                  pl.BlockSpec((D_MODEL, LANES), lambda i: (0, 0)),
                  pl.BlockSpec((1, LANES), lambda i: (0, 0))],
        out_specs=[tile, tile, tile, pl.BlockSpec((1, LANES), lambda i: (0, 0))],
        out_shape=[jax.ShapeDtypeStruct((n, LANES), jnp.int32),
                   jax.ShapeDtypeStruct((n, LANES), F32),
                   jax.ShapeDtypeStruct((n, LANES), jnp.int32),
                   jax.ShapeDtypeStruct((1, LANES), jnp.int32)],
        scratch_shapes=[pltpu.VMEM((1, LANES), F32)],
        compiler_params=_cparams(("arbitrary",)),
        name="route",
    )(x2d, w_router_pad, b_router_pad)


def _row_copy(src_ref, src_row, dst_ref, dst_row, sem):
    return pltpu.make_async_copy(src_ref.at[pl.ds(src_row, 1), :],
                                 dst_ref.at[pl.ds(dst_row, 1), :], sem)


def _dispatch_kernel(dest_ref, x_ref, xs_ref, sem):
    t = x_ref.shape[0]

    def issue(i, carry):
        for k in range(TOP_K):
            _row_copy(x_ref, i, xs_ref, dest_ref[0, 0, i * TOP_K + k], sem).start(priority=k % 2)
        return carry

    lax.fori_loop(0, t, issue, 0, unroll=2)
    for _ in range(TOP_K):
        pltpu.make_async_copy(x_ref, xs_ref.at[pl.ds(0, t), :], sem).wait()


def _dispatch(x2d, dest3):
    n = x2d.shape[0]
    t = ROUTE_T
    return pl.pallas_call(
        _dispatch_kernel,
        grid=(n // t,),
        in_specs=[pl.BlockSpec((1, 1, t * TOP_K), lambda i: (i, 0, 0), memory_space=pltpu.SMEM),
                  pl.BlockSpec((t, D_MODEL), lambda i: (i, 0))],
        out_specs=pl.BlockSpec(memory_space=pl.ANY),
        out_shape=jax.ShapeDtypeStruct((n * TOP_K, D_MODEL), F32),
        scratch_shapes=[pltpu.SemaphoreType.DMA(())],
        compiler_params=_cparams(("arbitrary",)),
        name="dispatch",
    )(dest3, x2d)


def _expert_weight_copies(wu_hbm, wd_hbm, wu_f32, wd_f32, sems, layer, expert, slot):
    return (pltpu.make_async_copy(wu_hbm.at[layer, expert], wu_f32.at[slot], sems.at[slot, 0]),
            pltpu.make_async_copy(wd_hbm.at[layer, expert], wd_f32.at[slot], sems.at[slot, 1]))


def _expert_kernel(it_e, it_b, it_lo, it_hi, it_first, it_new, it_slot, it_next, n_items,
                   xs_ref, wu_hbm, bu_ref, wd_hbm, bd_ref, ys_ref,
                   wu_f32, wd_f32, wu_bf, wd_bf, sems, *, layer):
    del it_b
    j = pl.program_id(0)
    copies = functools.partial(_expert_weight_copies, wu_hbm, wd_hbm, wu_f32, wd_f32, sems, layer)

    @pl.when(j == 0)
    def _():
        for c in copies(it_e[0], 0):
            c.start()

    @pl.when(jnp.logical_and(j < n_items[0], it_new[j] == 1))
    def _():
        slot = it_slot[j]
        for c in copies(it_e[j], slot):
            c.wait()
        wu_bf[...] = wu_f32[slot].astype(BF16)
        wd_bf[...] = wd_f32[slot].astype(BF16)

        @pl.when(it_next[j] >= 0)
        def _():
            for c in copies(it_next[j], 1 - slot):
                c.start()

    @pl.when(j < n_items[0])
    def _():
        xb = xs_ref[...].astype(BF16)
        bu = bu_ref[0, 0]
        hg = jnp.dot(xb, wu_bf[:, :D_EXPERT], preferred_element_type=F32) + bu[:, :D_EXPERT]
        hl = jnp.dot(xb, wu_bf[:, D_EXPERT:], preferred_element_type=F32) + bu[:, D_EXPERT:]
        hg = jnp.minimum(hg, SWIGLU_LIMIT)
        hl = jnp.clip(hl, -SWIGLU_LIMIT, SWIGLU_LIMIT)
        act = hg * jax.nn.sigmoid(SWIGLU_ALPHA * hg) * (hl + 1.0)
        y = jnp.dot(act.astype(BF16), wd_bf[...], preferred_element_type=F32) + bd_ref[0, 0]
        row = lax.broadcasted_iota(jnp.int32, (MOE_BM, 1), 0)
        mine = jnp.logical_and(row >= it_lo[j], row < it_hi[j])

        @pl.when(it_first[j] == 1)
        def _():
            ys_ref[...] = jnp.where(mine, y, 0.0)

        @pl.when(it_first[j] == 0)
        def _():
            ys_ref[...] = jnp.where(mine, y, ys_ref[...])


def _experts(items, xs, w_up, b_up4, w_down, b_down4, layer):
    n_rows = xs.shape[0]
    n_items = items[0].shape[0]

    def cur(j, ni):
        return jnp.minimum(j, ni[0] - 1)

    def b_map(j, ie, ib, *rest):
        return (layer, ie[cur(j, rest[-1])], 0, 0)

    def row_map(j, ie, ib, *rest):
        return (ib[cur(j, rest[-1])], 0)

    return pl.pallas_call(
        functools.partial(_expert_kernel, layer=layer),
        grid_spec=pltpu.PrefetchScalarGridSpec(
            num_scalar_prefetch=len(items),
            grid=(n_items,),
            in_specs=[pl.BlockSpec((MOE_BM, D_MODEL), row_map),
                      pl.BlockSpec(memory_space=pl.ANY),
                      pl.BlockSpec((1, 1, 1, 2 * D_EXPERT), b_map),
                      pl.BlockSpec(memory_space=pl.ANY),
                      pl.BlockSpec((1, 1, 1, D_MODEL), b_map)],
            out_specs=pl.BlockSpec((MOE_BM, D_MODEL), row_map),
            scratch_shapes=[pltpu.VMEM((2, D_MODEL, 2 * D_EXPERT), F32),
                            pltpu.VMEM((2, D_EXPERT, D_MODEL), F32),
                            pltpu.VMEM((D_MODEL, 2 * D_EXPERT), BF16),
                            pltpu.VMEM((D_EXPERT, D_MODEL), BF16),
                            pltpu.SemaphoreType.DMA((2, 2))]),
        out_shape=jax.ShapeDtypeStruct((n_rows, D_MODEL), F32),
        compiler_params=_cparams(("arbitrary",)),
        name="experts",
    )(*items, xs, w_up, b_up4, w_down, b_down4)


def _expert_items(counts, n_rows):
    n_items = n_rows // MOE_BM + N_EXPERTS
    end = jnp.cumsum(counts)
    start = end - counts
    first_b = start // MOE_BM
    nb = jnp.where(counts > 0, (end - 1) // MOE_BM - first_b + 1, 0)
    item_end = jnp.cumsum(nb)
    item_start = item_end - nb
    total = item_end[-1]
    jc = jnp.minimum(jnp.arange(n_items, dtype=jnp.int32), total - 1)
    it_e = jnp.sum((item_end[None, :] <= jc[:, None]).astype(jnp.int32), axis=1)
    sel = it_e[:, None] == jnp.arange(N_EXPERTS, dtype=jnp.int32)[None, :]
    pick = lambda v: jnp.sum(jnp.where(sel, v[None, :], 0), axis=1)
    it_b = pick(first_b) + jc - pick(item_start)
    it_lo = jnp.maximum(pick(start), it_b * MOE_BM) - it_b * MOE_BM
    it_hi = jnp.minimum(pick(end), (it_b + 1) * MOE_BM) - it_b * MOE_BM
    prev_b = jnp.concatenate([jnp.full((1,), -1, jnp.int32), it_b[:-1]])
    it_first = (it_b != prev_b).astype(jnp.int32)
    prev_e = jnp.concatenate([jnp.full((1,), -1, jnp.int32), it_e[:-1]])
    it_new = (it_e != prev_e).astype(jnp.int32)
    ar = jnp.arange(N_EXPERTS, dtype=jnp.int32)
    used = counts > 0
    slot_e = (jnp.cumsum(used.astype(jnp.int32)) - 1) % 2
    later = jnp.logical_and(ar[None, :] > ar[:, None], used[None, :])
    next_e = jnp.min(jnp.where(later, ar[None, :], N_EXPERTS), axis=1)
    next_e = jnp.where(next_e == N_EXPERTS, -1, next_e)
    i32 = lambda v: v.astype(jnp.int32)
    return (i32(it_e), i32(it_b), i32(it_lo), i32(it_hi), it_first, it_new, i32(pick(slot_e)),
            i32(pick(next_e)), i32(total).reshape(1)), start


def _combine_kernel(dest_ref, gate_ref, x_ref, ys_ref, g_ref, b_ref, o_ref, buf, sem):
    t = x_ref.shape[0]

    def issue(i, carry):
        for k in range(TOP_K):
            pltpu.make_async_copy(ys_ref.at[pl.ds(dest_ref[0, 0, i * TOP_K + k], 1), :],
                                  buf.at[k, pl.ds(i, 1), :], sem).start(priority=k % 2)
        return carry

    lax.fori_loop(0, t, issue, 0, unroll=2)
    for k in range(TOP_K):
        pltpu.make_async_copy(ys_ref.at[pl.ds(0, t), :], buf.at[k], sem).wait()
    gate = gate_ref[...]
    ffn = gate[:, 0:1] * buf[0]
    for k in range(1, TOP_K):
        ffn = ffn + gate[:, k:k + 1] * buf[k]
    o_ref[...] = _layer_norm(DEEPNORM_ALPHA * x_ref[...] + ffn, g_ref[...], b_ref[...])


def _combine(dest3, gate, x2d, ys, ln_g, ln_b):
    n = x2d.shape[0]
    t = ROUTE_T
    const2 = lambda i: (0, 0)
    return pl.pallas_call(
        _combine_kernel,
        grid=(n // t,),
        in_specs=[pl.BlockSpec((1, 1, t * TOP_K), lambda i: (i, 0, 0), memory_space=pltpu.SMEM),
                  pl.BlockSpec((t, LANES), lambda i: (i, 0)),
                  pl.BlockSpec((t, D_MODEL), lambda i: (i, 0)),
                  pl.BlockSpec(memory_space=pl.ANY),
                  pl.BlockSpec((1, D_MODEL), const2),
                  pl.BlockSpec((1, D_MODEL), const2)],
        out_specs=pl.BlockSpec((t, D_MODEL), lambda i: (i, 0)),
        out_shape=jax.ShapeDtypeStruct((n, D_MODEL), F32),
        scratch_shapes=[pltpu.VMEM((TOP_K, t, D_MODEL), F32), pltpu.SemaphoreType.DMA(())],
        compiler_params=_cparams(("arbitrary",)),
        name="combine_ln2",
    )(dest3, gate, x2d, ys, ln_g, ln_b)


def _a_head_perm():
    grp = A_HEADS // A_KV_HEADS
    order = []
    for j in range(grp):
        order += [j, grp + j]
    return np.concatenate([np.arange(h * HEAD_DIM, (h + 1) * HEAD_DIM) for h in order])


def _moe(x1, layer, w_router, b_router, w_up, b_up, w_down, b_down, ln_g, ln_b):
    n = x1.shape[0]
    wr = jnp.pad(w_router, ((0, 0), (0, LANES - N_EXPERTS)))
    br = jnp.pad(b_router, (0, LANES - N_EXPERTS), constant_values=NEG_INF).reshape(1, LANES)
    eidx, gate, rank, cnt = _route(x1, wr, br)
    items, start = _expert_items(cnt[0, :N_EXPERTS], n * TOP_K)
    e4 = eidx[:, :TOP_K]
    sel = e4[:, :, None] == jnp.arange(N_EXPERTS, dtype=jnp.int32)
    dest = jnp.sum(jnp.where(sel, start, 0), axis=-1) + rank[:, :TOP_K]
    dest3 = dest.reshape(n // ROUTE_T, 1, ROUTE_T * TOP_K).astype(jnp.int32)
    xs = _dispatch(x1, dest3)
    lead = (DEPTH, N_EXPERTS, 1)
    ys = _experts(items, xs, w_up, b_up.reshape(lead + (-1,)), w_down, b_down.reshape(lead + (-1,)), layer)
    return _combine(dest3, gate, x1, ys, ln_g.reshape(1, -1), ln_b.reshape(1, -1))


def kernel(x, w_in, a_sink, lambda_q1, lambda_k1, lambda_q2, lambda_k2, diff_norm_g, na_rpb,
           w_branch, w_out, ln1_g, ln1_b, w_router, b_router, w_up, b_up, w_down, b_down,
           ln2_g, ln2_b):
    bsz, seq, d = x.shape
    n = bsz * seq
    rows = seq // GRID_W
    perm = _a_head_perm()
    slopes_b = jnp.asarray(_ALIBI[A_HEADS:], F32)
    xcur = x.reshape(n, d)
    for l in range(DEPTH):
        w = w_in[l]
        qscale = HEAD_DIM ** -0.5
        w_qkv = jnp.concatenate([
            w[:, :512][:, perm] * qscale, w[:, 512:768],
            w[:, 768:1280] * qscale, w[:, 1280:2304],
            w[:, 2304:2816] * qscale, w[:, 2816:QKV_W]], axis=1).astype(BF16)
        w_gate = w[:, QKV_W:].astype(BF16)
        w_br = jnp.stack([w_branch[l, 0][perm], w_branch[l, 1], w_branch[l, 2]]).astype(BF16)
        lam_init = 0.8 - 0.6 * math.exp(-0.3 * l)
        lamv = jnp.stack([lambda_q1[l], lambda_k1[l], lambda_q2[l], lambda_k2[l]]).astype(F32)

        proj = _inproj(xcur, w_qkv)
        oa = _attn_a(proj, a_sink[l].astype(F32), bsz, seq)
        ob = _attn_b(proj, slopes_b, lamv, diff_norm_g[l].reshape(1, -1).astype(F32), bsz, seq, lam_init)
        oc = _attn_c(proj, _na_bias_table(na_rpb[l], rows), bsz, seq)
        x1 = _merge(xcur, oa, ob, oc, w_gate, w_br, w_out[l].astype(BF16),
                    ln1_g[l].reshape(1, -1), ln1_b[l].reshape(1, -1))
        xcur = _moe(x1, l, w_router[l], b_router[l], w_up, b_up, w_down, b_down,
                    ln2_g[l], ln2_b[l])
    return xcur.reshape(bsz, seq, d)
```

```python
import functools
import math

import numpy as np
import jax
import jax.numpy as jnp
from jax import lax
from jax.experimental import pallas as pl
from jax.experimental.pallas import tpu as pltpu

F32 = jnp.float32
BF16 = jnp.bfloat16

D_MODEL = 1024
DEPTH = 2
HEAD_DIM = 64
A_HEADS = 8
A_KV_HEADS = 2
WINDOW = 128
B_HEADS = 4
C_HEADS = 8
GRID_W = 64
NA_ROWS = 8
NA_COLS = 16
MIX_W = 512
N_BRANCH = 3
N_EXPERTS = 32
TOP_K = 4
D_EXPERT = 1024
SWIGLU_LIMIT = 7.0
SWIGLU_ALPHA = 1.702
LN_EPS = 1e-5
NEG_INF = -1e30
DEEPNORM_ALPHA = (2 * DEPTH) ** 0.25

LANES = 128
QKV_W = 3840
GATE_W = N_BRANCH * D_MODEL
A_Q_BLK, A_K_BLK, A_V_BLK = 0, 4, 5
B_Q_BLK, B_K_BLK, B_V_BLK = 6, 10, 14
C_Q_BLK, C_K_BLK, C_V_BLK = 18, 22, 26

A_BAND = 3 * WINDOW
B_TQ = 256
NA_QROWS = 4
NA_SLAB = NA_QROWS + NA_ROWS
MOE_BM = 512
ROUTE_T = 256
VMEM_LIMIT = 56 * 1024 * 1024

_ALIBI = [float(2.0 ** (-8.0 * (i + 1) / (A_HEADS + B_HEADS))) for i in range(A_HEADS + B_HEADS)]


def _cparams(sem):
    return pltpu.CompilerParams(dimension_semantics=sem, vmem_limit_bytes=VMEM_LIMIT)


def _layer_norm(y, g, b):
    mu = jnp.mean(y, axis=-1, keepdims=True)
    yc = y - mu
    var = jnp.mean(yc * yc, axis=-1, keepdims=True)
    return yc * lax.rsqrt(var + LN_EPS) * g + b


def _inproj_kernel(x_ref, w_ref, o_ref, *, chunk):
    xb = x_ref[...].astype(BF16)
    for c in range(QKV_W // chunk):
        sl = slice(c * chunk, (c + 1) * chunk)
        o_ref[:, sl] = jnp.dot(xb, w_ref[:, sl], preferred_element_type=F32).astype(BF16)


def _inproj(x2d, w_qkv):
    n = x2d.shape[0]
    tm = 512
    return pl.pallas_call(
        functools.partial(_inproj_kernel, chunk=768),
        grid=(n // tm,),
        in_specs=[pl.BlockSpec((tm, D_MODEL), lambda i: (i, 0)),
                  pl.BlockSpec((D_MODEL, QKV_W), lambda i: (0, 0))],
        out_specs=pl.BlockSpec((tm, QKV_W), lambda i: (i, 0)),
        out_shape=jax.ShapeDtypeStruct((n, QKV_W), BF16),
        compiler_params=_cparams(("arbitrary",)),
        name="inproj",
    )(x2d, w_qkv)


def _attn_a_kernel(sink_ref, q_ref, k_ref, v_ref, o_ref, bias_ref, sink_tab, *, seq):
    nblk = seq // WINDOW
    heads = [j + 4 * hf for j in range(4) for hf in range(2)]
    lane = lax.broadcasted_iota(jnp.int32, (WINDOW, LANES), 1)
    low = lane < HEAD_DIM
    dn = (((1,), (1,)), ((), ()))

    @pl.when(pl.program_id(0) == 0)
    def _():
        kq = (lax.broadcasted_iota(jnp.int32, (WINDOW, A_BAND), 1)
              - lax.broadcasted_iota(jnp.int32, (WINDOW, A_BAND), 0))
        for c in range(3):
            dist = jnp.abs(kq - c * WINDOW).astype(F32)
            for i, h in enumerate(heads):
                bias_ref[c, i * WINDOW:(i + 1) * WINDOW, :] = jnp.where(
                    dist <= float(WINDOW), _ALIBI[h] * dist, -NEG_INF)
        for i, h in enumerate(heads):
            sink_tab[i * WINDOW:(i + 1) * WINDOW, :] = jnp.full((WINDOW, LANES), sink_ref[h], F32)

    sink_col = jnp.max(sink_tab[...], axis=-1, keepdims=True)

    def body(n, carry):
        q0 = pl.multiple_of(n * WINDOW, WINDOW)
        start = pl.multiple_of(jnp.clip((n - 1) * WINDOW, 0, seq - A_BAND), WINDOW)
        cfg = jnp.where(n == 0, 0, jnp.where(n == nblk - 1, 2, 1))
        kb = k_ref[pl.ds(start, A_BAND), :]
        vb = v_ref[pl.ds(start, A_BAND), :]
        pieces = []
        for j in range(4):
            qt = q_ref[pl.ds(q0, WINDOW), j * LANES:(j + 1) * LANES]
            zero = jnp.zeros_like(qt)
            lhs = jnp.concatenate([jnp.where(low, qt, zero), jnp.where(low, zero, qt)], axis=0)
            pieces.append(lax.dot_general(lhs, kb, dn, preferred_element_type=F32))
        t = jnp.concatenate(pieces, axis=0) - bias_ref[cfg]
        m = jnp.maximum(jnp.max(t, axis=-1, keepdims=True), sink_col)
        e = jnp.exp(t - m)
        den = jnp.sum(e, axis=-1, keepdims=True) + jnp.exp(sink_col - m)
        p = (e * (1.0 / den)).astype(BF16)
        pv = jnp.dot(p, vb, preferred_element_type=F32)
        for j in range(4):
            o = jnp.where(low, pv[2 * j * WINDOW:(2 * j + 1) * WINDOW],
                          pv[(2 * j + 1) * WINDOW:(2 * j + 2) * WINDOW])
            o_ref[pl.ds(q0, WINDOW), j * LANES:(j + 1) * LANES] = o.astype(BF16)
        return carry

    lax.fori_loop(0, nblk, body, 0, unroll=4)


def _attn_a(proj, sink, bsz, seq):
    return pl.pallas_call(
        functools.partial(_attn_a_kernel, seq=seq),
        grid=(bsz,),
        in_specs=[pl.BlockSpec(memory_space=pltpu.SMEM),
                  pl.BlockSpec((seq, 4 * LANES), lambda b: (b, A_Q_BLK // 4)),
                  pl.BlockSpec((seq, LANES), lambda b: (b, A_K_BLK)),
                  pl.BlockSpec((seq, LANES), lambda b: (b, A_V_BLK))],
        out_specs=pl.BlockSpec((seq, MIX_W), lambda b: (b, 0)),
        out_shape=jax.ShapeDtypeStruct((bsz * seq, MIX_W), BF16),
        scratch_shapes=[pltpu.VMEM((3, A_HEADS * WINDOW, A_BAND), F32),
                        pltpu.VMEM((A_HEADS * WINDOW, LANES), F32)],
        compiler_params=_cparams(("arbitrary",)),
        name="attn_a",
    )(sink, proj, proj, proj)


def _attn_b_kernel(slope_ref, lamv_ref, gain_ref, q_ref, k_ref, v_ref, o_ref, bias_ref, *,
                   seq, lam_init):
    nblk = seq // B_TQ
    slope = slope_ref[pl.program_id(0)]

    @pl.when(pl.program_id(1) == 0)
    def _():
        ji = (lax.broadcasted_iota(jnp.int32, (B_TQ, B_TQ), 1)
              - lax.broadcasted_iota(jnp.int32, (B_TQ, B_TQ), 0))
        for dd in range(2 * nblk - 1):
            bias_ref[dd] = slope * jnp.abs(ji + (dd - (nblk - 1)) * B_TQ).astype(F32)

    lv = lamv_ref[...]
    lam = (jnp.exp(jnp.sum(lv[0:1] * lv[1:2], axis=-1, keepdims=True))
           - jnp.exp(jnp.sum(lv[2:3] * lv[3:4], axis=-1, keepdims=True)) + lam_init)
    scale = gain_ref[...] * (1.0 - lam_init)
    lane = lax.broadcasted_iota(jnp.int32, (B_TQ, LANES), 1)
    low = lane < HEAD_DIM
    kall = k_ref[...]
    vall = v_ref[...]
    dn = (((1,), (1,)), ((), ()))

    def body(n, carry):
        q0 = pl.multiple_of(n * B_TQ, B_TQ)
        qt = q_ref[pl.ds(q0, B_TQ), :]
        zero = jnp.zeros_like(qt)
        bias = jnp.concatenate([bias_ref[kc - n + (nblk - 1)] for kc in range(nblk)], axis=1)
        es, ls = [], []
        for mp in range(2):
            qm = jnp.where(low, qt, zero) if mp == 0 else jnp.where(low, zero, qt)
            s = lax.dot_general(qm, kall, dn, preferred_element_type=F32) - bias
            m = jnp.max(s, axis=-1, keepdims=True)
            e = jnp.exp(s - m)
            ls.append(jnp.sum(e, axis=-1, keepdims=True))
            es.append(e.astype(BF16))
        a = es[0] * (1.0 / ls[0]).astype(BF16) - es[1] * (lam / ls[1]).astype(BF16)
        o = jnp.dot(a, vall, preferred_element_type=F32)
        o = o * lax.rsqrt(jnp.mean(o * o, axis=-1, keepdims=True) + LN_EPS)
        o_ref[pl.ds(q0, B_TQ), :] = (o * scale).astype(BF16)
        return carry

    lax.fori_loop(0, nblk, body, 0, unroll=2)


def _attn_b(proj, slopes_b, lamv, gain, bsz, seq, lam_init):
    nblk = seq // B_TQ
    return pl.pallas_call(
        functools.partial(_attn_b_kernel, seq=seq, lam_init=lam_init),
        grid=(B_HEADS, bsz),
        in_specs=[pl.BlockSpec(memory_space=pltpu.SMEM),
                  pl.BlockSpec((4, HEAD_DIM), lambda h, b: (0, 0)),
                  pl.BlockSpec((1, 2 * HEAD_DIM), lambda h, b: (0, 0)),
                  pl.BlockSpec((seq, LANES), lambda h, b: (b, B_Q_BLK + h)),
                  pl.BlockSpec((seq, LANES), lambda h, b: (b, B_K_BLK + h)),
                  pl.BlockSpec((seq, LANES), lambda h, b: (b, B_V_BLK + h))],
        out_specs=pl.BlockSpec((seq, LANES), lambda h, b: (b, h)),
        out_shape=jax.ShapeDtypeStruct((bsz * seq, MIX_W), BF16),
        scratch_shapes=[pltpu.VMEM((2 * nblk - 1, B_TQ, B_TQ), F32)],
        compiler_params=_cparams(("arbitrary", "arbitrary")),
        name="attn_b",
    )(slopes_b, lamv, gain, proj, proj, proj)


def _attn_c_kernel(bias_ref, q_ref, k_ref, v_ref, o_ref, *, rows):
    qtok = NA_QROWS * GRID_W
    ktok = NA_SLAB * GRID_W
    ngrp = rows // NA_QROWS
    lane = lax.broadcasted_iota(jnp.int32, (qtok, LANES), 1)
    low = lane < HEAD_DIM
    dn = (((1,), (1,)), ((), ()))

    def body(g, carry):
        slab0 = jnp.clip(g * NA_QROWS - NA_ROWS // 2, 0, rows - NA_SLAB)
        k0 = pl.multiple_of(slab0 * GRID_W, GRID_W)
        q0 = pl.multiple_of(g * qtok, qtok)
        cfg = jnp.where(g == 0, 0, jnp.where(g == ngrp - 1, 2, 1))
        qr = q_ref[pl.ds(q0, qtok), :]
        zero = jnp.zeros_like(qr)
        lhs = jnp.concatenate([jnp.where(low, qr, zero), jnp.where(low, zero, qr)], axis=0)
        ks = k_ref[pl.ds(k0, ktok), :]
        vs = v_ref[pl.ds(k0, ktok), :]
        s = lax.dot_general(lhs, ks, dn, preferred_element_type=F32) + bias_ref[0, cfg]
        m = jnp.max(s, axis=-1, keepdims=True)
        e = jnp.exp(s - m)
        p = (e * (1.0 / jnp.sum(e, axis=-1, keepdims=True))).astype(BF16)
        pv = jnp.dot(p, vs, preferred_element_type=F32)
        o = jnp.where(low, pv[:qtok], pv[qtok:])
        o_ref[pl.ds(q0, qtok), :] = o.astype(BF16)
        return carry

    lax.fori_loop(0, ngrp, body, 0, unroll=True)


def _attn_c(proj, bias_tab, bsz, seq):
    rows = seq // GRID_W
    npair = C_HEADS // 2
    return pl.pallas_call(
        functools.partial(_attn_c_kernel, rows=rows),
        grid=(npair, bsz),
        in_specs=[pl.BlockSpec((1,) + bias_tab.shape[1:], lambda p, b: (p, 0, 0, 0)),
                  pl.BlockSpec((seq, LANES), lambda p, b: (b, C_Q_BLK + p)),
                  pl.BlockSpec((seq, LANES), lambda p, b: (b, C_K_BLK + p)),
                  pl.BlockSpec((seq, LANES), lambda p, b: (b, C_V_BLK + p))],
        out_specs=pl.BlockSpec((seq, LANES), lambda p, b: (b, p)),
        out_shape=jax.ShapeDtypeStruct((bsz * seq, MIX_W), BF16),
        compiler_params=_cparams(("arbitrary", "arbitrary")),
        name="attn_c",
    )(bias_tab, proj, proj, proj)


def _na_bias_table(rpb, rows):
    assert rows % NA_QROWS == 0 and rows >= NA_SLAB + NA_QROWS and rows >= NA_ROWS
    qc = np.arange(GRID_W)[:, None]
    kc = np.arange(GRID_W)[None, :]
    cs = np.clip(qc - NA_COLS // 2, 0, GRID_W - NA_COLS)
    valid = (kc >= cs) & (kc < cs + NA_COLS)
    off = GRID_W - NA_COLS
    rp = jnp.pad(rpb.astype(F32), ((0, 0), (0, 0), (off, off)))
    toep = jnp.stack([rp[:, :, off + NA_COLS - 1 - c:off + NA_COLS - 1 - c + GRID_W]
                      for c in range(GRID_W)], axis=2)
    toep = jnp.where(valid[None, None], toep, NEG_INF)
    masked = jnp.full((C_HEADS, GRID_W, GRID_W), NEG_INF, F32)
    ngrp = rows // NA_QROWS
    cfgs = []
    for g in (0, 1, ngrp - 1):
        slab0 = int(np.clip(g * NA_QROWS - NA_ROWS // 2, 0, rows - NA_SLAB))
        qrows = []
        for a in range(NA_QROWS):
            r = g * NA_QROWS + a
            rs = int(np.clip(r - NA_ROWS // 2, 0, rows - NA_ROWS))
            blocks = []
            for u in range(NA_SLAB):
                krow = slab0 + u
                inside = rs <= krow < rs + NA_ROWS
                blocks.append(toep[:, krow - r + NA_ROWS - 1] if inside else masked)
            qrows.append(jnp.concatenate(blocks, axis=-1))
        cfgs.append(jnp.concatenate(qrows, axis=1))
    tab = jnp.stack(cfgs, axis=1)
    tab = tab.reshape(C_HEADS // 2, 2, 3, NA_QROWS * GRID_W, NA_SLAB * GRID_W)
    return tab.transpose(0, 2, 1, 3, 4).reshape(C_HEADS // 2, 3, 2 * NA_QROWS * GRID_W,
                                                NA_SLAB * GRID_W)


def _merge_kernel(x_ref, oa_ref, ob_ref, oc_ref, wg_ref, wbr_ref, wout_ref, g_ref, b_ref, o_ref):
    x = x_ref[...]
    xb = x.astype(BF16)
    merged = None
    for i, br_ref in enumerate((oa_ref, ob_ref, oc_ref)):
        gate = jax.nn.sigmoid(jnp.dot(xb, wg_ref[:, i * D_MODEL:(i + 1) * D_MODEL],
                                      preferred_element_type=F32))
        br = jnp.dot(br_ref[...], wbr_ref[i], preferred_element_type=F32)
        merged = gate * br if merged is None else merged + gate * br
    mix = jnp.dot(merged.astype(BF16), wout_ref[...], preferred_element_type=F32)
    o_ref[...] = _layer_norm(DEEPNORM_ALPHA * x + mix, g_ref[...], b_ref[...])


def _merge(x2d, oa, ob, oc, w_gate, w_br, w_out, ln_g, ln_b):
    n = x2d.shape[0]
    tm = 512
    const2 = lambda i: (0, 0)
    return pl.pallas_call(
        _merge_kernel,
        grid=(n // tm,),
        in_specs=[pl.BlockSpec((tm, D_MODEL), lambda i: (i, 0)),
                  pl.BlockSpec((tm, MIX_W), lambda i: (i, 0)),
                  pl.BlockSpec((tm, MIX_W), lambda i: (i, 0)),
                  pl.BlockSpec((tm, MIX_W), lambda i: (i, 0)),
                  pl.BlockSpec((D_MODEL, GATE_W), const2),
                  pl.BlockSpec((N_BRANCH, MIX_W, D_MODEL), lambda i: (0, 0, 0)),
                  pl.BlockSpec((D_MODEL, D_MODEL), const2),
                  pl.BlockSpec((1, D_MODEL), const2),
                  pl.BlockSpec((1, D_MODEL), const2)],
        out_specs=pl.BlockSpec((tm, D_MODEL), lambda i: (i, 0)),
        out_shape=jax.ShapeDtypeStruct((n, D_MODEL), F32),
        compiler_params=_cparams(("arbitrary",)),
        name="merge_ln1",
    )(x2d, oa, ob, oc, w_gate, w_br, w_out, ln_g, ln_b)


def _route_kernel(x_ref, w_ref, b_ref, eidx_ref, gate_ref, rank_ref, cnt_ref, carry_ref):
    t = x_ref.shape[0]

    @pl.when(pl.program_id(0) == 0)
    def _():
        carry_ref[...] = jnp.zeros_like(carry_ref)

    logits = jnp.dot(x_ref[...], w_ref[...], preferred_element_type=F32,
                     precision=lax.Precision.HIGHEST) + b_ref[...]
    lane = lax.broadcasted_iota(jnp.int32, (t, LANES), 1).astype(F32)
    work = logits
    sels, vals, idxs = [], [], []
    for _ in range(TOP_K):
        m = jnp.max(work, axis=-1, keepdims=True)
        idx = jnp.min(jnp.where(work == m, lane, float(LANES)), axis=-1, keepdims=True)
        sel = lane == idx
        work = jnp.where(sel, -jnp.inf, work)
        sels.append(sel)
        vals.append(m)
        idxs.append(idx)
    ex = [jnp.exp(v - vals[0]) for v in vals]
    den = ex[0] + ex[1] + ex[2] + ex[3]
    onehot = jnp.zeros((t, LANES), F32)
    for sel in sels:
        onehot = jnp.where(sel, 1.0, onehot)
    ri = lax.broadcasted_iota(jnp.int32, (t, t), 0)
    ci = lax.broadcasted_iota(jnp.int32, (t, t), 1)
    tri = jnp.where(ci < ri, 1.0, 0.0).astype(BF16)
    before = jnp.dot(tri, onehot.astype(BF16), preferred_element_type=F32) + carry_ref[...]
    eidx = jnp.zeros((t, LANES), F32)
    gate = jnp.zeros((t, LANES), F32)
    rank = jnp.zeros((t, LANES), F32)
    for k in range(TOP_K):
        rk = jnp.sum(jnp.where(sels[k], before, 0.0), axis=-1, keepdims=True)
        eidx = jnp.where(lane == float(k), idxs[k], eidx)
        gate = jnp.where(lane == float(k), ex[k] / den, gate)
        rank = jnp.where(lane == float(k), rk, rank)
    eidx_ref[...] = eidx.astype(jnp.int32)
    gate_ref[...] = gate
    rank_ref[...] = rank.astype(jnp.int32)
    carry_ref[...] += jnp.sum(onehot, axis=0, keepdims=True)
    cnt_ref[...] = carry_ref[...].astype(jnp.int32)


def _route(x2d, w_router_pad, b_router_pad):
    n = x2d.shape[0]
    t = ROUTE_T
    tile = pl.BlockSpec((t, LANES), lambda i: (i, 0))
    return pl.pallas_call(
        _route_kernel,
        grid=(n // t,),
        in_specs=[pl.BlockSpec((t, D_MODEL), lambda i: (i, 0)),
                  pl.BlockSpec((D_MODEL, LANES), lambda i: (0, 0)),
                  pl.BlockSpec((1, LANES), lambda i: (0, 0))],
        out_specs=[tile, tile, tile, pl.BlockSpec((1, LANES), lambda i: (0, 0))],
        out_shape=[jax.ShapeDtypeStruct((n, LANES), jnp.int32),
                   jax.ShapeDtypeStruct((n, LANES), F32),
                   jax.ShapeDtypeStruct((n, LANES), jnp.int32),
                   jax.ShapeDtypeStruct((1, LANES), jnp.int32)],
        scratch_shapes=[pltpu.VMEM((1, LANES), F32)],
        compiler_params=_cparams(("arbitrary",)),
        name="route",
    )(x2d, w_router_pad, b_router_pad)


ROW_TILE = D_MODEL // LANES


def _to_row_tiled(dst_ref, val):
    rows = val.shape[0]
    for c in range(ROW_TILE):
        dst_ref[pl.ds(c, rows, stride=ROW_TILE), :] = val[:, c * LANES:(c + 1) * LANES]


def _from_row_tiled(src_ref, rows):
    return jnp.concatenate([src_ref[pl.ds(c, rows, stride=ROW_TILE), :] for c in range(ROW_TILE)],
                           axis=1)


def _tile_copy(src_ref, src_row, dst_ref, dst_row, sem):
    return pltpu.make_async_copy(src_ref.at[pl.ds(pl.multiple_of(src_row * ROW_TILE, ROW_TILE), ROW_TILE), :],
                                 dst_ref.at[pl.ds(pl.multiple_of(dst_row * ROW_TILE, ROW_TILE), ROW_TILE), :],
                                 sem)


def _dispatch_kernel(dest_ref, x_ref, xs_ref, xt_ref, sem):
    t = x_ref.shape[0]
    _to_row_tiled(xt_ref, x_ref[...])

    def issue(i, carry):
        for k in range(TOP_K):
            _tile_copy(xt_ref, i, xs_ref, dest_ref[0, 0, i * TOP_K + k], sem).start(priority=k % 2)
        return carry

    lax.fori_loop(0, t, issue, 0, unroll=2)
    for _ in range(TOP_K):
        pltpu.make_async_copy(xt_ref, xs_ref.at[pl.ds(0, t * ROW_TILE), :], sem).wait()


def _dispatch(x2d, dest3):
    n = x2d.shape[0]
    t = ROUTE_T
    return pl.pallas_call(
        _dispatch_kernel,
        grid=(n // t,),
        in_specs=[pl.BlockSpec((1, 1, t * TOP_K), lambda i: (i, 0, 0), memory_space=pltpu.SMEM),
                  pl.BlockSpec((t, D_MODEL), lambda i: (i, 0))],
        out_specs=pl.BlockSpec(memory_space=pl.ANY),
        out_shape=jax.ShapeDtypeStruct((n * TOP_K * ROW_TILE, LANES), F32),
        scratch_shapes=[pltpu.VMEM((t * ROW_TILE, LANES), F32), pltpu.SemaphoreType.DMA(())],
        compiler_params=_cparams(("arbitrary",)),
        name="dispatch",
    )(dest3, x2d)


def _expert_weight_copies(wu_hbm, wd_hbm, wu_f32, wd_f32, sems, layer, expert, slot):
    return (pltpu.make_async_copy(wu_hbm.at[layer, expert], wu_f32.at[slot], sems.at[slot, 0]),
            pltpu.make_async_copy(wd_hbm.at[layer, expert], wd_f32.at[slot], sems.at[slot, 1]))


def _expert_kernel(it_e, it_b, it_lo, it_hi, it_first, it_new, it_slot, it_next, n_items,
                   xs_ref, wu_hbm, bu_ref, wd_hbm, bd_ref, ys_ref,
                   wu_f32, wd_f32, wu_bf, wd_bf, sems, *, layer):
    del it_b
    j = pl.program_id(0)
    copies = functools.partial(_expert_weight_copies, wu_hbm, wd_hbm, wu_f32, wd_f32, sems, layer)

    @pl.when(j == 0)
    def _():
        for c in copies(it_e[0], 0):
            c.start()

    @pl.when(jnp.logical_and(j < n_items[0], it_new[j] == 1))
    def _():
        slot = it_slot[j]
        for c in copies(it_e[j], slot):
            c.wait()
        wu_bf[...] = wu_f32[slot].astype(BF16)
        wd_bf[...] = wd_f32[slot].astype(BF16)

        @pl.when(it_next[j] >= 0)
        def _():
            for c in copies(it_next[j], 1 - slot):
                c.start()

    @pl.when(j < n_items[0])
    def _():
        xb = _from_row_tiled(xs_ref, MOE_BM).astype(BF16)
        bu = bu_ref[0, 0]
        hg = jnp.dot(xb, wu_bf[:, :D_EXPERT], preferred_element_type=F32) + bu[:, :D_EXPERT]
        hl = jnp.dot(xb, wu_bf[:, D_EXPERT:], preferred_element_type=F32) + bu[:, D_EXPERT:]
        hg = jnp.minimum(hg, SWIGLU_LIMIT)
        hl = jnp.clip(hl, -SWIGLU_LIMIT, SWIGLU_LIMIT)
        act = hg * jax.nn.sigmoid(SWIGLU_ALPHA * hg) * (hl + 1.0)
        y = jnp.dot(act.astype(BF16), wd_bf[...], preferred_element_type=F32) + bd_ref[0, 0]
        row = lax.broadcasted_iota(jnp.int32, (MOE_BM, 1), 0)
        mine = jnp.logical_and(row >= it_lo[j], row < it_hi[j])

        @pl.when(it_first[j] == 1)
        def _():
            _to_row_tiled(ys_ref, jnp.where(mine, y, 0.0))

        @pl.when(it_first[j] == 0)
        def _():
            _to_row_tiled(ys_ref, jnp.where(mine, y, _from_row_tiled(ys_ref, MOE_BM)))


def _experts(items, xs, w_up, b_up4, w_down, b_down4, layer):
    n_items = items[0].shape[0]
    row_blk = pl.BlockSpec((MOE_BM * ROW_TILE, LANES), lambda j, ie, ib, *rest: (
        ib[jnp.minimum(j, rest[-1][0] - 1)], 0))

    def cur(j, ni):
        return jnp.minimum(j, ni[0] - 1)

    def b_map(j, ie, ib, *rest):
        return (layer, ie[cur(j, rest[-1])], 0, 0)

    return pl.pallas_call(
        functools.partial(_expert_kernel, layer=layer),
        grid_spec=pltpu.PrefetchScalarGridSpec(
            num_scalar_prefetch=len(items),
            grid=(n_items,),
            in_specs=[row_blk,
                      pl.BlockSpec(memory_space=pl.ANY),
                      pl.BlockSpec((1, 1, 1, 2 * D_EXPERT), b_map),
                      pl.BlockSpec(memory_space=pl.ANY),
                      pl.BlockSpec((1, 1, 1, D_MODEL), b_map)],
            out_specs=row_blk,
            scratch_shapes=[pltpu.VMEM((2, D_MODEL, 2 * D_EXPERT), F32),
                            pltpu.VMEM((2, D_EXPERT, D_MODEL), F32),
                            pltpu.VMEM((D_MODEL, 2 * D_EXPERT), BF16),
                            pltpu.VMEM((D_EXPERT, D_MODEL), BF16),
                            pltpu.SemaphoreType.DMA((2, 2))]),
        out_shape=jax.ShapeDtypeStruct(xs.shape, F32),
        compiler_params=_cparams(("arbitrary",)),
        name="experts",
    )(*items, xs, w_up, b_up4, w_down, b_down4)


def _expert_items(counts, n_rows):
    n_items = n_rows // MOE_BM + N_EXPERTS
    end = jnp.cumsum(counts)
    start = end - counts
    first_b = start // MOE_BM
    nb = jnp.where(counts > 0, (end - 1) // MOE_BM - first_b + 1, 0)
    item_end = jnp.cumsum(nb)
    item_start = item_end - nb
    total = item_end[-1]
    jc = jnp.minimum(jnp.arange(n_items, dtype=jnp.int32), total - 1)
    it_e = jnp.sum((item_end[None, :] <= jc[:, None]).astype(jnp.int32), axis=1)
    sel = it_e[:, None] == jnp.arange(N_EXPERTS, dtype=jnp.int32)[None, :]
    pick = lambda v: jnp.sum(jnp.where(sel, v[None, :], 0), axis=1)
    it_b = pick(first_b) + jc - pick(item_start)
    it_lo = jnp.maximum(pick(start), it_b * MOE_BM) - it_b * MOE_BM
    it_hi = jnp.minimum(pick(end), (it_b + 1) * MOE_BM) - it_b * MOE_BM
    prev_b = jnp.concatenate([jnp.full((1,), -1, jnp.int32), it_b[:-1]])
    it_first = (it_b != prev_b).astype(jnp.int32)
    prev_e = jnp.concatenate([jnp.full((1,), -1, jnp.int32), it_e[:-1]])
    it_new = (it_e != prev_e).astype(jnp.int32)
    ar = jnp.arange(N_EXPERTS, dtype=jnp.int32)
    used = counts > 0
    slot_e = (jnp.cumsum(used.astype(jnp.int32)) - 1) % 2
    later = jnp.logical_and(ar[None, :] > ar[:, None], used[None, :])
    next_e = jnp.min(jnp.where(later, ar[None, :], N_EXPERTS), axis=1)
    next_e = jnp.where(next_e == N_EXPERTS, -1, next_e)
    i32 = lambda v: v.astype(jnp.int32)
    return (i32(it_e), i32(it_b), i32(it_lo), i32(it_hi), it_first, it_new, i32(pick(slot_e)),
            i32(pick(next_e)), i32(total).reshape(1)), start


def _combine_kernel(dest_ref, gate_ref, x_ref, ys_ref, g_ref, b_ref, o_ref, buf, sem):
    t = x_ref.shape[0]

    def issue(i, carry):
        for k in range(TOP_K):
            _tile_copy(ys_ref, dest_ref[0, 0, i * TOP_K + k], buf.at[k], i, sem).start(priority=k % 2)
        return carry

    lax.fori_loop(0, t, issue, 0, unroll=2)
    for k in range(TOP_K):
        pltpu.make_async_copy(ys_ref.at[pl.ds(0, t * ROW_TILE), :], buf.at[k], sem).wait()
    gate = gate_ref[...]
    ffn = gate[:, 0:1] * _from_row_tiled(buf.at[0], t)
    for k in range(1, TOP_K):
        ffn = ffn + gate[:, k:k + 1] * _from_row_tiled(buf.at[k], t)
    o_ref[...] = _layer_norm(DEEPNORM_ALPHA * x_ref[...] + ffn, g_ref[...], b_ref[...])


def _combine(dest3, gate, x2d, ys, ln_g, ln_b):
    n = x2d.shape[0]
    t = ROUTE_T
    const2 = lambda i: (0, 0)
    return pl.pallas_call(
        _combine_kernel,
        grid=(n // t,),
        in_specs=[pl.BlockSpec((1, 1, t * TOP_K), lambda i: (i, 0, 0), memory_space=pltpu.SMEM),
                  pl.BlockSpec((t, LANES), lambda i: (i, 0)),
                  pl.BlockSpec((t, D_MODEL), lambda i: (i, 0)),
                  pl.BlockSpec(memory_space=pl.ANY),
                  pl.BlockSpec((1, D_MODEL), const2),
                  pl.BlockSpec((1, D_MODEL), const2)],
        out_specs=pl.BlockSpec((t, D_MODEL), lambda i: (i, 0)),
        out_shape=jax.ShapeDtypeStruct((n, D_MODEL), F32),
        scratch_shapes=[pltpu.VMEM((TOP_K, t * ROW_TILE, LANES), F32), pltpu.SemaphoreType.DMA(())],
        compiler_params=_cparams(("arbitrary",)),
        name="combine_ln2",
    )(dest3, gate, x2d, ys, ln_g, ln_b)


def _a_head_perm():
    grp = A_HEADS // A_KV_HEADS
    order = []
    for j in range(grp):
        order += [j, grp + j]
    return np.concatenate([np.arange(h * HEAD_DIM, (h + 1) * HEAD_DIM) for h in order])


def _moe(x1, layer, w_router, b_router, w_up, b_up, w_down, b_down, ln_g, ln_b):
    n = x1.shape[0]
    wr = jnp.pad(w_router, ((0, 0), (0, LANES - N_EXPERTS)))
    br = jnp.pad(b_router, (0, LANES - N_EXPERTS), constant_values=NEG_INF).reshape(1, LANES)
    eidx, gate, rank, cnt = _route(x1, wr, br)
    items, start = _expert_items(cnt[0, :N_EXPERTS], n * TOP_K)
    e4 = eidx[:, :TOP_K]
    sel = e4[:, :, None] == jnp.arange(N_EXPERTS, dtype=jnp.int32)
    dest = jnp.sum(jnp.where(sel, start, 0), axis=-1) + rank[:, :TOP_K]
    dest3 = dest.reshape(n // ROUTE_T, 1, ROUTE_T * TOP_K).astype(jnp.int32)
    xs = _dispatch(x1, dest3)
    lead = (DEPTH, N_EXPERTS, 1)
    ys = _experts(items, xs, w_up, b_up.reshape(lead + (-1,)), w_down, b_down.reshape(lead + (-1,)), layer)
    return _combine(dest3, gate, x1, ys, ln_g.reshape(1, -1), ln_b.reshape(1, -1))


def kernel(x, w_in, a_sink, lambda_q1, lambda_k1, lambda_q2, lambda_k2, diff_norm_g, na_rpb,
           w_branch, w_out, ln1_g, ln1_b, w_router, b_router, w_up, b_up, w_down, b_down,
           ln2_g, ln2_b):
    bsz, seq, d = x.shape
    n = bsz * seq
    rows = seq // GRID_W
    perm = _a_head_perm()
    slopes_b = jnp.asarray(_ALIBI[A_HEADS:], F32)
    xcur = x.reshape(n, d)
    for l in range(DEPTH):
        w = w_in[l]
        qscale = HEAD_DIM ** -0.5
        w_qkv = jnp.concatenate([
            w[:, :512][:, perm] * qscale, w[:, 512:768],
            w[:, 768:1280] * qscale, w[:, 1280:2304],
            w[:, 2304:2816] * qscale, w[:, 2816:QKV_W]], axis=1).astype(BF16)
        w_gate = w[:, QKV_W:].astype(BF16)
        w_br = jnp.stack([w_branch[l, 0][perm], w_branch[l, 1], w_branch[l, 2]]).astype(BF16)
        lam_init = 0.8 - 0.6 * math.exp(-0.3 * l)
        lamv = jnp.stack([lambda_q1[l], lambda_k1[l], lambda_q2[l], lambda_k2[l]]).astype(F32)

        proj = _inproj(xcur, w_qkv)
        oa = _attn_a(proj, a_sink[l].astype(F32), bsz, seq)
        ob = _attn_b(proj, slopes_b, lamv, diff_norm_g[l].reshape(1, -1).astype(F32), bsz, seq, lam_init)
        oc = _attn_c(proj, _na_bias_table(na_rpb[l], rows), bsz, seq)
        x1 = _merge(xcur, oa, ob, oc, w_gate, w_br, w_out[l].astype(BF16),
                    ln1_g[l].reshape(1, -1), ln1_b[l].reshape(1, -1))
        xcur = _moe(x1, l, w_router[l], b_router[l], w_up, b_up, w_down, b_down,
                    ln2_g[l], ln2_b[l])
    return xcur.reshape(bsz, seq, d)
```

```python
import functools
import math

import numpy as np
import jax
import jax.numpy as jnp
from jax import lax
from jax.experimental import pallas as pl
from jax.experimental.pallas import tpu as pltpu

F32 = jnp.float32
BF16 = jnp.bfloat16

D_MODEL = 1024
DEPTH = 2
HEAD_DIM = 64
A_HEADS = 8
A_KV_HEADS = 2
WINDOW = 128
B_HEADS = 4
C_HEADS = 8
GRID_W = 64
NA_ROWS = 8
NA_COLS = 16
MIX_W = 512
N_BRANCH = 3
N_EXPERTS = 32
TOP_K = 4
D_EXPERT = 1024
SWIGLU_LIMIT = 7.0
SWIGLU_ALPHA = 1.702
LN_EPS = 1e-5
NEG_INF = -1e30
DEEPNORM_ALPHA = (2 * DEPTH) ** 0.25

LANES = 128
QKV_W = 3840
GATE_W = N_BRANCH * D_MODEL
A_Q_BLK, A_K_BLK, A_V_BLK = 0, 4, 5
B_Q_BLK, B_K_BLK, B_V_BLK = 6, 10, 14
C_Q_BLK, C_K_BLK, C_V_BLK = 18, 22, 26

A_BAND = 3 * WINDOW
B_TQ = 256
NA_QROWS = 4
NA_SLAB = NA_QROWS + NA_ROWS
MOE_BM = 512
EXPERT_CHUNK = 256
ROUTE_T = 256
VMEM_LIMIT = 56 * 1024 * 1024

_ALIBI = [float(2.0 ** (-8.0 * (i + 1) / (A_HEADS + B_HEADS))) for i in range(A_HEADS + B_HEADS)]


def _cparams(sem):
    return pltpu.CompilerParams(dimension_semantics=sem, vmem_limit_bytes=VMEM_LIMIT)


def _layer_norm(y, g, b):
    mu = jnp.mean(y, axis=-1, keepdims=True)
    yc = y - mu
    var = jnp.mean(yc * yc, axis=-1, keepdims=True)
    return yc * lax.rsqrt(var + LN_EPS) * g + b


def _inproj_kernel(x_ref, w_ref, o_ref, *, chunk):
    xb = x_ref[...].astype(BF16)
    for c in range(QKV_W // chunk):
        sl = slice(c * chunk, (c + 1) * chunk)
        o_ref[:, sl] = jnp.dot(xb, w_ref[:, sl], preferred_element_type=F32).astype(BF16)


def _inproj(x2d, w_qkv):
    n = x2d.shape[0]
    tm = 512
    return pl.pallas_call(
        functools.partial(_inproj_kernel, chunk=768),
        grid=(n // tm,),
        in_specs=[pl.BlockSpec((tm, D_MODEL), lambda i: (i, 0)),
                  pl.BlockSpec((D_MODEL, QKV_W), lambda i: (0, 0))],
        out_specs=pl.BlockSpec((tm, QKV_W), lambda i: (i, 0)),
        out_shape=jax.ShapeDtypeStruct((n, QKV_W), BF16),
        compiler_params=_cparams(("arbitrary",)),
        name="inproj",
    )(x2d, w_qkv)


def _attn_a_kernel(sink_ref, q_ref, k_ref, v_ref, o_ref, bias_ref, sink_tab, *, seq):
    nblk = seq // WINDOW
    heads = [j + 4 * hf for j in range(4) for hf in range(2)]
    lane = lax.broadcasted_iota(jnp.int32, (WINDOW, LANES), 1)
    low = lane < HEAD_DIM
    dn = (((1,), (1,)), ((), ()))

    @pl.when(pl.program_id(0) == 0)
    def _():
        kq = (lax.broadcasted_iota(jnp.int32, (WINDOW, A_BAND), 1)
              - lax.broadcasted_iota(jnp.int32, (WINDOW, A_BAND), 0))
        for c in range(3):
            dist = jnp.abs(kq - c * WINDOW).astype(F32)
            for i, h in enumerate(heads):
                bias_ref[c, i * WINDOW:(i + 1) * WINDOW, :] = jnp.where(
                    dist <= float(WINDOW), _ALIBI[h] * dist, -NEG_INF)
        for i, h in enumerate(heads):
            sink_tab[i * WINDOW:(i + 1) * WINDOW, :] = jnp.full((WINDOW, LANES), sink_ref[h], F32)

    sink_col = jnp.max(sink_tab[...], axis=-1, keepdims=True)

    def body(n, carry):
        q0 = pl.multiple_of(n * WINDOW, WINDOW)
        start = pl.multiple_of(jnp.clip((n - 1) * WINDOW, 0, seq - A_BAND), WINDOW)
        cfg = jnp.where(n == 0, 0, jnp.where(n == nblk - 1, 2, 1))
        kb = k_ref[pl.ds(start, A_BAND), :]
        vb = v_ref[pl.ds(start, A_BAND), :]
        pieces = []
        for j in range(4):
            qt = q_ref[pl.ds(q0, WINDOW), j * LANES:(j + 1) * LANES]
            zero = jnp.zeros_like(qt)
            lhs = jnp.concatenate([jnp.where(low, qt, zero), jnp.where(low, zero, qt)], axis=0)
            pieces.append(lax.dot_general(lhs, kb, dn, preferred_element_type=F32))
        t = jnp.concatenate(pieces, axis=0) - bias_ref[cfg]
        m = jnp.maximum(jnp.max(t, axis=-1, keepdims=True), sink_col)
        e = jnp.exp(t - m)
        den = jnp.sum(e, axis=-1, keepdims=True) + jnp.exp(sink_col - m)
        p = (e * (1.0 / den)).astype(BF16)
        pv = jnp.dot(p, vb, preferred_element_type=F32)
        for j in range(4):
            o = jnp.where(low, pv[2 * j * WINDOW:(2 * j + 1) * WINDOW],
                          pv[(2 * j + 1) * WINDOW:(2 * j + 2) * WINDOW])
            o_ref[pl.ds(q0, WINDOW), j * LANES:(j + 1) * LANES] = o.astype(BF16)
        return carry

    lax.fori_loop(0, nblk, body, 0, unroll=4)


def _attn_a(proj, sink, bsz, seq):
    return pl.pallas_call(
        functools.partial(_attn_a_kernel, seq=seq),
        grid=(bsz,),
        in_specs=[pl.BlockSpec(memory_space=pltpu.SMEM),
                  pl.BlockSpec((seq, 4 * LANES), lambda b: (b, A_Q_BLK // 4)),
                  pl.BlockSpec((seq, LANES), lambda b: (b, A_K_BLK)),
                  pl.BlockSpec((seq, LANES), lambda b: (b, A_V_BLK))],
        out_specs=pl.BlockSpec((seq, MIX_W), lambda b: (b, 0)),
        out_shape=jax.ShapeDtypeStruct((bsz * seq, MIX_W), BF16),
        scratch_shapes=[pltpu.VMEM((3, A_HEADS * WINDOW, A_BAND), F32),
                        pltpu.VMEM((A_HEADS * WINDOW, LANES), F32)],
        compiler_params=_cparams(("arbitrary",)),
        name="attn_a",
    )(sink, proj, proj, proj)


def _attn_b_kernel(slope_ref, lamv_ref, gain_ref, q_ref, k_ref, v_ref, o_ref, bias_ref, *,
                   seq, lam_init):
    nblk = seq // B_TQ
    slope = slope_ref[pl.program_id(0)]

    @pl.when(pl.program_id(1) == 0)
    def _():
        ji = (lax.broadcasted_iota(jnp.int32, (B_TQ, B_TQ), 1)
              - lax.broadcasted_iota(jnp.int32, (B_TQ, B_TQ), 0))
        for dd in range(2 * nblk - 1):
            bias_ref[dd] = slope * jnp.abs(ji + (dd - (nblk - 1)) * B_TQ).astype(F32)

    lv = lamv_ref[...]
    lam = (jnp.exp(jnp.sum(lv[0:1] * lv[1:2], axis=-1, keepdims=True))
           - jnp.exp(jnp.sum(lv[2:3] * lv[3:4], axis=-1, keepdims=True)) + lam_init)
    scale = gain_ref[...] * (1.0 - lam_init)
    lane = lax.broadcasted_iota(jnp.int32, (B_TQ, LANES), 1)
    low = lane < HEAD_DIM
    kall = k_ref[...]
    vall = v_ref[...]
    dn = (((1,), (1,)), ((), ()))

    def body(n, carry):
        q0 = pl.multiple_of(n * B_TQ, B_TQ)
        qt = q_ref[pl.ds(q0, B_TQ), :]
        zero = jnp.zeros_like(qt)
        bias = jnp.concatenate([bias_ref[kc - n + (nblk - 1)] for kc in range(nblk)], axis=1)
        es, ls = [], []
        for mp in range(2):
            qm = jnp.where(low, qt, zero) if mp == 0 else jnp.where(low, zero, qt)
            s = lax.dot_general(qm, kall, dn, preferred_element_type=F32) - bias
            m = jnp.max(s, axis=-1, keepdims=True)
            e = jnp.exp(s - m)
            ls.append(jnp.sum(e, axis=-1, keepdims=True))
            es.append(e.astype(BF16))
        a = es[0] * (1.0 / ls[0]).astype(BF16) - es[1] * (lam / ls[1]).astype(BF16)
        o = jnp.dot(a, vall, preferred_element_type=F32)
        o = o * lax.rsqrt(jnp.mean(o * o, axis=-1, keepdims=True) + LN_EPS)
        o_ref[pl.ds(q0, B_TQ), :] = (o * scale).astype(BF16)
        return carry

    lax.fori_loop(0, nblk, body, 0, unroll=2)


def _attn_b(proj, slopes_b, lamv, gain, bsz, seq, lam_init):
    nblk = seq // B_TQ
    return pl.pallas_call(
        functools.partial(_attn_b_kernel, seq=seq, lam_init=lam_init),
        grid=(B_HEADS, bsz),
        in_specs=[pl.BlockSpec(memory_space=pltpu.SMEM),
                  pl.BlockSpec((4, HEAD_DIM), lambda h, b: (0, 0)),
                  pl.BlockSpec((1, 2 * HEAD_DIM), lambda h, b: (0, 0)),
                  pl.BlockSpec((seq, LANES), lambda h, b: (b, B_Q_BLK + h)),
                  pl.BlockSpec((seq, LANES), lambda h, b: (b, B_K_BLK + h)),
                  pl.BlockSpec((seq, LANES), lambda h, b: (b, B_V_BLK + h))],
        out_specs=pl.BlockSpec((seq, LANES), lambda h, b: (b, h)),
        out_shape=jax.ShapeDtypeStruct((bsz * seq, MIX_W), BF16),
        scratch_shapes=[pltpu.VMEM((2 * nblk - 1, B_TQ, B_TQ), F32)],
        compiler_params=_cparams(("arbitrary", "arbitrary")),
        name="attn_b",
    )(slopes_b, lamv, gain, proj, proj, proj)


def _attn_c_kernel(bias_ref, q_ref, k_ref, v_ref, o_ref, *, rows):
    qtok = NA_QROWS * GRID_W
    ktok = NA_SLAB * GRID_W
    ngrp = rows // NA_QROWS
    lane = lax.broadcasted_iota(jnp.int32, (qtok, LANES), 1)
    low = lane < HEAD_DIM
    dn = (((1,), (1,)), ((), ()))

    def body(g, carry):
        slab0 = jnp.clip(g * NA_QROWS - NA_ROWS // 2, 0, rows - NA_SLAB)
        k0 = pl.multiple_of(slab0 * GRID_W, GRID_W)
        q0 = pl.multiple_of(g * qtok, qtok)
        cfg = jnp.where(g == 0, 0, jnp.where(g == ngrp - 1, 2, 1))
        qr = q_ref[pl.ds(q0, qtok), :]
        zero = jnp.zeros_like(qr)
        lhs = jnp.concatenate([jnp.where(low, qr, zero), jnp.where(low, zero, qr)], axis=0)
        ks = k_ref[pl.ds(k0, ktok), :]
        vs = v_ref[pl.ds(k0, ktok), :]
        s = lax.dot_general(lhs, ks, dn, preferred_element_type=F32) + bias_ref[0, cfg]
        m = jnp.max(s, axis=-1, keepdims=True)
        e = jnp.exp(s - m)
        p = (e * (1.0 / jnp.sum(e, axis=-1, keepdims=True))).astype(BF16)
        pv = jnp.dot(p, vs, preferred_element_type=F32)
        o = jnp.where(low, pv[:qtok], pv[qtok:])
        o_ref[pl.ds(q0, qtok), :] = o.astype(BF16)
        return carry

    lax.fori_loop(0, ngrp, body, 0, unroll=True)


def _attn_c(proj, bias_tab, bsz, seq):
    rows = seq // GRID_W
    npair = C_HEADS // 2
    return pl.pallas_call(
        functools.partial(_attn_c_kernel, rows=rows),
        grid=(npair, bsz),
        in_specs=[pl.BlockSpec((1,) + bias_tab.shape[1:], lambda p, b: (p, 0, 0, 0)),
                  pl.BlockSpec((seq, LANES), lambda p, b: (b, C_Q_BLK + p)),
                  pl.BlockSpec((seq, LANES), lambda p, b: (b, C_K_BLK + p)),
                  pl.BlockSpec((seq, LANES), lambda p, b: (b, C_V_BLK + p))],
        out_specs=pl.BlockSpec((seq, LANES), lambda p, b: (b, p)),
        out_shape=jax.ShapeDtypeStruct((bsz * seq, MIX_W), BF16),
        compiler_params=_cparams(("arbitrary", "arbitrary")),
        name="attn_c",
    )(bias_tab, proj, proj, proj)


def _na_bias_table(rpb, rows):
    assert rows % NA_QROWS == 0 and rows >= NA_SLAB + NA_QROWS and rows >= NA_ROWS
    qc = np.arange(GRID_W)[:, None]
    kc = np.arange(GRID_W)[None, :]
    cs = np.clip(qc - NA_COLS // 2, 0, GRID_W - NA_COLS)
    valid = (kc >= cs) & (kc < cs + NA_COLS)
    off = GRID_W - NA_COLS
    rp = jnp.pad(rpb.astype(F32), ((0, 0), (0, 0), (off, off)))
    toep = jnp.stack([rp[:, :, off + NA_COLS - 1 - c:off + NA_COLS - 1 - c + GRID_W]
                      for c in range(GRID_W)], axis=2)
    toep = jnp.where(valid[None, None], toep, NEG_INF)
    masked = jnp.full((C_HEADS, GRID_W, GRID_W), NEG_INF, F32)
    ngrp = rows // NA_QROWS
    cfgs = []
    for g in (0, 1, ngrp - 1):
        slab0 = int(np.clip(g * NA_QROWS - NA_ROWS // 2, 0, rows - NA_SLAB))
        qrows = []
        for a in range(NA_QROWS):
            r = g * NA_QROWS + a
            rs = int(np.clip(r - NA_ROWS // 2, 0, rows - NA_ROWS))
            blocks = []
            for u in range(NA_SLAB):
                krow = slab0 + u
                inside = rs <= krow < rs + NA_ROWS
                blocks.append(toep[:, krow - r + NA_ROWS - 1] if inside else masked)
            qrows.append(jnp.concatenate(blocks, axis=-1))
        cfgs.append(jnp.concatenate(qrows, axis=1))
    tab = jnp.stack(cfgs, axis=1)
    tab = tab.reshape(C_HEADS // 2, 2, 3, NA_QROWS * GRID_W, NA_SLAB * GRID_W)
    return tab.transpose(0, 2, 1, 3, 4).reshape(C_HEADS // 2, 3, 2 * NA_QROWS * GRID_W,
                                                NA_SLAB * GRID_W)


def _merge_kernel(x_ref, oa_ref, ob_ref, oc_ref, wg_ref, wbr_ref, wout_ref, g_ref, b_ref, o_ref):
    x = x_ref[...]
    xb = x.astype(BF16)
    merged = None
    for i, br_ref in enumerate((oa_ref, ob_ref, oc_ref)):
        gate = jax.nn.sigmoid(jnp.dot(xb, wg_ref[:, i * D_MODEL:(i + 1) * D_MODEL],
                                      preferred_element_type=F32))
        br = jnp.dot(br_ref[...], wbr_ref[i], preferred_element_type=F32)
        merged = gate * br if merged is None else merged + gate * br
    mix = jnp.dot(merged.astype(BF16), wout_ref[...], preferred_element_type=F32)
    o_ref[...] = _layer_norm(DEEPNORM_ALPHA * x + mix, g_ref[...], b_ref[...])


def _merge(x2d, oa, ob, oc, w_gate, w_br, w_out, ln_g, ln_b):
    n = x2d.shape[0]
    tm = 512
    const2 = lambda i: (0, 0)
    return pl.pallas_call(
        _merge_kernel,
        grid=(n // tm,),
        in_specs=[pl.BlockSpec((tm, D_MODEL), lambda i: (i, 0)),
                  pl.BlockSpec((tm, MIX_W), lambda i: (i, 0)),
                  pl.BlockSpec((tm, MIX_W), lambda i: (i, 0)),
                  pl.BlockSpec((tm, MIX_W), lambda i: (i, 0)),
                  pl.BlockSpec((D_MODEL, GATE_W), const2),
                  pl.BlockSpec((N_BRANCH, MIX_W, D_MODEL), lambda i: (0, 0, 0)),
                  pl.BlockSpec((D_MODEL, D_MODEL), const2),
                  pl.BlockSpec((1, D_MODEL), const2),
                  pl.BlockSpec((1, D_MODEL), const2)],
        out_specs=pl.BlockSpec((tm, D_MODEL), lambda i: (i, 0)),
        out_shape=jax.ShapeDtypeStruct((n, D_MODEL), F32),
        compiler_params=_cparams(("arbitrary",)),
        name="merge_ln1",
    )(x2d, oa, ob, oc, w_gate, w_br, w_out, ln_g, ln_b)


def _route_kernel(x_ref, w_ref, b_ref, eidx_ref, gate_ref, rank_ref, cnt_ref, carry_ref):
    t = x_ref.shape[0]

    @pl.when(pl.program_id(0) == 0)
    def _():
        carry_ref[...] = jnp.zeros_like(carry_ref)

    logits = jnp.dot(x_ref[...], w_ref[...], preferred_element_type=F32,
                     precision=lax.Precision.HIGHEST) + b_ref[...]
    lane = lax.broadcasted_iota(jnp.int32, (t, LANES), 1).astype(F32)
    work = logits
    sels, vals, idxs = [], [], []
    for _ in range(TOP_K):
        m = jnp.max(work, axis=-1, keepdims=True)
        idx = jnp.min(jnp.where(work == m, lane, float(LANES)), axis=-1, keepdims=True)
        sel = lane == idx
        work = jnp.where(sel, -jnp.inf, work)
        sels.append(sel)
        vals.append(m)
        idxs.append(idx)
    ex = [jnp.exp(v - vals[0]) for v in vals]
    den = ex[0] + ex[1] + ex[2] + ex[3]
    onehot = jnp.zeros((t, LANES), F32)
    for sel in sels:
        onehot = jnp.where(sel, 1.0, onehot)
    ri = lax.broadcasted_iota(jnp.int32, (t, t), 0)
    ci = lax.broadcasted_iota(jnp.int32, (t, t), 1)
    tri = jnp.where(ci < ri, 1.0, 0.0).astype(BF16)
    before = jnp.dot(tri, onehot.astype(BF16), preferred_element_type=F32) + carry_ref[...]
    eidx = jnp.zeros((t, LANES), F32)
    gate = jnp.zeros((t, LANES), F32)
    rank = jnp.zeros((t, LANES), F32)
    for k in range(TOP_K):
        rk = jnp.sum(jnp.where(sels[k], before, 0.0), axis=-1, keepdims=True)
        eidx = jnp.where(lane == float(k), idxs[k], eidx)
        gate = jnp.where(lane == float(k), ex[k] / den, gate)
        rank = jnp.where(lane == float(k), rk, rank)
    eidx_ref[...] = eidx.astype(jnp.int32)
    gate_ref[...] = gate
    rank_ref[...] = rank.astype(jnp.int32)
    carry_ref[...] += jnp.sum(onehot, axis=0, keepdims=True)
    cnt_ref[...] = carry_ref[...].astype(jnp.int32)


def _route(x2d, w_router_pad, b_router_pad):
    n = x2d.shape[0]
    t = ROUTE_T
    tile = pl.BlockSpec((t, LANES), lambda i: (i, 0))
    return pl.pallas_call(
        _route_kernel,
        grid=(n // t,),
        in_specs=[pl.BlockSpec((t, D_MODEL), lambda i: (i, 0)),
                  pl.BlockSpec((D_MODEL, LANES), lambda i: (0, 0)),
                  pl.BlockSpec((1, LANES), lambda i: (0, 0))],
        out_specs=[tile, tile, tile, pl.BlockSpec((1, LANES), lambda i: (0, 0))],
        out_shape=[jax.ShapeDtypeStruct((n, LANES), jnp.int32),
                   jax.ShapeDtypeStruct((n, LANES), F32),
                   jax.ShapeDtypeStruct((n, LANES), jnp.int32),
                   jax.ShapeDtypeStruct((1, LANES), jnp.int32)],
        scratch_shapes=[pltpu.VMEM((1, LANES), F32)],
        compiler_params=_cparams(("arbitrary",)),
        name="route",
    )(x2d, w_router_pad, b_router_pad)


ROW_TILE = D_MODEL // LANES


def _to_row_tiled(dst_ref, val):
    rows = val.shape[0]
    for c in range(ROW_TILE):
        dst_ref[pl.ds(c, rows, stride=ROW_TILE), :] = val[:, c * LANES:(c + 1) * LANES]


def _from_row_tiled(src_ref, rows):
    return jnp.concatenate([src_ref[pl.ds(c, rows, stride=ROW_TILE), :] for c in range(ROW_TILE)],
                           axis=1)


def _tile_copy(src_ref, src_row, dst_ref, dst_row, sem):
    return pltpu.make_async_copy(src_ref.at[pl.ds(pl.multiple_of(src_row * ROW_TILE, ROW_TILE), ROW_TILE), :],
                                 dst_ref.at[pl.ds(pl.multiple_of(dst_row * ROW_TILE, ROW_TILE), ROW_TILE), :],
                                 sem)


def _dispatch_kernel(dest_ref, x_ref, xs_ref, xt_ref, sem):
    t = x_ref.shape[0]
    _to_row_tiled(xt_ref, x_ref[...])

    def issue(i, carry):
        for k in range(TOP_K):
            _tile_copy(xt_ref, i, xs_ref, dest_ref[0, 0, i * TOP_K + k], sem).start(priority=k % 2)
        return carry

    lax.fori_loop(0, t, issue, 0, unroll=2)
    for _ in range(TOP_K):
        pltpu.make_async_copy(xt_ref, xs_ref.at[pl.ds(0, t * ROW_TILE), :], sem).wait()


def _dispatch(x2d, dest3):
    n = x2d.shape[0]
    t = ROUTE_T
    return pl.pallas_call(
        _dispatch_kernel,
        grid=(n // t,),
        in_specs=[pl.BlockSpec((1, 1, t * TOP_K), lambda i: (i, 0, 0), memory_space=pltpu.SMEM),
                  pl.BlockSpec((t, D_MODEL), lambda i: (i, 0))],
        out_specs=pl.BlockSpec(memory_space=pl.ANY),
        out_shape=jax.ShapeDtypeStruct((n * TOP_K * ROW_TILE, LANES), F32),
        scratch_shapes=[pltpu.VMEM((t * ROW_TILE, LANES), F32), pltpu.SemaphoreType.DMA(())],
        compiler_params=_cparams(("arbitrary",)),
        name="dispatch",
    )(dest3, x2d)


def _expert_weight_copies(wu_hbm, wd_hbm, wu_f32, wd_f32, sems, layer, expert, slot):
    return (pltpu.make_async_copy(wu_hbm.at[layer, expert], wu_f32.at[slot], sems.at[slot, 0]),
            pltpu.make_async_copy(wd_hbm.at[layer, expert], wd_f32.at[slot], sems.at[slot, 1]))


def _expert_kernel(it_e, it_b, it_lo, it_hi, it_first, it_new, it_slot, it_next, n_items,
                   xs_ref, wu_hbm, bu_ref, wd_hbm, bd_ref, ys_ref,
                   wu_f32, wd_f32, wu_bf, wd_bf, sems, *, layer):
    del it_b
    j = pl.program_id(0)
    copies = functools.partial(_expert_weight_copies, wu_hbm, wd_hbm, wu_f32, wd_f32, sems, layer)

    @pl.when(j == 0)
    def _():
        for c in copies(it_e[0], 0):
            c.start()

    @pl.when(jnp.logical_and(j < n_items[0], it_new[j] == 1))
    def _():
        slot = it_slot[j]
        for c in copies(it_e[j], slot):
            c.wait()
        wu_bf[...] = wu_f32[slot].astype(BF16)
        wd_bf[...] = wd_f32[slot].astype(BF16)

        @pl.when(it_next[j] >= 0)
        def _():
            for c in copies(it_next[j], 1 - slot):
                c.start()

    @pl.when(j < n_items[0])
    def _():
        xb = _from_row_tiled(xs_ref, MOE_BM).astype(BF16)
        bu = bu_ref[0, 0]
        acts = []
        for c0 in range(0, D_EXPERT, EXPERT_CHUNK):
            c1 = c0 + EXPERT_CHUNK
            hg = jnp.dot(xb, wu_bf[:, c0:c1], preferred_element_type=F32) + bu[:, c0:c1]
            hl = (jnp.dot(xb, wu_bf[:, D_EXPERT + c0:D_EXPERT + c1], preferred_element_type=F32)
                  + bu[:, D_EXPERT + c0:D_EXPERT + c1])
            hg = jnp.minimum(hg, SWIGLU_LIMIT)
            hl = jnp.clip(hl, -SWIGLU_LIMIT, SWIGLU_LIMIT)
            acts.append((hg * jax.nn.sigmoid(SWIGLU_ALPHA * hg) * (hl + 1.0)).astype(BF16))
        act = jnp.concatenate(acts, axis=1)
        bd = bd_ref[0, 0]
        row = lax.broadcasted_iota(jnp.int32, (MOE_BM, 1), 0)
        mine = jnp.logical_and(row >= it_lo[j], row < it_hi[j])

        def down_proj(first):
            for c0 in range(0, D_MODEL, EXPERT_CHUNK):
                c1 = c0 + EXPERT_CHUNK
                y = jnp.dot(act, wd_bf[:, c0:c1], preferred_element_type=F32) + bd[:, c0:c1]
                for s0 in range(c0, c1, LANES):
                    dst = ys_ref.at[pl.ds(s0 // LANES, MOE_BM, stride=ROW_TILE), :]
                    other = 0.0 if first else dst[...]
                    dst[...] = jnp.where(mine, y[:, s0 - c0:s0 - c0 + LANES], other)

        @pl.when(it_first[j] == 1)
        def _():
            down_proj(True)

        @pl.when(it_first[j] == 0)
        def _():
            down_proj(False)


def _experts(items, xs, w_up, b_up4, w_down, b_down4, layer):
    n_items = items[0].shape[0]
    row_blk = pl.BlockSpec((MOE_BM * ROW_TILE, LANES), lambda j, ie, ib, *rest: (
        ib[jnp.minimum(j, rest[-1][0] - 1)], 0))

    def cur(j, ni):
        return jnp.minimum(j, ni[0] - 1)

    def b_map(j, ie, ib, *rest):
        return (layer, ie[cur(j, rest[-1])], 0, 0)

    return pl.pallas_call(
        functools.partial(_expert_kernel, layer=layer),
        grid_spec=pltpu.PrefetchScalarGridSpec(
            num_scalar_prefetch=len(items),
            grid=(n_items,),
            in_specs=[row_blk,
                      pl.BlockSpec(memory_space=pl.ANY),
                      pl.BlockSpec((1, 1, 1, 2 * D_EXPERT), b_map),
                      pl.BlockSpec(memory_space=pl.ANY),
                      pl.BlockSpec((1, 1, 1, D_MODEL), b_map)],
            out_specs=row_blk,
            scratch_shapes=[pltpu.VMEM((2, D_MODEL, 2 * D_EXPERT), F32),
                            pltpu.VMEM((2, D_EXPERT, D_MODEL), F32),
                            pltpu.VMEM((D_MODEL, 2 * D_EXPERT), BF16),
                            pltpu.VMEM((D_EXPERT, D_MODEL), BF16),
                            pltpu.SemaphoreType.DMA((2, 2))]),
        out_shape=jax.ShapeDtypeStruct(xs.shape, F32),
        compiler_params=_cparams(("arbitrary",)),
        name="experts",
    )(*items, xs, w_up, b_up4, w_down, b_down4)


def _expert_items(counts, n_rows):
    n_items = n_rows // MOE_BM + N_EXPERTS
    end = jnp.cumsum(counts)
    start = end - counts
    first_b = start // MOE_BM
    nb = jnp.where(counts > 0, (end - 1) // MOE_BM - first_b + 1, 0)
    item_end = jnp.cumsum(nb)
    item_start = item_end - nb
    total = item_end[-1]
    jc = jnp.minimum(jnp.arange(n_items, dtype=jnp.int32), total - 1)
    it_e = jnp.sum((item_end[None, :] <= jc[:, None]).astype(jnp.int32), axis=1)
    sel = it_e[:, None] == jnp.arange(N_EXPERTS, dtype=jnp.int32)[None, :]
    pick = lambda v: jnp.sum(jnp.where(sel, v[None, :], 0), axis=1)
    it_b = pick(first_b) + jc - pick(item_start)
    it_lo = jnp.maximum(pick(start), it_b * MOE_BM) - it_b * MOE_BM
    it_hi = jnp.minimum(pick(end), (it_b + 1) * MOE_BM) - it_b * MOE_BM
    prev_b = jnp.concatenate([jnp.full((1,), -1, jnp.int32), it_b[:-1]])
    it_first = (it_b != prev_b).astype(jnp.int32)
    prev_e = jnp.concatenate([jnp.full((1,), -1, jnp.int32), it_e[:-1]])
    it_new = (it_e != prev_e).astype(jnp.int32)
    ar = jnp.arange(N_EXPERTS, dtype=jnp.int32)
    used = counts > 0
    slot_e = (jnp.cumsum(used.astype(jnp.int32)) - 1) % 2
    later = jnp.logical_and(ar[None, :] > ar[:, None], used[None, :])
    next_e = jnp.min(jnp.where(later, ar[None, :], N_EXPERTS), axis=1)
    next_e = jnp.where(next_e == N_EXPERTS, -1, next_e)
    i32 = lambda v: v.astype(jnp.int32)
    return (i32(it_e), i32(it_b), i32(it_lo), i32(it_hi), it_first, it_new, i32(pick(slot_e)),
            i32(pick(next_e)), i32(total).reshape(1)), start


def _combine_kernel(dest_ref, gate_ref, x_ref, ys_ref, g_ref, b_ref, o_ref, buf, sem):
    t = x_ref.shape[0]

    def issue(i, carry):
        for k in range(TOP_K):
            _tile_copy(ys_ref, dest_ref[0, 0, i * TOP_K + k], buf.at[k], i, sem).start(priority=k % 2)
        return carry

    lax.fori_loop(0, t, issue, 0, unroll=2)
    for k in range(TOP_K):
        pltpu.make_async_copy(ys_ref.at[pl.ds(0, t * ROW_TILE), :], buf.at[k], sem).wait()
    gate = gate_ref[...]
    ffn = gate[:, 0:1] * _from_row_tiled(buf.at[0], t)
    for k in range(1, TOP_K):
        ffn = ffn + gate[:, k:k + 1] * _from_row_tiled(buf.at[k], t)
    o_ref[...] = _layer_norm(DEEPNORM_ALPHA * x_ref[...] + ffn, g_ref[...], b_ref[...])


def _combine(dest3, gate, x2d, ys, ln_g, ln_b):
    n = x2d.shape[0]
    t = ROUTE_T
    const2 = lambda i: (0, 0)
    return pl.pallas_call(
        _combine_kernel,
        grid=(n // t,),
        in_specs=[pl.BlockSpec((1, 1, t * TOP_K), lambda i: (i, 0, 0), memory_space=pltpu.SMEM),
                  pl.BlockSpec((t, LANES), lambda i: (i, 0)),
                  pl.BlockSpec((t, D_MODEL), lambda i: (i, 0)),
                  pl.BlockSpec(memory_space=pl.ANY),
                  pl.BlockSpec((1, D_MODEL), const2),
                  pl.BlockSpec((1, D_MODEL), const2)],
        out_specs=pl.BlockSpec((t, D_MODEL), lambda i: (i, 0)),
        out_shape=jax.ShapeDtypeStruct((n, D_MODEL), F32),
        scratch_shapes=[pltpu.VMEM((TOP_K, t * ROW_TILE, LANES), F32), pltpu.SemaphoreType.DMA(())],
        compiler_params=_cparams(("arbitrary",)),
        name="combine_ln2",
    )(dest3, gate, x2d, ys, ln_g, ln_b)


def _a_head_perm():
    grp = A_HEADS // A_KV_HEADS
    order = []
    for j in range(grp):
        order += [j, grp + j]
    return np.concatenate([np.arange(h * HEAD_DIM, (h + 1) * HEAD_DIM) for h in order])


def _moe(x1, layer, w_router, b_router, w_up, b_up, w_down, b_down, ln_g, ln_b):
    n = x1.shape[0]
    wr = jnp.pad(w_router, ((0, 0), (0, LANES - N_EXPERTS)))
    br = jnp.pad(b_router, (0, LANES - N_EXPERTS), constant_values=NEG_INF).reshape(1, LANES)
    eidx, gate, rank, cnt = _route(x1, wr, br)
    items, start = _expert_items(cnt[0, :N_EXPERTS], n * TOP_K)
    e4 = eidx[:, :TOP_K]
    sel = e4[:, :, None] == jnp.arange(N_EXPERTS, dtype=jnp.int32)
    dest = jnp.sum(jnp.where(sel, start, 0), axis=-1) + rank[:, :TOP_K]
    dest3 = dest.reshape(n // ROUTE_T, 1, ROUTE_T * TOP_K).astype(jnp.int32)
    xs = _dispatch(x1, dest3)
    lead = (DEPTH, N_EXPERTS, 1)
    ys = _experts(items, xs, w_up, b_up.reshape(lead + (-1,)), w_down, b_down.reshape(lead + (-1,)), layer)
    return _combine(dest3, gate, x1, ys, ln_g.reshape(1, -1), ln_b.reshape(1, -1))


def kernel(x, w_in, a_sink, lambda_q1, lambda_k1, lambda_q2, lambda_k2, diff_norm_g, na_rpb,
           w_branch, w_out, ln1_g, ln1_b, w_router, b_router, w_up, b_up, w_down, b_down,
           ln2_g, ln2_b):
    bsz, seq, d = x.shape
    n = bsz * seq
    rows = seq // GRID_W
    perm = _a_head_perm()
    slopes_b = jnp.asarray(_ALIBI[A_HEADS:], F32)
    xcur = x.reshape(n, d)
    for l in range(DEPTH):
        w = w_in[l]
        qscale = HEAD_DIM ** -0.5
        w_qkv = jnp.concatenate([
            w[:, :512][:, perm] * qscale, w[:, 512:768],
            w[:, 768:1280] * qscale, w[:, 1280:2304],
            w[:, 2304:2816] * qscale, w[:, 2816:QKV_W]], axis=1).astype(BF16)
        w_gate = w[:, QKV_W:].astype(BF16)
        w_br = jnp.stack([w_branch[l, 0][perm], w_branch[l, 1], w_branch[l, 2]]).astype(BF16)
        lam_init = 0.8 - 0.6 * math.exp(-0.3 * l)
        lamv = jnp.stack([lambda_q1[l], lambda_k1[l], lambda_q2[l], lambda_k2[l]]).astype(F32)

        proj = _inproj(xcur, w_qkv)
        oa = _attn_a(proj, a_sink[l].astype(F32), bsz, seq)
        ob = _attn_b(proj, slopes_b, lamv, diff_norm_g[l].reshape(1, -1).astype(F32), bsz, seq, lam_init)
        oc = _attn_c(proj, _na_bias_table(na_rpb[l], rows), bsz, seq)
        x1 = _merge(xcur, oa, ob, oc, w_gate, w_br, w_out[l].astype(BF16),
                    ln1_g[l].reshape(1, -1), ln1_b[l].reshape(1, -1))
        xcur = _moe(x1, l, w_router[l], b_router[l], w_up, b_up, w_down, b_down,
                    ln2_g[l], ln2_b[l])
    return xcur.reshape(bsz, seq, d)
```

```python
import functools
import math

import numpy as np
import jax
import jax.numpy as jnp
from jax import lax
from jax.experimental import pallas as pl
from jax.experimental.pallas import tpu as pltpu

F32 = jnp.float32
BF16 = jnp.bfloat16

D_MODEL = 1024
DEPTH = 2
HEAD_DIM = 64
A_HEADS = 8
A_KV_HEADS = 2
WINDOW = 128
B_HEADS = 4
C_HEADS = 8
GRID_W = 64
NA_ROWS = 8
NA_COLS = 16
MIX_W = 512
N_BRANCH = 3
N_EXPERTS = 32
TOP_K = 4
D_EXPERT = 1024
SWIGLU_LIMIT = 7.0
SWIGLU_ALPHA = 1.702
LN_EPS = 1e-5
NEG_INF = -1e30
DEEPNORM_ALPHA = (2 * DEPTH) ** 0.25

LANES = 128
QKV_W = 3840
GATE_W = N_BRANCH * D_MODEL
A_Q_BLK, A_K_BLK, A_V_BLK = 0, 4, 5
B_Q_BLK, B_K_BLK, B_V_BLK = 6, 10, 14
C_Q_BLK, C_K_BLK, C_V_BLK = 18, 22, 26

A_BAND = 3 * WINDOW
B_TQ = 256
NA_QROWS = 4
NA_SLAB = NA_QROWS + NA_ROWS
MOE_BM = 512
EXPERT_CHUNK = 256
ROUTE_T = 256
VMEM_LIMIT = 56 * 1024 * 1024

_ALIBI = [float(2.0 ** (-8.0 * (i + 1) / (A_HEADS + B_HEADS))) for i in range(A_HEADS + B_HEADS)]


def _cparams(sem):
    return pltpu.CompilerParams(dimension_semantics=sem, vmem_limit_bytes=VMEM_LIMIT)


def _layer_norm(y, g, b):
    mu = jnp.mean(y, axis=-1, keepdims=True)
    yc = y - mu
    var = jnp.mean(yc * yc, axis=-1, keepdims=True)
    return yc * lax.rsqrt(var + LN_EPS) * g + b


def _inproj_kernel(x_ref, w_ref, o_ref, *, chunk):
    xb = x_ref[...].astype(BF16)
    for c in range(QKV_W // chunk):
        sl = slice(c * chunk, (c + 1) * chunk)
        o_ref[:, sl] = jnp.dot(xb, w_ref[:, sl], preferred_element_type=F32).astype(BF16)


def _inproj(x2d, w_qkv):
    n = x2d.shape[0]
    tm = 512
    return pl.pallas_call(
        functools.partial(_inproj_kernel, chunk=768),
        grid=(n // tm,),
        in_specs=[pl.BlockSpec((tm, D_MODEL), lambda i: (i, 0)),
                  pl.BlockSpec((D_MODEL, QKV_W), lambda i: (0, 0))],
        out_specs=pl.BlockSpec((tm, QKV_W), lambda i: (i, 0)),
        out_shape=jax.ShapeDtypeStruct((n, QKV_W), BF16),
        compiler_params=_cparams(("arbitrary",)),
        name="inproj",
    )(x2d, w_qkv)


def _attn_a_kernel(sink_ref, q_ref, k_ref, v_ref, o_ref, bias_ref, sink_tab, *, seq):
    nblk = seq // WINDOW
    heads = [j + 4 * hf for j in range(4) for hf in range(2)]
    lane = lax.broadcasted_iota(jnp.int32, (WINDOW, LANES), 1)
    low = lane < HEAD_DIM
    dn = (((1,), (1,)), ((), ()))

    @pl.when(pl.program_id(0) == 0)
    def _():
        kq = (lax.broadcasted_iota(jnp.int32, (WINDOW, A_BAND), 1)
              - lax.broadcasted_iota(jnp.int32, (WINDOW, A_BAND), 0))
        for c in range(3):
            dist = jnp.abs(kq - c * WINDOW).astype(F32)
            for i, h in enumerate(heads):
                bias_ref[c, i * WINDOW:(i + 1) * WINDOW, :] = jnp.where(
                    dist <= float(WINDOW), _ALIBI[h] * dist, -NEG_INF)
        for i, h in enumerate(heads):
            sink_tab[i * WINDOW:(i + 1) * WINDOW, :] = jnp.full((WINDOW, LANES), sink_ref[h], F32)

    sink_col = jnp.max(sink_tab[...], axis=-1, keepdims=True)

    def body(n, carry):
        q0 = pl.multiple_of(n * WINDOW, WINDOW)
        start = pl.multiple_of(jnp.clip((n - 1) * WINDOW, 0, seq - A_BAND), WINDOW)
        cfg = jnp.where(n == 0, 0, jnp.where(n == nblk - 1, 2, 1))
        kb = k_ref[pl.ds(start, A_BAND), :]
        vb = v_ref[pl.ds(start, A_BAND), :]
        pieces = []
        for j in range(4):
            qt = q_ref[pl.ds(q0, WINDOW), j * LANES:(j + 1) * LANES]
            zero = jnp.zeros_like(qt)
            lhs = jnp.concatenate([jnp.where(low, qt, zero), jnp.where(low, zero, qt)], axis=0)
            pieces.append(lax.dot_general(lhs, kb, dn, preferred_element_type=F32))
        t = jnp.concatenate(pieces, axis=0) - bias_ref[cfg]
        m = jnp.maximum(jnp.max(t, axis=-1, keepdims=True), sink_col)
        e = jnp.exp(t - m)
        den = jnp.sum(e, axis=-1, keepdims=True) + jnp.exp(sink_col - m)
        p = (e * (1.0 / den)).astype(BF16)
        pv = jnp.dot(p, vb, preferred_element_type=F32)
        for j in range(4):
            o = jnp.where(low, pv[2 * j * WINDOW:(2 * j + 1) * WINDOW],
                          pv[(2 * j + 1) * WINDOW:(2 * j + 2) * WINDOW])
            o_ref[pl.ds(q0, WINDOW), j * LANES:(j + 1) * LANES] = o.astype(BF16)
        return carry

    lax.fori_loop(0, nblk, body, 0, unroll=4)


def _attn_a(proj, sink, bsz, seq):
    return pl.pallas_call(
        functools.partial(_attn_a_kernel, seq=seq),
        grid=(bsz,),
        in_specs=[pl.BlockSpec(memory_space=pltpu.SMEM),
                  pl.BlockSpec((seq, 4 * LANES), lambda b: (b, A_Q_BLK // 4)),
                  pl.BlockSpec((seq, LANES), lambda b: (b, A_K_BLK)),
                  pl.BlockSpec((seq, LANES), lambda b: (b, A_V_BLK))],
        out_specs=pl.BlockSpec((seq, MIX_W), lambda b: (b, 0)),
        out_shape=jax.ShapeDtypeStruct((bsz * seq, MIX_W), BF16),
        scratch_shapes=[pltpu.VMEM((3, A_HEADS * WINDOW, A_BAND), F32),
                        pltpu.VMEM((A_HEADS * WINDOW, LANES), F32)],
        compiler_params=_cparams(("arbitrary",)),
        name="attn_a",
    )(sink, proj, proj, proj)


def _attn_b_kernel(slope_ref, lamv_ref, gain_ref, q_ref, k_ref, v_ref, o_ref, bias_ref, *,
                   seq, lam_init):
    nblk = seq // B_TQ
    slope = slope_ref[pl.program_id(0)]

    @pl.when(pl.program_id(1) == 0)
    def _():
        ji = (lax.broadcasted_iota(jnp.int32, (B_TQ, B_TQ), 1)
              - lax.broadcasted_iota(jnp.int32, (B_TQ, B_TQ), 0))
        for dd in range(2 * nblk - 1):
            bias_ref[dd] = slope * jnp.abs(ji + (dd - (nblk - 1)) * B_TQ).astype(F32)

    lv = lamv_ref[...]
    lam = (jnp.exp(jnp.sum(lv[0:1] * lv[1:2], axis=-1, keepdims=True))
           - jnp.exp(jnp.sum(lv[2:3] * lv[3:4], axis=-1, keepdims=True)) + lam_init)
    scale = gain_ref[...] * (1.0 - lam_init)
    lane = lax.broadcasted_iota(jnp.int32, (B_TQ, LANES), 1)
    low = lane < HEAD_DIM
    kall = k_ref[...]
    vall = v_ref[...]
    dn = (((1,), (1,)), ((), ()))

    def body(n, carry):
        q0 = pl.multiple_of(n * B_TQ, B_TQ)
        qt = q_ref[pl.ds(q0, B_TQ), :]
        zero = jnp.zeros_like(qt)
        bias = jnp.concatenate([bias_ref[kc - n + (nblk - 1)] for kc in range(nblk)], axis=1)
        es, ls = [], []
        for mp in range(2):
            qm = jnp.where(low, qt, zero) if mp == 0 else jnp.where(low, zero, qt)
            s = lax.dot_general(qm, kall, dn, preferred_element_type=F32) - bias
            m = jnp.max(s, axis=-1, keepdims=True)
            e = jnp.exp(s - m)
            ls.append(jnp.sum(e, axis=-1, keepdims=True))
            es.append(e.astype(BF16))
        a = es[0] * (1.0 / ls[0]).astype(BF16) - es[1] * (lam / ls[1]).astype(BF16)
        o = jnp.dot(a, vall, preferred_element_type=F32)
        o = o * lax.rsqrt(jnp.mean(o * o, axis=-1, keepdims=True) + LN_EPS)
        o_ref[pl.ds(q0, B_TQ), :] = (o * scale).astype(BF16)
        return carry

    lax.fori_loop(0, nblk, body, 0, unroll=2)


def _attn_b(proj, slopes_b, lamv, gain, bsz, seq, lam_init):
    nblk = seq // B_TQ
    return pl.pallas_call(
        functools.partial(_attn_b_kernel, seq=seq, lam_init=lam_init),
        grid=(B_HEADS, bsz),
        in_specs=[pl.BlockSpec(memory_space=pltpu.SMEM),
                  pl.BlockSpec((4, HEAD_DIM), lambda h, b: (0, 0)),
                  pl.BlockSpec((1, 2 * HEAD_DIM), lambda h, b: (0, 0)),
                  pl.BlockSpec((seq, LANES), lambda h, b: (b, B_Q_BLK + h)),
                  pl.BlockSpec((seq, LANES), lambda h, b: (b, B_K_BLK + h)),
                  pl.BlockSpec((seq, LANES), lambda h, b: (b, B_V_BLK + h))],
        out_specs=pl.BlockSpec((seq, LANES), lambda h, b: (b, h)),
        out_shape=jax.ShapeDtypeStruct((bsz * seq, MIX_W), BF16),
        scratch_shapes=[pltpu.VMEM((2 * nblk - 1, B_TQ, B_TQ), F32)],
        compiler_params=_cparams(("arbitrary", "arbitrary")),
        name="attn_b",
    )(slopes_b, lamv, gain, proj, proj, proj)


def _attn_c_kernel(bias_ref, q_ref, k_ref, v_ref, o_ref, *, rows):
    qtok = NA_QROWS * GRID_W
    ktok = NA_SLAB * GRID_W
    ngrp = rows // NA_QROWS
    lane = lax.broadcasted_iota(jnp.int32, (qtok, LANES), 1)
    low = lane < HEAD_DIM
    dn = (((1,), (1,)), ((), ()))

    def body(g, carry):
        slab0 = jnp.clip(g * NA_QROWS - NA_ROWS // 2, 0, rows - NA_SLAB)
        k0 = pl.multiple_of(slab0 * GRID_W, GRID_W)
        q0 = pl.multiple_of(g * qtok, qtok)
        cfg = jnp.where(g == 0, 0, jnp.where(g == ngrp - 1, 2, 1))
        qr = q_ref[pl.ds(q0, qtok), :]
        zero = jnp.zeros_like(qr)
        lhs = jnp.concatenate([jnp.where(low, qr, zero), jnp.where(low, zero, qr)], axis=0)
        ks = k_ref[pl.ds(k0, ktok), :]
        vs = v_ref[pl.ds(k0, ktok), :]
        s = lax.dot_general(lhs, ks, dn, preferred_element_type=F32) + bias_ref[0, cfg]
        m = jnp.max(s, axis=-1, keepdims=True)
        e = jnp.exp(s - m)
        p = (e * (1.0 / jnp.sum(e, axis=-1, keepdims=True))).astype(BF16)
        pv = jnp.dot(p, vs, preferred_element_type=F32)
        o = jnp.where(low, pv[:qtok], pv[qtok:])
        o_ref[pl.ds(q0, qtok), :] = o.astype(BF16)
        return carry

    lax.fori_loop(0, ngrp, body, 0, unroll=True)


def _attn_c(proj, bias_tab, bsz, seq):
    rows = seq // GRID_W
    npair = C_HEADS // 2
    return pl.pallas_call(
        functools.partial(_attn_c_kernel, rows=rows),
        grid=(npair, bsz),
        in_specs=[pl.BlockSpec((1,) + bias_tab.shape[1:], lambda p, b: (p, 0, 0, 0)),
                  pl.BlockSpec((seq, LANES), lambda p, b: (b, C_Q_BLK + p)),
                  pl.BlockSpec((seq, LANES), lambda p, b: (b, C_K_BLK + p)),
                  pl.BlockSpec((seq, LANES), lambda p, b: (b, C_V_BLK + p))],
        out_specs=pl.BlockSpec((seq, LANES), lambda p, b: (b, p)),
        out_shape=jax.ShapeDtypeStruct((bsz * seq, MIX_W), BF16),
        compiler_params=_cparams(("arbitrary", "arbitrary")),
        name="attn_c",
    )(bias_tab, proj, proj, proj)


def _na_bias_table(rpb, rows):
    assert rows % NA_QROWS == 0 and rows >= NA_SLAB + NA_QROWS and rows >= NA_ROWS
    qc = np.arange(GRID_W)[:, None]
    kc = np.arange(GRID_W)[None, :]
    cs = np.clip(qc - NA_COLS // 2, 0, GRID_W - NA_COLS)
    valid = (kc >= cs) & (kc < cs + NA_COLS)
    off = GRID_W - NA_COLS
    rp = jnp.pad(rpb.astype(F32), ((0, 0), (0, 0), (off, off)))
    toep = jnp.stack([rp[:, :, off + NA_COLS - 1 - c:off + NA_COLS - 1 - c + GRID_W]
                      for c in range(GRID_W)], axis=2)
    toep = jnp.where(valid[None, None], toep, NEG_INF)
    masked = jnp.full((C_HEADS, GRID_W, GRID_W), NEG_INF, F32)
    ngrp = rows // NA_QROWS
    cfgs = []
    for g in (0, 1, ngrp - 1):
        slab0 = int(np.clip(g * NA_QROWS - NA_ROWS // 2, 0, rows - NA_SLAB))
        qrows = []
        for a in range(NA_QROWS):
            r = g * NA_QROWS + a
            rs = int(np.clip(r - NA_ROWS // 2, 0, rows - NA_ROWS))
            blocks = []
            for u in range(NA_SLAB):
                krow = slab0 + u
                inside = rs <= krow < rs + NA_ROWS
                blocks.append(toep[:, krow - r + NA_ROWS - 1] if inside else masked)
            qrows.append(jnp.concatenate(blocks, axis=-1))
        cfgs.append(jnp.concatenate(qrows, axis=1))
    tab = jnp.stack(cfgs, axis=1)
    tab = tab.reshape(C_HEADS // 2, 2, 3, NA_QROWS * GRID_W, NA_SLAB * GRID_W)
    return tab.transpose(0, 2, 1, 3, 4).reshape(C_HEADS // 2, 3, 2 * NA_QROWS * GRID_W,
                                                NA_SLAB * GRID_W)


def _merge_kernel(x_ref, oa_ref, ob_ref, oc_ref, wg_ref, wbr_ref, wout_ref, g_ref, b_ref, o_ref):
    x = x_ref[...]
    xb = x.astype(BF16)
    merged = None
    for i, br_ref in enumerate((oa_ref, ob_ref, oc_ref)):
        gate = jax.nn.sigmoid(jnp.dot(xb, wg_ref[:, i * D_MODEL:(i + 1) * D_MODEL],
                                      preferred_element_type=F32))
        br = jnp.dot(br_ref[...], wbr_ref[i], preferred_element_type=F32)
        merged = gate * br if merged is None else merged + gate * br
    mix = jnp.dot(merged.astype(BF16), wout_ref[...], preferred_element_type=F32)
    o_ref[...] = _layer_norm(DEEPNORM_ALPHA * x + mix, g_ref[...], b_ref[...])


def _merge(x2d, oa, ob, oc, w_gate, w_br, w_out, ln_g, ln_b):
    n = x2d.shape[0]
    tm = 512
    const2 = lambda i: (0, 0)
    return pl.pallas_call(
        _merge_kernel,
        grid=(n // tm,),
        in_specs=[pl.BlockSpec((tm, D_MODEL), lambda i: (i, 0)),
                  pl.BlockSpec((tm, MIX_W), lambda i: (i, 0)),
                  pl.BlockSpec((tm, MIX_W), lambda i: (i, 0)),
                  pl.BlockSpec((tm, MIX_W), lambda i: (i, 0)),
                  pl.BlockSpec((D_MODEL, GATE_W), const2),
                  pl.BlockSpec((N_BRANCH, MIX_W, D_MODEL), lambda i: (0, 0, 0)),
                  pl.BlockSpec((D_MODEL, D_MODEL), const2),
                  pl.BlockSpec((1, D_MODEL), const2),
                  pl.BlockSpec((1, D_MODEL), const2)],
        out_specs=pl.BlockSpec((tm, D_MODEL), lambda i: (i, 0)),
        out_shape=jax.ShapeDtypeStruct((n, D_MODEL), F32),
        compiler_params=_cparams(("arbitrary",)),
        name="merge_ln1",
    )(x2d, oa, ob, oc, w_gate, w_br, w_out, ln_g, ln_b)


def _route_kernel(x_ref, w_ref, b_ref, eidx_ref, gate_ref, rank_ref, cnt_ref, carry_ref):
    t = x_ref.shape[0]

    @pl.when(pl.program_id(0) == 0)
    def _():
        carry_ref[...] = jnp.zeros_like(carry_ref)

    logits = jnp.dot(x_ref[...], w_ref[...], preferred_element_type=F32,
                     precision=lax.Precision.HIGHEST) + b_ref[...]
    lane = lax.broadcasted_iota(jnp.int32, (t, LANES), 1).astype(F32)
    work = logits
    sels, vals, idxs = [], [], []
    for _ in range(TOP_K):
        m = jnp.max(work, axis=-1, keepdims=True)
        idx = jnp.min(jnp.where(work == m, lane, float(LANES)), axis=-1, keepdims=True)
        sel = lane == idx
        work = jnp.where(sel, -jnp.inf, work)
        sels.append(sel)
        vals.append(m)
        idxs.append(idx)
    ex = [jnp.exp(v - vals[0]) for v in vals]
    den = ex[0] + ex[1] + ex[2] + ex[3]
    onehot = jnp.zeros((t, LANES), F32)
    for sel in sels:
        onehot = jnp.where(sel, 1.0, onehot)
    ri = lax.broadcasted_iota(jnp.int32, (t, t), 0)
    ci = lax.broadcasted_iota(jnp.int32, (t, t), 1)
    tri = jnp.where(ci < ri, 1.0, 0.0).astype(BF16)
    before = jnp.dot(tri, onehot.astype(BF16), preferred_element_type=F32) + carry_ref[...]
    eidx = jnp.zeros((t, LANES), F32)
    gate = jnp.zeros((t, LANES), F32)
    rank = jnp.zeros((t, LANES), F32)
    for k in range(TOP_K):
        rk = jnp.sum(jnp.where(sels[k], before, 0.0), axis=-1, keepdims=True)
        eidx = jnp.where(lane == float(k), idxs[k], eidx)
        gate = jnp.where(lane == float(k), ex[k] / den, gate)
        rank = jnp.where(lane == float(k), rk, rank)
    eidx_ref[...] = eidx.astype(jnp.int32)
    gate_ref[...] = gate
    rank_ref[...] = rank.astype(jnp.int32)
    carry_ref[...] += jnp.sum(onehot, axis=0, keepdims=True)
    cnt_ref[...] = carry_ref[...].astype(jnp.int32)


def _route(x2d, w_router_pad, b_router_pad):
    n = x2d.shape[0]
    t = ROUTE_T
    tile = pl.BlockSpec((t, LANES), lambda i: (i, 0))
    return pl.pallas_call(
        _route_kernel,
        grid=(n // t,),
        in_specs=[pl.BlockSpec((t, D_MODEL), lambda i: (i, 0)),
                  pl.BlockSpec((D_MODEL, LANES), lambda i: (0, 0)),
                  pl.BlockSpec((1, LANES), lambda i: (0, 0))],
        out_specs=[tile, tile, tile, pl.BlockSpec((1, LANES), lambda i: (0, 0))],
        out_shape=[jax.ShapeDtypeStruct((n, LANES), jnp.int32),
                   jax.ShapeDtypeStruct((n, LANES), F32),
                   jax.ShapeDtypeStruct((n, LANES), jnp.int32),
                   jax.ShapeDtypeStruct((1, LANES), jnp.int32)],
        scratch_shapes=[pltpu.VMEM((1, LANES), F32)],
        compiler_params=_cparams(("arbitrary",)),
        name="route",
    )(x2d, w_router_pad, b_router_pad)


ROW_TILE = D_MODEL // LANES


def _to_row_tiled(dst_ref, val):
    rows = val.shape[0]
    for c in range(ROW_TILE):
        dst_ref[pl.ds(c, rows, stride=ROW_TILE), :] = val[:, c * LANES:(c + 1) * LANES]


def _from_row_tiled(src_ref, rows):
    return jnp.concatenate([src_ref[pl.ds(c, rows, stride=ROW_TILE), :] for c in range(ROW_TILE)],
                           axis=1)


def _tile_copy(src_ref, src_row, dst_ref, dst_row, sem):
    return pltpu.make_async_copy(src_ref.at[pl.ds(pl.multiple_of(src_row * ROW_TILE, ROW_TILE), ROW_TILE), :],
                                 dst_ref.at[pl.ds(pl.multiple_of(dst_row * ROW_TILE, ROW_TILE), ROW_TILE), :],
                                 sem)


def _dispatch_kernel(dest_ref, x_ref, xs_ref, xt_ref, sem):
    t = x_ref.shape[0]
    _to_row_tiled(xt_ref, x_ref[...])

    def issue(i, carry):
        for k in range(TOP_K):
            _tile_copy(xt_ref, i, xs_ref, dest_ref[0, 0, i * TOP_K + k], sem).start(priority=k % 2)
        return carry

    lax.fori_loop(0, t, issue, 0, unroll=2)
    for _ in range(TOP_K):
        pltpu.make_async_copy(xt_ref, xs_ref.at[pl.ds(0, t * ROW_TILE), :], sem).wait()


def _dispatch(x2d, dest3):
    n = x2d.shape[0]
    t = ROUTE_T
    return pl.pallas_call(
        _dispatch_kernel,
        grid=(n // t,),
        in_specs=[pl.BlockSpec((1, 1, t * TOP_K), lambda i: (i, 0, 0), memory_space=pltpu.SMEM),
                  pl.BlockSpec((t, D_MODEL), lambda i: (i, 0))],
        out_specs=pl.BlockSpec(memory_space=pl.ANY),
        out_shape=jax.ShapeDtypeStruct((n * TOP_K * ROW_TILE, LANES), F32),
        scratch_shapes=[pltpu.VMEM((t * ROW_TILE, LANES), F32), pltpu.SemaphoreType.DMA(())],
        compiler_params=_cparams(("arbitrary",)),
        name="dispatch",
    )(dest3, x2d)


def _expert_weight_copies(wu_hbm, wd_hbm, wu_f32, wd_f32, sems, layer, expert, slot):
    return (pltpu.make_async_copy(wu_hbm.at[layer, expert], wu_f32.at[slot], sems.at[slot, 0]),
            pltpu.make_async_copy(wd_hbm.at[layer, expert], wd_f32.at[slot], sems.at[slot, 1]))


def _expert_kernel(it_e, it_b, it_lo, it_hi, it_first, it_new, it_slot, it_next, n_items,
                   xs_ref, wu_hbm, bu_ref, wd_hbm, bd_ref, ys_ref,
                   wu_f32, wd_f32, wu_bf, wd_bf, sems, *, layer):
    del it_b
    j = pl.program_id(0)
    copies = functools.partial(_expert_weight_copies, wu_hbm, wd_hbm, wu_f32, wd_f32, sems, layer)

    @pl.when(j == 0)
    def _():
        for c in copies(it_e[0], 0):
            c.start()

    @pl.when(jnp.logical_and(j < n_items[0], it_new[j] == 1))
    def _():
        slot = it_slot[j]
        for c in copies(it_e[j], slot):
            c.wait()
        wu_bf[...] = wu_f32[slot].astype(BF16)
        wd_bf[...] = wd_f32[slot].astype(BF16)

        @pl.when(it_next[j] >= 0)
        def _():
            for c in copies(it_next[j], 1 - slot):
                c.start(priority=1)

    @pl.when(j < n_items[0])
    def _():
        xb = _from_row_tiled(xs_ref, MOE_BM).astype(BF16)
        bu = bu_ref[0, 0]
        acts = []
        for c0 in range(0, D_EXPERT, EXPERT_CHUNK):
            c1 = c0 + EXPERT_CHUNK
            hg = jnp.dot(xb, wu_bf[:, c0:c1], preferred_element_type=F32) + bu[:, c0:c1]
            hl = (jnp.dot(xb, wu_bf[:, D_EXPERT + c0:D_EXPERT + c1], preferred_element_type=F32)
                  + bu[:, D_EXPERT + c0:D_EXPERT + c1])
            hg = jnp.minimum(hg, SWIGLU_LIMIT)
            hl = jnp.clip(hl, -SWIGLU_LIMIT, SWIGLU_LIMIT)
            acts.append((hg * jax.nn.sigmoid(SWIGLU_ALPHA * hg) * (hl + 1.0)).astype(BF16))
        act = jnp.concatenate(acts, axis=1)
        bd = bd_ref[0, 0]
        row = lax.broadcasted_iota(jnp.int32, (MOE_BM, 1), 0)
        mine = jnp.logical_and(row >= it_lo[j], row < it_hi[j])

        def down_proj(first):
            for c0 in range(0, D_MODEL, EXPERT_CHUNK):
                c1 = c0 + EXPERT_CHUNK
                y = jnp.dot(act, wd_bf[:, c0:c1], preferred_element_type=F32) + bd[:, c0:c1]
                for s0 in range(c0, c1, LANES):
                    dst = ys_ref.at[pl.ds(s0 // LANES, MOE_BM, stride=ROW_TILE), :]
                    other = 0.0 if first else dst[...]
                    dst[...] = jnp.where(mine, y[:, s0 - c0:s0 - c0 + LANES], other)

        @pl.when(it_first[j] == 1)
        def _():
            down_proj(True)

        @pl.when(it_first[j] == 0)
        def _():
            down_proj(False)


def _experts(items, xs, w_up, b_up4, w_down, b_down4, layer):
    n_items = items[0].shape[0]
    row_blk = pl.BlockSpec((MOE_BM * ROW_TILE, LANES), lambda j, ie, ib, *rest: (
        ib[jnp.minimum(j, rest[-1][0] - 1)], 0))

    def cur(j, ni):
        return jnp.minimum(j, ni[0] - 1)

    def b_map(j, ie, ib, *rest):
        return (layer, ie[cur(j, rest[-1])], 0, 0)

    return pl.pallas_call(
        functools.partial(_expert_kernel, layer=layer),
        grid_spec=pltpu.PrefetchScalarGridSpec(
            num_scalar_prefetch=len(items),
            grid=(n_items,),
            in_specs=[row_blk,
                      pl.BlockSpec(memory_space=pl.ANY),
                      pl.BlockSpec((1, 1, 1, 2 * D_EXPERT), b_map),
                      pl.BlockSpec(memory_space=pl.ANY),
                      pl.BlockSpec((1, 1, 1, D_MODEL), b_map)],
            out_specs=row_blk,
            scratch_shapes=[pltpu.VMEM((2, D_MODEL, 2 * D_EXPERT), F32),
                            pltpu.VMEM((2, D_EXPERT, D_MODEL), F32),
                            pltpu.VMEM((D_MODEL, 2 * D_EXPERT), BF16),
                            pltpu.VMEM((D_EXPERT, D_MODEL), BF16),
                            pltpu.SemaphoreType.DMA((2, 2))]),
        out_shape=jax.ShapeDtypeStruct(xs.shape, F32),
        compiler_params=_cparams(("arbitrary",)),
        name="experts",
    )(*items, xs, w_up, b_up4, w_down, b_down4)


def _expert_items(counts, n_rows):
    n_items = n_rows // MOE_BM + N_EXPERTS
    end = jnp.cumsum(counts)
    start = end - counts
    first_b = start // MOE_BM
    nb = jnp.where(counts > 0, (end - 1) // MOE_BM - first_b + 1, 0)
    item_end = jnp.cumsum(nb)
    item_start = item_end - nb
    total = item_end[-1]
    jc = jnp.minimum(jnp.arange(n_items, dtype=jnp.int32), total - 1)
    it_e = jnp.sum((item_end[None, :] <= jc[:, None]).astype(jnp.int32), axis=1)
    sel = it_e[:, None] == jnp.arange(N_EXPERTS, dtype=jnp.int32)[None, :]
    pick = lambda v: jnp.sum(jnp.where(sel, v[None, :], 0), axis=1)
    it_b = pick(first_b) + jc - pick(item_start)
    it_lo = jnp.maximum(pick(start), it_b * MOE_BM) - it_b * MOE_BM
    it_hi = jnp.minimum(pick(end), (it_b + 1) * MOE_BM) - it_b * MOE_BM
    prev_b = jnp.concatenate([jnp.full((1,), -1, jnp.int32), it_b[:-1]])
    it_first = (it_b != prev_b).astype(jnp.int32)
    prev_e = jnp.concatenate([jnp.full((1,), -1, jnp.int32), it_e[:-1]])
    it_new = (it_e != prev_e).astype(jnp.int32)
    ar = jnp.arange(N_EXPERTS, dtype=jnp.int32)
    used = counts > 0
    slot_e = (jnp.cumsum(used.astype(jnp.int32)) - 1) % 2
    later = jnp.logical_and(ar[None, :] > ar[:, None], used[None, :])
    next_e = jnp.min(jnp.where(later, ar[None, :], N_EXPERTS), axis=1)
    next_e = jnp.where(next_e == N_EXPERTS, -1, next_e)
    i32 = lambda v: v.astype(jnp.int32)
    return (i32(it_e), i32(it_b), i32(it_lo), i32(it_hi), it_first, it_new, i32(pick(slot_e)),
            i32(pick(next_e)), i32(total).reshape(1)), start


def _combine_kernel(dest_ref, gate_ref, x_ref, ys_ref, g_ref, b_ref, o_ref, buf, sem):
    t = x_ref.shape[0]

    def issue(i, carry):
        for k in range(TOP_K):
            _tile_copy(ys_ref, dest_ref[0, 0, i * TOP_K + k], buf.at[k], i, sem).start(priority=k % 2)
        return carry

    lax.fori_loop(0, t, issue, 0, unroll=2)
    for k in range(TOP_K):
        pltpu.make_async_copy(ys_ref.at[pl.ds(0, t * ROW_TILE), :], buf.at[k], sem).wait()
    gate = gate_ref[...]
    ffn = gate[:, 0:1] * _from_row_tiled(buf.at[0], t)
    for k in range(1, TOP_K):
        ffn = ffn + gate[:, k:k + 1] * _from_row_tiled(buf.at[k], t)
    o_ref[...] = _layer_norm(DEEPNORM_ALPHA * x_ref[...] + ffn, g_ref[...], b_ref[...])


def _combine(dest3, gate, x2d, ys, ln_g, ln_b):
    n = x2d.shape[0]
    t = ROUTE_T
    const2 = lambda i: (0, 0)
    return pl.pallas_call(
        _combine_kernel,
        grid=(n // t,),
        in_specs=[pl.BlockSpec((1, 1, t * TOP_K), lambda i: (i, 0, 0), memory_space=pltpu.SMEM),
                  pl.BlockSpec((t, LANES), lambda i: (i, 0)),
                  pl.BlockSpec((t, D_MODEL), lambda i: (i, 0)),
                  pl.BlockSpec(memory_space=pl.ANY),
                  pl.BlockSpec((1, D_MODEL), const2),
                  pl.BlockSpec((1, D_MODEL), const2)],
        out_specs=pl.BlockSpec((t, D_MODEL), lambda i: (i, 0)),
        out_shape=jax.ShapeDtypeStruct((n, D_MODEL), F32),
        scratch_shapes=[pltpu.VMEM((TOP_K, t * ROW_TILE, LANES), F32), pltpu.SemaphoreType.DMA(())],
        compiler_params=_cparams(("arbitrary",)),
        name="combine_ln2",
    )(dest3, gate, x2d, ys, ln_g, ln_b)


def _a_head_perm():
    grp = A_HEADS // A_KV_HEADS
    order = []
    for j in range(grp):
        order += [j, grp + j]
    return np.concatenate([np.arange(h * HEAD_DIM, (h + 1) * HEAD_DIM) for h in order])


def _moe(x1, layer, w_router, b_router, w_up, b_up, w_down, b_down, ln_g, ln_b):
    n = x1.shape[0]
    wr = jnp.pad(w_router, ((0, 0), (0, LANES - N_EXPERTS)))
    br = jnp.pad(b_router, (0, LANES - N_EXPERTS), constant_values=NEG_INF).reshape(1, LANES)
    eidx, gate, rank, cnt = _route(x1, wr, br)
    items, start = _expert_items(cnt[0, :N_EXPERTS], n * TOP_K)
    e4 = eidx[:, :TOP_K]
    sel = e4[:, :, None] == jnp.arange(N_EXPERTS, dtype=jnp.int32)
    dest = jnp.sum(jnp.where(sel, start, 0), axis=-1) + rank[:, :TOP_K]
    dest3 = dest.reshape(n // ROUTE_T, 1, ROUTE_T * TOP_K).astype(jnp.int32)
    xs = _dispatch(x1, dest3)
    lead = (DEPTH, N_EXPERTS, 1)
    ys = _experts(items, xs, w_up, b_up.reshape(lead + (-1,)), w_down, b_down.reshape(lead + (-1,)), layer)
    return _combine(dest3, gate, x1, ys, ln_g.reshape(1, -1), ln_b.reshape(1, -1))


def kernel(x, w_in, a_sink, lambda_q1, lambda_k1, lambda_q2, lambda_k2, diff_norm_g, na_rpb,
           w_branch, w_out, ln1_g, ln1_b, w_router, b_router, w_up, b_up, w_down, b_down,
           ln2_g, ln2_b):
    bsz, seq, d = x.shape
    n = bsz * seq
    rows = seq // GRID_W
    perm = _a_head_perm()
    slopes_b = jnp.asarray(_ALIBI[A_HEADS:], F32)
    xcur = x.reshape(n, d)
    for l in range(DEPTH):
        w = w_in[l]
        qscale = HEAD_DIM ** -0.5
        w_qkv = jnp.concatenate([
            w[:, :512][:, perm] * qscale, w[:, 512:768],
            w[:, 768:1280] * qscale, w[:, 1280:2304],
            w[:, 2304:2816] * qscale, w[:, 2816:QKV_W]], axis=1).astype(BF16)
        w_gate = w[:, QKV_W:].astype(BF16)
        w_br = jnp.stack([w_branch[l, 0][perm], w_branch[l, 1], w_branch[l, 2]]).astype(BF16)
        lam_init = 0.8 - 0.6 * math.exp(-0.3 * l)
        lamv = jnp.stack([lambda_q1[l], lambda_k1[l], lambda_q2[l], lambda_k2[l]]).astype(F32)

        proj = _inproj(xcur, w_qkv)
        oa = _attn_a(proj, a_sink[l].astype(F32), bsz, seq)
        ob = _attn_b(proj, slopes_b, lamv, diff_norm_g[l].reshape(1, -1).astype(F32), bsz, seq, lam_init)
        oc = _attn_c(proj, _na_bias_table(na_rpb[l], rows), bsz, seq)
        x1 = _merge(xcur, oa, ob, oc, w_gate, w_br, w_out[l].astype(BF16),
                    ln1_g[l].reshape(1, -1), ln1_b[l].reshape(1, -1))
        xcur = _moe(x1, l, w_router[l], b_router[l], w_up, b_up, w_down, b_down,
                    ln2_g[l], ln2_b[l])
    return xcur.reshape(bsz, seq, d)
```

```python
import functools
import math

import numpy as np
import jax
import jax.numpy as jnp
from jax import lax
from jax.experimental import pallas as pl
from jax.experimental.pallas import tpu as pltpu

F32 = jnp.float32
BF16 = jnp.bfloat16

D_MODEL = 1024
DEPTH = 2
HEAD_DIM = 64
A_HEADS = 8
A_KV_HEADS = 2
WINDOW = 128
B_HEADS = 4
C_HEADS = 8
GRID_W = 64
NA_ROWS = 8
NA_COLS = 16
MIX_W = 512
N_BRANCH = 3
N_EXPERTS = 32
TOP_K = 4
D_EXPERT = 1024
SWIGLU_LIMIT = 7.0
SWIGLU_ALPHA = 1.702
LN_EPS = 1e-5
NEG_INF = -1e30
DEEPNORM_ALPHA = (2 * DEPTH) ** 0.25

LANES = 128
QKV_W = 3840
GATE_W = N_BRANCH * D_MODEL
A_Q_BLK, A_K_BLK, A_V_BLK = 0, 4, 5
B_Q_BLK, B_K_BLK, B_V_BLK = 6, 10, 14
C_Q_BLK, C_K_BLK, C_V_BLK = 18, 22, 26

A_BAND = 3 * WINDOW
B_TQ = 256
NA_QROWS = 4
NA_SLAB = NA_QROWS + NA_ROWS
MOE_BM = 512
EXPERT_CHUNK = 256
ROUTE_T = 256
VMEM_LIMIT = 56 * 1024 * 1024

_ALIBI = [float(2.0 ** (-8.0 * (i + 1) / (A_HEADS + B_HEADS))) for i in range(A_HEADS + B_HEADS)]


def _cparams(sem):
    return pltpu.CompilerParams(dimension_semantics=sem, vmem_limit_bytes=VMEM_LIMIT)


def _layer_norm(y, g, b):
    mu = jnp.mean(y, axis=-1, keepdims=True)
    yc = y - mu
    var = jnp.mean(yc * yc, axis=-1, keepdims=True)
    return yc * lax.rsqrt(var + LN_EPS) * g + b


def _inproj_kernel(x_ref, w_ref, o_ref, *, chunk):
    xb = x_ref[...].astype(BF16)
    for c in range(QKV_W // chunk):
        sl = slice(c * chunk, (c + 1) * chunk)
        o_ref[:, sl] = jnp.dot(xb, w_ref[:, sl], preferred_element_type=F32).astype(BF16)


def _inproj(x2d, w_qkv):
    n = x2d.shape[0]
    tm = 512
    return pl.pallas_call(
        functools.partial(_inproj_kernel, chunk=768),
        grid=(n // tm,),
        in_specs=[pl.BlockSpec((tm, D_MODEL), lambda i: (i, 0)),
                  pl.BlockSpec((D_MODEL, QKV_W), lambda i: (0, 0))],
        out_specs=pl.BlockSpec((tm, QKV_W), lambda i: (i, 0)),
        out_shape=jax.ShapeDtypeStruct((n, QKV_W), BF16),
        compiler_params=_cparams(("arbitrary",)),
        name="inproj",
    )(x2d, w_qkv)


def _attn_a_kernel(sink_ref, q_ref, k_ref, v_ref, o_ref, bias_ref, sink_tab, *, seq):
    nblk = seq // WINDOW
    heads = [j + 4 * hf for j in range(4) for hf in range(2)]
    lane = lax.broadcasted_iota(jnp.int32, (WINDOW, LANES), 1)
    low = lane < HEAD_DIM
    dn = (((1,), (1,)), ((), ()))

    @pl.when(pl.program_id(0) == 0)
    def _():
        kq = (lax.broadcasted_iota(jnp.int32, (WINDOW, A_BAND), 1)
              - lax.broadcasted_iota(jnp.int32, (WINDOW, A_BAND), 0))
        for c in range(3):
            dist = jnp.abs(kq - c * WINDOW).astype(F32)
            for i, h in enumerate(heads):
                bias_ref[c, i * WINDOW:(i + 1) * WINDOW, :] = jnp.where(
                    dist <= float(WINDOW), _ALIBI[h] * dist, -NEG_INF)
        for i, h in enumerate(heads):
            sink_tab[i * WINDOW:(i + 1) * WINDOW, :] = jnp.full((WINDOW, LANES), sink_ref[h], F32)

    sink_col = jnp.max(sink_tab[...], axis=-1, keepdims=True)

    def body(n, carry):
        q0 = pl.multiple_of(n * WINDOW, WINDOW)
        start = pl.multiple_of(jnp.clip((n - 1) * WINDOW, 0, seq - A_BAND), WINDOW)
        cfg = jnp.where(n == 0, 0, jnp.where(n == nblk - 1, 2, 1))
        kb = k_ref[pl.ds(start, A_BAND), :]
        vb = v_ref[pl.ds(start, A_BAND), :]
        pieces = []
        for j in range(4):
            qt = q_ref[pl.ds(q0, WINDOW), j * LANES:(j + 1) * LANES]
            zero = jnp.zeros_like(qt)
            lhs = jnp.concatenate([jnp.where(low, qt, zero), jnp.where(low, zero, qt)], axis=0)
            pieces.append(lax.dot_general(lhs, kb, dn, preferred_element_type=F32))
        t = jnp.concatenate(pieces, axis=0) - bias_ref[cfg]
        m = jnp.maximum(jnp.max(t, axis=-1, keepdims=True), sink_col)
        e = jnp.exp(t - m)
        den = jnp.sum(e, axis=-1, keepdims=True) + jnp.exp(sink_col - m)
        p = (e * (1.0 / den)).astype(BF16)
        pv = jnp.dot(p, vb, preferred_element_type=F32)
        for j in range(4):
            o = jnp.where(low, pv[2 * j * WINDOW:(2 * j + 1) * WINDOW],
                          pv[(2 * j + 1) * WINDOW:(2 * j + 2) * WINDOW])
            o_ref[pl.ds(q0, WINDOW), j * LANES:(j + 1) * LANES] = o.astype(BF16)
        return carry

    lax.fori_loop(0, nblk, body, 0, unroll=4)


def _attn_a(proj, sink, bsz, seq):
    return pl.pallas_call(
        functools.partial(_attn_a_kernel, seq=seq),
        grid=(bsz,),
        in_specs=[pl.BlockSpec(memory_space=pltpu.SMEM),
                  pl.BlockSpec((seq, 4 * LANES), lambda b: (b, A_Q_BLK // 4)),
                  pl.BlockSpec((seq, LANES), lambda b: (b, A_K_BLK)),
                  pl.BlockSpec((seq, LANES), lambda b: (b, A_V_BLK))],
        out_specs=pl.BlockSpec((seq, MIX_W), lambda b: (b, 0)),
        out_shape=jax.ShapeDtypeStruct((bsz * seq, MIX_W), BF16),
        scratch_shapes=[pltpu.VMEM((3, A_HEADS * WINDOW, A_BAND), F32),
                        pltpu.VMEM((A_HEADS * WINDOW, LANES), F32)],
        compiler_params=_cparams(("arbitrary",)),
        name="attn_a",
    )(sink, proj, proj, proj)


def _attn_b_kernel(slope_ref, lamv_ref, gain_ref, q_ref, k_ref, v_ref, o_ref, bias_ref, *,
                   seq, lam_init):
    nblk = seq // B_TQ
    slope = slope_ref[pl.program_id(0)]

    @pl.when(pl.program_id(1) == 0)
    def _():
        ji = (lax.broadcasted_iota(jnp.int32, (B_TQ, B_TQ), 1)
              - lax.broadcasted_iota(jnp.int32, (B_TQ, B_TQ), 0))
        for dd in range(2 * nblk - 1):
            bias_ref[dd] = slope * jnp.abs(ji + (dd - (nblk - 1)) * B_TQ).astype(F32)

    lv = lamv_ref[...]
    lam = (jnp.exp(jnp.sum(lv[0:1] * lv[1:2], axis=-1, keepdims=True))
           - jnp.exp(jnp.sum(lv[2:3] * lv[3:4], axis=-1, keepdims=True)) + lam_init)
    scale = gain_ref[...] * (1.0 - lam_init)
    lane = lax.broadcasted_iota(jnp.int32, (B_TQ, LANES), 1)
    low = lane < HEAD_DIM
    kall = k_ref[...]
    vall = v_ref[...]
    dn = (((1,), (1,)), ((), ()))

    def body(n, carry):
        q0 = pl.multiple_of(n * B_TQ, B_TQ)
        qt = q_ref[pl.ds(q0, B_TQ), :]
        zero = jnp.zeros_like(qt)
        bias = jnp.concatenate([bias_ref[kc - n + (nblk - 1)] for kc in range(nblk)], axis=1)
        es, ls = [], []
        for mp in range(2):
            qm = jnp.where(low, qt, zero) if mp == 0 else jnp.where(low, zero, qt)
            s = lax.dot_general(qm, kall, dn, preferred_element_type=F32) - bias
            m = jnp.max(s, axis=-1, keepdims=True)
            e = jnp.exp(s - m)
            ls.append(jnp.sum(e, axis=-1, keepdims=True))
            es.append(e.astype(BF16))
        a = es[0] * (1.0 / ls[0]).astype(BF16) - es[1] * (lam / ls[1]).astype(BF16)
        o = jnp.dot(a, vall, preferred_element_type=F32)
        o = o * lax.rsqrt(jnp.mean(o * o, axis=-1, keepdims=True) + LN_EPS)
        o_ref[pl.ds(q0, B_TQ), :] = (o * scale).astype(BF16)
        return carry

    lax.fori_loop(0, nblk, body, 0, unroll=2)


def _attn_b(proj, slopes_b, lamv, gain, bsz, seq, lam_init):
    nblk = seq // B_TQ
    return pl.pallas_call(
        functools.partial(_attn_b_kernel, seq=seq, lam_init=lam_init),
        grid=(B_HEADS, bsz),
        in_specs=[pl.BlockSpec(memory_space=pltpu.SMEM),
                  pl.BlockSpec((4, HEAD_DIM), lambda h, b: (0, 0)),
                  pl.BlockSpec((1, 2 * HEAD_DIM), lambda h, b: (0, 0)),
                  pl.BlockSpec((seq, LANES), lambda h, b: (b, B_Q_BLK + h)),
                  pl.BlockSpec((seq, LANES), lambda h, b: (b, B_K_BLK + h)),
                  pl.BlockSpec((seq, LANES), lambda h, b: (b, B_V_BLK + h))],
        out_specs=pl.BlockSpec((seq, LANES), lambda h, b: (b, h)),
        out_shape=jax.ShapeDtypeStruct((bsz * seq, MIX_W), BF16),
        scratch_shapes=[pltpu.VMEM((2 * nblk - 1, B_TQ, B_TQ), F32)],
        compiler_params=_cparams(("arbitrary", "arbitrary")),
        name="attn_b",
    )(slopes_b, lamv, gain, proj, proj, proj)


def _attn_c_kernel(bias_ref, q_ref, k_ref, v_ref, o_ref, *, rows):
    qtok = NA_QROWS * GRID_W
    ktok = NA_SLAB * GRID_W
    ngrp = rows // NA_QROWS
    lane = lax.broadcasted_iota(jnp.int32, (qtok, LANES), 1)
    low = lane < HEAD_DIM
    dn = (((1,), (1,)), ((), ()))

    def body(g, carry):
        slab0 = jnp.clip(g * NA_QROWS - NA_ROWS // 2, 0, rows - NA_SLAB)
        k0 = pl.multiple_of(slab0 * GRID_W, GRID_W)
        q0 = pl.multiple_of(g * qtok, qtok)
        cfg = jnp.where(g == 0, 0, jnp.where(g == ngrp - 1, 2, 1))
        qr = q_ref[pl.ds(q0, qtok), :]
        zero = jnp.zeros_like(qr)
        lhs = jnp.concatenate([jnp.where(low, qr, zero), jnp.where(low, zero, qr)], axis=0)
        ks = k_ref[pl.ds(k0, ktok), :]
        vs = v_ref[pl.ds(k0, ktok), :]
        s = lax.dot_general(lhs, ks, dn, preferred_element_type=F32) + bias_ref[0, cfg]
        m = jnp.max(s, axis=-1, keepdims=True)
        e = jnp.exp(s - m)
        p = (e * (1.0 / jnp.sum(e, axis=-1, keepdims=True))).astype(BF16)
        pv = jnp.dot(p, vs, preferred_element_type=F32)
        o = jnp.where(low, pv[:qtok], pv[qtok:])
        o_ref[pl.ds(q0, qtok), :] = o.astype(BF16)
        return carry

    lax.fori_loop(0, ngrp, body, 0, unroll=True)


def _attn_c(proj, bias_tab, bsz, seq):
    rows = seq // GRID_W
    npair = C_HEADS // 2
    return pl.pallas_call(
        functools.partial(_attn_c_kernel, rows=rows),
        grid=(npair, bsz),
        in_specs=[pl.BlockSpec((1,) + bias_tab.shape[1:], lambda p, b: (p, 0, 0, 0)),
                  pl.BlockSpec((seq, LANES), lambda p, b: (b, C_Q_BLK + p)),
                  pl.BlockSpec((seq, LANES), lambda p, b: (b, C_K_BLK + p)),
                  pl.BlockSpec((seq, LANES), lambda p, b: (b, C_V_BLK + p))],
        out_specs=pl.BlockSpec((seq, LANES), lambda p, b: (b, p)),
        out_shape=jax.ShapeDtypeStruct((bsz * seq, MIX_W), BF16),
        compiler_params=_cparams(("arbitrary", "arbitrary")),
        name="attn_c",
    )(bias_tab, proj, proj, proj)


def _na_bias_table(rpb, rows):
    assert rows % NA_QROWS == 0 and rows >= NA_SLAB + NA_QROWS and rows >= NA_ROWS
    qc = np.arange(GRID_W)[:, None]
    kc = np.arange(GRID_W)[None, :]
    cs = np.clip(qc - NA_COLS // 2, 0, GRID_W - NA_COLS)
    valid = (kc >= cs) & (kc < cs + NA_COLS)
    ncol = 2 * NA_COLS - 1
    colpick = (np.clip(kc - qc + NA_COLS - 1, 0, ncol - 1)[:, :, None] == np.arange(ncol))
    ngrp = rows // NA_QROWS
    rowpick = np.zeros((3, NA_QROWS, NA_SLAB, 2 * NA_ROWS - 1), np.float32)
    for ci, g in enumerate((0, 1, ngrp - 1)):
        slab0 = int(np.clip(g * NA_QROWS - NA_ROWS // 2, 0, rows - NA_SLAB))
        for a in range(NA_QROWS):
            r = g * NA_QROWS + a
            rs = int(np.clip(r - NA_ROWS // 2, 0, rows - NA_ROWS))
            for u in range(NA_SLAB):
                if rs <= slab0 + u < rs + NA_ROWS:
                    rowpick[ci, a, u, slab0 + u - r + NA_ROWS - 1] = 1.0
    hp = lax.Precision.HIGHEST
    toep = jnp.einsum('hdc,qkc->hdqk', rpb.astype(F32), jnp.asarray(colpick, F32), precision=hp)
    tab = jnp.einsum('caud,hdqk->hcaquk', jnp.asarray(rowpick), toep, precision=hp)
    inside = (rowpick.sum(-1) > 0)[None, :, :, None, :, None] & valid[None, None, None, :, None, :]
    tab = jnp.where(inside, tab, NEG_INF)
    tab = tab.reshape(C_HEADS // 2, 2, 3, NA_QROWS * GRID_W, NA_SLAB * GRID_W)
    return tab.transpose(0, 2, 1, 3, 4).reshape(C_HEADS // 2, 3, 2 * NA_QROWS * GRID_W,
                                                NA_SLAB * GRID_W)


def _merge_kernel(x_ref, oa_ref, ob_ref, oc_ref, wg_ref, wbr_ref, wout_ref, g_ref, b_ref, o_ref):
    x = x_ref[...]
    xb = x.astype(BF16)
    merged = None
    for i, br_ref in enumerate((oa_ref, ob_ref, oc_ref)):
        gate = jax.nn.sigmoid(jnp.dot(xb, wg_ref[:, i * D_MODEL:(i + 1) * D_MODEL],
                                      preferred_element_type=F32))
        br = jnp.dot(br_ref[...], wbr_ref[i], preferred_element_type=F32)
        merged = gate * br if merged is None else merged + gate * br
    mix = jnp.dot(merged.astype(BF16), wout_ref[...], preferred_element_type=F32)
    o_ref[...] = _layer_norm(DEEPNORM_ALPHA * x + mix, g_ref[...], b_ref[...])


def _merge(x2d, oa, ob, oc, w_gate, w_br, w_out, ln_g, ln_b):
    n = x2d.shape[0]
    tm = 512
    const2 = lambda i: (0, 0)
    return pl.pallas_call(
        _merge_kernel,
        grid=(n // tm,),
        in_specs=[pl.BlockSpec((tm, D_MODEL), lambda i: (i, 0)),
                  pl.BlockSpec((tm, MIX_W), lambda i: (i, 0)),
                  pl.BlockSpec((tm, MIX_W), lambda i: (i, 0)),
                  pl.BlockSpec((tm, MIX_W), lambda i: (i, 0)),
                  pl.BlockSpec((D_MODEL, GATE_W), const2),
                  pl.BlockSpec((N_BRANCH, MIX_W, D_MODEL), lambda i: (0, 0, 0)),
                  pl.BlockSpec((D_MODEL, D_MODEL), const2),
                  pl.BlockSpec((1, D_MODEL), const2),
                  pl.BlockSpec((1, D_MODEL), const2)],
        out_specs=pl.BlockSpec((tm, D_MODEL), lambda i: (i, 0)),
        out_shape=jax.ShapeDtypeStruct((n, D_MODEL), F32),
        compiler_params=_cparams(("arbitrary",)),
        name="merge_ln1",
    )(x2d, oa, ob, oc, w_gate, w_br, w_out, ln_g, ln_b)


def _route_kernel(x_ref, w_ref, b_ref, eidx_ref, gate_ref, rank_ref, cnt_ref, carry_ref):
    t = x_ref.shape[0]

    @pl.when(pl.program_id(0) == 0)
    def _():
        carry_ref[...] = jnp.zeros_like(carry_ref)

    logits = jnp.dot(x_ref[...], w_ref[...], preferred_element_type=F32,
                     precision=lax.Precision.HIGHEST) + b_ref[...]
    lane = lax.broadcasted_iota(jnp.int32, (t, LANES), 1).astype(F32)
    work = logits
    sels, vals, idxs = [], [], []
    for _ in range(TOP_K):
        m = jnp.max(work, axis=-1, keepdims=True)
        idx = jnp.min(jnp.where(work == m, lane, float(LANES)), axis=-1, keepdims=True)
        sel = lane == idx
        work = jnp.where(sel, -jnp.inf, work)
        sels.append(sel)
        vals.append(m)
        idxs.append(idx)
    ex = [jnp.exp(v - vals[0]) for v in vals]
    den = ex[0] + ex[1] + ex[2] + ex[3]
    onehot = jnp.zeros((t, LANES), F32)
    for sel in sels:
        onehot = jnp.where(sel, 1.0, onehot)
    ri = lax.broadcasted_iota(jnp.int32, (t, t), 0)
    ci = lax.broadcasted_iota(jnp.int32, (t, t), 1)
    tri = jnp.where(ci < ri, 1.0, 0.0).astype(BF16)
    before = jnp.dot(tri, onehot.astype(BF16), preferred_element_type=F32) + carry_ref[...]
    eidx = jnp.zeros((t, LANES), F32)
    gate = jnp.zeros((t, LANES), F32)
    rank = jnp.zeros((t, LANES), F32)
    for k in range(TOP_K):
        rk = jnp.sum(jnp.where(sels[k], before, 0.0), axis=-1, keepdims=True)
        eidx = jnp.where(lane == float(k), idxs[k], eidx)
        gate = jnp.where(lane == float(k), ex[k] / den, gate)
        rank = jnp.where(lane == float(k), rk, rank)
    eidx_ref[...] = eidx.astype(jnp.int32)
    gate_ref[...] = gate
    rank_ref[...] = rank.astype(jnp.int32)
    carry_ref[...] += jnp.sum(onehot, axis=0, keepdims=True)
    cnt_ref[...] = carry_ref[...].astype(jnp.int32)


def _route(x2d, w_router_pad, b_router_pad):
    n = x2d.shape[0]
    t = ROUTE_T
    tile = pl.BlockSpec((t, LANES), lambda i: (i, 0))
    return pl.pallas_call(
        _route_kernel,
        grid=(n // t,),
        in_specs=[pl.BlockSpec((t, D_MODEL), lambda i: (i, 0)),
                  pl.BlockSpec((D_MODEL, LANES), lambda i: (0, 0)),
                  pl.BlockSpec((1, LANES), lambda i: (0, 0))],
        out_specs=[tile, tile, tile, pl.BlockSpec((1, LANES), lambda i: (0, 0))],
        out_shape=[jax.ShapeDtypeStruct((n, LANES), jnp.int32),
                   jax.ShapeDtypeStruct((n, LANES), F32),
                   jax.ShapeDtypeStruct((n, LANES), jnp.int32),
                   jax.ShapeDtypeStruct((1, LANES), jnp.int32)],
        scratch_shapes=[pltpu.VMEM((1, LANES), F32)],
        compiler_params=_cparams(("arbitrary",)),
        name="route",
    )(x2d, w_router_pad, b_router_pad)


ROW_TILE = D_MODEL // LANES


def _to_row_tiled(dst_ref, val):
    rows = val.shape[0]
    for c in range(ROW_TILE):
        dst_ref[pl.ds(c, rows, stride=ROW_TILE), :] = val[:, c * LANES:(c + 1) * LANES]


def _from_row_tiled(src_ref, rows):
    return jnp.concatenate([src_ref[pl.ds(c, rows, stride=ROW_TILE), :] for c in range(ROW_TILE)],
                           axis=1)


def _tile_copy(src_ref, src_row, dst_ref, dst_row, sem):
    return pltpu.make_async_copy(src_ref.at[pl.ds(pl.multiple_of(src_row * ROW_TILE, ROW_TILE), ROW_TILE), :],
                                 dst_ref.at[pl.ds(pl.multiple_of(dst_row * ROW_TILE, ROW_TILE), ROW_TILE), :],
                                 sem)


def _dispatch_kernel(dest_ref, x_ref, xs_ref, xt_ref, sems):
    t = x_ref.shape[0]
    step = pl.program_id(0)
    last = pl.num_programs(0) - 1
    slot = step % 2

    def drain(s):
        for _ in range(TOP_K):
            pltpu.make_async_copy(xt_ref.at[s], xs_ref.at[pl.ds(0, t * ROW_TILE), :], sems.at[s]).wait()

    @pl.when(step >= 2)
    def _():
        drain(slot)

    _to_row_tiled(xt_ref.at[slot], x_ref[...])

    def issue(i, carry):
        for k in range(TOP_K):
            _tile_copy(xt_ref.at[slot], i, xs_ref, dest_ref[0, 0, i * TOP_K + k],
                       sems.at[slot]).start(priority=k % 2)
        return carry

    lax.fori_loop(0, t, issue, 0, unroll=2)

    @pl.when(step == last)
    def _():
        @pl.when(last >= 1)
        def _():
            drain(1 - slot)
        drain(slot)


def _dispatch(x2d, dest3):
    n = x2d.shape[0]
    t = ROUTE_T
    return pl.pallas_call(
        _dispatch_kernel,
        grid=(n // t,),
        in_specs=[pl.BlockSpec((1, 1, t * TOP_K), lambda i: (i, 0, 0), memory_space=pltpu.SMEM),
                  pl.BlockSpec((t, D_MODEL), lambda i: (i, 0))],
        out_specs=pl.BlockSpec(memory_space=pl.ANY),
        out_shape=jax.ShapeDtypeStruct((n * TOP_K * ROW_TILE, LANES), F32),
        scratch_shapes=[pltpu.VMEM((2, t * ROW_TILE, LANES), F32), pltpu.SemaphoreType.DMA((2,))],
        compiler_params=_cparams(("arbitrary",)),
        name="dispatch",
    )(dest3, x2d)


def _expert_weight_copies(wu_hbm, wd_hbm, wu_f32, wd_f32, sems, layer, expert, slot):
    return (pltpu.make_async_copy(wu_hbm.at[layer, expert], wu_f32.at[slot], sems.at[slot, 0]),
            pltpu.make_async_copy(wd_hbm.at[layer, expert], wd_f32.at[slot], sems.at[slot, 1]))


def _expert_kernel(it_e, it_b, it_lo, it_hi, it_first, it_new, it_slot, it_next, n_items,
                   xs_ref, wu_hbm, bu_ref, wd_hbm, bd_ref, ys_ref,
                   wu_f32, wd_f32, wu_bf, wd_bf, sems, *, layer):
    del it_b
    j = pl.program_id(0)
    copies = functools.partial(_expert_weight_copies, wu_hbm, wd_hbm, wu_f32, wd_f32, sems, layer)

    @pl.when(j == 0)
    def _():
        for c in copies(it_e[0], 0):
            c.start()

    @pl.when(jnp.logical_and(j < n_items[0], it_new[j] == 1))
    def _():
        slot = it_slot[j]
        for c in copies(it_e[j], slot):
            c.wait()
        wu_bf[...] = wu_f32[slot].astype(BF16)
        wd_bf[...] = wd_f32[slot].astype(BF16)

        @pl.when(it_next[j] >= 0)
        def _():
            for c in copies(it_next[j], 1 - slot):
                c.start(priority=1)

    @pl.when(j < n_items[0])
    def _():
        xb = _from_row_tiled(xs_ref, MOE_BM).astype(BF16)
        bu = bu_ref[0, 0]
        acts = []
        for c0 in range(0, D_EXPERT, EXPERT_CHUNK):
            c1 = c0 + EXPERT_CHUNK
            hg = jnp.dot(xb, wu_bf[:, c0:c1], preferred_element_type=F32) + bu[:, c0:c1]
            hl = (jnp.dot(xb, wu_bf[:, D_EXPERT + c0:D_EXPERT + c1], preferred_element_type=F32)
                  + bu[:, D_EXPERT + c0:D_EXPERT + c1])
            hg = jnp.minimum(hg, SWIGLU_LIMIT)
            hl = jnp.clip(hl, -SWIGLU_LIMIT, SWIGLU_LIMIT)
            acts.append((hg * jax.nn.sigmoid(SWIGLU_ALPHA * hg) * (hl + 1.0)).astype(BF16))
        act = jnp.concatenate(acts, axis=1)
        bd = bd_ref[0, 0]
        row = lax.broadcasted_iota(jnp.int32, (MOE_BM, 1), 0)
        mine = jnp.logical_and(row >= it_lo[j], row < it_hi[j])

        def down_proj(first):
            for c0 in range(0, D_MODEL, EXPERT_CHUNK):
                c1 = c0 + EXPERT_CHUNK
                y = jnp.dot(act, wd_bf[:, c0:c1], preferred_element_type=F32) + bd[:, c0:c1]
                for s0 in range(c0, c1, LANES):
                    dst = ys_ref.at[pl.ds(s0 // LANES, MOE_BM, stride=ROW_TILE), :]
                    other = 0.0 if first else dst[...]
                    dst[...] = jnp.where(mine, y[:, s0 - c0:s0 - c0 + LANES], other)

        @pl.when(it_first[j] == 1)
        def _():
            down_proj(True)

        @pl.when(it_first[j] == 0)
        def _():
            down_proj(False)


def _experts(items, xs, w_up, b_up4, w_down, b_down4, layer):
    n_items = items[0].shape[0]
    row_blk = pl.BlockSpec((MOE_BM * ROW_TILE, LANES), lambda j, ie, ib, *rest: (
        ib[jnp.minimum(j, rest[-1][0] - 1)], 0))

    def cur(j, ni):
        return jnp.minimum(j, ni[0] - 1)

    def b_map(j, ie, ib, *rest):
        return (layer, ie[cur(j, rest[-1])], 0, 0)

    return pl.pallas_call(
        functools.partial(_expert_kernel, layer=layer),
        grid_spec=pltpu.PrefetchScalarGridSpec(
            num_scalar_prefetch=len(items),
            grid=(n_items,),
            in_specs=[row_blk,
                      pl.BlockSpec(memory_space=pl.ANY),
                      pl.BlockSpec((1, 1, 1, 2 * D_EXPERT), b_map),
                      pl.BlockSpec(memory_space=pl.ANY),
                      pl.BlockSpec((1, 1, 1, D_MODEL), b_map)],
            out_specs=row_blk,
            scratch_shapes=[pltpu.VMEM((2, D_MODEL, 2 * D_EXPERT), F32),
                            pltpu.VMEM((2, D_EXPERT, D_MODEL), F32),
                            pltpu.VMEM((D_MODEL, 2 * D_EXPERT), BF16),
                            pltpu.VMEM((D_EXPERT, D_MODEL), BF16),
                            pltpu.SemaphoreType.DMA((2, 2))]),
        out_shape=jax.ShapeDtypeStruct(xs.shape, F32),
        compiler_params=_cparams(("arbitrary",)),
        name="experts",
    )(*items, xs, w_up, b_up4, w_down, b_down4)


def _expert_items(counts, n_rows):
    n_items = n_rows // MOE_BM + N_EXPERTS
    end = jnp.cumsum(counts)
    start = end - counts
    first_b = start // MOE_BM
    nb = jnp.where(counts > 0, (end - 1) // MOE_BM - first_b + 1, 0)
    item_end = jnp.cumsum(nb)
    item_start = item_end - nb
    total = item_end[-1]
    jc = jnp.minimum(jnp.arange(n_items, dtype=jnp.int32), total - 1)
    it_e = jnp.sum((item_end[None, :] <= jc[:, None]).astype(jnp.int32), axis=1)
    sel = it_e[:, None] == jnp.arange(N_EXPERTS, dtype=jnp.int32)[None, :]
    pick = lambda v: jnp.sum(jnp.where(sel, v[None, :], 0), axis=1)
    it_b = pick(first_b) + jc - pick(item_start)
    it_lo = jnp.maximum(pick(start), it_b * MOE_BM) - it_b * MOE_BM
    it_hi = jnp.minimum(pick(end), (it_b + 1) * MOE_BM) - it_b * MOE_BM
    prev_b = jnp.concatenate([jnp.full((1,), -1, jnp.int32), it_b[:-1]])
    it_first = (it_b != prev_b).astype(jnp.int32)
    prev_e = jnp.concatenate([jnp.full((1,), -1, jnp.int32), it_e[:-1]])
    it_new = (it_e != prev_e).astype(jnp.int32)
    ar = jnp.arange(N_EXPERTS, dtype=jnp.int32)
    used = counts > 0
    slot_e = (jnp.cumsum(used.astype(jnp.int32)) - 1) % 2
    later = jnp.logical_and(ar[None, :] > ar[:, None], used[None, :])
    next_e = jnp.min(jnp.where(later, ar[None, :], N_EXPERTS), axis=1)
    next_e = jnp.where(next_e == N_EXPERTS, -1, next_e)
    i32 = lambda v: v.astype(jnp.int32)
    return (i32(it_e), i32(it_b), i32(it_lo), i32(it_hi), it_first, it_new, i32(pick(slot_e)),
            i32(pick(next_e)), i32(total).reshape(1)), start


def _combine_kernel(dest_ref, dest_next_ref, gate_ref, x_ref, ys_ref, g_ref, b_ref, o_ref, buf, sems):
    t = x_ref.shape[0]
    step = pl.program_id(0)
    slot = step % 2

    def gather(idx_ref, s):
        def issue(i, carry):
            for k in range(TOP_K):
                _tile_copy(ys_ref, idx_ref[0, 0, i * TOP_K + k], buf.at[s, k], i,
                           sems.at[s]).start(priority=k % 2)
            return carry

        lax.fori_loop(0, t, issue, 0, unroll=2)

    @pl.when(step == 0)
    def _():
        gather(dest_ref, slot)

    @pl.when(step + 1 < pl.num_programs(0))
    def _():
        gather(dest_next_ref, 1 - slot)

    for k in range(TOP_K):
        pltpu.make_async_copy(ys_ref.at[pl.ds(0, t * ROW_TILE), :], buf.at[slot, k], sems.at[slot]).wait()
    gate = gate_ref[...]
    ffn = gate[:, 0:1] * _from_row_tiled(buf.at[slot, 0], t)
    for k in range(1, TOP_K):
        ffn = ffn + gate[:, k:k + 1] * _from_row_tiled(buf.at[slot, k], t)
    o_ref[...] = _layer_norm(DEEPNORM_ALPHA * x_ref[...] + ffn, g_ref[...], b_ref[...])


def _combine(dest3, gate, x2d, ys, ln_g, ln_b):
    n = x2d.shape[0]
    t = ROUTE_T
    const2 = lambda i: (0, 0)
    last = n // t - 1
    return pl.pallas_call(
        _combine_kernel,
        grid=(n // t,),
        in_specs=[pl.BlockSpec((1, 1, t * TOP_K), lambda i: (i, 0, 0), memory_space=pltpu.SMEM),
                  pl.BlockSpec((1, 1, t * TOP_K), lambda i: (jnp.minimum(i + 1, last), 0, 0),
                               memory_space=pltpu.SMEM),
                  pl.BlockSpec((t, LANES), lambda i: (i, 0)),
                  pl.BlockSpec((t, D_MODEL), lambda i: (i, 0)),
                  pl.BlockSpec(memory_space=pl.ANY),
                  pl.BlockSpec((1, D_MODEL), const2),
                  pl.BlockSpec((1, D_MODEL), const2)],
        out_specs=pl.BlockSpec((t, D_MODEL), lambda i: (i, 0)),
        out_shape=jax.ShapeDtypeStruct((n, D_MODEL), F32),
        scratch_shapes=[pltpu.VMEM((2, TOP_K, t * ROW_TILE, LANES), F32),
                        pltpu.SemaphoreType.DMA((2,))],
        compiler_params=_cparams(("arbitrary",)),
        name="combine_ln2",
    )(dest3, dest3, gate, x2d, ys, ln_g, ln_b)


def _a_head_perm():
    grp = A_HEADS // A_KV_HEADS
    order = []
    for j in range(grp):
        order += [j, grp + j]
    return np.concatenate([np.arange(h * HEAD_DIM, (h + 1) * HEAD_DIM) for h in order])


def _moe(x1, layer, w_router, b_router, w_up, b_up, w_down, b_down, ln_g, ln_b):
    n = x1.shape[0]
    wr = jnp.pad(w_router, ((0, 0), (0, LANES - N_EXPERTS)))
    br = jnp.pad(b_router, (0, LANES - N_EXPERTS), constant_values=NEG_INF).reshape(1, LANES)
    eidx, gate, rank, cnt = _route(x1, wr, br)
    items, start = _expert_items(cnt[0, :N_EXPERTS], n * TOP_K)
    e4 = eidx[:, :TOP_K]
    sel = e4[:, :, None] == jnp.arange(N_EXPERTS, dtype=jnp.int32)
    dest = jnp.sum(jnp.where(sel, start, 0), axis=-1) + rank[:, :TOP_K]
    dest3 = dest.reshape(n // ROUTE_T, 1, ROUTE_T * TOP_K).astype(jnp.int32)
    xs = _dispatch(x1, dest3)
    lead = (DEPTH, N_EXPERTS, 1)
    ys = _experts(items, xs, w_up, b_up.reshape(lead + (-1,)), w_down, b_down.reshape(lead + (-1,)), layer)
    return _combine(dest3, gate, x1, ys, ln_g.reshape(1, -1), ln_b.reshape(1, -1))


def kernel(x, w_in, a_sink, lambda_q1, lambda_k1, lambda_q2, lambda_k2, diff_norm_g, na_rpb,
           w_branch, w_out, ln1_g, ln1_b, w_router, b_router, w_up, b_up, w_down, b_down,
           ln2_g, ln2_b):
    bsz, seq, d = x.shape
    n = bsz * seq
    rows = seq // GRID_W
    perm = _a_head_perm()
    slopes_b = jnp.asarray(_ALIBI[A_HEADS:], F32)
    xcur = x.reshape(n, d)
    for l in range(DEPTH):
        w = w_in[l]
        qscale = HEAD_DIM ** -0.5
        w_qkv = jnp.concatenate([
            w[:, :512][:, perm] * qscale, w[:, 512:768],
            w[:, 768:1280] * qscale, w[:, 1280:2304],
            w[:, 2304:2816] * qscale, w[:, 2816:QKV_W]], axis=1).astype(BF16)
        w_gate = w[:, QKV_W:].astype(BF16)
        w_br = jnp.stack([w_branch[l, 0][perm], w_branch[l, 1], w_branch[l, 2]]).astype(BF16)
        lam_init = 0.8 - 0.6 * math.exp(-0.3 * l)
        lamv = jnp.stack([lambda_q1[l], lambda_k1[l], lambda_q2[l], lambda_k2[l]]).astype(F32)

        proj = _inproj(xcur, w_qkv)
        oa = _attn_a(proj, a_sink[l].astype(F32), bsz, seq)
        ob = _attn_b(proj, slopes_b, lamv, diff_norm_g[l].reshape(1, -1).astype(F32), bsz, seq, lam_init)
        oc = _attn_c(proj, _na_bias_table(na_rpb[l], rows), bsz, seq)
        x1 = _merge(xcur, oa, ob, oc, w_gate, w_br, w_out[l].astype(BF16),
                    ln1_g[l].reshape(1, -1), ln1_b[l].reshape(1, -1))
        xcur = _moe(x1, l, w_router[l], b_router[l], w_up, b_up, w_down, b_down,
                    ln2_g[l], ln2_b[l])
    return xcur.reshape(bsz, seq, d)
```

```python
import functools
import math

import numpy as np
import jax
import jax.numpy as jnp
from jax import lax
from jax.experimental import pallas as pl
from jax.experimental.pallas import tpu as pltpu

F32 = jnp.float32
BF16 = jnp.bfloat16

D_MODEL = 1024
DEPTH = 2
HEAD_DIM = 64
A_HEADS = 8
A_KV_HEADS = 2
WINDOW = 128
B_HEADS = 4
C_HEADS = 8
GRID_W = 64
NA_ROWS = 8
NA_COLS = 16
MIX_W = 512
N_BRANCH = 3
N_EXPERTS = 32
TOP_K = 4
D_EXPERT = 1024
SWIGLU_LIMIT = 7.0
SWIGLU_ALPHA = 1.702
LN_EPS = 1e-5
NEG_INF = -1e30
DEEPNORM_ALPHA = (2 * DEPTH) ** 0.25

LANES = 128
QKV_W = 3840
GATE_W = N_BRANCH * D_MODEL
A_Q_BLK, A_K_BLK, A_V_BLK = 0, 4, 5
B_Q_BLK, B_K_BLK, B_V_BLK = 6, 10, 14
C_Q_BLK, C_K_BLK, C_V_BLK = 18, 22, 26

A_BAND = 3 * WINDOW
B_TQ = 256
NA_QROWS = 4
NA_SLAB = NA_QROWS + NA_ROWS
MOE_BM = 512
EXPERT_CHUNK = 256
ROUTE_T = 256
VMEM_LIMIT = 56 * 1024 * 1024

_ALIBI = [float(2.0 ** (-8.0 * (i + 1) / (A_HEADS + B_HEADS))) for i in range(A_HEADS + B_HEADS)]


def _cparams(sem):
    return pltpu.CompilerParams(dimension_semantics=sem, vmem_limit_bytes=VMEM_LIMIT)


def _layer_norm(y, g, b):
    mu = jnp.mean(y, axis=-1, keepdims=True)
    yc = y - mu
    var = jnp.mean(yc * yc, axis=-1, keepdims=True)
    return yc * lax.rsqrt(var + LN_EPS) * g + b


def _inproj_kernel(x_ref, w_ref, o_ref, *, chunk):
    xb = x_ref[...].astype(BF16)
    for c in range(QKV_W // chunk):
        sl = slice(c * chunk, (c + 1) * chunk)
        o_ref[:, sl] = jnp.dot(xb, w_ref[:, sl], preferred_element_type=F32).astype(BF16)


def _inproj(x2d, w_qkv):
    n = x2d.shape[0]
    tm = 512
    return pl.pallas_call(
        functools.partial(_inproj_kernel, chunk=768),
        grid=(n // tm,),
        in_specs=[pl.BlockSpec((tm, D_MODEL), lambda i: (i, 0)),
                  pl.BlockSpec((D_MODEL, QKV_W), lambda i: (0, 0))],
        out_specs=pl.BlockSpec((tm, QKV_W), lambda i: (i, 0)),
        out_shape=jax.ShapeDtypeStruct((n, QKV_W), BF16),
        compiler_params=_cparams(("arbitrary",)),
        name="inproj",
    )(x2d, w_qkv)


def _attn_a_kernel(sink_ref, q_ref, k_ref, v_ref, o_ref, bias_ref, sink_tab, *, seq):
    nblk = seq // WINDOW
    heads = [j + 4 * hf for j in range(4) for hf in range(2)]
    lane = lax.broadcasted_iota(jnp.int32, (WINDOW, LANES), 1)
    low = lane < HEAD_DIM
    dn = (((1,), (1,)), ((), ()))

    @pl.when(pl.program_id(0) == 0)
    def _():
        kq = (lax.broadcasted_iota(jnp.int32, (WINDOW, A_BAND), 1)
              - lax.broadcasted_iota(jnp.int32, (WINDOW, A_BAND), 0))
        for c in range(3):
            dist = jnp.abs(kq - c * WINDOW).astype(F32)
            for i, h in enumerate(heads):
                bias_ref[c, i * WINDOW:(i + 1) * WINDOW, :] = jnp.where(
                    dist <= float(WINDOW), _ALIBI[h] * dist, -NEG_INF)
        for i, h in enumerate(heads):
            sink_tab[i * WINDOW:(i + 1) * WINDOW, :] = jnp.full((WINDOW, LANES), sink_ref[h], F32)

    sink_col = jnp.max(sink_tab[...], axis=-1, keepdims=True)

    def body(n, carry):
        q0 = pl.multiple_of(n * WINDOW, WINDOW)
        start = pl.multiple_of(jnp.clip((n - 1) * WINDOW, 0, seq - A_BAND), WINDOW)
        cfg = jnp.where(n == 0, 0, jnp.where(n == nblk - 1, 2, 1))
        kb = k_ref[pl.ds(start, A_BAND), :]
        vb = v_ref[pl.ds(start, A_BAND), :]
        pieces = []
        for j in range(4):
            qt = q_ref[pl.ds(q0, WINDOW), j * LANES:(j + 1) * LANES]
            zero = jnp.zeros_like(qt)
            lhs = jnp.concatenate([jnp.where(low, qt, zero), jnp.where(low, zero, qt)], axis=0)
            pieces.append(lax.dot_general(lhs, kb, dn, preferred_element_type=F32))
        t = jnp.concatenate(pieces, axis=0) - bias_ref[cfg]
        m = jnp.maximum(jnp.max(t, axis=-1, keepdims=True), sink_col)
        e = jnp.exp(t - m)
        den = jnp.sum(e, axis=-1, keepdims=True) + jnp.exp(sink_col - m)
        p = (e * (1.0 / den)).astype(BF16)
        pv = jnp.dot(p, vb, preferred_element_type=F32)
        for j in range(4):
            o = jnp.where(low, pv[2 * j * WINDOW:(2 * j + 1) * WINDOW],
                          pv[(2 * j + 1) * WINDOW:(2 * j + 2) * WINDOW])
            o_ref[pl.ds(q0, WINDOW), j * LANES:(j + 1) * LANES] = o.astype(BF16)
        return carry

    lax.fori_loop(0, nblk, body, 0, unroll=4)


def _attn_a(proj, sink, bsz, seq):
    return pl.pallas_call(
        functools.partial(_attn_a_kernel, seq=seq),
        grid=(bsz,),
        in_specs=[pl.BlockSpec(memory_space=pltpu.SMEM),
                  pl.BlockSpec((seq, 4 * LANES), lambda b: (b, A_Q_BLK // 4)),
                  pl.BlockSpec((seq, LANES), lambda b: (b, A_K_BLK)),
                  pl.BlockSpec((seq, LANES), lambda b: (b, A_V_BLK))],
        out_specs=pl.BlockSpec((seq, MIX_W), lambda b: (b, 0)),
        out_shape=jax.ShapeDtypeStruct((bsz * seq, MIX_W), BF16),
        scratch_shapes=[pltpu.VMEM((3, A_HEADS * WINDOW, A_BAND), F32),
                        pltpu.VMEM((A_HEADS * WINDOW, LANES), F32)],
        compiler_params=_cparams(("arbitrary",)),
        name="attn_a",
    )(sink, proj, proj, proj)


def _attn_b_kernel(slope_ref, lamv_ref, gain_ref, q_ref, k_ref, v_ref, o_ref, bias_ref, *,
                   seq, lam_init):
    nblk = seq // B_TQ
    slope = slope_ref[pl.program_id(0)]

    @pl.when(pl.program_id(1) == 0)
    def _():
        ji = (lax.broadcasted_iota(jnp.int32, (B_TQ, B_TQ), 1)
              - lax.broadcasted_iota(jnp.int32, (B_TQ, B_TQ), 0))
        for dd in range(2 * nblk - 1):
            bias_ref[dd] = slope * jnp.abs(ji + (dd - (nblk - 1)) * B_TQ).astype(F32)

    lv = lamv_ref[...]
    lam = (jnp.exp(jnp.sum(lv[0:1] * lv[1:2], axis=-1, keepdims=True))
           - jnp.exp(jnp.sum(lv[2:3] * lv[3:4], axis=-1, keepdims=True)) + lam_init)
    scale = gain_ref[...] * (1.0 - lam_init)
    lane = lax.broadcasted_iota(jnp.int32, (B_TQ, LANES), 1)
    low = lane < HEAD_DIM
    kall = k_ref[...]
    vall = v_ref[...]
    dn = (((1,), (1,)), ((), ()))

    def body(n, carry):
        q0 = pl.multiple_of(n * B_TQ, B_TQ)
        qt = q_ref[pl.ds(q0, B_TQ), :]
        zero = jnp.zeros_like(qt)
        bias = jnp.concatenate([bias_ref[kc - n + (nblk - 1)] for kc in range(nblk)], axis=1)
        es, ls = [], []
        for mp in range(2):
            qm = jnp.where(low, qt, zero) if mp == 0 else jnp.where(low, zero, qt)
            s = lax.dot_general(qm, kall, dn, preferred_element_type=F32) - bias
            m = jnp.max(s, axis=-1, keepdims=True)
            e = jnp.exp(s - m)
            ls.append(jnp.sum(e, axis=-1, keepdims=True))
            es.append(e.astype(BF16))
        a = es[0] * (1.0 / ls[0]).astype(BF16) - es[1] * (lam / ls[1]).astype(BF16)
        o = jnp.dot(a, vall, preferred_element_type=F32)
        o = o * lax.rsqrt(jnp.mean(o * o, axis=-1, keepdims=True) + LN_EPS)
        o_ref[pl.ds(q0, B_TQ), :] = (o * scale).astype(BF16)
        return carry

    lax.fori_loop(0, nblk, body, 0, unroll=2)


def _attn_b(proj, slopes_b, lamv, gain, bsz, seq, lam_init):
    nblk = seq // B_TQ
    return pl.pallas_call(
        functools.partial(_attn_b_kernel, seq=seq, lam_init=lam_init),
        grid=(B_HEADS, bsz),
        in_specs=[pl.BlockSpec(memory_space=pltpu.SMEM),
                  pl.BlockSpec((4, HEAD_DIM), lambda h, b: (0, 0)),
                  pl.BlockSpec((1, 2 * HEAD_DIM), lambda h, b: (0, 0)),
                  pl.BlockSpec((seq, LANES), lambda h, b: (b, B_Q_BLK + h)),
                  pl.BlockSpec((seq, LANES), lambda h, b: (b, B_K_BLK + h)),
                  pl.BlockSpec((seq, LANES), lambda h, b: (b, B_V_BLK + h))],
        out_specs=pl.BlockSpec((seq, LANES), lambda h, b: (b, h)),
        out_shape=jax.ShapeDtypeStruct((bsz * seq, MIX_W), BF16),
        scratch_shapes=[pltpu.VMEM((2 * nblk - 1, B_TQ, B_TQ), F32)],
        compiler_params=_cparams(("arbitrary", "arbitrary")),
        name="attn_b",
    )(slopes_b, lamv, gain, proj, proj, proj)


def _attn_c_kernel(bias_ref, q_ref, k_ref, v_ref, o_ref, *, rows):
    qtok = NA_QROWS * GRID_W
    ktok = NA_SLAB * GRID_W
    ngrp = rows // NA_QROWS
    lane = lax.broadcasted_iota(jnp.int32, (qtok, LANES), 1)
    low = lane < HEAD_DIM
    dn = (((1,), (1,)), ((), ()))

    def body(g, carry):
        slab0 = jnp.clip(g * NA_QROWS - NA_ROWS // 2, 0, rows - NA_SLAB)
        k0 = pl.multiple_of(slab0 * GRID_W, GRID_W)
        q0 = pl.multiple_of(g * qtok, qtok)
        cfg = jnp.where(g == 0, 0, jnp.where(g == ngrp - 1, 2, 1))
        qr = q_ref[pl.ds(q0, qtok), :]
        zero = jnp.zeros_like(qr)
        lhs = jnp.concatenate([jnp.where(low, qr, zero), jnp.where(low, zero, qr)], axis=0)
        ks = k_ref[pl.ds(k0, ktok), :]
        vs = v_ref[pl.ds(k0, ktok), :]
        s = lax.dot_general(lhs, ks, dn, preferred_element_type=F32) + bias_ref[0, cfg]
        m = jnp.max(s, axis=-1, keepdims=True)
        e = jnp.exp(s - m)
        p = (e * (1.0 / jnp.sum(e, axis=-1, keepdims=True))).astype(BF16)
        pv = jnp.dot(p, vs, preferred_element_type=F32)
        o = jnp.where(low, pv[:qtok], pv[qtok:])
        o_ref[pl.ds(q0, qtok), :] = o.astype(BF16)
        return carry

    lax.fori_loop(0, ngrp, body, 0, unroll=True)


def _attn_c(proj, bias_tab, bsz, seq):
    rows = seq // GRID_W
    npair = C_HEADS // 2
    return pl.pallas_call(
        functools.partial(_attn_c_kernel, rows=rows),
        grid=(npair, bsz),
        in_specs=[pl.BlockSpec((1,) + bias_tab.shape[1:], lambda p, b: (p, 0, 0, 0)),
                  pl.BlockSpec((seq, LANES), lambda p, b: (b, C_Q_BLK + p)),
                  pl.BlockSpec((seq, LANES), lambda p, b: (b, C_K_BLK + p)),
                  pl.BlockSpec((seq, LANES), lambda p, b: (b, C_V_BLK + p))],
        out_specs=pl.BlockSpec((seq, LANES), lambda p, b: (b, p)),
        out_shape=jax.ShapeDtypeStruct((bsz * seq, MIX_W), BF16),
        compiler_params=_cparams(("arbitrary", "arbitrary")),
        name="attn_c",
    )(bias_tab, proj, proj, proj)


def _na_bias_table(rpb, rows):
    assert rows % NA_QROWS == 0 and rows >= NA_SLAB + NA_QROWS and rows >= NA_ROWS
    qc = np.arange(GRID_W)[:, None]
    kc = np.arange(GRID_W)[None, :]
    cs = np.clip(qc - NA_COLS // 2, 0, GRID_W - NA_COLS)
    valid = (kc >= cs) & (kc < cs + NA_COLS)
    off = GRID_W - NA_COLS
    ndr = 2 * NA_ROWS - 1
    rp = jnp.pad(rpb.astype(F32), ((0, 0), (0, 0), (off, off)))
    toep = jnp.stack([rp[:, :, off + NA_COLS - 1 - c:off + NA_COLS - 1 - c + GRID_W]
                      for c in range(GRID_W)], axis=1)
    toep = jnp.where(valid[None, :, None, :], toep, NEG_INF).reshape(C_HEADS, GRID_W, ndr * GRID_W)
    wide = jnp.pad(toep, ((0, 0), (0, 0), (NA_SLAB * GRID_W, NA_SLAB * GRID_W)),
                   constant_values=NEG_INF)
    ngrp = rows // NA_QROWS
    cfgs = []
    for g in (0, 1, ngrp - 1):
        slab0 = int(np.clip(g * NA_QROWS - NA_ROWS // 2, 0, rows - NA_SLAB))
        qrows = []
        for a in range(NA_QROWS):
            r = g * NA_QROWS + a
            rs = int(np.clip(r - NA_ROWS // 2, 0, rows - NA_ROWS))
            dr0 = slab0 - r + NA_ROWS - 1
            win = wide[:, :, (NA_SLAB + dr0) * GRID_W:(2 * NA_SLAB + dr0) * GRID_W]
            inside = np.repeat([rs <= slab0 + u < rs + NA_ROWS for u in range(NA_SLAB)], GRID_W)
            qrows.append(jnp.where(inside[None, None, :], win, NEG_INF))
        cfgs.append(jnp.concatenate(qrows, axis=1))
    tab = jnp.stack(cfgs, axis=1)
    tab = tab.reshape(C_HEADS // 2, 2, 3, NA_QROWS * GRID_W, NA_SLAB * GRID_W)
    return tab.transpose(0, 2, 1, 3, 4).reshape(C_HEADS // 2, 3, 2 * NA_QROWS * GRID_W,
                                                NA_SLAB * GRID_W)


def _merge_kernel(x_ref, oa_ref, ob_ref, oc_ref, wg_ref, wbr_ref, wout_ref, g_ref, b_ref, o_ref):
    x = x_ref[...]
    xb = x.astype(BF16)
    merged = None
    for i, br_ref in enumerate((oa_ref, ob_ref, oc_ref)):
        gate = jax.nn.sigmoid(jnp.dot(xb, wg_ref[:, i * D_MODEL:(i + 1) * D_MODEL],
                                      preferred_element_type=F32))
        br = jnp.dot(br_ref[...], wbr_ref[i], preferred_element_type=F32)
        merged = gate * br if merged is None else merged + gate * br
    mix = jnp.dot(merged.astype(BF16), wout_ref[...], preferred_element_type=F32)
    o_ref[...] = _layer_norm(DEEPNORM_ALPHA * x + mix, g_ref[...], b_ref[...])


def _merge(x2d, oa, ob, oc, w_gate, w_br, w_out, ln_g, ln_b):
    n = x2d.shape[0]
    tm = 512
    const2 = lambda i: (0, 0)
    return pl.pallas_call(
        _merge_kernel,
        grid=(n // tm,),
        in_specs=[pl.BlockSpec((tm, D_MODEL), lambda i: (i, 0)),
                  pl.BlockSpec((tm, MIX_W), lambda i: (i, 0)),
                  pl.BlockSpec((tm, MIX_W), lambda i: (i, 0)),
                  pl.BlockSpec((tm, MIX_W), lambda i: (i, 0)),
                  pl.BlockSpec((D_MODEL, GATE_W), const2),
                  pl.BlockSpec((N_BRANCH, MIX_W, D_MODEL), lambda i: (0, 0, 0)),
                  pl.BlockSpec((D_MODEL, D_MODEL), const2),
                  pl.BlockSpec((1, D_MODEL), const2),
                  pl.BlockSpec((1, D_MODEL), const2)],
        out_specs=pl.BlockSpec((tm, D_MODEL), lambda i: (i, 0)),
        out_shape=jax.ShapeDtypeStruct((n, D_MODEL), F32),
        compiler_params=_cparams(("arbitrary",)),
        name="merge_ln1",
    )(x2d, oa, ob, oc, w_gate, w_br, w_out, ln_g, ln_b)


def _route_kernel(x_ref, w_ref, b_ref, eidx_ref, gate_ref, rank_ref, cnt_ref, carry_ref):
    t = x_ref.shape[0]

    @pl.when(pl.program_id(0) == 0)
    def _():
        carry_ref[...] = jnp.zeros_like(carry_ref)

    logits = jnp.dot(x_ref[...], w_ref[...], preferred_element_type=F32,
                     precision=lax.Precision.HIGHEST) + b_ref[...]
    lane = lax.broadcasted_iota(jnp.int32, (t, LANES), 1).astype(F32)
    work = logits
    sels, vals, idxs = [], [], []
    for _ in range(TOP_K):
        m = jnp.max(work, axis=-1, keepdims=True)
        idx = jnp.min(jnp.where(work == m, lane, float(LANES)), axis=-1, keepdims=True)
        sel = lane == idx
        work = jnp.where(sel, -jnp.inf, work)
        sels.append(sel)
        vals.append(m)
        idxs.append(idx)
    ex = [jnp.exp(v - vals[0]) for v in vals]
    den = ex[0] + ex[1] + ex[2] + ex[3]
    onehot = jnp.zeros((t, LANES), F32)
    for sel in sels:
        onehot = jnp.where(sel, 1.0, onehot)
    ri = lax.broadcasted_iota(jnp.int32, (t, t), 0)
    ci = lax.broadcasted_iota(jnp.int32, (t, t), 1)
    tri = jnp.where(ci < ri, 1.0, 0.0).astype(BF16)
    before = jnp.dot(tri, onehot.astype(BF16), preferred_element_type=F32) + carry_ref[...]
    eidx = jnp.zeros((t, LANES), F32)
    gate = jnp.zeros((t, LANES), F32)
    rank = jnp.zeros((t, LANES), F32)
    for k in range(TOP_K):
        rk = jnp.sum(jnp.where(sels[k], before, 0.0), axis=-1, keepdims=True)
        eidx = jnp.where(lane == float(k), idxs[k], eidx)
        gate = jnp.where(lane == float(k), ex[k] / den, gate)
        rank = jnp.where(lane == float(k), rk, rank)
    eidx_ref[...] = eidx.astype(jnp.int32)
    gate_ref[...] = gate
    rank_ref[...] = rank.astype(jnp.int32)
    carry_ref[...] += jnp.sum(onehot, axis=0, keepdims=True)
    cnt_ref[...] = carry_ref[...].astype(jnp.int32)


def _route(x2d, w_router_pad, b_router_pad):
    n = x2d.shape[0]
    t = ROUTE_T
    tile = pl.BlockSpec((t, LANES), lambda i: (i, 0))
    return pl.pallas_call(
        _route_kernel,
        grid=(n // t,),
        in_specs=[pl.BlockSpec((t, D_MODEL), lambda i: (i, 0)),
                  pl.BlockSpec((D_MODEL, LANES), lambda i: (0, 0)),
                  pl.BlockSpec((1, LANES), lambda i: (0, 0))],
        out_specs=[tile, tile, tile, pl.BlockSpec((1, LANES), lambda i: (0, 0))],
        out_shape=[jax.ShapeDtypeStruct((n, LANES), jnp.int32),
                   jax.ShapeDtypeStruct((n, LANES), F32),
                   jax.ShapeDtypeStruct((n, LANES), jnp.int32),
                   jax.ShapeDtypeStruct((1, LANES), jnp.int32)],
        scratch_shapes=[pltpu.VMEM((1, LANES), F32)],
        compiler_params=_cparams(("arbitrary",)),
        name="route",
    )(x2d, w_router_pad, b_router_pad)


ROW_TILE = D_MODEL // LANES


def _to_row_tiled(dst_ref, val):
    rows = val.shape[0]
    for c in range(ROW_TILE):
        dst_ref[pl.ds(c, rows, stride=ROW_TILE), :] = val[:, c * LANES:(c + 1) * LANES]


def _from_row_tiled(src_ref, rows):
    return jnp.concatenate([src_ref[pl.ds(c, rows, stride=ROW_TILE), :] for c in range(ROW_TILE)],
                           axis=1)


def _tile_copy(src_ref, src_row, dst_ref, dst_row, sem):
    return pltpu.make_async_copy(src_ref.at[pl.ds(pl.multiple_of(src_row * ROW_TILE, ROW_TILE), ROW_TILE), :],
                                 dst_ref.at[pl.ds(pl.multiple_of(dst_row * ROW_TILE, ROW_TILE), ROW_TILE), :],
                                 sem)


def _dispatch_kernel(dest_ref, x_ref, xs_ref, xt_ref, sems):
    t = x_ref.shape[0]
    step = pl.program_id(0)
    last = pl.num_programs(0) - 1
    slot = step % 2

    def drain(s):
        for _ in range(TOP_K):
            pltpu.make_async_copy(xt_ref.at[s], xs_ref.at[pl.ds(0, t * ROW_TILE), :], sems.at[s]).wait()

    @pl.when(step >= 2)
    def _():
        drain(slot)

    _to_row_tiled(xt_ref.at[slot], x_ref[...])

    def issue(i, carry):
        for k in range(TOP_K):
            _tile_copy(xt_ref.at[slot], i, xs_ref, dest_ref[0, 0, i * TOP_K + k],
                       sems.at[slot]).start(priority=k % 2)
        return carry

    lax.fori_loop(0, t, issue, 0, unroll=2)

    @pl.when(step == last)
    def _():
        @pl.when(last >= 1)
        def _():
            drain(1 - slot)
        drain(slot)


def _dispatch(x2d, dest3):
    n = x2d.shape[0]
    t = ROUTE_T
    return pl.pallas_call(
        _dispatch_kernel,
        grid=(n // t,),
        in_specs=[pl.BlockSpec((1, 1, t * TOP_K), lambda i: (i, 0, 0), memory_space=pltpu.SMEM),
                  pl.BlockSpec((t, D_MODEL), lambda i: (i, 0))],
        out_specs=pl.BlockSpec(memory_space=pl.ANY),
        out_shape=jax.ShapeDtypeStruct((n * TOP_K * ROW_TILE, LANES), F32),
        scratch_shapes=[pltpu.VMEM((2, t * ROW_TILE, LANES), F32), pltpu.SemaphoreType.DMA((2,))],
        compiler_params=_cparams(("arbitrary",)),
        name="dispatch",
    )(dest3, x2d)


def _expert_weight_copies(wu_hbm, wd_hbm, wu_f32, wd_f32, sems, layer, expert, slot):
    return (pltpu.make_async_copy(wu_hbm.at[layer, expert], wu_f32.at[slot], sems.at[slot, 0]),
            pltpu.make_async_copy(wd_hbm.at[layer, expert], wd_f32.at[slot], sems.at[slot, 1]))


def _expert_kernel(it_e, it_b, it_lo, it_hi, it_first, it_new, it_slot, it_next, n_items,
                   xs_ref, wu_hbm, bu_ref, wd_hbm, bd_ref, ys_ref,
                   wu_f32, wd_f32, wu_bf, wd_bf, sems, *, layer):
    del it_b
    j = pl.program_id(0)
    copies = functools.partial(_expert_weight_copies, wu_hbm, wd_hbm, wu_f32, wd_f32, sems, layer)

    @pl.when(j == 0)
    def _():
        for c in copies(it_e[0], 0):
            c.start()

    @pl.when(jnp.logical_and(j < n_items[0], it_new[j] == 1))
    def _():
        slot = it_slot[j]
        for c in copies(it_e[j], slot):
            c.wait()
        wu_bf[...] = wu_f32[slot].astype(BF16)
        wd_bf[...] = wd_f32[slot].astype(BF16)

        @pl.when(it_next[j] >= 0)
        def _():
            for c in copies(it_next[j], 1 - slot):
                c.start(priority=1)

    def ffn(r0, nr):
        def rows_of(ref, c):
            return ref.at[pl.ds(r0 * ROW_TILE + c, nr, stride=ROW_TILE), :]

        xb = jnp.concatenate([rows_of(xs_ref, c)[...] for c in range(ROW_TILE)], axis=1).astype(BF16)
        bu = bu_ref[0, 0]
        acts = []
        for c0 in range(0, D_EXPERT, EXPERT_CHUNK):
            c1 = c0 + EXPERT_CHUNK
            hg = jnp.dot(xb, wu_bf[:, c0:c1], preferred_element_type=F32) + bu[:, c0:c1]
            hl = (jnp.dot(xb, wu_bf[:, D_EXPERT + c0:D_EXPERT + c1], preferred_element_type=F32)
                  + bu[:, D_EXPERT + c0:D_EXPERT + c1])
            hg = jnp.minimum(hg, SWIGLU_LIMIT)
            hl = jnp.clip(hl, -SWIGLU_LIMIT, SWIGLU_LIMIT)
            acts.append((hg * jax.nn.sigmoid(SWIGLU_ALPHA * hg) * (hl + 1.0)).astype(BF16))
        act = jnp.concatenate(acts, axis=1)
        bd = bd_ref[0, 0]
        row = lax.broadcasted_iota(jnp.int32, (nr, 1), 0) + r0
        mine = jnp.logical_and(row >= it_lo[j], row < it_hi[j])

        def down_proj(first):
            for c0 in range(0, D_MODEL, EXPERT_CHUNK):
                c1 = c0 + EXPERT_CHUNK
                y = jnp.dot(act, wd_bf[:, c0:c1], preferred_element_type=F32) + bd[:, c0:c1]
                for s0 in range(c0, c1, LANES):
                    dst = rows_of(ys_ref, s0 // LANES)
                    other = 0.0 if first else dst[...]
                    dst[...] = jnp.where(mine, y[:, s0 - c0:s0 - c0 + LANES], other)

        @pl.when(it_first[j] == 1)
        def _():
            down_proj(True)
            if nr < MOE_BM:
                other0 = (MOE_BM - nr - r0) * ROW_TILE
                ys_ref[other0:other0 + nr * ROW_TILE, :] = jnp.zeros((nr * ROW_TILE, LANES), F32)

        @pl.when(it_first[j] == 0)
        def _():
            down_proj(False)

    half = MOE_BM // 2
    valid = j < n_items[0]
    low_only = it_hi[j] <= half
    high_only = it_lo[j] >= half

    @pl.when(jnp.logical_and(valid, low_only))
    def _():
        ffn(0, half)

    @pl.when(jnp.logical_and(valid, high_only))
    def _():
        ffn(half, half)

    @pl.when(jnp.logical_and(valid, jnp.logical_not(jnp.logical_or(low_only, high_only))))
    def _():
        ffn(0, MOE_BM)


def _experts(items, xs, w_up, b_up4, w_down, b_down4, layer):
    n_items = items[0].shape[0]
    row_blk = pl.BlockSpec((MOE_BM * ROW_TILE, LANES), lambda j, ie, ib, *rest: (
        ib[jnp.minimum(j, rest[-1][0] - 1)], 0))

    def cur(j, ni):
        return jnp.minimum(j, ni[0] - 1)

    def b_map(j, ie, ib, *rest):
        return (layer, ie[cur(j, rest[-1])], 0, 0)

    return pl.pallas_call(
        functools.partial(_expert_kernel, layer=layer),
        grid_spec=pltpu.PrefetchScalarGridSpec(
            num_scalar_prefetch=len(items),
            grid=(n_items,),
            in_specs=[row_blk,
                      pl.BlockSpec(memory_space=pl.ANY),
                      pl.BlockSpec((1, 1, 1, 2 * D_EXPERT), b_map),
                      pl.BlockSpec(memory_space=pl.ANY),
                      pl.BlockSpec((1, 1, 1, D_MODEL), b_map)],
            out_specs=row_blk,
            scratch_shapes=[pltpu.VMEM((2, D_MODEL, 2 * D_EXPERT), F32),
                            pltpu.VMEM((2, D_EXPERT, D_MODEL), F32),
                            pltpu.VMEM((D_MODEL, 2 * D_EXPERT), BF16),
                            pltpu.VMEM((D_EXPERT, D_MODEL), BF16),
                            pltpu.SemaphoreType.DMA((2, 2))]),
        out_shape=jax.ShapeDtypeStruct(xs.shape, F32),
        compiler_params=_cparams(("arbitrary",)),
        name="experts",
    )(*items, xs, w_up, b_up4, w_down, b_down4)


def _expert_items(counts, n_rows):
    n_items = n_rows // MOE_BM + N_EXPERTS
    end = jnp.cumsum(counts)
    start = end - counts
    first_b = start // MOE_BM
    nb = jnp.where(counts > 0, (end - 1) // MOE_BM - first_b + 1, 0)
    item_end = jnp.cumsum(nb)
    item_start = item_end - nb
    total = item_end[-1]
    jc = jnp.minimum(jnp.arange(n_items, dtype=jnp.int32), total - 1)
    it_e = jnp.sum((item_end[None, :] <= jc[:, None]).astype(jnp.int32), axis=1)
    sel = it_e[:, None] == jnp.arange(N_EXPERTS, dtype=jnp.int32)[None, :]
    pick = lambda v: jnp.sum(jnp.where(sel, v[None, :], 0), axis=1)
    it_b = pick(first_b) + jc - pick(item_start)
    it_lo = jnp.maximum(pick(start), it_b * MOE_BM) - it_b * MOE_BM
    it_hi = jnp.minimum(pick(end), (it_b + 1) * MOE_BM) - it_b * MOE_BM
    prev_b = jnp.concatenate([jnp.full((1,), -1, jnp.int32), it_b[:-1]])
    it_first = (it_b != prev_b).astype(jnp.int32)
    prev_e = jnp.concatenate([jnp.full((1,), -1, jnp.int32), it_e[:-1]])
    it_new = (it_e != prev_e).astype(jnp.int32)
    ar = jnp.arange(N_EXPERTS, dtype=jnp.int32)
    used = counts > 0
    slot_e = (jnp.cumsum(used.astype(jnp.int32)) - 1) % 2
    later = jnp.logical_and(ar[None, :] > ar[:, None], used[None, :])
    next_e = jnp.min(jnp.where(later, ar[None, :], N_EXPERTS), axis=1)
    next_e = jnp.where(next_e == N_EXPERTS, -1, next_e)
    i32 = lambda v: v.astype(jnp.int32)
    return (i32(it_e), i32(it_b), i32(it_lo), i32(it_hi), it_first, it_new, i32(pick(slot_e)),
            i32(pick(next_e)), i32(total).reshape(1)), start


def _combine_kernel(dest_ref, dest_next_ref, gate_ref, x_ref, ys_ref, g_ref, b_ref, o_ref, buf, sems):
    t = x_ref.shape[0]
    step = pl.program_id(0)
    slot = step % 2

    def gather(idx_ref, s):
        def issue(i, carry):
            for k in range(TOP_K):
                _tile_copy(ys_ref, idx_ref[0, 0, i * TOP_K + k], buf.at[s, k], i,
                           sems.at[s]).start(priority=k % 2)
            return carry

        lax.fori_loop(0, t, issue, 0, unroll=2)

    @pl.when(step == 0)
    def _():
        gather(dest_ref, slot)

    @pl.when(step + 1 < pl.num_programs(0))
    def _():
        gather(dest_next_ref, 1 - slot)

    for k in range(TOP_K):
        pltpu.make_async_copy(ys_ref.at[pl.ds(0, t * ROW_TILE), :], buf.at[slot, k], sems.at[slot]).wait()
    gate = gate_ref[...]
    ffn = gate[:, 0:1] * _from_row_tiled(buf.at[slot, 0], t)
    for k in range(1, TOP_K):
        ffn = ffn + gate[:, k:k + 1] * _from_row_tiled(buf.at[slot, k], t)
    o_ref[...] = _layer_norm(DEEPNORM_ALPHA * x_ref[...] + ffn, g_ref[...], b_ref[...])


def _combine(dest3, gate, x2d, ys, ln_g, ln_b):
    n = x2d.shape[0]
    t = ROUTE_T
    const2 = lambda i: (0, 0)
    last = n // t - 1
    return pl.pallas_call(
        _combine_kernel,
        grid=(n // t,),
        in_specs=[pl.BlockSpec((1, 1, t * TOP_K), lambda i: (i, 0, 0), memory_space=pltpu.SMEM),
                  pl.BlockSpec((1, 1, t * TOP_K), lambda i: (jnp.minimum(i + 1, last), 0, 0),
                               memory_space=pltpu.SMEM),
                  pl.BlockSpec((t, LANES), lambda i: (i, 0)),
                  pl.BlockSpec((t, D_MODEL), lambda i: (i, 0)),
                  pl.BlockSpec(memory_space=pl.ANY),
                  pl.BlockSpec((1, D_MODEL), const2),
                  pl.BlockSpec((1, D_MODEL), const2)],
        out_specs=pl.BlockSpec((t, D_MODEL), lambda i: (i, 0)),
        out_shape=jax.ShapeDtypeStruct((n, D_MODEL), F32),
        scratch_shapes=[pltpu.VMEM((2, TOP_K, t * ROW_TILE, LANES), F32),
                        pltpu.SemaphoreType.DMA((2,))],
        compiler_params=_cparams(("arbitrary",)),
        name="combine_ln2",
    )(dest3, dest3, gate, x2d, ys, ln_g, ln_b)


def _a_head_perm():
    grp = A_HEADS // A_KV_HEADS
    order = []
    for j in range(grp):
        order += [j, grp + j]
    return np.concatenate([np.arange(h * HEAD_DIM, (h + 1) * HEAD_DIM) for h in order])


def _moe(x1, layer, w_router, b_router, w_up, b_up, w_down, b_down, ln_g, ln_b):
    n = x1.shape[0]
    wr = jnp.pad(w_router, ((0, 0), (0, LANES - N_EXPERTS)))
    br = jnp.pad(b_router, (0, LANES - N_EXPERTS), constant_values=NEG_INF).reshape(1, LANES)
    eidx, gate, rank, cnt = _route(x1, wr, br)
    items, start = _expert_items(cnt[0, :N_EXPERTS], n * TOP_K)
    e4 = eidx[:, :TOP_K]
    sel = e4[:, :, None] == jnp.arange(N_EXPERTS, dtype=jnp.int32)
    dest = jnp.sum(jnp.where(sel, start, 0), axis=-1) + rank[:, :TOP_K]
    dest3 = dest.reshape(n // ROUTE_T, 1, ROUTE_T * TOP_K).astype(jnp.int32)
    xs = _dispatch(x1, dest3)
    lead = (DEPTH, N_EXPERTS, 1)
    ys = _experts(items, xs, w_up, b_up.reshape(lead + (-1,)), w_down, b_down.reshape(lead + (-1,)), layer)
    return _combine(dest3, gate, x1, ys, ln_g.reshape(1, -1), ln_b.reshape(1, -1))


def kernel(x, w_in, a_sink, lambda_q1, lambda_k1, lambda_q2, lambda_k2, diff_norm_g, na_rpb,
           w_branch, w_out, ln1_g, ln1_b, w_router, b_router, w_up, b_up, w_down, b_down,
           ln2_g, ln2_b):
    bsz, seq, d = x.shape
    n = bsz * seq
    rows = seq // GRID_W
    perm = _a_head_perm()
    slopes_b = jnp.asarray(_ALIBI[A_HEADS:], F32)
    xcur = x.reshape(n, d)
    for l in range(DEPTH):
        w = w_in[l]
        qscale = HEAD_DIM ** -0.5
        w_qkv = jnp.concatenate([
            w[:, :512][:, perm] * qscale, w[:, 512:768],
            w[:, 768:1280] * qscale, w[:, 1280:2304],
            w[:, 2304:2816] * qscale, w[:, 2816:QKV_W]], axis=1).astype(BF16)
        w_gate = w[:, QKV_W:].astype(BF16)
        w_br = jnp.stack([w_branch[l, 0][perm], w_branch[l, 1], w_branch[l, 2]]).astype(BF16)
        lam_init = 0.8 - 0.6 * math.exp(-0.3 * l)
        lamv = jnp.stack([lambda_q1[l], lambda_k1[l], lambda_q2[l], lambda_k2[l]]).astype(F32)

        proj = _inproj(xcur, w_qkv)
        oa = _attn_a(proj, a_sink[l].astype(F32), bsz, seq)
        ob = _attn_b(proj, slopes_b, lamv, diff_norm_g[l].reshape(1, -1).astype(F32), bsz, seq, lam_init)
        oc = _attn_c(proj, _na_bias_table(na_rpb[l], rows), bsz, seq)
        x1 = _merge(xcur, oa, ob, oc, w_gate, w_br, w_out[l].astype(BF16),
                    ln1_g[l].reshape(1, -1), ln1_b[l].reshape(1, -1))
        xcur = _moe(x1, l, w_router[l], b_router[l], w_up, b_up, w_down, b_down,
                    ln2_g[l], ln2_b[l])
    return xcur.reshape(bsz, seq, d)
```

```python
import functools
import math

import numpy as np
import jax
import jax.numpy as jnp
from jax import lax
from jax.experimental import pallas as pl
from jax.experimental.pallas import tpu as pltpu

F32 = jnp.float32
BF16 = jnp.bfloat16

D_MODEL = 1024
DEPTH = 2
HEAD_DIM = 64
A_HEADS = 8
A_KV_HEADS = 2
WINDOW = 128
B_HEADS = 4
C_HEADS = 8
GRID_W = 64
NA_ROWS = 8
NA_COLS = 16
MIX_W = 512
N_BRANCH = 3
N_EXPERTS = 32
TOP_K = 4
D_EXPERT = 1024
SWIGLU_LIMIT = 7.0
SWIGLU_ALPHA = 1.702
LN_EPS = 1e-5
NEG_INF = -1e30
DEEPNORM_ALPHA = (2 * DEPTH) ** 0.25

LANES = 128
QKV_W = 3840
GATE_W = N_BRANCH * D_MODEL
A_Q_BLK, A_K_BLK, A_V_BLK = 0, 4, 5
B_Q_BLK, B_K_BLK, B_V_BLK = 6, 10, 14
C_Q_BLK, C_K_BLK, C_V_BLK = 18, 22, 26

A_BAND = 3 * WINDOW
B_TQ = 256
NA_QROWS = 4
NA_SLAB = NA_QROWS + NA_ROWS
MOE_BM = 512
EXPERT_CHUNK = 256
ROUTE_T = 256
ROUTER_T = 512
VMEM_LIMIT = 56 * 1024 * 1024

_ALIBI = [float(2.0 ** (-8.0 * (i + 1) / (A_HEADS + B_HEADS))) for i in range(A_HEADS + B_HEADS)]


def _cparams(sem):
    return pltpu.CompilerParams(dimension_semantics=sem, vmem_limit_bytes=VMEM_LIMIT)


def _layer_norm(y, g, b):
    mu = jnp.mean(y, axis=-1, keepdims=True)
    yc = y - mu
    var = jnp.mean(yc * yc, axis=-1, keepdims=True)
    return yc * lax.rsqrt(var + LN_EPS) * g + b


def _inproj_kernel(x_ref, w_ref, o_ref, *, chunk):
    xb = x_ref[...].astype(BF16)
    for c in range(QKV_W // chunk):
        sl = slice(c * chunk, (c + 1) * chunk)
        o_ref[:, sl] = jnp.dot(xb, w_ref[:, sl], preferred_element_type=F32).astype(BF16)


def _inproj(x2d, w_qkv):
    n = x2d.shape[0]
    tm = 512
    return pl.pallas_call(
        functools.partial(_inproj_kernel, chunk=768),
        grid=(n // tm,),
        in_specs=[pl.BlockSpec((tm, D_MODEL), lambda i: (i, 0)),
                  pl.BlockSpec((D_MODEL, QKV_W), lambda i: (0, 0))],
        out_specs=pl.BlockSpec((tm, QKV_W), lambda i: (i, 0)),
        out_shape=jax.ShapeDtypeStruct((n, QKV_W), BF16),
        compiler_params=_cparams(("arbitrary",)),
        name="inproj",
    )(x2d, w_qkv)


def _attn_a_kernel(sink_ref, q_ref, k_ref, v_ref, o_ref, bias_ref, sink_tab, *, seq):
    nblk = seq // WINDOW
    heads = [j + 4 * hf for j in range(4) for hf in range(2)]
    lane = lax.broadcasted_iota(jnp.int32, (WINDOW, LANES), 1)
    low = lane < HEAD_DIM
    dn = (((1,), (1,)), ((), ()))

    @pl.when(pl.program_id(0) == 0)
    def _():
        kq = (lax.broadcasted_iota(jnp.int32, (WINDOW, A_BAND), 1)
              - lax.broadcasted_iota(jnp.int32, (WINDOW, A_BAND), 0))
        for c in range(3):
            dist = jnp.abs(kq - c * WINDOW).astype(F32)
            for i, h in enumerate(heads):
                bias_ref[c, i * WINDOW:(i + 1) * WINDOW, :] = jnp.where(
                    dist <= float(WINDOW), _ALIBI[h] * dist, -NEG_INF)
        for i, h in enumerate(heads):
            sink_tab[i * WINDOW:(i + 1) * WINDOW, :] = jnp.full((WINDOW, LANES), sink_ref[h], F32)

    sink_col = jnp.max(sink_tab[...], axis=-1, keepdims=True)

    def body(n, carry):
        q0 = pl.multiple_of(n * WINDOW, WINDOW)
        start = pl.multiple_of(jnp.clip((n - 1) * WINDOW, 0, seq - A_BAND), WINDOW)
        cfg = jnp.where(n == 0, 0, jnp.where(n == nblk - 1, 2, 1))
        kb = k_ref[pl.ds(start, A_BAND), :]
        vb = v_ref[pl.ds(start, A_BAND), :]
        pieces = []
        for j in range(4):
            qt = q_ref[pl.ds(q0, WINDOW), j * LANES:(j + 1) * LANES]
            zero = jnp.zeros_like(qt)
            lhs = jnp.concatenate([jnp.where(low, qt, zero), jnp.where(low, zero, qt)], axis=0)
            pieces.append(lax.dot_general(lhs, kb, dn, preferred_element_type=F32))
        t = jnp.concatenate(pieces, axis=0) - bias_ref[cfg]
        m = jnp.maximum(jnp.max(t, axis=-1, keepdims=True), sink_col)
        e = jnp.exp(t - m)
        den = jnp.sum(e, axis=-1, keepdims=True) + jnp.exp(sink_col - m)
        p = (e * (1.0 / den)).astype(BF16)
        pv = jnp.dot(p, vb, preferred_element_type=F32)
        for j in range(4):
            o = jnp.where(low, pv[2 * j * WINDOW:(2 * j + 1) * WINDOW],
                          pv[(2 * j + 1) * WINDOW:(2 * j + 2) * WINDOW])
            o_ref[pl.ds(q0, WINDOW), j * LANES:(j + 1) * LANES] = o.astype(BF16)
        return carry

    lax.fori_loop(0, nblk, body, 0, unroll=4)


def _attn_a(proj, sink, bsz, seq):
    return pl.pallas_call(
        functools.partial(_attn_a_kernel, seq=seq),
        grid=(bsz,),
        in_specs=[pl.BlockSpec(memory_space=pltpu.SMEM),
                  pl.BlockSpec((seq, 4 * LANES), lambda b: (b, A_Q_BLK // 4)),
                  pl.BlockSpec((seq, LANES), lambda b: (b, A_K_BLK)),
                  pl.BlockSpec((seq, LANES), lambda b: (b, A_V_BLK))],
        out_specs=pl.BlockSpec((seq, MIX_W), lambda b: (b, 0)),
        out_shape=jax.ShapeDtypeStruct((bsz * seq, MIX_W), BF16),
        scratch_shapes=[pltpu.VMEM((3, A_HEADS * WINDOW, A_BAND), F32),
                        pltpu.VMEM((A_HEADS * WINDOW, LANES), F32)],
        compiler_params=_cparams(("arbitrary",)),
        name="attn_a",
    )(sink, proj, proj, proj)


def _attn_b_kernel(slope_ref, lamv_ref, gain_ref, q_ref, k_ref, v_ref, o_ref, bias_ref, *,
                   seq, lam_init):
    nblk = seq // B_TQ
    slope = slope_ref[pl.program_id(0)]

    @pl.when(pl.program_id(1) == 0)
    def _():
        ji = (lax.broadcasted_iota(jnp.int32, (B_TQ, B_TQ), 1)
              - lax.broadcasted_iota(jnp.int32, (B_TQ, B_TQ), 0))
        for dd in range(2 * nblk - 1):
            bias_ref[dd] = slope * jnp.abs(ji + (dd - (nblk - 1)) * B_TQ).astype(F32)

    lv = lamv_ref[...]
    lam = (jnp.exp(jnp.sum(lv[0:1] * lv[1:2], axis=-1, keepdims=True))
           - jnp.exp(jnp.sum(lv[2:3] * lv[3:4], axis=-1, keepdims=True)) + lam_init)
    scale = gain_ref[...] * (1.0 - lam_init)
    lane = lax.broadcasted_iota(jnp.int32, (B_TQ, LANES), 1)
    low = lane < HEAD_DIM
    kall = k_ref[...]
    vall = v_ref[...]
    dn = (((1,), (1,)), ((), ()))

    def body(n, carry):
        q0 = pl.multiple_of(n * B_TQ, B_TQ)
        qt = q_ref[pl.ds(q0, B_TQ), :]
        zero = jnp.zeros_like(qt)
        bias = jnp.concatenate([bias_ref[kc - n + (nblk - 1)] for kc in range(nblk)], axis=1)
        es, ls = [], []
        for mp in range(2):
            qm = jnp.where(low, qt, zero) if mp == 0 else jnp.where(low, zero, qt)
            s = lax.dot_general(qm, kall, dn, preferred_element_type=F32) - bias
            m = jnp.max(s, axis=-1, keepdims=True)
            e = jnp.exp(s - m)
            ls.append(jnp.sum(e, axis=-1, keepdims=True))
            es.append(e.astype(BF16))
        a = es[0] * (1.0 / ls[0]).astype(BF16) - es[1] * (lam / ls[1]).astype(BF16)
        o = jnp.dot(a, vall, preferred_element_type=F32)
        o = o * lax.rsqrt(jnp.mean(o * o, axis=-1, keepdims=True) + LN_EPS)
        o_ref[pl.ds(q0, B_TQ), :] = (o * scale).astype(BF16)
        return carry

    lax.fori_loop(0, nblk, body, 0, unroll=2)


def _attn_b(proj, slopes_b, lamv, gain, bsz, seq, lam_init):
    nblk = seq // B_TQ
    return pl.pallas_call(
        functools.partial(_attn_b_kernel, seq=seq, lam_init=lam_init),
        grid=(B_HEADS, bsz),
        in_specs=[pl.BlockSpec(memory_space=pltpu.SMEM),
                  pl.BlockSpec((4, HEAD_DIM), lambda h, b: (0, 0)),
                  pl.BlockSpec((1, 2 * HEAD_DIM), lambda h, b: (0, 0)),
                  pl.BlockSpec((seq, LANES), lambda h, b: (b, B_Q_BLK + h)),
                  pl.BlockSpec((seq, LANES), lambda h, b: (b, B_K_BLK + h)),
                  pl.BlockSpec((seq, LANES), lambda h, b: (b, B_V_BLK + h))],
        out_specs=pl.BlockSpec((seq, LANES), lambda h, b: (b, h)),
        out_shape=jax.ShapeDtypeStruct((bsz * seq, MIX_W), BF16),
        scratch_shapes=[pltpu.VMEM((2 * nblk - 1, B_TQ, B_TQ), F32)],
        compiler_params=_cparams(("arbitrary", "arbitrary")),
        name="attn_b",
    )(slopes_b, lamv, gain, proj, proj, proj)


def _attn_c_kernel(bias_ref, q_ref, k_ref, v_ref, o_ref, *, rows):
    qtok = NA_QROWS * GRID_W
    ktok = NA_SLAB * GRID_W
    ngrp = rows // NA_QROWS
    lane = lax.broadcasted_iota(jnp.int32, (qtok, LANES), 1)
    low = lane < HEAD_DIM
    dn = (((1,), (1,)), ((), ()))

    def body(g, carry):
        slab0 = jnp.clip(g * NA_QROWS - NA_ROWS // 2, 0, rows - NA_SLAB)
        k0 = pl.multiple_of(slab0 * GRID_W, GRID_W)
        q0 = pl.multiple_of(g * qtok, qtok)
        cfg = jnp.where(g == 0, 0, jnp.where(g == ngrp - 1, 2, 1))
        qr = q_ref[pl.ds(q0, qtok), :]
        zero = jnp.zeros_like(qr)
        lhs = jnp.concatenate([jnp.where(low, qr, zero), jnp.where(low, zero, qr)], axis=0)
        ks = k_ref[pl.ds(k0, ktok), :]
        vs = v_ref[pl.ds(k0, ktok), :]
        s = lax.dot_general(lhs, ks, dn, preferred_element_type=F32) + bias_ref[0, cfg]
        m = jnp.max(s, axis=-1, keepdims=True)
        e = jnp.exp(s - m)
        p = (e * (1.0 / jnp.sum(e, axis=-1, keepdims=True))).astype(BF16)
        pv = jnp.dot(p, vs, preferred_element_type=F32)
        o = jnp.where(low, pv[:qtok], pv[qtok:])
        o_ref[pl.ds(q0, qtok), :] = o.astype(BF16)
        return carry

    lax.fori_loop(0, ngrp, body, 0, unroll=True)


def _attn_c(proj, bias_tab, layer, bsz, seq):
    rows = seq // GRID_W
    npair = C_HEADS // 2
    return pl.pallas_call(
        functools.partial(_attn_c_kernel, rows=rows),
        grid=(npair, bsz),
        in_specs=[pl.BlockSpec((1,) + bias_tab.shape[1:], lambda p, b: (layer * npair + p, 0, 0, 0)),
                  pl.BlockSpec((seq, LANES), lambda p, b: (b, C_Q_BLK + p)),
                  pl.BlockSpec((seq, LANES), lambda p, b: (b, C_K_BLK + p)),
                  pl.BlockSpec((seq, LANES), lambda p, b: (b, C_V_BLK + p))],
        out_specs=pl.BlockSpec((seq, LANES), lambda p, b: (b, p)),
        out_shape=jax.ShapeDtypeStruct((bsz * seq, MIX_W), BF16),
        compiler_params=_cparams(("arbitrary", "arbitrary")),
        name="attn_c",
    )(bias_tab, proj, proj, proj)


def _na_bias_table(rpb, rows):
    assert rows % NA_QROWS == 0 and rows >= NA_SLAB + NA_QROWS and rows >= NA_ROWS
    nh = rpb.shape[0] * C_HEADS
    qc = np.arange(GRID_W)[:, None]
    kc = np.arange(GRID_W)[None, :]
    cs = np.clip(qc - NA_COLS // 2, 0, GRID_W - NA_COLS)
    valid = (kc >= cs) & (kc < cs + NA_COLS)
    ndr = 2 * NA_ROWS - 1
    ncol = 2 * NA_COLS - 1
    colpick = (np.clip(kc - qc + NA_COLS - 1, 0, ncol - 1)[:, :, None] == np.arange(ncol))
    toep = jnp.einsum('hdc,qkc->hqdk', rpb.astype(F32).reshape(nh, ndr, ncol),
                      jnp.asarray(colpick, F32), precision=lax.Precision.HIGHEST)
    toep = jnp.where(valid[None, :, None, :], toep, NEG_INF).reshape(nh, GRID_W, ndr * GRID_W)
    wide = jnp.pad(toep, ((0, 0), (0, 0), (NA_SLAB * GRID_W, NA_SLAB * GRID_W)),
                   constant_values=NEG_INF)
    ngrp = rows // NA_QROWS
    cfgs = []
    for g in (0, 1, ngrp - 1):
        slab0 = int(np.clip(g * NA_QROWS - NA_ROWS // 2, 0, rows - NA_SLAB))
        qrows = []
        for a in range(NA_QROWS):
            r = g * NA_QROWS + a
            rs = int(np.clip(r - NA_ROWS // 2, 0, rows - NA_ROWS))
            dr0 = slab0 - r + NA_ROWS - 1
            win = wide[:, :, (NA_SLAB + dr0) * GRID_W:(2 * NA_SLAB + dr0) * GRID_W]
            inside = np.repeat([rs <= slab0 + u < rs + NA_ROWS for u in range(NA_SLAB)], GRID_W)
            qrows.append(jnp.where(inside[None, None, :], win, NEG_INF))
        cfgs.append(jnp.concatenate(qrows, axis=1))
    tab = jnp.stack(cfgs, axis=1)
    tab = tab.reshape(nh // 2, 2, 3, NA_QROWS * GRID_W, NA_SLAB * GRID_W)
    return tab.transpose(0, 2, 1, 3, 4).reshape(nh // 2, 3, 2 * NA_QROWS * GRID_W,
                                                NA_SLAB * GRID_W)


def _merge_kernel(x_ref, oa_ref, ob_ref, oc_ref, wg_ref, wbr_ref, wout_ref, g_ref, b_ref, o_ref):
    x = x_ref[...]
    xb = x.astype(BF16)
    merged = None
    for i, br_ref in enumerate((oa_ref, ob_ref, oc_ref)):
        gate = jax.nn.sigmoid(jnp.dot(xb, wg_ref[:, i * D_MODEL:(i + 1) * D_MODEL],
                                      preferred_element_type=F32))
        br = jnp.dot(br_ref[...], wbr_ref[i], preferred_element_type=F32)
        merged = gate * br if merged is None else merged + gate * br
    mix = jnp.dot(merged.astype(BF16), wout_ref[...], preferred_element_type=F32)
    o_ref[...] = _layer_norm(DEEPNORM_ALPHA * x + mix, g_ref[...], b_ref[...])


def _merge(x2d, oa, ob, oc, w_gate, w_br, w_out, ln_g, ln_b):
    n = x2d.shape[0]
    tm = 512
    const2 = lambda i: (0, 0)
    return pl.pallas_call(
        _merge_kernel,
        grid=(n // tm,),
        in_specs=[pl.BlockSpec((tm, D_MODEL), lambda i: (i, 0)),
                  pl.BlockSpec((tm, MIX_W), lambda i: (i, 0)),
                  pl.BlockSpec((tm, MIX_W), lambda i: (i, 0)),
                  pl.BlockSpec((tm, MIX_W), lambda i: (i, 0)),
                  pl.BlockSpec((D_MODEL, GATE_W), const2),
                  pl.BlockSpec((N_BRANCH, MIX_W, D_MODEL), lambda i: (0, 0, 0)),
                  pl.BlockSpec((D_MODEL, D_MODEL), const2),
                  pl.BlockSpec((1, D_MODEL), const2),
                  pl.BlockSpec((1, D_MODEL), const2)],
        out_specs=pl.BlockSpec((tm, D_MODEL), lambda i: (i, 0)),
        out_shape=jax.ShapeDtypeStruct((n, D_MODEL), F32),
        compiler_params=_cparams(("arbitrary",)),
        name="merge_ln1",
    )(x2d, oa, ob, oc, w_gate, w_br, w_out, ln_g, ln_b)


def _route_kernel(x_ref, w_ref, b_ref, eidx_ref, gate_ref, rank_ref, cnt_ref, carry_ref):
    t = x_ref.shape[0]

    @pl.when(pl.program_id(0) == 0)
    def _():
        carry_ref[...] = jnp.zeros_like(carry_ref)

    x = x_ref[...]
    w = w_ref[...]
    xh = x.astype(BF16)
    xt = (x - xh.astype(F32)).astype(BF16)
    wh = w.astype(BF16)
    wt = (w - wh.astype(F32)).astype(BF16)
    hh_ht = jnp.dot(xh, jnp.concatenate([wh, wt], axis=1), preferred_element_type=F32)
    th = jnp.dot(xt, wh, preferred_element_type=F32)
    logits = hh_ht[:, :LANES] + hh_ht[:, LANES:] + th + b_ref[...]
    lane = lax.broadcasted_iota(jnp.int32, (t, LANES), 1).astype(F32)
    work = logits
    sels, vals, idxs = [], [], []
    for _ in range(TOP_K):
        m = jnp.max(work, axis=-1, keepdims=True)
        idx = jnp.min(jnp.where(work == m, lane, float(LANES)), axis=-1, keepdims=True)
        sel = lane == idx
        work = jnp.where(sel, -jnp.inf, work)
        sels.append(sel)
        vals.append(m)
        idxs.append(idx)
    ex = [jnp.exp(v - vals[0]) for v in vals]
    den = ex[0] + ex[1] + ex[2] + ex[3]
    onehot = jnp.zeros((t, LANES), F32)
    for sel in sels:
        onehot = jnp.where(sel, 1.0, onehot)
    ri = lax.broadcasted_iota(jnp.int32, (t, t), 0)
    ci = lax.broadcasted_iota(jnp.int32, (t, t), 1)
    tri = jnp.where(ci < ri, 1.0, 0.0).astype(BF16)
    before = jnp.dot(tri, onehot.astype(BF16), preferred_element_type=F32) + carry_ref[...]
    eidx = jnp.zeros((t, LANES), F32)
    gate = jnp.zeros((t, LANES), F32)
    rank = jnp.zeros((t, LANES), F32)
    for k in range(TOP_K):
        rk = jnp.sum(jnp.where(sels[k], before, 0.0), axis=-1, keepdims=True)
        eidx = jnp.where(lane == float(k), idxs[k], eidx)
        gate = jnp.where(lane == float(k), ex[k] / den, gate)
        rank = jnp.where(lane == float(k), rk, rank)
    eidx_ref[...] = eidx.astype(jnp.int32)
    gate_ref[...] = gate
    rank_ref[...] = rank.astype(jnp.int32)
    carry_ref[...] += jnp.sum(onehot, axis=0, keepdims=True)
    cnt_ref[...] = carry_ref[...].astype(jnp.int32)


def _route(x2d, w_router_pad, b_router_pad):
    n = x2d.shape[0]
    t = ROUTER_T
    tile = pl.BlockSpec((t, LANES), lambda i: (i, 0))
    return pl.pallas_call(
        _route_kernel,
        grid=(n // t,),
        in_specs=[pl.BlockSpec((t, D_MODEL), lambda i: (i, 0)),
                  pl.BlockSpec((D_MODEL, LANES), lambda i: (0, 0)),
                  pl.BlockSpec((1, LANES), lambda i: (0, 0))],
        out_specs=[tile, tile, tile, pl.BlockSpec((1, LANES), lambda i: (0, 0))],
        out_shape=[jax.ShapeDtypeStruct((n, LANES), jnp.int32),
                   jax.ShapeDtypeStruct((n, LANES), F32),
                   jax.ShapeDtypeStruct((n, LANES), jnp.int32),
                   jax.ShapeDtypeStruct((1, LANES), jnp.int32)],
        scratch_shapes=[pltpu.VMEM((1, LANES), F32)],
        compiler_params=_cparams(("arbitrary",)),
        name="route",
    )(x2d, w_router_pad, b_router_pad)


ROW_TILE = D_MODEL // LANES


def _to_row_tiled(dst_ref, val):
    rows = val.shape[0]
    for c in range(ROW_TILE):
        dst_ref[pl.ds(c, rows, stride=ROW_TILE), :] = val[:, c * LANES:(c + 1) * LANES]


def _from_row_tiled(src_ref, rows):
    return jnp.concatenate([src_ref[pl.ds(c, rows, stride=ROW_TILE), :] for c in range(ROW_TILE)],
                           axis=1)


def _tile_copy(src_ref, src_row, dst_ref, dst_row, sem):
    return pltpu.make_async_copy(src_ref.at[pl.ds(pl.multiple_of(src_row * ROW_TILE, ROW_TILE), ROW_TILE), :],
                                 dst_ref.at[pl.ds(pl.multiple_of(dst_row * ROW_TILE, ROW_TILE), ROW_TILE), :],
                                 sem)


def _dispatch_kernel(dest_ref, x_ref, xs_ref, xt_ref, sems):
    t = x_ref.shape[0]
    step = pl.program_id(0)
    last = pl.num_programs(0) - 1
    slot = step % 2

    def drain(s):
        for _ in range(TOP_K):
            pltpu.make_async_copy(xt_ref.at[s], xs_ref.at[pl.ds(0, t * ROW_TILE), :], sems.at[s]).wait()

    @pl.when(step >= 2)
    def _():
        drain(slot)

    _to_row_tiled(xt_ref.at[slot], x_ref[...])

    def issue(i, carry):
        for k in range(TOP_K):
            _tile_copy(xt_ref.at[slot], i, xs_ref, dest_ref[0, 0, i * TOP_K + k],
                       sems.at[slot]).start(priority=k % 2)
        return carry

    lax.fori_loop(0, t, issue, 0, unroll=2)

    @pl.when(step == last)
    def _():
        @pl.when(last >= 1)
        def _():
            drain(1 - slot)
        drain(slot)


def _dispatch(x2d, dest3):
    n = x2d.shape[0]
    t = ROUTE_T
    return pl.pallas_call(
        _dispatch_kernel,
        grid=(n // t,),
        in_specs=[pl.BlockSpec((1, 1, t * TOP_K), lambda i: (i, 0, 0), memory_space=pltpu.SMEM),
                  pl.BlockSpec((t, D_MODEL), lambda i: (i, 0))],
        out_specs=pl.BlockSpec(memory_space=pl.ANY),
        out_shape=jax.ShapeDtypeStruct((n * TOP_K * ROW_TILE, LANES), F32),
        scratch_shapes=[pltpu.VMEM((2, t * ROW_TILE, LANES), F32), pltpu.SemaphoreType.DMA((2,))],
        compiler_params=_cparams(("arbitrary",)),
        name="dispatch",
    )(dest3, x2d)


def _expert_weight_copies(wu_hbm, wd_hbm, wu_f32, wd_f32, sems, layer, expert, slot):
    return (pltpu.make_async_copy(wu_hbm.at[layer, expert], wu_f32.at[slot], sems.at[slot, 0]),
            pltpu.make_async_copy(wd_hbm.at[layer, expert], wd_f32.at[slot], sems.at[slot, 1]))


def _expert_kernel(it_e, it_b, it_lo, it_hi, it_first, it_new, it_slot, it_next, n_items,
                   xs_ref, wu_hbm, bu_ref, wd_hbm, bd_ref, ys_ref,
                   wu_f32, wd_f32, wu_bf, wd_bf, sems, *, layer):
    del it_b
    j = pl.program_id(0)
    copies = functools.partial(_expert_weight_copies, wu_hbm, wd_hbm, wu_f32, wd_f32, sems, layer)

    @pl.when(j == 0)
    def _():
        for c in copies(it_e[0], 0):
            c.start()

    @pl.when(jnp.logical_and(j < n_items[0], it_new[j] == 1))
    def _():
        slot = it_slot[j]
        for c in copies(it_e[j], slot):
            c.wait()
        wu_bf[...] = wu_f32[slot].astype(BF16)
        wd_bf[...] = wd_f32[slot].astype(BF16)

        @pl.when(it_next[j] >= 0)
        def _():
            for c in copies(it_next[j], 1 - slot):
                c.start(priority=1)

    def ffn(r0, nr):
        def rows_of(ref, c):
            return ref.at[pl.ds(r0 * ROW_TILE + c, nr, stride=ROW_TILE), :]

        xb = jnp.concatenate([rows_of(xs_ref, c)[...] for c in range(ROW_TILE)], axis=1).astype(BF16)
        bu = bu_ref[0, 0]
        acts = []
        for c0 in range(0, D_EXPERT, EXPERT_CHUNK):
            c1 = c0 + EXPERT_CHUNK
            hg = jnp.dot(xb, wu_bf[:, c0:c1], preferred_element_type=F32) + bu[:, c0:c1]
            hl = (jnp.dot(xb, wu_bf[:, D_EXPERT + c0:D_EXPERT + c1], preferred_element_type=F32)
                  + bu[:, D_EXPERT + c0:D_EXPERT + c1])
            hg = jnp.minimum(hg, SWIGLU_LIMIT)
            hl = jnp.clip(hl, -SWIGLU_LIMIT, SWIGLU_LIMIT)
            acts.append((hg * jax.nn.sigmoid(SWIGLU_ALPHA * hg) * (hl + 1.0)).astype(BF16))
        act = jnp.concatenate(acts, axis=1)
        bd = bd_ref[0, 0]
        row = lax.broadcasted_iota(jnp.int32, (nr, 1), 0) + r0
        mine = jnp.logical_and(row >= it_lo[j], row < it_hi[j])

        def down_proj(first):
            for c0 in range(0, D_MODEL, EXPERT_CHUNK):
                c1 = c0 + EXPERT_CHUNK
                y = jnp.dot(act, wd_bf[:, c0:c1], preferred_element_type=F32) + bd[:, c0:c1]
                for s0 in range(c0, c1, LANES):
                    dst = rows_of(ys_ref, s0 // LANES)
                    other = 0.0 if first else dst[...]
                    dst[...] = jnp.where(mine, y[:, s0 - c0:s0 - c0 + LANES], other)

        @pl.when(it_first[j] == 1)
        def _():
            down_proj(True)
            if nr < MOE_BM:
                other0 = (MOE_BM - nr - r0) * ROW_TILE
                ys_ref[other0:other0 + nr * ROW_TILE, :] = jnp.zeros((nr * ROW_TILE, LANES), F32)

        @pl.when(it_first[j] == 0)
        def _():
            down_proj(False)

    half = MOE_BM // 2
    valid = j < n_items[0]
    low_only = it_hi[j] <= half
    high_only = it_lo[j] >= half

    @pl.when(jnp.logical_and(valid, low_only))
    def _():
        ffn(0, half)

    @pl.when(jnp.logical_and(valid, high_only))
    def _():
        ffn(half, half)

    @pl.when(jnp.logical_and(valid, jnp.logical_not(jnp.logical_or(low_only, high_only))))
    def _():
        ffn(0, MOE_BM)


def _experts(items, xs, w_up, b_up4, w_down, b_down4, layer):
    n_items = items[0].shape[0]
    row_blk = pl.BlockSpec((MOE_BM * ROW_TILE, LANES), lambda j, ie, ib, *rest: (
        ib[jnp.minimum(j, rest[-1][0] - 1)], 0))

    def cur(j, ni):
        return jnp.minimum(j, ni[0] - 1)

    def b_map(j, ie, ib, *rest):
        return (layer, ie[cur(j, rest[-1])], 0, 0)

    return pl.pallas_call(
        functools.partial(_expert_kernel, layer=layer),
        grid_spec=pltpu.PrefetchScalarGridSpec(
            num_scalar_prefetch=len(items),
            grid=(n_items,),
            in_specs=[row_blk,
                      pl.BlockSpec(memory_space=pl.ANY),
                      pl.BlockSpec((1, 1, 1, 2 * D_EXPERT), b_map),
                      pl.BlockSpec(memory_space=pl.ANY),
                      pl.BlockSpec((1, 1, 1, D_MODEL), b_map)],
            out_specs=row_blk,
            scratch_shapes=[pltpu.VMEM((2, D_MODEL, 2 * D_EXPERT), F32),
                            pltpu.VMEM((2, D_EXPERT, D_MODEL), F32),
                            pltpu.VMEM((D_MODEL, 2 * D_EXPERT), BF16),
                            pltpu.VMEM((D_EXPERT, D_MODEL), BF16),
                            pltpu.SemaphoreType.DMA((2, 2))]),
        out_shape=jax.ShapeDtypeStruct(xs.shape, F32),
        compiler_params=_cparams(("arbitrary",)),
        name="experts",
    )(*items, xs, w_up, b_up4, w_down, b_down4)


def _expert_items(counts, n_rows):
    n_items = n_rows // MOE_BM + N_EXPERTS
    end = jnp.cumsum(counts)
    start = end - counts
    first_b = start // MOE_BM
    nb = jnp.where(counts > 0, (end - 1) // MOE_BM - first_b + 1, 0)
    item_end = jnp.cumsum(nb)
    item_start = item_end - nb
    total = item_end[-1]
    jc = jnp.minimum(jnp.arange(n_items, dtype=jnp.int32), total - 1)
    it_e = jnp.sum((item_end[None, :] <= jc[:, None]).astype(jnp.int32), axis=1)
    sel = it_e[:, None] == jnp.arange(N_EXPERTS, dtype=jnp.int32)[None, :]
    pick = lambda v: jnp.sum(jnp.where(sel, v[None, :], 0), axis=1)
    it_b = pick(first_b) + jc - pick(item_start)
    it_lo = jnp.maximum(pick(start), it_b * MOE_BM) - it_b * MOE_BM
    it_hi = jnp.minimum(pick(end), (it_b + 1) * MOE_BM) - it_b * MOE_BM
    prev_b = jnp.concatenate([jnp.full((1,), -1, jnp.int32), it_b[:-1]])
    it_first = (it_b != prev_b).astype(jnp.int32)
    prev_e = jnp.concatenate([jnp.full((1,), -1, jnp.int32), it_e[:-1]])
    it_new = (it_e != prev_e).astype(jnp.int32)
    ar = jnp.arange(N_EXPERTS, dtype=jnp.int32)
    used = counts > 0
    slot_e = (jnp.cumsum(used.astype(jnp.int32)) - 1) % 2
    later = jnp.logical_and(ar[None, :] > ar[:, None], used[None, :])
    next_e = jnp.min(jnp.where(later, ar[None, :], N_EXPERTS), axis=1)
    next_e = jnp.where(next_e == N_EXPERTS, -1, next_e)
    i32 = lambda v: v.astype(jnp.int32)
    return (i32(it_e), i32(it_b), i32(it_lo), i32(it_hi), it_first, it_new, i32(pick(slot_e)),
            i32(pick(next_e)), i32(total).reshape(1)), start


def _combine_kernel(dest_ref, dest_next_ref, gate_ref, x_ref, ys_ref, g_ref, b_ref, o_ref, buf, sems):
    t = x_ref.shape[0]
    step = pl.program_id(0)
    slot = step % 2

    def gather(idx_ref, s):
        def issue(i, carry):
            for k in range(TOP_K):
                _tile_copy(ys_ref, idx_ref[0, 0, i * TOP_K + k], buf.at[s, k], i,
                           sems.at[s]).start(priority=k % 2)
            return carry

        lax.fori_loop(0, t, issue, 0, unroll=2)

    @pl.when(step == 0)
    def _():
        gather(dest_ref, slot)

    @pl.when(step + 1 < pl.num_programs(0))
    def _():
        gather(dest_next_ref, 1 - slot)

    for k in range(TOP_K):
        pltpu.make_async_copy(ys_ref.at[pl.ds(0, t * ROW_TILE), :], buf.at[slot, k], sems.at[slot]).wait()
    gate = gate_ref[...]
    ffn = gate[:, 0:1] * _from_row_tiled(buf.at[slot, 0], t)
    for k in range(1, TOP_K):
        ffn = ffn + gate[:, k:k + 1] * _from_row_tiled(buf.at[slot, k], t)
    o_ref[...] = _layer_norm(DEEPNORM_ALPHA * x_ref[...] + ffn, g_ref[...], b_ref[...])


def _combine(dest3, gate, x2d, ys, ln_g, ln_b):
    n = x2d.shape[0]
    t = ROUTE_T
    const2 = lambda i: (0, 0)
    last = n // t - 1
    return pl.pallas_call(
        _combine_kernel,
        grid=(n // t,),
        in_specs=[pl.BlockSpec((1, 1, t * TOP_K), lambda i: (i, 0, 0), memory_space=pltpu.SMEM),
                  pl.BlockSpec((1, 1, t * TOP_K), lambda i: (jnp.minimum(i + 1, last), 0, 0),
                               memory_space=pltpu.SMEM),
                  pl.BlockSpec((t, LANES), lambda i: (i, 0)),
                  pl.BlockSpec((t, D_MODEL), lambda i: (i, 0)),
                  pl.BlockSpec(memory_space=pl.ANY),
                  pl.BlockSpec((1, D_MODEL), const2),
                  pl.BlockSpec((1, D_MODEL), const2)],
        out_specs=pl.BlockSpec((t, D_MODEL), lambda i: (i, 0)),
        out_shape=jax.ShapeDtypeStruct((n, D_MODEL), F32),
        scratch_shapes=[pltpu.VMEM((2, TOP_K, t * ROW_TILE, LANES), F32),
                        pltpu.SemaphoreType.DMA((2,))],
        compiler_params=_cparams(("arbitrary",)),
        name="combine_ln2",
    )(dest3, dest3, gate, x2d, ys, ln_g, ln_b)


def _a_head_perm():
    grp = A_HEADS // A_KV_HEADS
    order = []
    for j in range(grp):
        order += [j, grp + j]
    return np.concatenate([np.arange(h * HEAD_DIM, (h + 1) * HEAD_DIM) for h in order])


def _moe(x1, layer, w_router, b_router, w_up, b_up, w_down, b_down, ln_g, ln_b):
    n = x1.shape[0]
    wr = jnp.pad(w_router, ((0, 0), (0, LANES - N_EXPERTS)))
    br = jnp.pad(b_router, (0, LANES - N_EXPERTS), constant_values=NEG_INF).reshape(1, LANES)
    eidx, gate, rank, cnt = _route(x1, wr, br)
    items, start = _expert_items(cnt[0, :N_EXPERTS], n * TOP_K)
    e4 = eidx[:, :TOP_K]
    sel = e4[:, :, None] == jnp.arange(N_EXPERTS, dtype=jnp.int32)
    dest = jnp.sum(jnp.where(sel, start, 0), axis=-1) + rank[:, :TOP_K]
    dest3 = dest.reshape(n // ROUTE_T, 1, ROUTE_T * TOP_K).astype(jnp.int32)
    xs = _dispatch(x1, dest3)
    lead = (DEPTH, N_EXPERTS, 1)
    ys = _experts(items, xs, w_up, b_up.reshape(lead + (-1,)), w_down, b_down.reshape(lead + (-1,)), layer)
    return _combine(dest3, gate, x1, ys, ln_g.reshape(1, -1), ln_b.reshape(1, -1))


def kernel(x, w_in, a_sink, lambda_q1, lambda_k1, lambda_q2, lambda_k2, diff_norm_g, na_rpb,
           w_branch, w_out, ln1_g, ln1_b, w_router, b_router, w_up, b_up, w_down, b_down,
           ln2_g, ln2_b):
    bsz, seq, d = x.shape
    n = bsz * seq
    rows = seq // GRID_W
    perm = _a_head_perm()
    slopes_b = jnp.asarray(_ALIBI[A_HEADS:], F32)
    na_bias = _na_bias_table(na_rpb, rows)
    xcur = x.reshape(n, d)
    for l in range(DEPTH):
        w = w_in[l]
        qscale = HEAD_DIM ** -0.5
        w_qkv = jnp.concatenate([
            w[:, :512][:, perm] * qscale, w[:, 512:768],
            w[:, 768:1280] * qscale, w[:, 1280:2304],
            w[:, 2304:2816] * qscale, w[:, 2816:QKV_W]], axis=1).astype(BF16)
        w_gate = w[:, QKV_W:].astype(BF16)
        w_br = jnp.stack([w_branch[l, 0][perm], w_branch[l, 1], w_branch[l, 2]]).astype(BF16)
        lam_init = 0.8 - 0.6 * math.exp(-0.3 * l)
        lamv = jnp.stack([lambda_q1[l], lambda_k1[l], lambda_q2[l], lambda_k2[l]]).astype(F32)

        proj = _inproj(xcur, w_qkv)
        oa = _attn_a(proj, a_sink[l].astype(F32), bsz, seq)
        ob = _attn_b(proj, slopes_b, lamv, diff_norm_g[l].reshape(1, -1).astype(F32), bsz, seq, lam_init)
        oc = _attn_c(proj, na_bias, l, bsz, seq)
        x1 = _merge(xcur, oa, ob, oc, w_gate, w_br, w_out[l].astype(BF16),
                    ln1_g[l].reshape(1, -1), ln1_b[l].reshape(1, -1))
        xcur = _moe(x1, l, w_router[l], b_router[l], w_up, b_up, w_down, b_down,
                    ln2_g[l], ln2_b[l])
    return xcur.reshape(bsz, seq, d)
```

```python
import functools
import math

import numpy as np
import jax
import jax.numpy as jnp
from jax import lax
from jax.experimental import pallas as pl
from jax.experimental.pallas import tpu as pltpu

F32 = jnp.float32
BF16 = jnp.bfloat16

D_MODEL = 1024
DEPTH = 2
HEAD_DIM = 64
A_HEADS = 8
A_KV_HEADS = 2
WINDOW = 128
B_HEADS = 4
C_HEADS = 8
GRID_W = 64
NA_ROWS = 8
NA_COLS = 16
MIX_W = 512
N_BRANCH = 3
N_EXPERTS = 32
TOP_K = 4
D_EXPERT = 1024
SWIGLU_LIMIT = 7.0
SWIGLU_ALPHA = 1.702
LN_EPS = 1e-5
NEG_INF = -1e30
DEEPNORM_ALPHA = (2 * DEPTH) ** 0.25

LANES = 128
QKV_W = 3840
GATE_W = N_BRANCH * D_MODEL
A_Q_BLK, A_K_BLK, A_V_BLK = 0, 4, 5
B_Q_BLK, B_K_BLK, B_V_BLK = 6, 10, 14
C_Q_BLK, C_K_BLK, C_V_BLK = 18, 22, 26

A_BAND = 3 * WINDOW
B_TQ = 256
NA_QROWS = 4
NA_SLAB = NA_QROWS + NA_ROWS
MOE_BM = 512
EXPERT_CHUNK = 256
ROUTE_T = 512
ROUTER_T = 512
VMEM_LIMIT = 56 * 1024 * 1024

_ALIBI = [float(2.0 ** (-8.0 * (i + 1) / (A_HEADS + B_HEADS))) for i in range(A_HEADS + B_HEADS)]


def _cparams(sem):
    return pltpu.CompilerParams(dimension_semantics=sem, vmem_limit_bytes=VMEM_LIMIT)


def _layer_norm(y, g, b):
    mu = jnp.mean(y, axis=-1, keepdims=True)
    yc = y - mu
    var = jnp.mean(yc * yc, axis=-1, keepdims=True)
    return yc * lax.rsqrt(var + LN_EPS) * g + b


def _inproj_kernel(x_ref, w_ref, o_ref, *, chunk):
    xb = x_ref[...].astype(BF16)
    for c in range(QKV_W // chunk):
        sl = slice(c * chunk, (c + 1) * chunk)
        o_ref[:, sl] = jnp.dot(xb, w_ref[:, sl], preferred_element_type=F32).astype(BF16)


def _inproj(x2d, w_qkv):
    n = x2d.shape[0]
    tm = 512
    return pl.pallas_call(
        functools.partial(_inproj_kernel, chunk=768),
        grid=(n // tm,),
        in_specs=[pl.BlockSpec((tm, D_MODEL), lambda i: (i, 0)),
                  pl.BlockSpec((D_MODEL, QKV_W), lambda i: (0, 0))],
        out_specs=pl.BlockSpec((tm, QKV_W), lambda i: (i, 0)),
        out_shape=jax.ShapeDtypeStruct((n, QKV_W), BF16),
        compiler_params=_cparams(("arbitrary",)),
        name="inproj",
    )(x2d, w_qkv)


def _attn_a_kernel(sink_ref, q_ref, k_ref, v_ref, o_ref, bias_ref, sink_tab, *, seq):
    nblk = seq // WINDOW
    heads = [j + 4 * hf for j in range(4) for hf in range(2)]
    lane = lax.broadcasted_iota(jnp.int32, (WINDOW, LANES), 1)
    low = lane < HEAD_DIM
    dn = (((1,), (1,)), ((), ()))

    @pl.when(pl.program_id(0) == 0)
    def _():
        kq = (lax.broadcasted_iota(jnp.int32, (WINDOW, A_BAND), 1)
              - lax.broadcasted_iota(jnp.int32, (WINDOW, A_BAND), 0))
        for c in range(3):
            dist = jnp.abs(kq - c * WINDOW).astype(F32)
            for i, h in enumerate(heads):
                bias_ref[c, i * WINDOW:(i + 1) * WINDOW, :] = jnp.where(
                    dist <= float(WINDOW), _ALIBI[h] * dist, -NEG_INF)
        for i, h in enumerate(heads):
            sink_tab[i * WINDOW:(i + 1) * WINDOW, :] = jnp.full((WINDOW, LANES), sink_ref[h], F32)

    sink_col = jnp.max(sink_tab[...], axis=-1, keepdims=True)

    def body(n, carry):
        q0 = pl.multiple_of(n * WINDOW, WINDOW)
        start = pl.multiple_of(jnp.clip((n - 1) * WINDOW, 0, seq - A_BAND), WINDOW)
        cfg = jnp.where(n == 0, 0, jnp.where(n == nblk - 1, 2, 1))
        kb = k_ref[pl.ds(start, A_BAND), :]
        vb = v_ref[pl.ds(start, A_BAND), :]
        pieces = []
        for j in range(4):
            qt = q_ref[pl.ds(q0, WINDOW), j * LANES:(j + 1) * LANES]
            zero = jnp.zeros_like(qt)
            lhs = jnp.concatenate([jnp.where(low, qt, zero), jnp.where(low, zero, qt)], axis=0)
            pieces.append(lax.dot_general(lhs, kb, dn, preferred_element_type=F32))
        t = jnp.concatenate(pieces, axis=0) - bias_ref[cfg]
        m = jnp.maximum(jnp.max(t, axis=-1, keepdims=True), sink_col)
        e = jnp.exp(t - m)
        den = jnp.sum(e, axis=-1, keepdims=True) + jnp.exp(sink_col - m)
        p = (e * (1.0 / den)).astype(BF16)
        pv = jnp.dot(p, vb, preferred_element_type=F32)
        for j in range(4):
            o = jnp.where(low, pv[2 * j * WINDOW:(2 * j + 1) * WINDOW],
                          pv[(2 * j + 1) * WINDOW:(2 * j + 2) * WINDOW])
            o_ref[pl.ds(q0, WINDOW), j * LANES:(j + 1) * LANES] = o.astype(BF16)
        return carry

    lax.fori_loop(0, nblk, body, 0, unroll=8)


def _attn_a(proj, sink, bsz, seq):
    return pl.pallas_call(
        functools.partial(_attn_a_kernel, seq=seq),
        grid=(bsz,),
        in_specs=[pl.BlockSpec(memory_space=pltpu.SMEM),
                  pl.BlockSpec((seq, 4 * LANES), lambda b: (b, A_Q_BLK // 4)),
                  pl.BlockSpec((seq, LANES), lambda b: (b, A_K_BLK)),
                  pl.BlockSpec((seq, LANES), lambda b: (b, A_V_BLK))],
        out_specs=pl.BlockSpec((seq, MIX_W), lambda b: (b, 0)),
        out_shape=jax.ShapeDtypeStruct((bsz * seq, MIX_W), BF16),
        scratch_shapes=[pltpu.VMEM((3, A_HEADS * WINDOW, A_BAND), F32),
                        pltpu.VMEM((A_HEADS * WINDOW, LANES), F32)],
        compiler_params=_cparams(("arbitrary",)),
        name="attn_a",
    )(sink, proj, proj, proj)


def _attn_b_kernel(slope_ref, lamv_ref, gain_ref, q_ref, k_ref, v_ref, o_ref, bias_ref, *,
                   seq, lam_init):
    nblk = seq // B_TQ
    slope = slope_ref[pl.program_id(0)]

    @pl.when(pl.program_id(1) == 0)
    def _():
        ji = (lax.broadcasted_iota(jnp.int32, (B_TQ, B_TQ), 1)
              - lax.broadcasted_iota(jnp.int32, (B_TQ, B_TQ), 0))
        for dd in range(2 * nblk - 1):
            bias_ref[dd] = slope * jnp.abs(ji + (dd - (nblk - 1)) * B_TQ).astype(F32)

    lv = lamv_ref[...]
    lam = (jnp.exp(jnp.sum(lv[0:1] * lv[1:2], axis=-1, keepdims=True))
           - jnp.exp(jnp.sum(lv[2:3] * lv[3:4], axis=-1, keepdims=True)) + lam_init)
    scale = gain_ref[...] * (1.0 - lam_init)
    lane = lax.broadcasted_iota(jnp.int32, (B_TQ, LANES), 1)
    low = lane < HEAD_DIM
    kall = k_ref[...]
    vall = v_ref[...]
    dn = (((1,), (1,)), ((), ()))

    def body(n, carry):
        q0 = pl.multiple_of(n * B_TQ, B_TQ)
        qt = q_ref[pl.ds(q0, B_TQ), :]
        zero = jnp.zeros_like(qt)
        bias = jnp.concatenate([bias_ref[kc - n + (nblk - 1)] for kc in range(nblk)], axis=1)
        es, ls = [], []
        for mp in range(2):
            qm = jnp.where(low, qt, zero) if mp == 0 else jnp.where(low, zero, qt)
            s = lax.dot_general(qm, kall, dn, preferred_element_type=F32) - bias
            m = jnp.max(s, axis=-1, keepdims=True)
            e = jnp.exp(s - m)
            ls.append(jnp.sum(e, axis=-1, keepdims=True))
            es.append(e.astype(BF16))
        a = es[0] * (1.0 / ls[0]).astype(BF16) - es[1] * (lam / ls[1]).astype(BF16)
        o = jnp.dot(a, vall, preferred_element_type=F32)
        o = o * lax.rsqrt(jnp.mean(o * o, axis=-1, keepdims=True) + LN_EPS)
        o_ref[pl.ds(q0, B_TQ), :] = (o * scale).astype(BF16)
        return carry

    lax.fori_loop(0, nblk, body, 0, unroll=4)


def _attn_b(proj, slopes_b, lamv, gain, bsz, seq, lam_init):
    nblk = seq // B_TQ
    return pl.pallas_call(
        functools.partial(_attn_b_kernel, seq=seq, lam_init=lam_init),
        grid=(B_HEADS, bsz),
        in_specs=[pl.BlockSpec(memory_space=pltpu.SMEM),
                  pl.BlockSpec((4, HEAD_DIM), lambda h, b: (0, 0)),
                  pl.BlockSpec((1, 2 * HEAD_DIM), lambda h, b: (0, 0)),
                  pl.BlockSpec((seq, LANES), lambda h, b: (b, B_Q_BLK + h)),
                  pl.BlockSpec((seq, LANES), lambda h, b: (b, B_K_BLK + h)),
                  pl.BlockSpec((seq, LANES), lambda h, b: (b, B_V_BLK + h))],
        out_specs=pl.BlockSpec((seq, LANES), lambda h, b: (b, h)),
        out_shape=jax.ShapeDtypeStruct((bsz * seq, MIX_W), BF16),
        scratch_shapes=[pltpu.VMEM((2 * nblk - 1, B_TQ, B_TQ), F32)],
        compiler_params=_cparams(("arbitrary", "arbitrary")),
        name="attn_b",
    )(slopes_b, lamv, gain, proj, proj, proj)


def _attn_c_kernel(bias_ref, q_ref, k_ref, v_ref, o_ref, *, rows):
    qtok = NA_QROWS * GRID_W
    ktok = NA_SLAB * GRID_W
    ngrp = rows // NA_QROWS
    lane = lax.broadcasted_iota(jnp.int32, (qtok, LANES), 1)
    low = lane < HEAD_DIM
    dn = (((1,), (1,)), ((), ()))

    def body(g, carry):
        slab0 = jnp.clip(g * NA_QROWS - NA_ROWS // 2, 0, rows - NA_SLAB)
        k0 = pl.multiple_of(slab0 * GRID_W, GRID_W)
        q0 = pl.multiple_of(g * qtok, qtok)
        cfg = jnp.where(g == 0, 0, jnp.where(g == ngrp - 1, 2, 1))
        qr = q_ref[pl.ds(q0, qtok), :]
        zero = jnp.zeros_like(qr)
        lhs = jnp.concatenate([jnp.where(low, qr, zero), jnp.where(low, zero, qr)], axis=0)
        ks = k_ref[pl.ds(k0, ktok), :]
        vs = v_ref[pl.ds(k0, ktok), :]
        s = lax.dot_general(lhs, ks, dn, preferred_element_type=F32) + bias_ref[0, cfg]
        m = jnp.max(s, axis=-1, keepdims=True)
        e = jnp.exp(s - m)
        p = (e * (1.0 / jnp.sum(e, axis=-1, keepdims=True))).astype(BF16)
        pv = jnp.dot(p, vs, preferred_element_type=F32)
        o = jnp.where(low, pv[:qtok], pv[qtok:])
        o_ref[pl.ds(q0, qtok), :] = o.astype(BF16)
        return carry

    lax.fori_loop(0, ngrp, body, 0, unroll=True)


def _attn_c(proj, bias_tab, layer, bsz, seq):
    rows = seq // GRID_W
    npair = C_HEADS // 2
    return pl.pallas_call(
        functools.partial(_attn_c_kernel, rows=rows),
        grid=(npair, bsz),
        in_specs=[pl.BlockSpec((1,) + bias_tab.shape[1:], lambda p, b: (layer * npair + p, 0, 0, 0)),
                  pl.BlockSpec((seq, LANES), lambda p, b: (b, C_Q_BLK + p)),
                  pl.BlockSpec((seq, LANES), lambda p, b: (b, C_K_BLK + p)),
                  pl.BlockSpec((seq, LANES), lambda p, b: (b, C_V_BLK + p))],
        out_specs=pl.BlockSpec((seq, LANES), lambda p, b: (b, p)),
        out_shape=jax.ShapeDtypeStruct((bsz * seq, MIX_W), BF16),
        compiler_params=_cparams(("arbitrary", "arbitrary")),
        name="attn_c",
    )(bias_tab, proj, proj, proj)


def _na_bias_table(rpb, rows):
    assert rows % NA_QROWS == 0 and rows >= NA_SLAB + NA_QROWS and rows >= NA_ROWS
    nh = rpb.shape[0] * C_HEADS
    qc = np.arange(GRID_W)[:, None]
    kc = np.arange(GRID_W)[None, :]
    cs = np.clip(qc - NA_COLS // 2, 0, GRID_W - NA_COLS)
    valid = (kc >= cs) & (kc < cs + NA_COLS)
    ndr = 2 * NA_ROWS - 1
    ncol = 2 * NA_COLS - 1
    colpick = (np.clip(kc - qc + NA_COLS - 1, 0, ncol - 1)[:, :, None] == np.arange(ncol))
    toep = jnp.einsum('hdc,qkc->hqdk', rpb.astype(F32).reshape(nh, ndr, ncol),
                      jnp.asarray(colpick, F32), precision=lax.Precision.HIGHEST)
    toep = jnp.where(valid[None, :, None, :], toep, NEG_INF).reshape(nh, GRID_W, ndr * GRID_W)
    wide = jnp.pad(toep, ((0, 0), (0, 0), (NA_SLAB * GRID_W, NA_SLAB * GRID_W)),
                   constant_values=NEG_INF)
    ngrp = rows // NA_QROWS
    cfgs = []
    for g in (0, 1, ngrp - 1):
        slab0 = int(np.clip(g * NA_QROWS - NA_ROWS // 2, 0, rows - NA_SLAB))
        qrows = []
        for a in range(NA_QROWS):
            r = g * NA_QROWS + a
            rs = int(np.clip(r - NA_ROWS // 2, 0, rows - NA_ROWS))
            dr0 = slab0 - r + NA_ROWS - 1
            win = wide[:, :, (NA_SLAB + dr0) * GRID_W:(2 * NA_SLAB + dr0) * GRID_W]
            inside = np.repeat([rs <= slab0 + u < rs + NA_ROWS for u in range(NA_SLAB)], GRID_W)
            qrows.append(jnp.where(inside[None, None, :], win, NEG_INF))
        cfgs.append(jnp.concatenate(qrows, axis=1))
    tab = jnp.stack(cfgs, axis=1)
    tab = tab.reshape(nh // 2, 2, 3, NA_QROWS * GRID_W, NA_SLAB * GRID_W)
    return tab.transpose(0, 2, 1, 3, 4).reshape(nh // 2, 3, 2 * NA_QROWS * GRID_W,
                                                NA_SLAB * GRID_W)


def _merge_kernel(x_ref, oa_ref, ob_ref, oc_ref, wg_ref, wbr_ref, wout_ref, g_ref, b_ref, o_ref):
    x = x_ref[...]
    xb = x.astype(BF16)
    merged = None
    for i, br_ref in enumerate((oa_ref, ob_ref, oc_ref)):
        gate = jax.nn.sigmoid(jnp.dot(xb, wg_ref[:, i * D_MODEL:(i + 1) * D_MODEL],
                                      preferred_element_type=F32))
        br = jnp.dot(br_ref[...], wbr_ref[i], preferred_element_type=F32)
        merged = gate * br if merged is None else merged + gate * br
    mix = jnp.dot(merged.astype(BF16), wout_ref[...], preferred_element_type=F32)
    o_ref[...] = _layer_norm(DEEPNORM_ALPHA * x + mix, g_ref[...], b_ref[...])


def _merge(x2d, oa, ob, oc, w_gate, w_br, w_out, ln_g, ln_b):
    n = x2d.shape[0]
    tm = 512
    const2 = lambda i: (0, 0)
    return pl.pallas_call(
        _merge_kernel,
        grid=(n // tm,),
        in_specs=[pl.BlockSpec((tm, D_MODEL), lambda i: (i, 0)),
                  pl.BlockSpec((tm, MIX_W), lambda i: (i, 0)),
                  pl.BlockSpec((tm, MIX_W), lambda i: (i, 0)),
                  pl.BlockSpec((tm, MIX_W), lambda i: (i, 0)),
                  pl.BlockSpec((D_MODEL, GATE_W), const2),
                  pl.BlockSpec((N_BRANCH, MIX_W, D_MODEL), lambda i: (0, 0, 0)),
                  pl.BlockSpec((D_MODEL, D_MODEL), const2),
                  pl.BlockSpec((1, D_MODEL), const2),
                  pl.BlockSpec((1, D_MODEL), const2)],
        out_specs=pl.BlockSpec((tm, D_MODEL), lambda i: (i, 0)),
        out_shape=jax.ShapeDtypeStruct((n, D_MODEL), F32),
        compiler_params=_cparams(("arbitrary",)),
        name="merge_ln1",
    )(x2d, oa, ob, oc, w_gate, w_br, w_out, ln_g, ln_b)


def _route_kernel(x_ref, w_ref, b_ref, eidx_ref, gate_ref, rank_ref, cnt_ref, carry_ref):
    t = x_ref.shape[0]

    @pl.when(pl.program_id(0) == 0)
    def _():
        carry_ref[...] = jnp.zeros_like(carry_ref)

    x = x_ref[...]
    w = w_ref[...]
    xh = x.astype(BF16)
    xt = (x - xh.astype(F32)).astype(BF16)
    wh = w.astype(BF16)
    wt = (w - wh.astype(F32)).astype(BF16)
    hh_ht = jnp.dot(xh, jnp.concatenate([wh, wt], axis=1), preferred_element_type=F32)
    th = jnp.dot(xt, wh, preferred_element_type=F32)
    logits = hh_ht[:, :LANES] + hh_ht[:, LANES:] + th + b_ref[...]
    lane = lax.broadcasted_iota(jnp.int32, (t, LANES), 1).astype(F32)
    work = logits
    sels, vals, idxs = [], [], []
    for _ in range(TOP_K):
        m = jnp.max(work, axis=-1, keepdims=True)
        idx = jnp.min(jnp.where(work == m, lane, float(LANES)), axis=-1, keepdims=True)
        sel = lane == idx
        work = jnp.where(sel, -jnp.inf, work)
        sels.append(sel)
        vals.append(m)
        idxs.append(idx)
    ex = [jnp.exp(v - vals[0]) for v in vals]
    den = ex[0] + ex[1] + ex[2] + ex[3]
    onehot = jnp.zeros((t, LANES), F32)
    for sel in sels:
        onehot = jnp.where(sel, 1.0, onehot)
    ri = lax.broadcasted_iota(jnp.int32, (t, t), 0)
    ci = lax.broadcasted_iota(jnp.int32, (t, t), 1)
    tri = jnp.where(ci < ri, 1.0, 0.0).astype(BF16)
    before = jnp.dot(tri, onehot.astype(BF16), preferred_element_type=F32) + carry_ref[...]
    eidx = jnp.zeros((t, LANES), F32)
    gate = jnp.zeros((t, LANES), F32)
    rank = jnp.zeros((t, LANES), F32)
    for k in range(TOP_K):
        rk = jnp.sum(jnp.where(sels[k], before, 0.0), axis=-1, keepdims=True)
        eidx = jnp.where(lane == float(k), idxs[k], eidx)
        gate = jnp.where(lane == float(k), ex[k] / den, gate)
        rank = jnp.where(lane == float(k), rk, rank)
    eidx_ref[...] = eidx.astype(jnp.int32)
    gate_ref[...] = gate
    rank_ref[...] = rank.astype(jnp.int32)
    carry_ref[...] += jnp.sum(onehot, axis=0, keepdims=True)
    cnt_ref[...] = carry_ref[...].astype(jnp.int32)


def _route(x2d, w_router_pad, b_router_pad):
    n = x2d.shape[0]
    t = ROUTER_T
    tile = pl.BlockSpec((t, LANES), lambda i: (i, 0))
    return pl.pallas_call(
        _route_kernel,
        grid=(n // t,),
        in_specs=[pl.BlockSpec((t, D_MODEL), lambda i: (i, 0)),
                  pl.BlockSpec((D_MODEL, LANES), lambda i: (0, 0)),
                  pl.BlockSpec((1, LANES), lambda i: (0, 0))],
        out_specs=[tile, tile, tile, pl.BlockSpec((1, LANES), lambda i: (0, 0))],
        out_shape=[jax.ShapeDtypeStruct((n, LANES), jnp.int32),
                   jax.ShapeDtypeStruct((n, LANES), F32),
                   jax.ShapeDtypeStruct((n, LANES), jnp.int32),
                   jax.ShapeDtypeStruct((1, LANES), jnp.int32)],
        scratch_shapes=[pltpu.VMEM((1, LANES), F32)],
        compiler_params=_cparams(("arbitrary",)),
        name="route",
    )(x2d, w_router_pad, b_router_pad)


ROW_TILE = D_MODEL // LANES


def _to_row_tiled(dst_ref, val):
    rows = val.shape[0]
    for c in range(ROW_TILE):
        dst_ref[pl.ds(c, rows, stride=ROW_TILE), :] = val[:, c * LANES:(c + 1) * LANES]


def _from_row_tiled(src_ref, rows):
    return jnp.concatenate([src_ref[pl.ds(c, rows, stride=ROW_TILE), :] for c in range(ROW_TILE)],
                           axis=1)


def _tile_copy(src_ref, src_row, dst_ref, dst_row, sem):
    return pltpu.make_async_copy(src_ref.at[pl.ds(pl.multiple_of(src_row * ROW_TILE, ROW_TILE), ROW_TILE), :],
                                 dst_ref.at[pl.ds(pl.multiple_of(dst_row * ROW_TILE, ROW_TILE), ROW_TILE), :],
                                 sem)


def _dispatch_kernel(dest_ref, x_ref, xs_ref, xt_ref, sems):
    t = x_ref.shape[0]
    step = pl.program_id(0)
    last = pl.num_programs(0) - 1
    slot = step % 2

    def drain(s):
        for _ in range(TOP_K):
            pltpu.make_async_copy(xt_ref.at[s], xs_ref.at[pl.ds(0, t * ROW_TILE), :], sems.at[s]).wait()

    @pl.when(step >= 2)
    def _():
        drain(slot)

    _to_row_tiled(xt_ref.at[slot], x_ref[...])

    def issue(i, carry):
        for k in range(TOP_K):
            _tile_copy(xt_ref.at[slot], i, xs_ref, dest_ref[0, 0, i * TOP_K + k],
                       sems.at[slot]).start(priority=k % 2)
        return carry

    lax.fori_loop(0, t, issue, 0, unroll=2)

    @pl.when(step == last)
    def _():
        @pl.when(last >= 1)
        def _():
            drain(1 - slot)
        drain(slot)


def _dispatch(x2d, dest3):
    n = x2d.shape[0]
    t = ROUTE_T
    return pl.pallas_call(
        _dispatch_kernel,
        grid=(n // t,),
        in_specs=[pl.BlockSpec((1, 1, t * TOP_K), lambda i: (i, 0, 0), memory_space=pltpu.SMEM),
                  pl.BlockSpec((t, D_MODEL), lambda i: (i, 0))],
        out_specs=pl.BlockSpec(memory_space=pl.ANY),
        out_shape=jax.ShapeDtypeStruct((n * TOP_K * ROW_TILE, LANES), F32),
        scratch_shapes=[pltpu.VMEM((2, t * ROW_TILE, LANES), F32), pltpu.SemaphoreType.DMA((2,))],
        compiler_params=_cparams(("arbitrary",)),
        name="dispatch",
    )(dest3, x2d)


def _expert_weight_copies(wu_hbm, wd_hbm, wu_f32, wd_f32, sems, layer, expert, slot):
    return (pltpu.make_async_copy(wu_hbm.at[layer, expert], wu_f32.at[slot], sems.at[slot, 0]),
            pltpu.make_async_copy(wd_hbm.at[layer, expert], wd_f32.at[slot], sems.at[slot, 1]))


def _expert_kernel(it_e, it_b, it_lo, it_hi, it_first, it_new, it_slot, it_next, n_items,
                   xs_ref, wu_hbm, bu_ref, wd_hbm, bd_ref, ys_ref,
                   wu_f32, wd_f32, wu_bf, wd_bf, sems, *, layer):
    del it_b
    j = pl.program_id(0)
    copies = functools.partial(_expert_weight_copies, wu_hbm, wd_hbm, wu_f32, wd_f32, sems, layer)

    @pl.when(j == 0)
    def _():
        for c in copies(it_e[0], 0):
            c.start()

    @pl.when(jnp.logical_and(j < n_items[0], it_new[j] == 1))
    def _():
        slot = it_slot[j]
        for c in copies(it_e[j], slot):
            c.wait()
        wu_bf[...] = wu_f32[slot].astype(BF16)
        wd_bf[...] = wd_f32[slot].astype(BF16)

        @pl.when(it_next[j] >= 0)
        def _():
            for c in copies(it_next[j], 1 - slot):
                c.start(priority=1)

    def ffn(r0, nr):
        def rows_of(ref, c):
            return ref.at[pl.ds(r0 * ROW_TILE + c, nr, stride=ROW_TILE), :]

        xb = jnp.concatenate([rows_of(xs_ref, c)[...] for c in range(ROW_TILE)], axis=1).astype(BF16)
        bu = bu_ref[0, 0]
        acts = []
        for c0 in range(0, D_EXPERT, EXPERT_CHUNK):
            c1 = c0 + EXPERT_CHUNK
            hg = jnp.dot(xb, wu_bf[:, c0:c1], preferred_element_type=F32) + bu[:, c0:c1]
            hl = (jnp.dot(xb, wu_bf[:, D_EXPERT + c0:D_EXPERT + c1], preferred_element_type=F32)
                  + bu[:, D_EXPERT + c0:D_EXPERT + c1])
            hg = jnp.minimum(hg, SWIGLU_LIMIT)
            hl = jnp.clip(hl, -SWIGLU_LIMIT, SWIGLU_LIMIT)
            acts.append((hg * jax.nn.sigmoid(SWIGLU_ALPHA * hg) * (hl + 1.0)).astype(BF16))
        act = jnp.concatenate(acts, axis=1)
        bd = bd_ref[0, 0]
        row = lax.broadcasted_iota(jnp.int32, (nr, 1), 0) + r0
        mine = jnp.logical_and(row >= it_lo[j], row < it_hi[j])

        def down_proj(first):
            for c0 in range(0, D_MODEL, EXPERT_CHUNK):
                c1 = c0 + EXPERT_CHUNK
                y = jnp.dot(act, wd_bf[:, c0:c1], preferred_element_type=F32) + bd[:, c0:c1]
                for s0 in range(c0, c1, LANES):
                    dst = rows_of(ys_ref, s0 // LANES)
                    other = 0.0 if first else dst[...]
                    dst[...] = jnp.where(mine, y[:, s0 - c0:s0 - c0 + LANES], other)

        @pl.when(it_first[j] == 1)
        def _():
            down_proj(True)
            if nr < MOE_BM:
                other0 = (MOE_BM - nr - r0) * ROW_TILE
                ys_ref[other0:other0 + nr * ROW_TILE, :] = jnp.zeros((nr * ROW_TILE, LANES), F32)

        @pl.when(it_first[j] == 0)
        def _():
            down_proj(False)

    half = MOE_BM // 2
    valid = j < n_items[0]
    low_only = it_hi[j] <= half
    high_only = it_lo[j] >= half

    @pl.when(jnp.logical_and(valid, low_only))
    def _():
        ffn(0, half)

    @pl.when(jnp.logical_and(valid, high_only))
    def _():
        ffn(half, half)

    @pl.when(jnp.logical_and(valid, jnp.logical_not(jnp.logical_or(low_only, high_only))))
    def _():
        ffn(0, MOE_BM)


def _experts(items, xs, w_up, b_up4, w_down, b_down4, layer):
    n_items = items[0].shape[0]
    row_blk = pl.BlockSpec((MOE_BM * ROW_TILE, LANES), lambda j, ie, ib, *rest: (
        ib[jnp.minimum(j, rest[-1][0] - 1)], 0))

    def cur(j, ni):
        return jnp.minimum(j, ni[0] - 1)

    def b_map(j, ie, ib, *rest):
        return (layer, ie[cur(j, rest[-1])], 0, 0)

    return pl.pallas_call(
        functools.partial(_expert_kernel, layer=layer),
        grid_spec=pltpu.PrefetchScalarGridSpec(
            num_scalar_prefetch=len(items),
            grid=(n_items,),
            in_specs=[row_blk,
                      pl.BlockSpec(memory_space=pl.ANY),
                      pl.BlockSpec((1, 1, 1, 2 * D_EXPERT), b_map),
                      pl.BlockSpec(memory_space=pl.ANY),
                      pl.BlockSpec((1, 1, 1, D_MODEL), b_map)],
            out_specs=row_blk,
            scratch_shapes=[pltpu.VMEM((2, D_MODEL, 2 * D_EXPERT), F32),
                            pltpu.VMEM((2, D_EXPERT, D_MODEL), F32),
                            pltpu.VMEM((D_MODEL, 2 * D_EXPERT), BF16),
                            pltpu.VMEM((D_EXPERT, D_MODEL), BF16),
                            pltpu.SemaphoreType.DMA((2, 2))]),
        out_shape=jax.ShapeDtypeStruct(xs.shape, F32),
        compiler_params=_cparams(("arbitrary",)),
        name="experts",
    )(*items, xs, w_up, b_up4, w_down, b_down4)


def _expert_items(counts, n_rows):
    n_items = n_rows // MOE_BM + N_EXPERTS
    end = jnp.cumsum(counts)
    start = end - counts
    first_b = start // MOE_BM
    nb = jnp.where(counts > 0, (end - 1) // MOE_BM - first_b + 1, 0)
    item_end = jnp.cumsum(nb)
    item_start = item_end - nb
    total = item_end[-1]
    jc = jnp.minimum(jnp.arange(n_items, dtype=jnp.int32), total - 1)
    it_e = jnp.sum((item_end[None, :] <= jc[:, None]).astype(jnp.int32), axis=1)
    sel = it_e[:, None] == jnp.arange(N_EXPERTS, dtype=jnp.int32)[None, :]
    pick = lambda v: jnp.sum(jnp.where(sel, v[None, :], 0), axis=1)
    it_b = pick(first_b) + jc - pick(item_start)
    it_lo = jnp.maximum(pick(start), it_b * MOE_BM) - it_b * MOE_BM
    it_hi = jnp.minimum(pick(end), (it_b + 1) * MOE_BM) - it_b * MOE_BM
    prev_b = jnp.concatenate([jnp.full((1,), -1, jnp.int32), it_b[:-1]])
    it_first = (it_b != prev_b).astype(jnp.int32)
    prev_e = jnp.concatenate([jnp.full((1,), -1, jnp.int32), it_e[:-1]])
    it_new = (it_e != prev_e).astype(jnp.int32)
    ar = jnp.arange(N_EXPERTS, dtype=jnp.int32)
    used = counts > 0
    slot_e = (jnp.cumsum(used.astype(jnp.int32)) - 1) % 2
    later = jnp.logical_and(ar[None, :] > ar[:, None], used[None, :])
    next_e = jnp.min(jnp.where(later, ar[None, :], N_EXPERTS), axis=1)
    next_e = jnp.where(next_e == N_EXPERTS, -1, next_e)
    i32 = lambda v: v.astype(jnp.int32)
    return (i32(it_e), i32(it_b), i32(it_lo), i32(it_hi), it_first, it_new, i32(pick(slot_e)),
            i32(pick(next_e)), i32(total).reshape(1)), start


def _combine_kernel(dest_ref, dest_next_ref, gate_ref, x_ref, ys_ref, g_ref, b_ref, o_ref, buf, sems):
    t = x_ref.shape[0]
    step = pl.program_id(0)
    slot = step % 2

    def gather(idx_ref, s):
        def issue(i, carry):
            for k in range(TOP_K):
                _tile_copy(ys_ref, idx_ref[0, 0, i * TOP_K + k], buf.at[s, k], i,
                           sems.at[s]).start(priority=k % 2)
            return carry

        lax.fori_loop(0, t, issue, 0, unroll=2)

    @pl.when(step == 0)
    def _():
        gather(dest_ref, slot)

    @pl.when(step + 1 < pl.num_programs(0))
    def _():
        gather(dest_next_ref, 1 - slot)

    for k in range(TOP_K):
        pltpu.make_async_copy(ys_ref.at[pl.ds(0, t * ROW_TILE), :], buf.at[slot, k], sems.at[slot]).wait()
    gate = gate_ref[...]
    ffn = gate[:, 0:1] * _from_row_tiled(buf.at[slot, 0], t)
    for k in range(1, TOP_K):
        ffn = ffn + gate[:, k:k + 1] * _from_row_tiled(buf.at[slot, k], t)
    o_ref[...] = _layer_norm(DEEPNORM_ALPHA * x_ref[...] + ffn, g_ref[...], b_ref[...])


def _combine(dest3, gate, x2d, ys, ln_g, ln_b):
    n = x2d.shape[0]
    t = ROUTE_T
    const2 = lambda i: (0, 0)
    last = n // t - 1
    return pl.pallas_call(
        _combine_kernel,
        grid=(n // t,),
        in_specs=[pl.BlockSpec((1, 1, t * TOP_K), lambda i: (i, 0, 0), memory_space=pltpu.SMEM),
                  pl.BlockSpec((1, 1, t * TOP_K), lambda i: (jnp.minimum(i + 1, last), 0, 0),
                               memory_space=pltpu.SMEM),
                  pl.BlockSpec((t, LANES), lambda i: (i, 0)),
                  pl.BlockSpec((t, D_MODEL), lambda i: (i, 0)),
                  pl.BlockSpec(memory_space=pl.ANY),
                  pl.BlockSpec((1, D_MODEL), const2),
                  pl.BlockSpec((1, D_MODEL), const2)],
        out_specs=pl.BlockSpec((t, D_MODEL), lambda i: (i, 0)),
        out_shape=jax.ShapeDtypeStruct((n, D_MODEL), F32),
        scratch_shapes=[pltpu.VMEM((2, TOP_K, t * ROW_TILE, LANES), F32),
                        pltpu.SemaphoreType.DMA((2,))],
        compiler_params=_cparams(("arbitrary",)),
        name="combine_ln2",
    )(dest3, dest3, gate, x2d, ys, ln_g, ln_b)


def _a_head_perm():
    grp = A_HEADS // A_KV_HEADS
    order = []
    for j in range(grp):
        order += [j, grp + j]
    return np.concatenate([np.arange(h * HEAD_DIM, (h + 1) * HEAD_DIM) for h in order])


def _moe(x1, layer, w_router, b_router, w_up, b_up, w_down, b_down, ln_g, ln_b):
    n = x1.shape[0]
    wr = jnp.pad(w_router, ((0, 0), (0, LANES - N_EXPERTS)))
    br = jnp.pad(b_router, (0, LANES - N_EXPERTS), constant_values=NEG_INF).reshape(1, LANES)
    eidx, gate, rank, cnt = _route(x1, wr, br)
    items, start = _expert_items(cnt[0, :N_EXPERTS], n * TOP_K)
    e4 = eidx[:, :TOP_K]
    sel = e4[:, :, None] == jnp.arange(N_EXPERTS, dtype=jnp.int32)
    dest = jnp.sum(jnp.where(sel, start, 0), axis=-1) + rank[:, :TOP_K]
    dest3 = dest.reshape(n // ROUTE_T, 1, ROUTE_T * TOP_K).astype(jnp.int32)
    xs = _dispatch(x1, dest3)
    lead = (DEPTH, N_EXPERTS, 1)
    ys = _experts(items, xs, w_up, b_up.reshape(lead + (-1,)), w_down, b_down.reshape(lead + (-1,)), layer)
    return _combine(dest3, gate, x1, ys, ln_g.reshape(1, -1), ln_b.reshape(1, -1))


def kernel(x, w_in, a_sink, lambda_q1, lambda_k1, lambda_q2, lambda_k2, diff_norm_g, na_rpb,
           w_branch, w_out, ln1_g, ln1_b, w_router, b_router, w_up, b_up, w_down, b_down,
           ln2_g, ln2_b):
    bsz, seq, d = x.shape
    n = bsz * seq
    rows = seq // GRID_W
    perm = _a_head_perm()
    slopes_b = jnp.asarray(_ALIBI[A_HEADS:], F32)
    na_bias = _na_bias_table(na_rpb, rows)
    xcur = x.reshape(n, d)
    for l in range(DEPTH):
        w = w_in[l]
        qscale = HEAD_DIM ** -0.5
        w_qkv = jnp.concatenate([
            w[:, :512][:, perm] * qscale, w[:, 512:768],
            w[:, 768:1280] * qscale, w[:, 1280:2304],
            w[:, 2304:2816] * qscale, w[:, 2816:QKV_W]], axis=1).astype(BF16)
        w_gate = w[:, QKV_W:].astype(BF16)
        w_br = jnp.stack([w_branch[l, 0][perm], w_branch[l, 1], w_branch[l, 2]]).astype(BF16)
        lam_init = 0.8 - 0.6 * math.exp(-0.3 * l)
        lamv = jnp.stack([lambda_q1[l], lambda_k1[l], lambda_q2[l], lambda_k2[l]]).astype(F32)

        proj = _inproj(xcur, w_qkv)
        oa = _attn_a(proj, a_sink[l].astype(F32), bsz, seq)
        ob = _attn_b(proj, slopes_b, lamv, diff_norm_g[l].reshape(1, -1).astype(F32), bsz, seq, lam_init)
        oc = _attn_c(proj, na_bias, l, bsz, seq)
        x1 = _merge(xcur, oa, ob, oc, w_gate, w_br, w_out[l].astype(BF16),
                    ln1_g[l].reshape(1, -1), ln1_b[l].reshape(1, -1))
        xcur = _moe(x1, l, w_router[l], b_router[l], w_up, b_up, w_down, b_down,
                    ln2_g[l], ln2_b[l])
    return xcur.reshape(bsz, seq, d)
```

```python
import functools
import math

import numpy as np
import jax
import jax.numpy as jnp
from jax import lax
from jax.experimental import pallas as pl
from jax.experimental.pallas import tpu as pltpu

F32 = jnp.float32
BF16 = jnp.bfloat16

D_MODEL = 1024
DEPTH = 2
HEAD_DIM = 64
A_HEADS = 8
A_KV_HEADS = 2
WINDOW = 128
B_HEADS = 4
C_HEADS = 8
GRID_W = 64
NA_ROWS = 8
NA_COLS = 16
MIX_W = 512
N_BRANCH = 3
N_EXPERTS = 32
TOP_K = 4
D_EXPERT = 1024
SWIGLU_LIMIT = 7.0
SWIGLU_ALPHA = 1.702
LN_EPS = 1e-5
NEG_INF = -1e30
DEEPNORM_ALPHA = (2 * DEPTH) ** 0.25

LANES = 128
QKV_W = 3840
GATE_W = N_BRANCH * D_MODEL
A_Q_BLK, A_K_BLK, A_V_BLK = 0, 4, 5
B_Q_BLK, B_K_BLK, B_V_BLK = 6, 10, 14
C_Q_BLK, C_K_BLK, C_V_BLK = 18, 22, 26

A_BAND = 3 * WINDOW
B_TQ = 256
NA_QROWS = 4
NA_SLAB = NA_QROWS + NA_ROWS
MOE_BM = 512
EXPERT_CHUNK = 256
ROUTE_T = 512
ROUTER_T = 512
VMEM_LIMIT = 56 * 1024 * 1024

_ALIBI = [float(2.0 ** (-8.0 * (i + 1) / (A_HEADS + B_HEADS))) for i in range(A_HEADS + B_HEADS)]


def _cparams(sem):
    return pltpu.CompilerParams(dimension_semantics=sem, vmem_limit_bytes=VMEM_LIMIT)


def _layer_norm(y, g, b):
    mu = jnp.mean(y, axis=-1, keepdims=True)
    yc = y - mu
    var = jnp.mean(yc * yc, axis=-1, keepdims=True)
    return yc * lax.rsqrt(var + LN_EPS) * g + b


def _inproj_kernel(x_ref, w_ref, o_ref, *, chunk):
    xb = x_ref[...].astype(BF16)
    for c in range(QKV_W // chunk):
        sl = slice(c * chunk, (c + 1) * chunk)
        o_ref[:, sl] = jnp.dot(xb, w_ref[:, sl], preferred_element_type=F32).astype(BF16)


def _inproj(x2d, w_qkv):
    n = x2d.shape[0]
    tm = 1024
    return pl.pallas_call(
        functools.partial(_inproj_kernel, chunk=768),
        grid=(n // tm,),
        in_specs=[pl.BlockSpec((tm, D_MODEL), lambda i: (i, 0)),
                  pl.BlockSpec((D_MODEL, QKV_W), lambda i: (0, 0))],
        out_specs=pl.BlockSpec((tm, QKV_W), lambda i: (i, 0)),
        out_shape=jax.ShapeDtypeStruct((n, QKV_W), BF16),
        compiler_params=_cparams(("arbitrary",)),
        name="inproj",
    )(x2d, w_qkv)


def _attn_a_kernel(sink_ref, q_ref, k_ref, v_ref, o_ref, bias_ref, sink_tab, *, seq):
    nblk = seq // WINDOW
    heads = [j + 4 * hf for j in range(4) for hf in range(2)]
    lane = lax.broadcasted_iota(jnp.int32, (WINDOW, LANES), 1)
    low = lane < HEAD_DIM
    dn = (((1,), (1,)), ((), ()))

    @pl.when(pl.program_id(0) == 0)
    def _():
        kq = (lax.broadcasted_iota(jnp.int32, (WINDOW, A_BAND), 1)
              - lax.broadcasted_iota(jnp.int32, (WINDOW, A_BAND), 0))
        for c in range(3):
            dist = jnp.abs(kq - c * WINDOW).astype(F32)
            for i, h in enumerate(heads):
                bias_ref[c, i * WINDOW:(i + 1) * WINDOW, :] = jnp.where(
                    dist <= float(WINDOW), _ALIBI[h] * dist, -NEG_INF)
        for i, h in enumerate(heads):
            sink_tab[i * WINDOW:(i + 1) * WINDOW, :] = jnp.full((WINDOW, LANES), sink_ref[h], F32)

    sink_col = jnp.max(sink_tab[...], axis=-1, keepdims=True)

    def body(n, carry):
        q0 = pl.multiple_of(n * WINDOW, WINDOW)
        start = pl.multiple_of(jnp.clip((n - 1) * WINDOW, 0, seq - A_BAND), WINDOW)
        cfg = jnp.where(n == 0, 0, jnp.where(n == nblk - 1, 2, 1))
        kb = k_ref[pl.ds(start, A_BAND), :]
        vb = v_ref[pl.ds(start, A_BAND), :]
        pieces = []
        for j in range(4):
            qt = q_ref[pl.ds(q0, WINDOW), j * LANES:(j + 1) * LANES]
            zero = jnp.zeros_like(qt)
            lhs = jnp.concatenate([jnp.where(low, qt, zero), jnp.where(low, zero, qt)], axis=0)
            pieces.append(lax.dot_general(lhs, kb, dn, preferred_element_type=F32))
        t = jnp.concatenate(pieces, axis=0) - bias_ref[cfg]
        m = jnp.maximum(jnp.max(t, axis=-1, keepdims=True), sink_col)
        e = jnp.exp(t - m)
        den = jnp.sum(e, axis=-1, keepdims=True) + jnp.exp(sink_col - m)
        p = (e * (1.0 / den)).astype(BF16)
        pv = jnp.dot(p, vb, preferred_element_type=F32)
        for j in range(4):
            o = jnp.where(low, pv[2 * j * WINDOW:(2 * j + 1) * WINDOW],
                          pv[(2 * j + 1) * WINDOW:(2 * j + 2) * WINDOW])
            o_ref[pl.ds(q0, WINDOW), j * LANES:(j + 1) * LANES] = o.astype(BF16)
        return carry

    lax.fori_loop(0, nblk, body, 0, unroll=8)


def _attn_a(proj, sink, bsz, seq):
    return pl.pallas_call(
        functools.partial(_attn_a_kernel, seq=seq),
        grid=(bsz,),
        in_specs=[pl.BlockSpec(memory_space=pltpu.SMEM),
                  pl.BlockSpec((seq, 4 * LANES), lambda b: (b, A_Q_BLK // 4)),
                  pl.BlockSpec((seq, LANES), lambda b: (b, A_K_BLK)),
                  pl.BlockSpec((seq, LANES), lambda b: (b, A_V_BLK))],
        out_specs=pl.BlockSpec((seq, MIX_W), lambda b: (b, 0)),
        out_shape=jax.ShapeDtypeStruct((bsz * seq, MIX_W), BF16),
        scratch_shapes=[pltpu.VMEM((3, A_HEADS * WINDOW, A_BAND), F32),
                        pltpu.VMEM((A_HEADS * WINDOW, LANES), F32)],
        compiler_params=_cparams(("arbitrary",)),
        name="attn_a",
    )(sink, proj, proj, proj)


def _attn_b_kernel(slope_ref, lamv_ref, gain_ref, q_ref, k_ref, v_ref, o_ref, bias_ref, *,
                   seq, lam_init):
    nblk = seq // B_TQ
    slope = slope_ref[pl.program_id(0)]

    @pl.when(pl.program_id(1) == 0)
    def _():
        ji = (lax.broadcasted_iota(jnp.int32, (B_TQ, B_TQ), 1)
              - lax.broadcasted_iota(jnp.int32, (B_TQ, B_TQ), 0))
        for dd in range(2 * nblk - 1):
            bias_ref[dd] = slope * jnp.abs(ji + (dd - (nblk - 1)) * B_TQ).astype(F32)

    lv = lamv_ref[...]
    lam = (jnp.exp(jnp.sum(lv[0:1] * lv[1:2], axis=-1, keepdims=True))
           - jnp.exp(jnp.sum(lv[2:3] * lv[3:4], axis=-1, keepdims=True)) + lam_init)
    scale = gain_ref[...] * (1.0 - lam_init)
    lane = lax.broadcasted_iota(jnp.int32, (B_TQ, LANES), 1)
    low = lane < HEAD_DIM
    kall = k_ref[...]
    vall = v_ref[...]
    dn = (((1,), (1,)), ((), ()))

    def body(n, carry):
        q0 = pl.multiple_of(n * B_TQ, B_TQ)
        qt = q_ref[pl.ds(q0, B_TQ), :]
        zero = jnp.zeros_like(qt)
        bias = jnp.concatenate([bias_ref[kc - n + (nblk - 1)] for kc in range(nblk)], axis=1)
        es, ls = [], []
        for mp in range(2):
            qm = jnp.where(low, qt, zero) if mp == 0 else jnp.where(low, zero, qt)
            s = lax.dot_general(qm, kall, dn, preferred_element_type=F32) - bias
            m = jnp.max(s, axis=-1, keepdims=True)
            e = jnp.exp(s - m)
            ls.append(jnp.sum(e, axis=-1, keepdims=True))
            es.append(e.astype(BF16))
        a = es[0] * (1.0 / ls[0]).astype(BF16) - es[1] * (lam / ls[1]).astype(BF16)
        o = jnp.dot(a, vall, preferred_element_type=F32)
        o = o * lax.rsqrt(jnp.mean(o * o, axis=-1, keepdims=True) + LN_EPS)
        o_ref[pl.ds(q0, B_TQ), :] = (o * scale).astype(BF16)
        return carry

    lax.fori_loop(0, nblk, body, 0, unroll=4)


def _attn_b(proj, slopes_b, lamv, gain, bsz, seq, lam_init):
    nblk = seq // B_TQ
    return pl.pallas_call(
        functools.partial(_attn_b_kernel, seq=seq, lam_init=lam_init),
        grid=(B_HEADS, bsz),
        in_specs=[pl.BlockSpec(memory_space=pltpu.SMEM),
                  pl.BlockSpec((4, HEAD_DIM), lambda h, b: (0, 0)),
                  pl.BlockSpec((1, 2 * HEAD_DIM), lambda h, b: (0, 0)),
                  pl.BlockSpec((seq, LANES), lambda h, b: (b, B_Q_BLK + h)),
                  pl.BlockSpec((seq, LANES), lambda h, b: (b, B_K_BLK + h)),
                  pl.BlockSpec((seq, LANES), lambda h, b: (b, B_V_BLK + h))],
        out_specs=pl.BlockSpec((seq, LANES), lambda h, b: (b, h)),
        out_shape=jax.ShapeDtypeStruct((bsz * seq, MIX_W), BF16),
        scratch_shapes=[pltpu.VMEM((2 * nblk - 1, B_TQ, B_TQ), F32)],
        compiler_params=_cparams(("arbitrary", "arbitrary")),
        name="attn_b",
    )(slopes_b, lamv, gain, proj, proj, proj)


def _attn_c_kernel(bias_ref, q_ref, k_ref, v_ref, o_ref, *, rows):
    qtok = NA_QROWS * GRID_W
    ktok = NA_SLAB * GRID_W
    ngrp = rows // NA_QROWS
    lane = lax.broadcasted_iota(jnp.int32, (qtok, LANES), 1)
    low = lane < HEAD_DIM
    dn = (((1,), (1,)), ((), ()))

    def body(g, carry):
        slab0 = jnp.clip(g * NA_QROWS - NA_ROWS // 2, 0, rows - NA_SLAB)
        k0 = pl.multiple_of(slab0 * GRID_W, GRID_W)
        q0 = pl.multiple_of(g * qtok, qtok)
        cfg = jnp.where(g == 0, 0, jnp.where(g == ngrp - 1, 2, 1))
        qr = q_ref[pl.ds(q0, qtok), :]
        zero = jnp.zeros_like(qr)
        lhs = jnp.concatenate([jnp.where(low, qr, zero), jnp.where(low, zero, qr)], axis=0)
        ks = k_ref[pl.ds(k0, ktok), :]
        vs = v_ref[pl.ds(k0, ktok), :]
        s = lax.dot_general(lhs, ks, dn, preferred_element_type=F32) + bias_ref[0, cfg]
        m = jnp.max(s, axis=-1, keepdims=True)
        e = jnp.exp(s - m)
        p = (e * (1.0 / jnp.sum(e, axis=-1, keepdims=True))).astype(BF16)
        pv = jnp.dot(p, vs, preferred_element_type=F32)
        o = jnp.where(low, pv[:qtok], pv[qtok:])
        o_ref[pl.ds(q0, qtok), :] = o.astype(BF16)
        return carry

    lax.fori_loop(0, ngrp, body, 0, unroll=True)


def _attn_c(proj, bias_tab, layer, bsz, seq):
    rows = seq // GRID_W
    npair = C_HEADS // 2
    return pl.pallas_call(
        functools.partial(_attn_c_kernel, rows=rows),
        grid=(npair, bsz),
        in_specs=[pl.BlockSpec((1,) + bias_tab.shape[1:], lambda p, b: (layer * npair + p, 0, 0, 0)),
                  pl.BlockSpec((seq, LANES), lambda p, b: (b, C_Q_BLK + p)),
                  pl.BlockSpec((seq, LANES), lambda p, b: (b, C_K_BLK + p)),
                  pl.BlockSpec((seq, LANES), lambda p, b: (b, C_V_BLK + p))],
        out_specs=pl.BlockSpec((seq, LANES), lambda p, b: (b, p)),
        out_shape=jax.ShapeDtypeStruct((bsz * seq, MIX_W), BF16),
        compiler_params=_cparams(("arbitrary", "arbitrary")),
        name="attn_c",
    )(bias_tab, proj, proj, proj)


def _na_bias_table(rpb, rows):
    assert rows % NA_QROWS == 0 and rows >= NA_SLAB + NA_QROWS and rows >= NA_ROWS
    nh = rpb.shape[0] * C_HEADS
    qc = np.arange(GRID_W)[:, None]
    kc = np.arange(GRID_W)[None, :]
    cs = np.clip(qc - NA_COLS // 2, 0, GRID_W - NA_COLS)
    valid = (kc >= cs) & (kc < cs + NA_COLS)
    ndr = 2 * NA_ROWS - 1
    ncol = 2 * NA_COLS - 1
    colpick = (np.clip(kc - qc + NA_COLS - 1, 0, ncol - 1)[:, :, None] == np.arange(ncol))
    toep = jnp.einsum('hdc,qkc->hqdk', rpb.astype(F32).reshape(nh, ndr, ncol),
                      jnp.asarray(colpick, F32), precision=lax.Precision.HIGHEST)
    toep = jnp.where(valid[None, :, None, :], toep, NEG_INF).reshape(nh, GRID_W, ndr * GRID_W)
    wide = jnp.pad(toep, ((0, 0), (0, 0), (NA_SLAB * GRID_W, NA_SLAB * GRID_W)),
                   constant_values=NEG_INF)
    ngrp = rows // NA_QROWS
    cfgs = []
    for g in (0, 1, ngrp - 1):
        slab0 = int(np.clip(g * NA_QROWS - NA_ROWS // 2, 0, rows - NA_SLAB))
        qrows = []
        for a in range(NA_QROWS):
            r = g * NA_QROWS + a
            rs = int(np.clip(r - NA_ROWS // 2, 0, rows - NA_ROWS))
            dr0 = slab0 - r + NA_ROWS - 1
            win = wide[:, :, (NA_SLAB + dr0) * GRID_W:(2 * NA_SLAB + dr0) * GRID_W]
            inside = np.repeat([rs <= slab0 + u < rs + NA_ROWS for u in range(NA_SLAB)], GRID_W)
            qrows.append(jnp.where(inside[None, None, :], win, NEG_INF))
        cfgs.append(jnp.concatenate(qrows, axis=1))
    tab = jnp.stack(cfgs, axis=1)
    tab = tab.reshape(nh // 2, 2, 3, NA_QROWS * GRID_W, NA_SLAB * GRID_W)
    return tab.transpose(0, 2, 1, 3, 4).reshape(nh // 2, 3, 2 * NA_QROWS * GRID_W,
                                                NA_SLAB * GRID_W)


def _merge_kernel(x_ref, oa_ref, ob_ref, oc_ref, wg_ref, wbr_ref, wout_ref, g_ref, b_ref, o_ref):
    x = x_ref[...]
    xb = x.astype(BF16)
    merged = None
    for i, br_ref in enumerate((oa_ref, ob_ref, oc_ref)):
        gate = jax.nn.sigmoid(jnp.dot(xb, wg_ref[:, i * D_MODEL:(i + 1) * D_MODEL],
                                      preferred_element_type=F32))
        br = jnp.dot(br_ref[...], wbr_ref[i], preferred_element_type=F32)
        merged = gate * br if merged is None else merged + gate * br
    mix = jnp.dot(merged.astype(BF16), wout_ref[...], preferred_element_type=F32)
    o_ref[...] = _layer_norm(DEEPNORM_ALPHA * x + mix, g_ref[...], b_ref[...])


def _merge(x2d, oa, ob, oc, w_gate, w_br, w_out, ln_g, ln_b):
    n = x2d.shape[0]
    tm = 512
    const2 = lambda i: (0, 0)
    return pl.pallas_call(
        _merge_kernel,
        grid=(n // tm,),
        in_specs=[pl.BlockSpec((tm, D_MODEL), lambda i: (i, 0)),
                  pl.BlockSpec((tm, MIX_W), lambda i: (i, 0)),
                  pl.BlockSpec((tm, MIX_W), lambda i: (i, 0)),
                  pl.BlockSpec((tm, MIX_W), lambda i: (i, 0)),
                  pl.BlockSpec((D_MODEL, GATE_W), const2),
                  pl.BlockSpec((N_BRANCH, MIX_W, D_MODEL), lambda i: (0, 0, 0)),
                  pl.BlockSpec((D_MODEL, D_MODEL), const2),
                  pl.BlockSpec((1, D_MODEL), const2),
                  pl.BlockSpec((1, D_MODEL), const2)],
        out_specs=pl.BlockSpec((tm, D_MODEL), lambda i: (i, 0)),
        out_shape=jax.ShapeDtypeStruct((n, D_MODEL), F32),
        compiler_params=_cparams(("arbitrary",)),
        name="merge_ln1",
    )(x2d, oa, ob, oc, w_gate, w_br, w_out, ln_g, ln_b)


def _route_kernel(x_ref, w_ref, b_ref, eidx_ref, gate_ref, rank_ref, cnt_ref, carry_ref):
    t = x_ref.shape[0]

    @pl.when(pl.program_id(0) == 0)
    def _():
        carry_ref[...] = jnp.zeros_like(carry_ref)

    x = x_ref[...]
    w = w_ref[...]
    xh = x.astype(BF16)
    xt = (x - xh.astype(F32)).astype(BF16)
    wh = w.astype(BF16)
    wt = (w - wh.astype(F32)).astype(BF16)
    hh_ht = jnp.dot(xh, jnp.concatenate([wh, wt], axis=1), preferred_element_type=F32)
    th = jnp.dot(xt, wh, preferred_element_type=F32)
    logits = hh_ht[:, :LANES] + hh_ht[:, LANES:] + th + b_ref[...]
    lane = lax.broadcasted_iota(jnp.int32, (t, LANES), 1).astype(F32)
    work = logits
    sels, vals, idxs = [], [], []
    for _ in range(TOP_K):
        m = jnp.max(work, axis=-1, keepdims=True)
        idx = jnp.min(jnp.where(work == m, lane, float(LANES)), axis=-1, keepdims=True)
        sel = lane == idx
        work = jnp.where(sel, -jnp.inf, work)
        sels.append(sel)
        vals.append(m)
        idxs.append(idx)
    ex = [jnp.exp(v - vals[0]) for v in vals]
    den = ex[0] + ex[1] + ex[2] + ex[3]
    onehot = jnp.zeros((t, LANES), F32)
    for sel in sels:
        onehot = jnp.where(sel, 1.0, onehot)
    ri = lax.broadcasted_iota(jnp.int32, (t, t), 0)
    ci = lax.broadcasted_iota(jnp.int32, (t, t), 1)
    tri = jnp.where(ci < ri, 1.0, 0.0).astype(BF16)
    before = jnp.dot(tri, onehot.astype(BF16), preferred_element_type=F32) + carry_ref[...]
    eidx = jnp.zeros((t, LANES), F32)
    gate = jnp.zeros((t, LANES), F32)
    rank = jnp.zeros((t, LANES), F32)
    for k in range(TOP_K):
        rk = jnp.sum(jnp.where(sels[k], before, 0.0), axis=-1, keepdims=True)
        eidx = jnp.where(lane == float(k), idxs[k], eidx)
        gate = jnp.where(lane == float(k), ex[k] / den, gate)
        rank = jnp.where(lane == float(k), rk, rank)
    eidx_ref[...] = eidx.astype(jnp.int32)
    gate_ref[...] = gate
    rank_ref[...] = rank.astype(jnp.int32)
    carry_ref[...] += jnp.sum(onehot, axis=0, keepdims=True)
    cnt_ref[...] = carry_ref[...].astype(jnp.int32)


def _route(x2d, w_router_pad, b_router_pad):
    n = x2d.shape[0]
    t = ROUTER_T
    tile = pl.BlockSpec((t, LANES), lambda i: (i, 0))
    return pl.pallas_call(
        _route_kernel,
        grid=(n // t,),
        in_specs=[pl.BlockSpec((t, D_MODEL), lambda i: (i, 0)),
                  pl.BlockSpec((D_MODEL, LANES), lambda i: (0, 0)),
                  pl.BlockSpec((1, LANES), lambda i: (0, 0))],
        out_specs=[tile, tile, tile, pl.BlockSpec((1, LANES), lambda i: (0, 0))],
        out_shape=[jax.ShapeDtypeStruct((n, LANES), jnp.int32),
                   jax.ShapeDtypeStruct((n, LANES), F32),
                   jax.ShapeDtypeStruct((n, LANES), jnp.int32),
                   jax.ShapeDtypeStruct((1, LANES), jnp.int32)],
        scratch_shapes=[pltpu.VMEM((1, LANES), F32)],
        compiler_params=_cparams(("arbitrary",)),
        name="route",
    )(x2d, w_router_pad, b_router_pad)


ROW_TILE = D_MODEL // LANES


def _to_row_tiled(dst_ref, val):
    rows = val.shape[0]
    for c in range(ROW_TILE):
        dst_ref[pl.ds(c, rows, stride=ROW_TILE), :] = val[:, c * LANES:(c + 1) * LANES]


def _from_row_tiled(src_ref, rows):
    return jnp.concatenate([src_ref[pl.ds(c, rows, stride=ROW_TILE), :] for c in range(ROW_TILE)],
                           axis=1)


def _tile_copy(src_ref, src_row, dst_ref, dst_row, sem):
    return pltpu.make_async_copy(src_ref.at[pl.ds(pl.multiple_of(src_row * ROW_TILE, ROW_TILE), ROW_TILE), :],
                                 dst_ref.at[pl.ds(pl.multiple_of(dst_row * ROW_TILE, ROW_TILE), ROW_TILE), :],
                                 sem)


def _dispatch_kernel(dest_ref, x_ref, xs_ref, xt_ref, sems):
    t = x_ref.shape[0]
    step = pl.program_id(0)
    last = pl.num_programs(0) - 1
    slot = step % 2

    def drain(s):
        for _ in range(TOP_K):
            pltpu.make_async_copy(xt_ref.at[s], xs_ref.at[pl.ds(0, t * ROW_TILE), :], sems.at[s]).wait()

    @pl.when(step >= 2)
    def _():
        drain(slot)

    _to_row_tiled(xt_ref.at[slot], x_ref[...])

    def issue(i, carry):
        for k in range(TOP_K):
            _tile_copy(xt_ref.at[slot], i, xs_ref, dest_ref[0, 0, i * TOP_K + k],
                       sems.at[slot]).start(priority=k % 2)
        return carry

    lax.fori_loop(0, t, issue, 0, unroll=2)

    @pl.when(step == last)
    def _():
        @pl.when(last >= 1)
        def _():
            drain(1 - slot)
        drain(slot)


def _dispatch(x2d, dest3):
    n = x2d.shape[0]
    t = ROUTE_T
    return pl.pallas_call(
        _dispatch_kernel,
        grid=(n // t,),
        in_specs=[pl.BlockSpec((1, 1, t * TOP_K), lambda i: (i, 0, 0), memory_space=pltpu.SMEM),
                  pl.BlockSpec((t, D_MODEL), lambda i: (i, 0))],
        out_specs=pl.BlockSpec(memory_space=pl.ANY),
        out_shape=jax.ShapeDtypeStruct((n * TOP_K * ROW_TILE, LANES), F32),
        scratch_shapes=[pltpu.VMEM((2, t * ROW_TILE, LANES), F32), pltpu.SemaphoreType.DMA((2,))],
        compiler_params=_cparams(("arbitrary",)),
        name="dispatch",
    )(dest3, x2d)


def _expert_weight_copies(wu_hbm, wd_hbm, wu_f32, wd_f32, sems, layer, expert, slot):
    return (pltpu.make_async_copy(wu_hbm.at[layer, expert], wu_f32.at[slot], sems.at[slot, 0]),
            pltpu.make_async_copy(wd_hbm.at[layer, expert], wd_f32.at[slot], sems.at[slot, 1]))


def _expert_kernel(it_e, it_b, it_lo, it_hi, it_first, it_new, it_slot, it_next, n_items,
                   xs_ref, wu_hbm, bu_ref, wd_hbm, bd_ref, ys_ref,
                   wu_f32, wd_f32, wu_bf, wd_bf, sems, *, layer):
    del it_b
    j = pl.program_id(0)
    copies = functools.partial(_expert_weight_copies, wu_hbm, wd_hbm, wu_f32, wd_f32, sems, layer)

    @pl.when(j == 0)
    def _():
        for c in copies(it_e[0], 0):
            c.start()

    @pl.when(jnp.logical_and(j < n_items[0], it_new[j] == 1))
    def _():
        slot = it_slot[j]
        for c in copies(it_e[j], slot):
            c.wait()
        wu_bf[...] = wu_f32[slot].astype(BF16)
        wd_bf[...] = wd_f32[slot].astype(BF16)

        @pl.when(it_next[j] >= 0)
        def _():
            for c in copies(it_next[j], 1 - slot):
                c.start()

    def ffn(r0, nr):
        def rows_of(ref, c):
            return ref.at[pl.ds(r0 * ROW_TILE + c, nr, stride=ROW_TILE), :]

        xb = jnp.concatenate([rows_of(xs_ref, c)[...] for c in range(ROW_TILE)], axis=1).astype(BF16)
        bu = bu_ref[0, 0]
        acts = []
        for c0 in range(0, D_EXPERT, EXPERT_CHUNK):
            c1 = c0 + EXPERT_CHUNK
            hg = jnp.dot(xb, wu_bf[:, c0:c1], preferred_element_type=F32) + bu[:, c0:c1]
            hl = (jnp.dot(xb, wu_bf[:, D_EXPERT + c0:D_EXPERT + c1], preferred_element_type=F32)
                  + bu[:, D_EXPERT + c0:D_EXPERT + c1])
            hg = jnp.minimum(hg, SWIGLU_LIMIT)
            hl = jnp.clip(hl, -SWIGLU_LIMIT, SWIGLU_LIMIT)
            acts.append((hg * jax.nn.sigmoid(SWIGLU_ALPHA * hg) * (hl + 1.0)).astype(BF16))
        act = jnp.concatenate(acts, axis=1)
        bd = bd_ref[0, 0]
        row = lax.broadcasted_iota(jnp.int32, (nr, 1), 0) + r0
        mine = jnp.logical_and(row >= it_lo[j], row < it_hi[j])

        def down_proj(first):
            for c0 in range(0, D_MODEL, EXPERT_CHUNK):
                c1 = c0 + EXPERT_CHUNK
                y = jnp.dot(act, wd_bf[:, c0:c1], preferred_element_type=F32) + bd[:, c0:c1]
                for s0 in range(c0, c1, LANES):
                    dst = rows_of(ys_ref, s0 // LANES)
                    other = 0.0 if first else dst[...]
                    dst[...] = jnp.where(mine, y[:, s0 - c0:s0 - c0 + LANES], other)

        @pl.when(it_first[j] == 1)
        def _():
            down_proj(True)
            if nr < MOE_BM:
                other0 = (MOE_BM - nr - r0) * ROW_TILE
                ys_ref[other0:other0 + nr * ROW_TILE, :] = jnp.zeros((nr * ROW_TILE, LANES), F32)

        @pl.when(it_first[j] == 0)
        def _():
            down_proj(False)

    half = MOE_BM // 2
    valid = j < n_items[0]
    low_only = it_hi[j] <= half
    high_only = it_lo[j] >= half

    @pl.when(jnp.logical_and(valid, low_only))
    def _():
        ffn(0, half)

    @pl.when(jnp.logical_and(valid, high_only))
    def _():
        ffn(half, half)

    @pl.when(jnp.logical_and(valid, jnp.logical_not(jnp.logical_or(low_only, high_only))))
    def _():
        ffn(0, MOE_BM)


def _experts(items, xs, w_up, b_up4, w_down, b_down4, layer):
    n_items = items[0].shape[0]
    row_blk = pl.BlockSpec((MOE_BM * ROW_TILE, LANES), lambda j, ie, ib, *rest: (
        ib[jnp.minimum(j, rest[-1][0] - 1)], 0))

    def cur(j, ni):
        return jnp.minimum(j, ni[0] - 1)

    def b_map(j, ie, ib, *rest):
        return (layer, ie[cur(j, rest[-1])], 0, 0)

    return pl.pallas_call(
        functools.partial(_expert_kernel, layer=layer),
        grid_spec=pltpu.PrefetchScalarGridSpec(
            num_scalar_prefetch=len(items),
            grid=(n_items,),
            in_specs=[row_blk,
                      pl.BlockSpec(memory_space=pl.ANY),
                      pl.BlockSpec((1, 1, 1, 2 * D_EXPERT), b_map),
                      pl.BlockSpec(memory_space=pl.ANY),
                      pl.BlockSpec((1, 1, 1, D_MODEL), b_map)],
            out_specs=row_blk,
            scratch_shapes=[pltpu.VMEM((2, D_MODEL, 2 * D_EXPERT), F32),
                            pltpu.VMEM((2, D_EXPERT, D_MODEL), F32),
                            pltpu.VMEM((D_MODEL, 2 * D_EXPERT), BF16),
                            pltpu.VMEM((D_EXPERT, D_MODEL), BF16),
                            pltpu.SemaphoreType.DMA((2, 2))]),
        out_shape=jax.ShapeDtypeStruct(xs.shape, F32),
        compiler_params=_cparams(("arbitrary",)),
        name="experts",
    )(*items, xs, w_up, b_up4, w_down, b_down4)


def _expert_items(counts, n_rows):
    n_items = n_rows // MOE_BM + N_EXPERTS
    end = jnp.cumsum(counts)
    start = end - counts
    first_b = start // MOE_BM
    nb = jnp.where(counts > 0, (end - 1) // MOE_BM - first_b + 1, 0)
    item_end = jnp.cumsum(nb)
    item_start = item_end - nb
    total = item_end[-1]
    jc = jnp.minimum(jnp.arange(n_items, dtype=jnp.int32), total - 1)
    it_e = jnp.sum((item_end[None, :] <= jc[:, None]).astype(jnp.int32), axis=1)
    sel = it_e[:, None] == jnp.arange(N_EXPERTS, dtype=jnp.int32)[None, :]
    pick = lambda v: jnp.sum(jnp.where(sel, v[None, :], 0), axis=1)
    it_b = pick(first_b) + jc - pick(item_start)
    it_lo = jnp.maximum(pick(start), it_b * MOE_BM) - it_b * MOE_BM
    it_hi = jnp.minimum(pick(end), (it_b + 1) * MOE_BM) - it_b * MOE_BM
    prev_b = jnp.concatenate([jnp.full((1,), -1, jnp.int32), it_b[:-1]])
    it_first = (it_b != prev_b).astype(jnp.int32)
    prev_e = jnp.concatenate([jnp.full((1,), -1, jnp.int32), it_e[:-1]])
    it_new = (it_e != prev_e).astype(jnp.int32)
    ar = jnp.arange(N_EXPERTS, dtype=jnp.int32)
    used = counts > 0
    slot_e = (jnp.cumsum(used.astype(jnp.int32)) - 1) % 2
    later = jnp.logical_and(ar[None, :] > ar[:, None], used[None, :])
    next_e = jnp.min(jnp.where(later, ar[None, :], N_EXPERTS), axis=1)
    next_e = jnp.where(next_e == N_EXPERTS, -1, next_e)
    i32 = lambda v: v.astype(jnp.int32)
    return (i32(it_e), i32(it_b), i32(it_lo), i32(it_hi), it_first, it_new, i32(pick(slot_e)),
            i32(pick(next_e)), i32(total).reshape(1)), start


def _combine_kernel(dest_ref, dest_next_ref, gate_ref, x_ref, ys_ref, g_ref, b_ref, o_ref, buf, sems):
    t = x_ref.shape[0]
    step = pl.program_id(0)
    slot = step % 2

    def gather(idx_ref, s):
        def issue(i, carry):
            for k in range(TOP_K):
                _tile_copy(ys_ref, idx_ref[0, 0, i * TOP_K + k], buf.at[s, k], i,
                           sems.at[s]).start(priority=k % 2)
            return carry

        lax.fori_loop(0, t, issue, 0, unroll=2)

    @pl.when(step == 0)
    def _():
        gather(dest_ref, slot)

    @pl.when(step + 1 < pl.num_programs(0))
    def _():
        gather(dest_next_ref, 1 - slot)

    for k in range(TOP_K):
        pltpu.make_async_copy(ys_ref.at[pl.ds(0, t * ROW_TILE), :], buf.at[slot, k], sems.at[slot]).wait()
    gate = gate_ref[...]
    ffn = gate[:, 0:1] * _from_row_tiled(buf.at[slot, 0], t)
    for k in range(1, TOP_K):
        ffn = ffn + gate[:, k:k + 1] * _from_row_tiled(buf.at[slot, k], t)
    o_ref[...] = _layer_norm(DEEPNORM_ALPHA * x_ref[...] + ffn, g_ref[...], b_ref[...])


def _combine(dest3, gate, x2d, ys, ln_g, ln_b):
    n = x2d.shape[0]
    t = ROUTE_T
    const2 = lambda i: (0, 0)
    last = n // t - 1
    return pl.pallas_call(
        _combine_kernel,
        grid=(n // t,),
        in_specs=[pl.BlockSpec((1, 1, t * TOP_K), lambda i: (i, 0, 0), memory_space=pltpu.SMEM),
                  pl.BlockSpec((1, 1, t * TOP_K), lambda i: (jnp.minimum(i + 1, last), 0, 0),
                               memory_space=pltpu.SMEM),
                  pl.BlockSpec((t, LANES), lambda i: (i, 0)),
                  pl.BlockSpec((t, D_MODEL), lambda i: (i, 0)),
                  pl.BlockSpec(memory_space=pl.ANY),
                  pl.BlockSpec((1, D_MODEL), const2),
                  pl.BlockSpec((1, D_MODEL), const2)],
        out_specs=pl.BlockSpec((t, D_MODEL), lambda i: (i, 0)),
        out_shape=jax.ShapeDtypeStruct((n, D_MODEL), F32),
        scratch_shapes=[pltpu.VMEM((2, TOP_K, t * ROW_TILE, LANES), F32),
                        pltpu.SemaphoreType.DMA((2,))],
        compiler_params=_cparams(("arbitrary",)),
        name="combine_ln2",
    )(dest3, dest3, gate, x2d, ys, ln_g, ln_b)


def _a_head_perm():
    grp = A_HEADS // A_KV_HEADS
    order = []
    for j in range(grp):
        order += [j, grp + j]
    return np.concatenate([np.arange(h * HEAD_DIM, (h + 1) * HEAD_DIM) for h in order])


def _moe(x1, layer, w_router, b_router, w_up, b_up, w_down, b_down, ln_g, ln_b):
    n = x1.shape[0]
    wr = jnp.pad(w_router, ((0, 0), (0, LANES - N_EXPERTS)))
    br = jnp.pad(b_router, (0, LANES - N_EXPERTS), constant_values=NEG_INF).reshape(1, LANES)
    eidx, gate, rank, cnt = _route(x1, wr, br)
    items, start = _expert_items(cnt[0, :N_EXPERTS], n * TOP_K)
    e4 = eidx[:, :TOP_K]
    sel = e4[:, :, None] == jnp.arange(N_EXPERTS, dtype=jnp.int32)
    dest = jnp.sum(jnp.where(sel, start, 0), axis=-1) + rank[:, :TOP_K]
    dest3 = dest.reshape(n // ROUTE_T, 1, ROUTE_T * TOP_K).astype(jnp.int32)
    xs = _dispatch(x1, dest3)
    lead = (DEPTH, N_EXPERTS, 1)
    ys = _experts(items, xs, w_up, b_up.reshape(lead + (-1,)), w_down, b_down.reshape(lead + (-1,)), layer)
    return _combine(dest3, gate, x1, ys, ln_g.reshape(1, -1), ln_b.reshape(1, -1))


def kernel(x, w_in, a_sink, lambda_q1, lambda_k1, lambda_q2, lambda_k2, diff_norm_g, na_rpb,
           w_branch, w_out, ln1_g, ln1_b, w_router, b_router, w_up, b_up, w_down, b_down,
           ln2_g, ln2_b):
    bsz, seq, d = x.shape
    n = bsz * seq
    rows = seq // GRID_W
    perm = _a_head_perm()
    slopes_b = jnp.asarray(_ALIBI[A_HEADS:], F32)
    na_bias = _na_bias_table(na_rpb, rows)
    xcur = x.reshape(n, d)
    for l in range(DEPTH):
        w = w_in[l]
        qscale = HEAD_DIM ** -0.5
        w_qkv = jnp.concatenate([
            w[:, :512][:, perm] * qscale, w[:, 512:768],
            w[:, 768:1280] * qscale, w[:, 1280:2304],
            w[:, 2304:2816] * qscale, w[:, 2816:QKV_W]], axis=1).astype(BF16)
        w_gate = w[:, QKV_W:].astype(BF16)
        w_br = jnp.stack([w_branch[l, 0][perm], w_branch[l, 1], w_branch[l, 2]]).astype(BF16)
        lam_init = 0.8 - 0.6 * math.exp(-0.3 * l)
        lamv = jnp.stack([lambda_q1[l], lambda_k1[l], lambda_q2[l], lambda_k2[l]]).astype(F32)

        proj = _inproj(xcur, w_qkv)
        oa = _attn_a(proj, a_sink[l].astype(F32), bsz, seq)
        ob = _attn_b(proj, slopes_b, lamv, diff_norm_g[l].reshape(1, -1).astype(F32), bsz, seq, lam_init)
        oc = _attn_c(proj, na_bias, l, bsz, seq)
        x1 = _merge(xcur, oa, ob, oc, w_gate, w_br, w_out[l].astype(BF16),
                    ln1_g[l].reshape(1, -1), ln1_b[l].reshape(1, -1))
        xcur = _moe(x1, l, w_router[l], b_router[l], w_up, b_up, w_down, b_down,
                    ln2_g[l], ln2_b[l])
    return xcur.reshape(bsz, seq, d)
```

```python
import functools
import math

import numpy as np
import jax
import jax.numpy as jnp
from jax import lax
from jax.experimental import pallas as pl
from jax.experimental.pallas import tpu as pltpu

F32 = jnp.float32
BF16 = jnp.bfloat16

D_MODEL = 1024
DEPTH = 2
HEAD_DIM = 64
A_HEADS = 8
A_KV_HEADS = 2
WINDOW = 128
B_HEADS = 4
C_HEADS = 8
GRID_W = 64
NA_ROWS = 8
NA_COLS = 16
MIX_W = 512
N_BRANCH = 3
N_EXPERTS = 32
TOP_K = 4
D_EXPERT = 1024
SWIGLU_LIMIT = 7.0
SWIGLU_ALPHA = 1.702
LN_EPS = 1e-5
NEG_INF = -1e30
DEEPNORM_ALPHA = (2 * DEPTH) ** 0.25

LANES = 128
QKV_W = 3840
GATE_W = N_BRANCH * D_MODEL
A_Q_BLK, A_K_BLK, A_V_BLK = 0, 4, 5
B_Q_BLK, B_K_BLK, B_V_BLK = 6, 10, 14
C_Q_BLK, C_K_BLK, C_V_BLK = 18, 22, 26

A_BAND = 3 * WINDOW
B_TQ = 256
NA_QROWS = 4
NA_SLAB = NA_QROWS + NA_ROWS
MOE_BM = 512
EXPERT_CHUNK = 256
ROUTE_T = 512
ROUTER_T = 512
VMEM_LIMIT = 56 * 1024 * 1024

_ALIBI = [float(2.0 ** (-8.0 * (i + 1) / (A_HEADS + B_HEADS))) for i in range(A_HEADS + B_HEADS)]


def _cparams(sem):
    return pltpu.CompilerParams(dimension_semantics=sem, vmem_limit_bytes=VMEM_LIMIT)


def _layer_norm(y, g, b):
    mu = jnp.mean(y, axis=-1, keepdims=True)
    yc = y - mu
    var = jnp.mean(yc * yc, axis=-1, keepdims=True)
    return yc * lax.rsqrt(var + LN_EPS) * g + b


def _inproj_kernel(x_ref, w_ref, o_ref, *, chunk):
    xb = x_ref[...].astype(BF16)
    for c in range(QKV_W // chunk):
        sl = slice(c * chunk, (c + 1) * chunk)
        o_ref[:, sl] = jnp.dot(xb, w_ref[:, sl], preferred_element_type=F32).astype(BF16)


def _inproj(x2d, w_qkv):
    n = x2d.shape[0]
    tm = 1024
    return pl.pallas_call(
        functools.partial(_inproj_kernel, chunk=768),
        grid=(n // tm,),
        in_specs=[pl.BlockSpec((tm, D_MODEL), lambda i: (i, 0)),
                  pl.BlockSpec((D_MODEL, QKV_W), lambda i: (0, 0))],
        out_specs=pl.BlockSpec((tm, QKV_W), lambda i: (i, 0)),
        out_shape=jax.ShapeDtypeStruct((n, QKV_W), BF16),
        compiler_params=_cparams(("arbitrary",)),
        name="inproj",
    )(x2d, w_qkv)


def _attn_a_kernel(sink_ref, q_ref, k_ref, v_ref, o_ref, bias_ref, sink_tab, *, seq):
    nblk = seq // WINDOW
    heads = [j + 4 * hf for j in range(4) for hf in range(2)]
    lane = lax.broadcasted_iota(jnp.int32, (WINDOW, LANES), 1)
    low = lane < HEAD_DIM
    dn = (((1,), (1,)), ((), ()))

    @pl.when(pl.program_id(0) == 0)
    def _():
        kq = (lax.broadcasted_iota(jnp.int32, (WINDOW, A_BAND), 1)
              - lax.broadcasted_iota(jnp.int32, (WINDOW, A_BAND), 0))
        for c in range(3):
            dist = jnp.abs(kq - c * WINDOW).astype(F32)
            for i, h in enumerate(heads):
                bias_ref[c, i * WINDOW:(i + 1) * WINDOW, :] = jnp.where(
                    dist <= float(WINDOW), _ALIBI[h] * dist, -NEG_INF)
        for i, h in enumerate(heads):
            sink_tab[i * WINDOW:(i + 1) * WINDOW, :] = jnp.full((WINDOW, LANES), sink_ref[h], F32)

    sink_col = jnp.max(sink_tab[...], axis=-1, keepdims=True)

    def body(n, carry):
        q0 = pl.multiple_of(n * WINDOW, WINDOW)
        start = pl.multiple_of(jnp.clip((n - 1) * WINDOW, 0, seq - A_BAND), WINDOW)
        cfg = jnp.where(n == 0, 0, jnp.where(n == nblk - 1, 2, 1))
        kb = k_ref[pl.ds(start, A_BAND), :]
        vb = v_ref[pl.ds(start, A_BAND), :]
        pieces = []
        for j in range(4):
            qt = q_ref[pl.ds(q0, WINDOW), j * LANES:(j + 1) * LANES]
            zero = jnp.zeros_like(qt)
            lhs = jnp.concatenate([jnp.where(low, qt, zero), jnp.where(low, zero, qt)], axis=0)
            pieces.append(lax.dot_general(lhs, kb, dn, preferred_element_type=F32))
        t = jnp.concatenate(pieces, axis=0) - bias_ref[cfg]
        m = jnp.maximum(jnp.max(t, axis=-1, keepdims=True), sink_col)
        e = jnp.exp(t - m)
        den = jnp.sum(e, axis=-1, keepdims=True) + jnp.exp(sink_col - m)
        p = (e * (1.0 / den)).astype(BF16)
        pv = jnp.dot(p, vb, preferred_element_type=F32)
        for j in range(4):
            o = jnp.where(low, pv[2 * j * WINDOW:(2 * j + 1) * WINDOW],
                          pv[(2 * j + 1) * WINDOW:(2 * j + 2) * WINDOW])
            o_ref[pl.ds(q0, WINDOW), j * LANES:(j + 1) * LANES] = o.astype(BF16)
        return carry

    lax.fori_loop(0, nblk, body, 0, unroll=8)


def _attn_a(proj, sink, bsz, seq):
    return pl.pallas_call(
        functools.partial(_attn_a_kernel, seq=seq),
        grid=(bsz,),
        in_specs=[pl.BlockSpec(memory_space=pltpu.SMEM),
                  pl.BlockSpec((seq, 4 * LANES), lambda b: (b, A_Q_BLK // 4)),
                  pl.BlockSpec((seq, LANES), lambda b: (b, A_K_BLK)),
                  pl.BlockSpec((seq, LANES), lambda b: (b, A_V_BLK))],
        out_specs=pl.BlockSpec((seq, MIX_W), lambda b: (b, 0)),
        out_shape=jax.ShapeDtypeStruct((bsz * seq, MIX_W), BF16),
        scratch_shapes=[pltpu.VMEM((3, A_HEADS * WINDOW, A_BAND), F32),
                        pltpu.VMEM((A_HEADS * WINDOW, LANES), F32)],
        compiler_params=_cparams(("arbitrary",)),
        name="attn_a",
    )(sink, proj, proj, proj)


def _attn_b_kernel(slope_ref, lamv_ref, gain_ref, q_ref, k_ref, v_ref, o_ref, bias_ref, *,
                   seq, lam_init):
    nblk = seq // B_TQ
    slope = slope_ref[pl.program_id(0)]

    @pl.when(pl.program_id(1) == 0)
    def _():
        ji = (lax.broadcasted_iota(jnp.int32, (B_TQ, B_TQ), 1)
              - lax.broadcasted_iota(jnp.int32, (B_TQ, B_TQ), 0))
        for dd in range(2 * nblk - 1):
            bias_ref[dd] = slope * jnp.abs(ji + (dd - (nblk - 1)) * B_TQ).astype(F32)

    lv = lamv_ref[...]
    lam = (jnp.exp(jnp.sum(lv[0:1] * lv[1:2], axis=-1, keepdims=True))
           - jnp.exp(jnp.sum(lv[2:3] * lv[3:4], axis=-1, keepdims=True)) + lam_init)
    scale = gain_ref[...] * (1.0 - lam_init)
    lane = lax.broadcasted_iota(jnp.int32, (B_TQ, LANES), 1)
    low = lane < HEAD_DIM
    kall = k_ref[...]
    vall = v_ref[...]
    dn = (((1,), (1,)), ((), ()))

    def body(n, carry):
        q0 = pl.multiple_of(n * B_TQ, B_TQ)
        qt = q_ref[pl.ds(q0, B_TQ), :]
        zero = jnp.zeros_like(qt)
        bias = jnp.concatenate([bias_ref[kc - n + (nblk - 1)] for kc in range(nblk)], axis=1)
        es, ls = [], []
        for mp in range(2):
            qm = jnp.where(low, qt, zero) if mp == 0 else jnp.where(low, zero, qt)
            s = lax.dot_general(qm, kall, dn, preferred_element_type=F32) - bias
            m = jnp.max(s, axis=-1, keepdims=True)
            e = jnp.exp(s - m)
            ls.append(jnp.sum(e, axis=-1, keepdims=True))
            es.append(e.astype(BF16))
        a = es[0] * (1.0 / ls[0]).astype(BF16) - es[1] * (lam / ls[1]).astype(BF16)
        o = jnp.dot(a, vall, preferred_element_type=F32)
        o = o * lax.rsqrt(jnp.mean(o * o, axis=-1, keepdims=True) + LN_EPS)
        o_ref[pl.ds(q0, B_TQ), :] = (o * scale).astype(BF16)
        return carry

    lax.fori_loop(0, nblk, body, 0, unroll=4)


def _attn_b(proj, slopes_b, lamv, gain, bsz, seq, lam_init):
    nblk = seq // B_TQ
    return pl.pallas_call(
        functools.partial(_attn_b_kernel, seq=seq, lam_init=lam_init),
        grid=(B_HEADS, bsz),
        in_specs=[pl.BlockSpec(memory_space=pltpu.SMEM),
                  pl.BlockSpec((4, HEAD_DIM), lambda h, b: (0, 0)),
                  pl.BlockSpec((1, 2 * HEAD_DIM), lambda h, b: (0, 0)),
                  pl.BlockSpec((seq, LANES), lambda h, b: (b, B_Q_BLK + h)),
                  pl.BlockSpec((seq, LANES), lambda h, b: (b, B_K_BLK + h)),
                  pl.BlockSpec((seq, LANES), lambda h, b: (b, B_V_BLK + h))],
        out_specs=pl.BlockSpec((seq, LANES), lambda h, b: (b, h)),
        out_shape=jax.ShapeDtypeStruct((bsz * seq, MIX_W), BF16),
        scratch_shapes=[pltpu.VMEM((2 * nblk - 1, B_TQ, B_TQ), F32)],
        compiler_params=_cparams(("arbitrary", "arbitrary")),
        name="attn_b",
    )(slopes_b, lamv, gain, proj, proj, proj)


def _attn_c_kernel(bias_ref, q_ref, k_ref, v_ref, o_ref, *, rows):
    qtok = NA_QROWS * GRID_W
    ktok = NA_SLAB * GRID_W
    ngrp = rows // NA_QROWS
    lane = lax.broadcasted_iota(jnp.int32, (qtok, LANES), 1)
    low = lane < HEAD_DIM
    dn = (((1,), (1,)), ((), ()))

    def body(g, carry):
        slab0 = jnp.clip(g * NA_QROWS - NA_ROWS // 2, 0, rows - NA_SLAB)
        k0 = pl.multiple_of(slab0 * GRID_W, GRID_W)
        q0 = pl.multiple_of(g * qtok, qtok)
        cfg = jnp.where(g == 0, 0, jnp.where(g == ngrp - 1, 2, 1))
        qr = q_ref[pl.ds(q0, qtok), :]
        zero = jnp.zeros_like(qr)
        lhs = jnp.concatenate([jnp.where(low, qr, zero), jnp.where(low, zero, qr)], axis=0)
        ks = k_ref[pl.ds(k0, ktok), :]
        vs = v_ref[pl.ds(k0, ktok), :]
        s = lax.dot_general(lhs, ks, dn, preferred_element_type=F32) + bias_ref[0, cfg]
        m = jnp.max(s, axis=-1, keepdims=True)
        e = jnp.exp(s - m)
        p = (e * (1.0 / jnp.sum(e, axis=-1, keepdims=True))).astype(BF16)
        pv = jnp.dot(p, vs, preferred_element_type=F32)
        o = jnp.where(low, pv[:qtok], pv[qtok:])
        o_ref[pl.ds(q0, qtok), :] = o.astype(BF16)
        return carry

    lax.fori_loop(0, ngrp, body, 0, unroll=True)


def _attn_c(proj, bias_tab, layer, bsz, seq):
    rows = seq // GRID_W
    npair = C_HEADS // 2
    return pl.pallas_call(
        functools.partial(_attn_c_kernel, rows=rows),
        grid=(npair, bsz),
        in_specs=[pl.BlockSpec((1,) + bias_tab.shape[1:], lambda p, b: (layer * npair + p, 0, 0, 0)),
                  pl.BlockSpec((seq, LANES), lambda p, b: (b, C_Q_BLK + p)),
                  pl.BlockSpec((seq, LANES), lambda p, b: (b, C_K_BLK + p)),
                  pl.BlockSpec((seq, LANES), lambda p, b: (b, C_V_BLK + p))],
        out_specs=pl.BlockSpec((seq, LANES), lambda p, b: (b, p)),
        out_shape=jax.ShapeDtypeStruct((bsz * seq, MIX_W), BF16),
        compiler_params=_cparams(("arbitrary", "arbitrary")),
        name="attn_c",
    )(bias_tab, proj, proj, proj)


def _na_bias_table(rpb, rows):
    assert rows % NA_QROWS == 0 and rows >= NA_SLAB + NA_QROWS and rows >= NA_ROWS
    nh = rpb.shape[0] * C_HEADS
    qc = np.arange(GRID_W)[:, None]
    kc = np.arange(GRID_W)[None, :]
    cs = np.clip(qc - NA_COLS // 2, 0, GRID_W - NA_COLS)
    valid = (kc >= cs) & (kc < cs + NA_COLS)
    ndr = 2 * NA_ROWS - 1
    ncol = 2 * NA_COLS - 1
    colpick = (np.clip(kc - qc + NA_COLS - 1, 0, ncol - 1)[:, :, None] == np.arange(ncol))
    toep = jnp.einsum('hdc,qkc->hqdk', rpb.astype(F32).reshape(nh, ndr, ncol),
                      jnp.asarray(colpick, F32), precision=lax.Precision.HIGHEST)
    toep = jnp.where(valid[None, :, None, :], toep, NEG_INF).reshape(nh, GRID_W, ndr * GRID_W)
    wide = jnp.pad(toep, ((0, 0), (0, 0), (NA_SLAB * GRID_W, NA_SLAB * GRID_W)),
                   constant_values=NEG_INF)
    ngrp = rows // NA_QROWS
    cfgs = []
    for g in (0, 1, ngrp - 1):
        slab0 = int(np.clip(g * NA_QROWS - NA_ROWS // 2, 0, rows - NA_SLAB))
        qrows = []
        for a in range(NA_QROWS):
            r = g * NA_QROWS + a
            rs = int(np.clip(r - NA_ROWS // 2, 0, rows - NA_ROWS))
            dr0 = slab0 - r + NA_ROWS - 1
            win = wide[:, :, (NA_SLAB + dr0) * GRID_W:(2 * NA_SLAB + dr0) * GRID_W]
            inside = np.repeat([rs <= slab0 + u < rs + NA_ROWS for u in range(NA_SLAB)], GRID_W)
            qrows.append(jnp.where(inside[None, None, :], win, NEG_INF))
        cfgs.append(jnp.concatenate(qrows, axis=1))
    tab = jnp.stack(cfgs, axis=1)
    tab = tab.reshape(nh // 2, 2, 3, NA_QROWS * GRID_W, NA_SLAB * GRID_W)
    return tab.transpose(0, 2, 1, 3, 4).reshape(nh // 2, 3, 2 * NA_QROWS * GRID_W,
                                                NA_SLAB * GRID_W)


def _merge_kernel(x_ref, oa_ref, ob_ref, oc_ref, wg_ref, wbr_ref, wout_ref, g_ref, b_ref, o_ref):
    x = x_ref[...]
    xb = x.astype(BF16)
    merged = None
    for i, br_ref in enumerate((oa_ref, ob_ref, oc_ref)):
        gate = jax.nn.sigmoid(jnp.dot(xb, wg_ref[:, i * D_MODEL:(i + 1) * D_MODEL],
                                      preferred_element_type=F32))
        br = jnp.dot(br_ref[...], wbr_ref[i], preferred_element_type=F32)
        merged = gate * br if merged is None else merged + gate * br
    mix = jnp.dot(merged.astype(BF16), wout_ref[...], preferred_element_type=F32)
    o_ref[...] = _layer_norm(DEEPNORM_ALPHA * x + mix, g_ref[...], b_ref[...])


def _merge(x2d, oa, ob, oc, w_gate, w_br, w_out, ln_g, ln_b):
    n = x2d.shape[0]
    tm = 512
    const2 = lambda i: (0, 0)
    return pl.pallas_call(
        _merge_kernel,
        grid=(n // tm,),
        in_specs=[pl.BlockSpec((tm, D_MODEL), lambda i: (i, 0)),
                  pl.BlockSpec((tm, MIX_W), lambda i: (i, 0)),
                  pl.BlockSpec((tm, MIX_W), lambda i: (i, 0)),
                  pl.BlockSpec((tm, MIX_W), lambda i: (i, 0)),
                  pl.BlockSpec((D_MODEL, GATE_W), const2),
                  pl.BlockSpec((N_BRANCH, MIX_W, D_MODEL), lambda i: (0, 0, 0)),
                  pl.BlockSpec((D_MODEL, D_MODEL), const2),
                  pl.BlockSpec((1, D_MODEL), const2),
                  pl.BlockSpec((1, D_MODEL), const2)],
        out_specs=pl.BlockSpec((tm, D_MODEL), lambda i: (i, 0)),
        out_shape=jax.ShapeDtypeStruct((n, D_MODEL), F32),
        compiler_params=_cparams(("arbitrary",)),
        name="merge_ln1",
    )(x2d, oa, ob, oc, w_gate, w_br, w_out, ln_g, ln_b)


def _route_kernel(x_ref, w_ref, b_ref, eidx_ref, gate_ref, rank_ref, cnt_ref, carry_ref):
    t = x_ref.shape[0]

    @pl.when(pl.program_id(0) == 0)
    def _():
        carry_ref[...] = jnp.zeros_like(carry_ref)

    x = x_ref[...]
    w = w_ref[...]
    xh = x.astype(BF16)
    xt = (x - xh.astype(F32)).astype(BF16)
    wh = w.astype(BF16)
    wt = (w - wh.astype(F32)).astype(BF16)
    hh_ht = jnp.dot(xh, jnp.concatenate([wh, wt], axis=1), preferred_element_type=F32)
    th = jnp.dot(xt, wh, preferred_element_type=F32)
    logits = hh_ht[:, :LANES] + hh_ht[:, LANES:] + th + b_ref[...]
    lane = lax.broadcasted_iota(jnp.int32, (t, LANES), 1).astype(F32)
    work = logits
    sels, vals, idxs = [], [], []
    for _ in range(TOP_K):
        m = jnp.max(work, axis=-1, keepdims=True)
        idx = jnp.min(jnp.where(work == m, lane, float(LANES)), axis=-1, keepdims=True)
        sel = lane == idx
        work = jnp.where(sel, -jnp.inf, work)
        sels.append(sel)
        vals.append(m)
        idxs.append(idx)
    ex = [jnp.exp(v - vals[0]) for v in vals]
    den = ex[0] + ex[1] + ex[2] + ex[3]
    onehot = jnp.zeros((t, LANES), F32)
    for sel in sels:
        onehot = jnp.where(sel, 1.0, onehot)
    ri = lax.broadcasted_iota(jnp.int32, (t, t), 0)
    ci = lax.broadcasted_iota(jnp.int32, (t, t), 1)
    tri = jnp.where(ci < ri, 1.0, 0.0).astype(BF16)
    before = jnp.dot(tri, onehot.astype(BF16), preferred_element_type=F32) + carry_ref[...]
    eidx = jnp.zeros((t, LANES), F32)
    gate = jnp.zeros((t, LANES), F32)
    rank = jnp.zeros((t, LANES), F32)
    for k in range(TOP_K):
        rk = jnp.sum(jnp.where(sels[k], before, 0.0), axis=-1, keepdims=True)
        eidx = jnp.where(lane == float(k), idxs[k], eidx)
        gate = jnp.where(lane == float(k), ex[k] / den, gate)
        rank = jnp.where(lane == float(k), rk, rank)
    eidx_ref[...] = eidx.astype(jnp.int32)
    gate_ref[...] = gate
    rank_ref[...] = rank.astype(jnp.int32)
    carry_ref[...] += jnp.sum(onehot, axis=0, keepdims=True)
    cnt_ref[...] = carry_ref[...].astype(jnp.int32)


def _route(x2d, w_router_pad, b_router_pad):
    n = x2d.shape[0]
    t = ROUTER_T
    tile = pl.BlockSpec((t, LANES), lambda i: (i, 0))
    return pl.pallas_call(
        _route_kernel,
        grid=(n // t,),
        in_specs=[pl.BlockSpec((t, D_MODEL), lambda i: (i, 0)),
                  pl.BlockSpec((D_MODEL, LANES), lambda i: (0, 0)),
                  pl.BlockSpec((1, LANES), lambda i: (0, 0))],
        out_specs=[tile, tile, tile, pl.BlockSpec((1, LANES), lambda i: (0, 0))],
        out_shape=[jax.ShapeDtypeStruct((n, LANES), jnp.int32),
                   jax.ShapeDtypeStruct((n, LANES), F32),
                   jax.ShapeDtypeStruct((n, LANES), jnp.int32),
                   jax.ShapeDtypeStruct((1, LANES), jnp.int32)],
        scratch_shapes=[pltpu.VMEM((1, LANES), F32)],
        compiler_params=_cparams(("arbitrary",)),
        name="route",
    )(x2d, w_router_pad, b_router_pad)


ROW_TILE = D_MODEL // LANES


def _to_row_tiled(dst_ref, val):
    rows = val.shape[0]
    for c in range(ROW_TILE):
        dst_ref[pl.ds(c, rows, stride=ROW_TILE), :] = val[:, c * LANES:(c + 1) * LANES]


def _from_row_tiled(src_ref, rows):
    return jnp.concatenate([src_ref[pl.ds(c, rows, stride=ROW_TILE), :] for c in range(ROW_TILE)],
                           axis=1)


def _tile_copy(src_ref, src_row, dst_ref, dst_row, sem):
    return pltpu.make_async_copy(src_ref.at[pl.ds(pl.multiple_of(src_row * ROW_TILE, ROW_TILE), ROW_TILE), :],
                                 dst_ref.at[pl.ds(pl.multiple_of(dst_row * ROW_TILE, ROW_TILE), ROW_TILE), :],
                                 sem)


def _dispatch_kernel(dest_ref, x_ref, xs_ref, xt_ref, sems):
    t = x_ref.shape[0]
    step = pl.program_id(0)
    last = pl.num_programs(0) - 1
    slot = step % 2

    def drain(s):
        for _ in range(TOP_K):
            pltpu.make_async_copy(xt_ref.at[s], xs_ref.at[pl.ds(0, t * ROW_TILE), :], sems.at[s]).wait()

    @pl.when(step >= 2)
    def _():
        drain(slot)

    _to_row_tiled(xt_ref.at[slot], x_ref[...])

    def issue(i, carry):
        for k in range(TOP_K):
            _tile_copy(xt_ref.at[slot], i, xs_ref, dest_ref[0, 0, i * TOP_K + k],
                       sems.at[slot]).start(priority=k % 2)
        return carry

    lax.fori_loop(0, t, issue, 0, unroll=2)

    @pl.when(step == last)
    def _():
        @pl.when(last >= 1)
        def _():
            drain(1 - slot)
        drain(slot)


def _dispatch(x2d, dest3):
    n = x2d.shape[0]
    t = ROUTE_T
    return pl.pallas_call(
        _dispatch_kernel,
        grid=(n // t,),
        in_specs=[pl.BlockSpec((1, 1, t * TOP_K), lambda i: (i, 0, 0), memory_space=pltpu.SMEM),
                  pl.BlockSpec((t, D_MODEL), lambda i: (i, 0))],
        out_specs=pl.BlockSpec(memory_space=pl.ANY),
        out_shape=jax.ShapeDtypeStruct((n * TOP_K * ROW_TILE, LANES), F32),
        scratch_shapes=[pltpu.VMEM((2, t * ROW_TILE, LANES), F32), pltpu.SemaphoreType.DMA((2,))],
        compiler_params=_cparams(("arbitrary",)),
        name="dispatch",
    )(dest3, x2d)


def _expert_weight_copies(wu_hbm, wd_hbm, wu_f32, wd_f32, sems, layer, expert, slot):
    return (pltpu.make_async_copy(wu_hbm.at[layer, expert], wu_f32.at[slot], sems.at[slot, 0]),
            pltpu.make_async_copy(wd_hbm.at[layer, expert], wd_f32.at[slot], sems.at[slot, 1]))


def _expert_kernel(it_e, it_b, it_lo, it_hi, it_first, it_new, it_slot, it_next, n_items,
                   xs_ref, wu_hbm, bu_ref, wd_hbm, bd_ref, ys_ref,
                   wu_f32, wd_f32, wu_bf, wd_bf, sems, *, layer):
    del it_b
    j = pl.program_id(0)
    copies = functools.partial(_expert_weight_copies, wu_hbm, wd_hbm, wu_f32, wd_f32, sems, layer)

    @pl.when(j == 0)
    def _():
        for c in copies(it_e[0], 0):
            c.start()

    @pl.when(jnp.logical_and(j < n_items[0], it_new[j] == 1))
    def _():
        slot = it_slot[j]
        for c in copies(it_e[j], slot):
            c.wait()
        wu_bf[...] = wu_f32[slot].astype(BF16)
        wd_bf[...] = wd_f32[slot].astype(BF16)

        @pl.when(it_next[j] >= 0)
        def _():
            for c in copies(it_next[j], 1 - slot):
                c.start(priority=1)

    def ffn(r0, nr):
        def rows_of(ref, c):
            return ref.at[pl.ds(r0 * ROW_TILE + c, nr, stride=ROW_TILE), :]

        xb = jnp.concatenate([rows_of(xs_ref, c)[...] for c in range(ROW_TILE)], axis=1).astype(BF16)
        bu = bu_ref[0, 0]
        acts = []
        for c0 in range(0, D_EXPERT, EXPERT_CHUNK):
            c1 = c0 + EXPERT_CHUNK
            hg = jnp.dot(xb, wu_bf[:, c0:c1], preferred_element_type=F32) + bu[:, c0:c1]
            hl = (jnp.dot(xb, wu_bf[:, D_EXPERT + c0:D_EXPERT + c1], preferred_element_type=F32)
                  + bu[:, D_EXPERT + c0:D_EXPERT + c1])
            hg = jnp.minimum(hg, SWIGLU_LIMIT)
            hl = jnp.clip(hl, -SWIGLU_LIMIT, SWIGLU_LIMIT)
            acts.append((hg * jax.nn.sigmoid(SWIGLU_ALPHA * hg) * (hl + 1.0)).astype(BF16))
        act = jnp.concatenate(acts, axis=1)
        bd = bd_ref[0, 0]
        row = lax.broadcasted_iota(jnp.int32, (nr, 1), 0) + r0
        mine = jnp.logical_and(row >= it_lo[j], row < it_hi[j])

        def down_proj(first):
            for c0 in range(0, D_MODEL, EXPERT_CHUNK):
                c1 = c0 + EXPERT_CHUNK
                y = jnp.dot(act, wd_bf[:, c0:c1], preferred_element_type=F32) + bd[:, c0:c1]
                for s0 in range(c0, c1, LANES):
                    dst = rows_of(ys_ref, s0 // LANES)
                    other = 0.0 if first else dst[...]
                    dst[...] = jnp.where(mine, y[:, s0 - c0:s0 - c0 + LANES], other)

        @pl.when(it_first[j] == 1)
        def _():
            down_proj(True)
            if nr < MOE_BM:
                other0 = (MOE_BM - nr - r0) * ROW_TILE
                ys_ref[other0:other0 + nr * ROW_TILE, :] = jnp.zeros((nr * ROW_TILE, LANES), F32)

        @pl.when(it_first[j] == 0)
        def _():
            down_proj(False)

    half = MOE_BM // 2
    valid = j < n_items[0]
    low_only = it_hi[j] <= half
    high_only = it_lo[j] >= half

    @pl.when(jnp.logical_and(valid, low_only))
    def _():
        ffn(0, half)

    @pl.when(jnp.logical_and(valid, high_only))
    def _():
        ffn(half, half)

    @pl.when(jnp.logical_and(valid, jnp.logical_not(jnp.logical_or(low_only, high_only))))
    def _():
        ffn(0, MOE_BM)


def _experts(items, xs, w_up, b_up4, w_down, b_down4, layer):
    n_items = items[0].shape[0]
    row_blk = pl.BlockSpec((MOE_BM * ROW_TILE, LANES), lambda j, ie, ib, *rest: (
        ib[jnp.minimum(j, rest[-1][0] - 1)], 0))

    def cur(j, ni):
        return jnp.minimum(j, ni[0] - 1)

    def b_map(j, ie, ib, *rest):
        return (layer, ie[cur(j, rest[-1])], 0, 0)

    return pl.pallas_call(
        functools.partial(_expert_kernel, layer=layer),
        grid_spec=pltpu.PrefetchScalarGridSpec(
            num_scalar_prefetch=len(items),
            grid=(n_items,),
            in_specs=[row_blk,
                      pl.BlockSpec(memory_space=pl.ANY),
                      pl.BlockSpec((1, 1, 1, 2 * D_EXPERT), b_map),
                      pl.BlockSpec(memory_space=pl.ANY),
                      pl.BlockSpec((1, 1, 1, D_MODEL), b_map)],
            out_specs=row_blk,
            scratch_shapes=[pltpu.VMEM((2, D_MODEL, 2 * D_EXPERT), F32),
                            pltpu.VMEM((2, D_EXPERT, D_MODEL), F32),
                            pltpu.VMEM((D_MODEL, 2 * D_EXPERT), BF16),
                            pltpu.VMEM((D_EXPERT, D_MODEL), BF16),
                            pltpu.SemaphoreType.DMA((2, 2))]),
        out_shape=jax.ShapeDtypeStruct(xs.shape, F32),
        compiler_params=_cparams(("arbitrary",)),
        name="experts",
    )(*items, xs, w_up, b_up4, w_down, b_down4)


def _expert_items(counts, n_rows):
    n_items = n_rows // MOE_BM + N_EXPERTS
    end = jnp.cumsum(counts)
    start = end - counts
    first_b = start // MOE_BM
    nb = jnp.where(counts > 0, (end - 1) // MOE_BM - first_b + 1, 0)
    item_end = jnp.cumsum(nb)
    item_start = item_end - nb
    total = item_end[-1]
    jc = jnp.minimum(jnp.arange(n_items, dtype=jnp.int32), total - 1)
    it_e = jnp.sum((item_end[None, :] <= jc[:, None]).astype(jnp.int32), axis=1)
    sel = it_e[:, None] == jnp.arange(N_EXPERTS, dtype=jnp.int32)[None, :]
    pick = lambda v: jnp.sum(jnp.where(sel, v[None, :], 0), axis=1)
    it_b = pick(first_b) + jc - pick(item_start)
    it_lo = jnp.maximum(pick(start), it_b * MOE_BM) - it_b * MOE_BM
    it_hi = jnp.minimum(pick(end), (it_b + 1) * MOE_BM) - it_b * MOE_BM
    prev_b = jnp.concatenate([jnp.full((1,), -1, jnp.int32), it_b[:-1]])
    it_first = (it_b != prev_b).astype(jnp.int32)
    prev_e = jnp.concatenate([jnp.full((1,), -1, jnp.int32), it_e[:-1]])
    it_new = (it_e != prev_e).astype(jnp.int32)
    ar = jnp.arange(N_EXPERTS, dtype=jnp.int32)
    used = counts > 0
    slot_e = (jnp.cumsum(used.astype(jnp.int32)) - 1) % 2
    later = jnp.logical_and(ar[None, :] > ar[:, None], used[None, :])
    next_e = jnp.min(jnp.where(later, ar[None, :], N_EXPERTS), axis=1)
    next_e = jnp.where(next_e == N_EXPERTS, -1, next_e)
    i32 = lambda v: v.astype(jnp.int32)
    return (i32(it_e), i32(it_b), i32(it_lo), i32(it_hi), it_first, it_new, i32(pick(slot_e)),
            i32(pick(next_e)), i32(total).reshape(1)), start


def _combine_kernel(dest_ref, dest_next_ref, gate_ref, x_ref, ys_ref, g_ref, b_ref, o_ref, buf, sems):
    t = x_ref.shape[0]
    step = pl.program_id(0)
    slot = step % 2

    def gather(idx_ref, s):
        def issue(i, carry):
            for k in range(TOP_K):
                _tile_copy(ys_ref, idx_ref[0, 0, i * TOP_K + k], buf.at[s, k], i,
                           sems.at[s]).start(priority=k % 2)
            return carry

        lax.fori_loop(0, t, issue, 0, unroll=2)

    @pl.when(step == 0)
    def _():
        gather(dest_ref, slot)

    @pl.when(step + 1 < pl.num_programs(0))
    def _():
        gather(dest_next_ref, 1 - slot)

    for k in range(TOP_K):
        pltpu.make_async_copy(ys_ref.at[pl.ds(0, t * ROW_TILE), :], buf.at[slot, k], sems.at[slot]).wait()
    gate = gate_ref[...]
    ffn = gate[:, 0:1] * _from_row_tiled(buf.at[slot, 0], t)
    for k in range(1, TOP_K):
        ffn = ffn + gate[:, k:k + 1] * _from_row_tiled(buf.at[slot, k], t)
    o_ref[...] = _layer_norm(DEEPNORM_ALPHA * x_ref[...] + ffn, g_ref[...], b_ref[...])


def _combine(dest3, gate, x2d, ys, ln_g, ln_b):
    n = x2d.shape[0]
    t = ROUTE_T
    const2 = lambda i: (0, 0)
    last = n // t - 1
    return pl.pallas_call(
        _combine_kernel,
        grid=(n // t,),
        in_specs=[pl.BlockSpec((1, 1, t * TOP_K), lambda i: (i, 0, 0), memory_space=pltpu.SMEM),
                  pl.BlockSpec((1, 1, t * TOP_K), lambda i: (jnp.minimum(i + 1, last), 0, 0),
                               memory_space=pltpu.SMEM),
                  pl.BlockSpec((t, LANES), lambda i: (i, 0)),
                  pl.BlockSpec((t, D_MODEL), lambda i: (i, 0)),
                  pl.BlockSpec(memory_space=pl.ANY),
                  pl.BlockSpec((1, D_MODEL), const2),
                  pl.BlockSpec((1, D_MODEL), const2)],
        out_specs=pl.BlockSpec((t, D_MODEL), lambda i: (i, 0)),
        out_shape=jax.ShapeDtypeStruct((n, D_MODEL), F32),
        scratch_shapes=[pltpu.VMEM((2, TOP_K, t * ROW_TILE, LANES), F32),
                        pltpu.SemaphoreType.DMA((2,))],
        compiler_params=_cparams(("arbitrary",)),
        name="combine_ln2",
    )(dest3, dest3, gate, x2d, ys, ln_g, ln_b)


def _a_head_perm():
    grp = A_HEADS // A_KV_HEADS
    order = []
    for j in range(grp):
        order += [j, grp + j]
    return np.concatenate([np.arange(h * HEAD_DIM, (h + 1) * HEAD_DIM) for h in order])


def _moe(x1, layer, w_router, b_router, w_up, b_up, w_down, b_down, ln_g, ln_b):
    n = x1.shape[0]
    wr = jnp.pad(w_router, ((0, 0), (0, LANES - N_EXPERTS)))
    br = jnp.pad(b_router, (0, LANES - N_EXPERTS), constant_values=NEG_INF).reshape(1, LANES)
    eidx, gate, rank, cnt = _route(x1, wr, br)
    items, start = _expert_items(cnt[0, :N_EXPERTS], n * TOP_K)
    e4 = eidx[:, :TOP_K]
    sel = e4[:, :, None] == jnp.arange(N_EXPERTS, dtype=jnp.int32)
    dest = jnp.sum(jnp.where(sel, start, 0), axis=-1) + rank[:, :TOP_K]
    dest3 = dest.reshape(n // ROUTE_T, 1, ROUTE_T * TOP_K).astype(jnp.int32)
    xs = _dispatch(x1, dest3)
    lead = (DEPTH, N_EXPERTS, 1)
    ys = _experts(items, xs, w_up, b_up.reshape(lead + (-1,)), w_down, b_down.reshape(lead + (-1,)), layer)
    return _combine(dest3, gate, x1, ys, ln_g.reshape(1, -1), ln_b.reshape(1, -1))


def kernel(x, w_in, a_sink, lambda_q1, lambda_k1, lambda_q2, lambda_k2, diff_norm_g, na_rpb,
           w_branch, w_out, ln1_g, ln1_b, w_router, b_router, w_up, b_up, w_down, b_down,
           ln2_g, ln2_b):
    bsz, seq, d = x.shape
    n = bsz * seq
    rows = seq // GRID_W
    perm = _a_head_perm()
    slopes_b = jnp.asarray(_ALIBI[A_HEADS:], F32)
    na_bias = _na_bias_table(na_rpb, rows)
    xcur = x.reshape(n, d)
    for l in range(DEPTH):
        w = w_in[l]
        qscale = HEAD_DIM ** -0.5
        w_qkv = jnp.concatenate([
            w[:, :512][:, perm] * qscale, w[:, 512:768],
            w[:, 768:1280] * qscale, w[:, 1280:2304],
            w[:, 2304:2816] * qscale, w[:, 2816:QKV_W]], axis=1).astype(BF16)
        w_gate = w[:, QKV_W:].astype(BF16)
        w_br = jnp.stack([w_branch[l, 0][perm], w_branch[l, 1], w_branch[l, 2]]).astype(BF16)
        lam_init = 0.8 - 0.6 * math.exp(-0.3 * l)
        lamv = jnp.stack([lambda_q1[l], lambda_k1[l], lambda_q2[l], lambda_k2[l]]).astype(F32)

        proj = _inproj(xcur, w_qkv)
        oa = _attn_a(proj, a_sink[l].astype(F32), bsz, seq)
        ob = _attn_b(proj, slopes_b, lamv, diff_norm_g[l].reshape(1, -1).astype(F32), bsz, seq, lam_init)
        oc = _attn_c(proj, na_bias, l, bsz, seq)
        x1 = _merge(xcur, oa, ob, oc, w_gate, w_br, w_out[l].astype(BF16),
                    ln1_g[l].reshape(1, -1), ln1_b[l].reshape(1, -1))
        xcur = _moe(x1, l, w_router[l], b_router[l], w_up, b_up, w_down, b_down,
                    ln2_g[l], ln2_b[l])
    return xcur.reshape(bsz, seq, d)
```

```python
import functools
import math

import numpy as np
import jax
import jax.numpy as jnp
from jax import lax
from jax.experimental import pallas as pl
from jax.experimental.pallas import tpu as pltpu

F32 = jnp.float32
BF16 = jnp.bfloat16

D_MODEL = 1024
DEPTH = 2
HEAD_DIM = 64
A_HEADS = 8
A_KV_HEADS = 2
WINDOW = 128
B_HEADS = 4
C_HEADS = 8
GRID_W = 64
NA_ROWS = 8
NA_COLS = 16
MIX_W = 512
N_BRANCH = 3
N_EXPERTS = 32
TOP_K = 4
D_EXPERT = 1024
SWIGLU_LIMIT = 7.0
SWIGLU_ALPHA = 1.702
LN_EPS = 1e-5
NEG_INF = -1e30
DEEPNORM_ALPHA = (2 * DEPTH) ** 0.25

LANES = 128
QKV_W = 3840
GATE_W = N_BRANCH * D_MODEL
A_Q_BLK, A_K_BLK, A_V_BLK = 0, 4, 5
B_Q_BLK, B_K_BLK, B_V_BLK = 6, 10, 14
C_Q_BLK, C_K_BLK, C_V_BLK = 18, 22, 26

A_BAND = 3 * WINDOW
B_TQ = 256
NA_QROWS = 4
NA_SLAB = NA_QROWS + NA_ROWS
MOE_BM = 512
EXPERT_CHUNK = 256
ROUTE_T = 512
ROUTER_T = 512
VMEM_LIMIT = 56 * 1024 * 1024

_ALIBI = [float(2.0 ** (-8.0 * (i + 1) / (A_HEADS + B_HEADS))) for i in range(A_HEADS + B_HEADS)]


def _cparams(sem):
    return pltpu.CompilerParams(dimension_semantics=sem, vmem_limit_bytes=VMEM_LIMIT)


def _layer_norm(y, g, b):
    mu = jnp.mean(y, axis=-1, keepdims=True)
    yc = y - mu
    var = jnp.mean(yc * yc, axis=-1, keepdims=True)
    return yc * lax.rsqrt(var + LN_EPS) * g + b


def _inproj_kernel(x_ref, w_ref, o_ref, *, chunk):
    xb = x_ref[...].astype(BF16)
    for c in range(QKV_W // chunk):
        sl = slice(c * chunk, (c + 1) * chunk)
        o_ref[:, sl] = jnp.dot(xb, w_ref[:, sl], preferred_element_type=F32).astype(BF16)


def _inproj(x2d, w_qkv):
    n = x2d.shape[0]
    tm = 1024
    return pl.pallas_call(
        functools.partial(_inproj_kernel, chunk=768),
        grid=(n // tm,),
        in_specs=[pl.BlockSpec((tm, D_MODEL), lambda i: (i, 0)),
                  pl.BlockSpec((D_MODEL, QKV_W), lambda i: (0, 0))],
        out_specs=pl.BlockSpec((tm, QKV_W), lambda i: (i, 0)),
        out_shape=jax.ShapeDtypeStruct((n, QKV_W), BF16),
        compiler_params=_cparams(("arbitrary",)),
        name="inproj",
    )(x2d, w_qkv)


def _attn_a_kernel(sink_ref, q_ref, k_ref, v_ref, o_ref, bias_ref, sink_tab, *, seq):
    nblk = seq // WINDOW
    heads = [j + 4 * hf for j in range(4) for hf in range(2)]
    lane = lax.broadcasted_iota(jnp.int32, (WINDOW, LANES), 1)
    low = lane < HEAD_DIM
    dn = (((1,), (1,)), ((), ()))

    @pl.when(pl.program_id(0) == 0)
    def _():
        kq = (lax.broadcasted_iota(jnp.int32, (WINDOW, A_BAND), 1)
              - lax.broadcasted_iota(jnp.int32, (WINDOW, A_BAND), 0))
        for c in range(3):
            dist = jnp.abs(kq - c * WINDOW).astype(F32)
            for i, h in enumerate(heads):
                bias_ref[c, i * WINDOW:(i + 1) * WINDOW, :] = jnp.where(
                    dist <= float(WINDOW), _ALIBI[h] * dist, -NEG_INF)
        for i, h in enumerate(heads):
            sink_tab[i * WINDOW:(i + 1) * WINDOW, :] = jnp.full((WINDOW, LANES), sink_ref[h], F32)

    sink_col = jnp.max(sink_tab[...], axis=-1, keepdims=True)

    def body(n, carry):
        q0 = pl.multiple_of(n * WINDOW, WINDOW)
        start = pl.multiple_of(jnp.clip((n - 1) * WINDOW, 0, seq - A_BAND), WINDOW)
        cfg = jnp.where(n == 0, 0, jnp.where(n == nblk - 1, 2, 1))
        kb = k_ref[pl.ds(start, A_BAND), :]
        vb = v_ref[pl.ds(start, A_BAND), :]
        pieces = []
        for j in range(4):
            qt = q_ref[pl.ds(q0, WINDOW), j * LANES:(j + 1) * LANES]
            zero = jnp.zeros_like(qt)
            lhs = jnp.concatenate([jnp.where(low, qt, zero), jnp.where(low, zero, qt)], axis=0)
            pieces.append(lax.dot_general(lhs, kb, dn, preferred_element_type=F32))
        t = jnp.concatenate(pieces, axis=0) - bias_ref[cfg]
        m = jnp.maximum(jnp.max(t, axis=-1, keepdims=True), sink_col)
        e = jnp.exp(t - m)
        den = jnp.sum(e, axis=-1, keepdims=True) + jnp.exp(sink_col - m)
        p = (e * (1.0 / den)).astype(BF16)
        pv = jnp.dot(p, vb, preferred_element_type=F32)
        for j in range(4):
            o = jnp.where(low, pv[2 * j * WINDOW:(2 * j + 1) * WINDOW],
                          pv[(2 * j + 1) * WINDOW:(2 * j + 2) * WINDOW])
            o_ref[pl.ds(q0, WINDOW), j * LANES:(j + 1) * LANES] = o.astype(BF16)
        return carry

    lax.fori_loop(0, nblk, body, 0, unroll=8)


def _attn_a(proj, sink, bsz, seq):
    return pl.pallas_call(
        functools.partial(_attn_a_kernel, seq=seq),
        grid=(bsz,),
        in_specs=[pl.BlockSpec(memory_space=pltpu.SMEM),
                  pl.BlockSpec((seq, 4 * LANES), lambda b: (b, A_Q_BLK // 4)),
                  pl.BlockSpec((seq, LANES), lambda b: (b, A_K_BLK)),
                  pl.BlockSpec((seq, LANES), lambda b: (b, A_V_BLK))],
        out_specs=pl.BlockSpec((seq, MIX_W), lambda b: (b, 0)),
        out_shape=jax.ShapeDtypeStruct((bsz * seq, MIX_W), BF16),
        scratch_shapes=[pltpu.VMEM((3, A_HEADS * WINDOW, A_BAND), F32),
                        pltpu.VMEM((A_HEADS * WINDOW, LANES), F32)],
        compiler_params=_cparams(("arbitrary",)),
        name="attn_a",
    )(sink, proj, proj, proj)


def _attn_b_kernel(slope_ref, lamv_ref, gain_ref, q_ref, k_ref, v_ref, o_ref, bias_ref, *,
                   seq, lam_init):
    nblk = seq // B_TQ
    slope = slope_ref[pl.program_id(0)]

    @pl.when(pl.program_id(1) == 0)
    def _():
        ji = (lax.broadcasted_iota(jnp.int32, (B_TQ, B_TQ), 1)
              - lax.broadcasted_iota(jnp.int32, (B_TQ, B_TQ), 0))
        for dd in range(2 * nblk - 1):
            bias_ref[dd] = slope * jnp.abs(ji + (dd - (nblk - 1)) * B_TQ).astype(F32)

    lv = lamv_ref[...]
    lam = (jnp.exp(jnp.sum(lv[0:1] * lv[1:2], axis=-1, keepdims=True))
           - jnp.exp(jnp.sum(lv[2:3] * lv[3:4], axis=-1, keepdims=True)) + lam_init)
    scale = gain_ref[...] * (1.0 - lam_init)
    lane = lax.broadcasted_iota(jnp.int32, (B_TQ, LANES), 1)
    low = lane < HEAD_DIM
    kall = k_ref[...]
    vall = v_ref[...]
    dn = (((1,), (1,)), ((), ()))

    def body(n, carry):
        q0 = pl.multiple_of(n * B_TQ, B_TQ)
        qt = q_ref[pl.ds(q0, B_TQ), :]
        zero = jnp.zeros_like(qt)
        bias = jnp.concatenate([bias_ref[kc - n + (nblk - 1)] for kc in range(nblk)], axis=1)
        es, ls = [], []
        for mp in range(2):
            qm = jnp.where(low, qt, zero) if mp == 0 else jnp.where(low, zero, qt)
            s = lax.dot_general(qm, kall, dn, preferred_element_type=F32) - bias
            m = jnp.max(s, axis=-1, keepdims=True)
            e = jnp.exp(s - m)
            ls.append(jnp.sum(e, axis=-1, keepdims=True))
            es.append(e.astype(BF16))
        a = es[0] * (1.0 / ls[0]).astype(BF16) - es[1] * (lam / ls[1]).astype(BF16)
        o = jnp.dot(a, vall, preferred_element_type=F32)
        o = o * lax.rsqrt(jnp.mean(o * o, axis=-1, keepdims=True) + LN_EPS)
        o_ref[pl.ds(q0, B_TQ), :] = (o * scale).astype(BF16)
        return carry

    lax.fori_loop(0, nblk, body, 0, unroll=4)


def _attn_b(proj, slopes_b, lamv, gain, bsz, seq, lam_init):
    nblk = seq // B_TQ
    return pl.pallas_call(
        functools.partial(_attn_b_kernel, seq=seq, lam_init=lam_init),
        grid=(B_HEADS, bsz),
        in_specs=[pl.BlockSpec(memory_space=pltpu.SMEM),
                  pl.BlockSpec((4, HEAD_DIM), lambda h, b: (0, 0)),
                  pl.BlockSpec((1, 2 * HEAD_DIM), lambda h, b: (0, 0)),
                  pl.BlockSpec((seq, LANES), lambda h, b: (b, B_Q_BLK + h)),
                  pl.BlockSpec((seq, LANES), lambda h, b: (b, B_K_BLK + h)),
                  pl.BlockSpec((seq, LANES), lambda h, b: (b, B_V_BLK + h))],
        out_specs=pl.BlockSpec((seq, LANES), lambda h, b: (b, h)),
        out_shape=jax.ShapeDtypeStruct((bsz * seq, MIX_W), BF16),
        scratch_shapes=[pltpu.VMEM((2 * nblk - 1, B_TQ, B_TQ), F32)],
        compiler_params=_cparams(("arbitrary", "arbitrary")),
        name="attn_b",
    )(slopes_b, lamv, gain, proj, proj, proj)


def _attn_c_kernel(wide_ref, q_ref, k_ref, v_ref, o_ref, bias_ref, *, rows):
    qtok = NA_QROWS * GRID_W
    ktok = NA_SLAB * GRID_W
    ngrp = rows // NA_QROWS
    lane = lax.broadcasted_iota(jnp.int32, (qtok, LANES), 1)
    low = lane < HEAD_DIM
    dn = (((1,), (1,)), ((), ()))

    @pl.when(pl.program_id(1) == 0)
    def _():
        ublk = lax.broadcasted_iota(jnp.int32, (GRID_W, ktok), 1) // GRID_W
        for ci, g in enumerate((0, 1, ngrp - 1)):
            slab0 = min(max(g * NA_QROWS - NA_ROWS // 2, 0), rows - NA_SLAB)
            for a in range(NA_QROWS):
                r = g * NA_QROWS + a
                rs = min(max(r - NA_ROWS // 2, 0), rows - NA_ROWS)
                off = (NA_SLAB + slab0 - r + NA_ROWS - 1) * GRID_W
                inside = jnp.logical_and(ublk >= rs - slab0, ublk < rs - slab0 + NA_ROWS)
                for hh in range(2):
                    win = wide_ref[0, hh, :, off:off + ktok]
                    r0 = hh * qtok + a * GRID_W
                    bias_ref[ci, r0:r0 + GRID_W, :] = jnp.where(inside, win, NEG_INF)

    def body(g, carry):
        slab0 = jnp.clip(g * NA_QROWS - NA_ROWS // 2, 0, rows - NA_SLAB)
        k0 = pl.multiple_of(slab0 * GRID_W, GRID_W)
        q0 = pl.multiple_of(g * qtok, qtok)
        cfg = jnp.where(g == 0, 0, jnp.where(g == ngrp - 1, 2, 1))
        qr = q_ref[pl.ds(q0, qtok), :]
        zero = jnp.zeros_like(qr)
        lhs = jnp.concatenate([jnp.where(low, qr, zero), jnp.where(low, zero, qr)], axis=0)
        ks = k_ref[pl.ds(k0, ktok), :]
        vs = v_ref[pl.ds(k0, ktok), :]
        s = lax.dot_general(lhs, ks, dn, preferred_element_type=F32) + bias_ref[cfg]
        m = jnp.max(s, axis=-1, keepdims=True)
        e = jnp.exp(s - m)
        p = (e * (1.0 / jnp.sum(e, axis=-1, keepdims=True))).astype(BF16)
        pv = jnp.dot(p, vs, preferred_element_type=F32)
        o = jnp.where(low, pv[:qtok], pv[qtok:])
        o_ref[pl.ds(q0, qtok), :] = o.astype(BF16)
        return carry

    lax.fori_loop(0, ngrp, body, 0, unroll=True)


def _attn_c(proj, bias_tab, layer, bsz, seq):
    rows = seq // GRID_W
    npair = C_HEADS // 2
    return pl.pallas_call(
        functools.partial(_attn_c_kernel, rows=rows),
        grid=(npair, bsz),
        in_specs=[pl.BlockSpec((1,) + bias_tab.shape[1:], lambda p, b: (layer * npair + p, 0, 0, 0)),
                  pl.BlockSpec((seq, LANES), lambda p, b: (b, C_Q_BLK + p)),
                  pl.BlockSpec((seq, LANES), lambda p, b: (b, C_K_BLK + p)),
                  pl.BlockSpec((seq, LANES), lambda p, b: (b, C_V_BLK + p))],
        out_specs=pl.BlockSpec((seq, LANES), lambda p, b: (b, p)),
        out_shape=jax.ShapeDtypeStruct((bsz * seq, MIX_W), BF16),
        scratch_shapes=[pltpu.VMEM((3, 2 * NA_QROWS * GRID_W, NA_SLAB * GRID_W), F32)],
        compiler_params=_cparams(("arbitrary", "arbitrary")),
        name="attn_c",
    )(bias_tab, proj, proj, proj)


def _na_bias_strip(rpb, rows):
    assert rows % NA_QROWS == 0 and rows >= NA_SLAB + NA_QROWS and rows >= NA_ROWS
    nh = rpb.shape[0] * C_HEADS
    qc = np.arange(GRID_W)[:, None]
    kc = np.arange(GRID_W)[None, :]
    cs = np.clip(qc - NA_COLS // 2, 0, GRID_W - NA_COLS)
    valid = (kc >= cs) & (kc < cs + NA_COLS)
    ndr = 2 * NA_ROWS - 1
    ncol = 2 * NA_COLS - 1
    colpick = (np.clip(kc - qc + NA_COLS - 1, 0, ncol - 1)[:, :, None] == np.arange(ncol))
    toep = jnp.einsum('hdc,qkc->hqdk', rpb.astype(F32).reshape(nh, ndr, ncol),
                      jnp.asarray(colpick, F32), precision=lax.Precision.HIGHEST)
    toep = jnp.where(valid[None, :, None, :], toep, NEG_INF).reshape(nh, GRID_W, ndr * GRID_W)
    wide = jnp.pad(toep, ((0, 0), (0, 0), (NA_SLAB * GRID_W, NA_SLAB * GRID_W)),
                   constant_values=NEG_INF)
    return wide.reshape(nh // 2, 2, GRID_W, (ndr + 2 * NA_SLAB) * GRID_W)


def _merge_kernel(x_ref, oa_ref, ob_ref, oc_ref, wg_ref, wbr_ref, wout_ref, g_ref, b_ref, o_ref):
    x = x_ref[...]
    xb = x.astype(BF16)
    merged = None
    for i, br_ref in enumerate((oa_ref, ob_ref, oc_ref)):
        gate = jax.nn.sigmoid(jnp.dot(xb, wg_ref[:, i * D_MODEL:(i + 1) * D_MODEL],
                                      preferred_element_type=F32))
        br = jnp.dot(br_ref[...], wbr_ref[i], preferred_element_type=F32)
        merged = gate * br if merged is None else merged + gate * br
    mix = jnp.dot(merged.astype(BF16), wout_ref[...], preferred_element_type=F32)
    o_ref[...] = _layer_norm(DEEPNORM_ALPHA * x + mix, g_ref[...], b_ref[...])


def _merge(x2d, oa, ob, oc, w_gate, w_br, w_out, ln_g, ln_b):
    n = x2d.shape[0]
    tm = 512
    const2 = lambda i: (0, 0)
    return pl.pallas_call(
        _merge_kernel,
        grid=(n // tm,),
        in_specs=[pl.BlockSpec((tm, D_MODEL), lambda i: (i, 0)),
                  pl.BlockSpec((tm, MIX_W), lambda i: (i, 0)),
                  pl.BlockSpec((tm, MIX_W), lambda i: (i, 0)),
                  pl.BlockSpec((tm, MIX_W), lambda i: (i, 0)),
                  pl.BlockSpec((D_MODEL, GATE_W), const2),
                  pl.BlockSpec((N_BRANCH, MIX_W, D_MODEL), lambda i: (0, 0, 0)),
                  pl.BlockSpec((D_MODEL, D_MODEL), const2),
                  pl.BlockSpec((1, D_MODEL), const2),
                  pl.BlockSpec((1, D_MODEL), const2)],
        out_specs=pl.BlockSpec((tm, D_MODEL), lambda i: (i, 0)),
        out_shape=jax.ShapeDtypeStruct((n, D_MODEL), F32),
        compiler_params=_cparams(("arbitrary",)),
        name="merge_ln1",
    )(x2d, oa, ob, oc, w_gate, w_br, w_out, ln_g, ln_b)


def _route_kernel(x_ref, w_ref, b_ref, eidx_ref, gate_ref, rank_ref, cnt_ref, carry_ref):
    t = x_ref.shape[0]

    @pl.when(pl.program_id(0) == 0)
    def _():
        carry_ref[...] = jnp.zeros_like(carry_ref)

    x = x_ref[...]
    w = w_ref[...]
    xh = x.astype(BF16)
    xt = (x - xh.astype(F32)).astype(BF16)
    wh = w.astype(BF16)
    wt = (w - wh.astype(F32)).astype(BF16)
    hh_ht = jnp.dot(xh, jnp.concatenate([wh, wt], axis=1), preferred_element_type=F32)
    th = jnp.dot(xt, wh, preferred_element_type=F32)
    logits = hh_ht[:, :LANES] + hh_ht[:, LANES:] + th + b_ref[...]
    lane = lax.broadcasted_iota(jnp.int32, (t, LANES), 1).astype(F32)
    work = logits
    sels, vals, idxs = [], [], []
    for _ in range(TOP_K):
        m = jnp.max(work, axis=-1, keepdims=True)
        idx = jnp.min(jnp.where(work == m, lane, float(LANES)), axis=-1, keepdims=True)
        sel = lane == idx
        work = jnp.where(sel, -jnp.inf, work)
        sels.append(sel)
        vals.append(m)
        idxs.append(idx)
    ex = [jnp.exp(v - vals[0]) for v in vals]
    den = ex[0] + ex[1] + ex[2] + ex[3]
    onehot = jnp.zeros((t, LANES), F32)
    for sel in sels:
        onehot = jnp.where(sel, 1.0, onehot)
    ri = lax.broadcasted_iota(jnp.int32, (t, t), 0)
    ci = lax.broadcasted_iota(jnp.int32, (t, t), 1)
    tri = jnp.where(ci < ri, 1.0, 0.0).astype(BF16)
    before = jnp.dot(tri, onehot.astype(BF16), preferred_element_type=F32) + carry_ref[...]
    eidx = jnp.zeros((t, LANES), F32)
    gate = jnp.zeros((t, LANES), F32)
    rank = jnp.zeros((t, LANES), F32)
    for k in range(TOP_K):
        rk = jnp.sum(jnp.where(sels[k], before, 0.0), axis=-1, keepdims=True)
        eidx = jnp.where(lane == float(k), idxs[k], eidx)
        gate = jnp.where(lane == float(k), ex[k] / den, gate)
        rank = jnp.where(lane == float(k), rk, rank)
    eidx_ref[...] = eidx.astype(jnp.int32)
    gate_ref[...] = gate
    rank_ref[...] = rank.astype(jnp.int32)
    carry_ref[...] += jnp.sum(onehot, axis=0, keepdims=True)
    cnt_ref[...] = carry_ref[...].astype(jnp.int32)


def _route(x2d, w_router_pad, b_router_pad):
    n = x2d.shape[0]
    t = ROUTER_T
    tile = pl.BlockSpec((t, LANES), lambda i: (i, 0))
    return pl.pallas_call(
        _route_kernel,
        grid=(n // t,),
        in_specs=[pl.BlockSpec((t, D_MODEL), lambda i: (i, 0)),
                  pl.BlockSpec((D_MODEL, LANES), lambda i: (0, 0)),
                  pl.BlockSpec((1, LANES), lambda i: (0, 0))],
        out_specs=[tile, tile, tile, pl.BlockSpec((1, LANES), lambda i: (0, 0))],
        out_shape=[jax.ShapeDtypeStruct((n, LANES), jnp.int32),
                   jax.ShapeDtypeStruct((n, LANES), F32),
                   jax.ShapeDtypeStruct((n, LANES), jnp.int32),
                   jax.ShapeDtypeStruct((1, LANES), jnp.int32)],
        scratch_shapes=[pltpu.VMEM((1, LANES), F32)],
        compiler_params=_cparams(("arbitrary",)),
        name="route",
    )(x2d, w_router_pad, b_router_pad)


ROW_TILE = D_MODEL // LANES


def _to_row_tiled(dst_ref, val):
    rows = val.shape[0]
    for c in range(ROW_TILE):
        dst_ref[pl.ds(c, rows, stride=ROW_TILE), :] = val[:, c * LANES:(c + 1) * LANES]


def _from_row_tiled(src_ref, rows):
    return jnp.concatenate([src_ref[pl.ds(c, rows, stride=ROW_TILE), :] for c in range(ROW_TILE)],
                           axis=1)


def _tile_copy(src_ref, src_row, dst_ref, dst_row, sem):
    return pltpu.make_async_copy(src_ref.at[pl.ds(pl.multiple_of(src_row * ROW_TILE, ROW_TILE), ROW_TILE), :],
                                 dst_ref.at[pl.ds(pl.multiple_of(dst_row * ROW_TILE, ROW_TILE), ROW_TILE), :],
                                 sem)


def _dispatch_kernel(dest_ref, x_ref, xs_ref, xt_ref, sems):
    t = x_ref.shape[0]
    step = pl.program_id(0)
    last = pl.num_programs(0) - 1
    slot = step % 2

    def drain(s):
        for _ in range(TOP_K):
            pltpu.make_async_copy(xt_ref.at[s], xs_ref.at[pl.ds(0, t * ROW_TILE), :], sems.at[s]).wait()

    @pl.when(step >= 2)
    def _():
        drain(slot)

    _to_row_tiled(xt_ref.at[slot], x_ref[...])

    def issue(i, carry):
        for k in range(TOP_K):
            _tile_copy(xt_ref.at[slot], i, xs_ref, dest_ref[0, 0, i * TOP_K + k],
                       sems.at[slot]).start(priority=k % 2)
        return carry

    lax.fori_loop(0, t, issue, 0, unroll=2)

    @pl.when(step == last)
    def _():
        @pl.when(last >= 1)
        def _():
            drain(1 - slot)
        drain(slot)


def _dispatch(x2d, dest3):
    n = x2d.shape[0]
    t = ROUTE_T
    return pl.pallas_call(
        _dispatch_kernel,
        grid=(n // t,),
        in_specs=[pl.BlockSpec((1, 1, t * TOP_K), lambda i: (i, 0, 0), memory_space=pltpu.SMEM),
                  pl.BlockSpec((t, D_MODEL), lambda i: (i, 0))],
        out_specs=pl.BlockSpec(memory_space=pl.ANY),
        out_shape=jax.ShapeDtypeStruct((n * TOP_K * ROW_TILE, LANES), F32),
        scratch_shapes=[pltpu.VMEM((2, t * ROW_TILE, LANES), F32), pltpu.SemaphoreType.DMA((2,))],
        compiler_params=_cparams(("arbitrary",)),
        name="dispatch",
    )(dest3, x2d)


def _expert_weight_copies(wu_hbm, wd_hbm, wu_f32, wd_f32, sems, layer, expert, slot):
    return (pltpu.make_async_copy(wu_hbm.at[layer, expert], wu_f32.at[slot], sems.at[slot, 0]),
            pltpu.make_async_copy(wd_hbm.at[layer, expert], wd_f32.at[slot], sems.at[slot, 1]))


def _expert_kernel(it_e, it_b, it_lo, it_hi, it_first, it_new, it_slot, it_next, n_items,
                   xs_ref, wu_hbm, bu_ref, wd_hbm, bd_ref, ys_ref,
                   wu_f32, wd_f32, wu_bf, wd_bf, sems, *, layer):
    del it_b
    j = pl.program_id(0)
    copies = functools.partial(_expert_weight_copies, wu_hbm, wd_hbm, wu_f32, wd_f32, sems, layer)

    @pl.when(j == 0)
    def _():
        for c in copies(it_e[0], 0):
            c.start()

    @pl.when(jnp.logical_and(j < n_items[0], it_new[j] == 1))
    def _():
        slot = it_slot[j]
        for c in copies(it_e[j], slot):
            c.wait()
        wu_bf[...] = wu_f32[slot].astype(BF16)
        wd_bf[...] = wd_f32[slot].astype(BF16)

        @pl.when(it_next[j] >= 0)
        def _():
            for c in copies(it_next[j], 1 - slot):
                c.start(priority=1)

    def ffn(r0, nr):
        def rows_of(ref, c):
            return ref.at[pl.ds(r0 * ROW_TILE + c, nr, stride=ROW_TILE), :]

        xb = jnp.concatenate([rows_of(xs_ref, c)[...] for c in range(ROW_TILE)], axis=1).astype(BF16)
        bu = bu_ref[0, 0]
        acts = []
        for c0 in range(0, D_EXPERT, EXPERT_CHUNK):
            c1 = c0 + EXPERT_CHUNK
            hg = jnp.dot(xb, wu_bf[:, c0:c1], preferred_element_type=F32) + bu[:, c0:c1]
            hl = (jnp.dot(xb, wu_bf[:, D_EXPERT + c0:D_EXPERT + c1], preferred_element_type=F32)
                  + bu[:, D_EXPERT + c0:D_EXPERT + c1])
            hg = jnp.minimum(hg, SWIGLU_LIMIT)
            hl = jnp.clip(hl, -SWIGLU_LIMIT, SWIGLU_LIMIT)
            acts.append((hg * jax.nn.sigmoid(SWIGLU_ALPHA * hg) * (hl + 1.0)).astype(BF16))
        act = jnp.concatenate(acts, axis=1)
        bd = bd_ref[0, 0]
        row = lax.broadcasted_iota(jnp.int32, (nr, 1), 0) + r0
        mine = jnp.logical_and(row >= it_lo[j], row < it_hi[j])

        def down_proj(first):
            for c0 in range(0, D_MODEL, EXPERT_CHUNK):
                c1 = c0 + EXPERT_CHUNK
                y = jnp.dot(act, wd_bf[:, c0:c1], preferred_element_type=F32) + bd[:, c0:c1]
                for s0 in range(c0, c1, LANES):
                    dst = rows_of(ys_ref, s0 // LANES)
                    other = 0.0 if first else dst[...]
                    dst[...] = jnp.where(mine, y[:, s0 - c0:s0 - c0 + LANES], other)

        @pl.when(it_first[j] == 1)
        def _():
            down_proj(True)
            if nr < MOE_BM:
                other0 = (MOE_BM - nr - r0) * ROW_TILE
                ys_ref[other0:other0 + nr * ROW_TILE, :] = jnp.zeros((nr * ROW_TILE, LANES), F32)

        @pl.when(it_first[j] == 0)
        def _():
            down_proj(False)

    half = MOE_BM // 2
    valid = j < n_items[0]
    low_only = it_hi[j] <= half
    high_only = it_lo[j] >= half

    @pl.when(jnp.logical_and(valid, low_only))
    def _():
        ffn(0, half)

    @pl.when(jnp.logical_and(valid, high_only))
    def _():
        ffn(half, half)

    @pl.when(jnp.logical_and(valid, jnp.logical_not(jnp.logical_or(low_only, high_only))))
    def _():
        ffn(0, MOE_BM)


def _experts(items, xs, w_up, b_up4, w_down, b_down4, layer):
    n_items = items[0].shape[0]
    row_blk = pl.BlockSpec((MOE_BM * ROW_TILE, LANES), lambda j, ie, ib, *rest: (
        ib[jnp.minimum(j, rest[-1][0] - 1)], 0))

    def cur(j, ni):
        return jnp.minimum(j, ni[0] - 1)

    def b_map(j, ie, ib, *rest):
        return (layer, ie[cur(j, rest[-1])], 0, 0)

    return pl.pallas_call(
        functools.partial(_expert_kernel, layer=layer),
        grid_spec=pltpu.PrefetchScalarGridSpec(
            num_scalar_prefetch=len(items),
            grid=(n_items,),
            in_specs=[row_blk,
                      pl.BlockSpec(memory_space=pl.ANY),
                      pl.BlockSpec((1, 1, 1, 2 * D_EXPERT), b_map),
                      pl.BlockSpec(memory_space=pl.ANY),
                      pl.BlockSpec((1, 1, 1, D_MODEL), b_map)],
            out_specs=row_blk,
            scratch_shapes=[pltpu.VMEM((2, D_MODEL, 2 * D_EXPERT), F32),
                            pltpu.VMEM((2, D_EXPERT, D_MODEL), F32),
                            pltpu.VMEM((D_MODEL, 2 * D_EXPERT), BF16),
                            pltpu.VMEM((D_EXPERT, D_MODEL), BF16),
                            pltpu.SemaphoreType.DMA((2, 2))]),
        out_shape=jax.ShapeDtypeStruct(xs.shape, F32),
        compiler_params=_cparams(("arbitrary",)),
        name="experts",
    )(*items, xs, w_up, b_up4, w_down, b_down4)


def _expert_items(counts, n_rows):
    n_items = n_rows // MOE_BM + N_EXPERTS
    end = jnp.cumsum(counts)
    start = end - counts
    first_b = start // MOE_BM
    nb = jnp.where(counts > 0, (end - 1) // MOE_BM - first_b + 1, 0)
    item_end = jnp.cumsum(nb)
    item_start = item_end - nb
    total = item_end[-1]
    jc = jnp.minimum(jnp.arange(n_items, dtype=jnp.int32), total - 1)
    it_e = jnp.sum((item_end[None, :] <= jc[:, None]).astype(jnp.int32), axis=1)
    sel = it_e[:, None] == jnp.arange(N_EXPERTS, dtype=jnp.int32)[None, :]
    pick = lambda v: jnp.sum(jnp.where(sel, v[None, :], 0), axis=1)
    it_b = pick(first_b) + jc - pick(item_start)
    it_lo = jnp.maximum(pick(start), it_b * MOE_BM) - it_b * MOE_BM
    it_hi = jnp.minimum(pick(end), (it_b + 1) * MOE_BM) - it_b * MOE_BM
    prev_b = jnp.concatenate([jnp.full((1,), -1, jnp.int32), it_b[:-1]])
    it_first = (it_b != prev_b).astype(jnp.int32)
    prev_e = jnp.concatenate([jnp.full((1,), -1, jnp.int32), it_e[:-1]])
    it_new = (it_e != prev_e).astype(jnp.int32)
    ar = jnp.arange(N_EXPERTS, dtype=jnp.int32)
    used = counts > 0
    slot_e = (jnp.cumsum(used.astype(jnp.int32)) - 1) % 2
    later = jnp.logical_and(ar[None, :] > ar[:, None], used[None, :])
    next_e = jnp.min(jnp.where(later, ar[None, :], N_EXPERTS), axis=1)
    next_e = jnp.where(next_e == N_EXPERTS, -1, next_e)
    i32 = lambda v: v.astype(jnp.int32)
    return (i32(it_e), i32(it_b), i32(it_lo), i32(it_hi), it_first, it_new, i32(pick(slot_e)),
            i32(pick(next_e)), i32(total).reshape(1)), start


def _combine_kernel(dest_ref, dest_next_ref, gate_ref, x_ref, ys_ref, g_ref, b_ref, o_ref, buf, sems):
    t = x_ref.shape[0]
    step = pl.program_id(0)
    slot = step % 2

    def gather(idx_ref, s):
        def issue(i, carry):
            for k in range(TOP_K):
                _tile_copy(ys_ref, idx_ref[0, 0, i * TOP_K + k], buf.at[s, k], i,
                           sems.at[s]).start(priority=k % 2)
            return carry

        lax.fori_loop(0, t, issue, 0, unroll=2)

    @pl.when(step == 0)
    def _():
        gather(dest_ref, slot)

    @pl.when(step + 1 < pl.num_programs(0))
    def _():
        gather(dest_next_ref, 1 - slot)

    for k in range(TOP_K):
        pltpu.make_async_copy(ys_ref.at[pl.ds(0, t * ROW_TILE), :], buf.at[slot, k], sems.at[slot]).wait()
    gate = gate_ref[...]
    ffn = gate[:, 0:1] * _from_row_tiled(buf.at[slot, 0], t)
    for k in range(1, TOP_K):
        ffn = ffn + gate[:, k:k + 1] * _from_row_tiled(buf.at[slot, k], t)
    o_ref[...] = _layer_norm(DEEPNORM_ALPHA * x_ref[...] + ffn, g_ref[...], b_ref[...])


def _combine(dest3, gate, x2d, ys, ln_g, ln_b):
    n = x2d.shape[0]
    t = ROUTE_T
    const2 = lambda i: (0, 0)
    last = n // t - 1
    return pl.pallas_call(
        _combine_kernel,
        grid=(n // t,),
        in_specs=[pl.BlockSpec((1, 1, t * TOP_K), lambda i: (i, 0, 0), memory_space=pltpu.SMEM),
                  pl.BlockSpec((1, 1, t * TOP_K), lambda i: (jnp.minimum(i + 1, last), 0, 0),
                               memory_space=pltpu.SMEM),
                  pl.BlockSpec((t, LANES), lambda i: (i, 0)),
                  pl.BlockSpec((t, D_MODEL), lambda i: (i, 0)),
                  pl.BlockSpec(memory_space=pl.ANY),
                  pl.BlockSpec((1, D_MODEL), const2),
                  pl.BlockSpec((1, D_MODEL), const2)],
        out_specs=pl.BlockSpec((t, D_MODEL), lambda i: (i, 0)),
        out_shape=jax.ShapeDtypeStruct((n, D_MODEL), F32),
        scratch_shapes=[pltpu.VMEM((2, TOP_K, t * ROW_TILE, LANES), F32),
                        pltpu.SemaphoreType.DMA((2,))],
        compiler_params=_cparams(("arbitrary",)),
        name="combine_ln2",
    )(dest3, dest3, gate, x2d, ys, ln_g, ln_b)


def _a_head_perm():
    grp = A_HEADS // A_KV_HEADS
    order = []
    for j in range(grp):
        order += [j, grp + j]
    return np.concatenate([np.arange(h * HEAD_DIM, (h + 1) * HEAD_DIM) for h in order])


def _moe(x1, layer, w_router, b_router, w_up, b_up, w_down, b_down, ln_g, ln_b):
    n = x1.shape[0]
    wr = jnp.pad(w_router, ((0, 0), (0, LANES - N_EXPERTS)))
    br = jnp.pad(b_router, (0, LANES - N_EXPERTS), constant_values=NEG_INF).reshape(1, LANES)
    eidx, gate, rank, cnt = _route(x1, wr, br)
    items, start = _expert_items(cnt[0, :N_EXPERTS], n * TOP_K)
    e4 = eidx[:, :TOP_K]
    sel = e4[:, :, None] == jnp.arange(N_EXPERTS, dtype=jnp.int32)
    dest = jnp.sum(jnp.where(sel, start, 0), axis=-1) + rank[:, :TOP_K]
    dest3 = dest.reshape(n // ROUTE_T, 1, ROUTE_T * TOP_K).astype(jnp.int32)
    xs = _dispatch(x1, dest3)
    lead = (DEPTH, N_EXPERTS, 1)
    ys = _experts(items, xs, w_up, b_up.reshape(lead + (-1,)), w_down, b_down.reshape(lead + (-1,)), layer)
    return _combine(dest3, gate, x1, ys, ln_g.reshape(1, -1), ln_b.reshape(1, -1))


def kernel(x, w_in, a_sink, lambda_q1, lambda_k1, lambda_q2, lambda_k2, diff_norm_g, na_rpb,
           w_branch, w_out, ln1_g, ln1_b, w_router, b_router, w_up, b_up, w_down, b_down,
           ln2_g, ln2_b):
    bsz, seq, d = x.shape
    n = bsz * seq
    rows = seq // GRID_W
    perm = _a_head_perm()
    slopes_b = jnp.asarray(_ALIBI[A_HEADS:], F32)
    na_bias = _na_bias_strip(na_rpb, rows)
    xcur = x.reshape(n, d)
    for l in range(DEPTH):
        w = w_in[l]
        qscale = HEAD_DIM ** -0.5
        w_qkv = jnp.concatenate([
            w[:, :512][:, perm] * qscale, w[:, 512:768],
            w[:, 768:1280] * qscale, w[:, 1280:2304],
            w[:, 2304:2816] * qscale, w[:, 2816:QKV_W]], axis=1).astype(BF16)
        w_gate = w[:, QKV_W:].astype(BF16)
        w_br = jnp.stack([w_branch[l, 0][perm], w_branch[l, 1], w_branch[l, 2]]).astype(BF16)
        lam_init = 0.8 - 0.6 * math.exp(-0.3 * l)
        lamv = jnp.stack([lambda_q1[l], lambda_k1[l], lambda_q2[l], lambda_k2[l]]).astype(F32)

        proj = _inproj(xcur, w_qkv)
        oa = _attn_a(proj, a_sink[l].astype(F32), bsz, seq)
        ob = _attn_b(proj, slopes_b, lamv, diff_norm_g[l].reshape(1, -1).astype(F32), bsz, seq, lam_init)
        oc = _attn_c(proj, na_bias, l, bsz, seq)
        x1 = _merge(xcur, oa, ob, oc, w_gate, w_br, w_out[l].astype(BF16),
                    ln1_g[l].reshape(1, -1), ln1_b[l].reshape(1, -1))
        xcur = _moe(x1, l, w_router[l], b_router[l], w_up, b_up, w_down, b_down,
                    ln2_g[l], ln2_b[l])
    return xcur.reshape(bsz, seq, d)
```

```python
import functools
import math

import numpy as np
import jax
import jax.numpy as jnp
from jax import lax
from jax.experimental import pallas as pl
from jax.experimental.pallas import tpu as pltpu

F32 = jnp.float32
BF16 = jnp.bfloat16

D_MODEL = 1024
DEPTH = 2
HEAD_DIM = 64
A_HEADS = 8
A_KV_HEADS = 2
WINDOW = 128
B_HEADS = 4
C_HEADS = 8
GRID_W = 64
NA_ROWS = 8
NA_COLS = 16
MIX_W = 512
N_BRANCH = 3
N_EXPERTS = 32
TOP_K = 4
D_EXPERT = 1024
SWIGLU_LIMIT = 7.0
SWIGLU_ALPHA = 1.702
LN_EPS = 1e-5
NEG_INF = -1e30
DEEPNORM_ALPHA = (2 * DEPTH) ** 0.25

LANES = 128
QKV_W = 3840
GATE_W = N_BRANCH * D_MODEL
A_Q_BLK, A_K_BLK, A_V_BLK = 0, 4, 5
B_Q_BLK, B_K_BLK, B_V_BLK = 6, 10, 14
C_Q_BLK, C_K_BLK, C_V_BLK = 18, 22, 26

A_BAND = 3 * WINDOW
B_TQ = 256
NA_QROWS = 4
NA_SLAB = NA_QROWS + NA_ROWS
MOE_BM = 512
EXPERT_CHUNK = 256
ROUTE_T = 512
ROUTER_T = 512
VMEM_LIMIT = 56 * 1024 * 1024

_ALIBI = [float(2.0 ** (-8.0 * (i + 1) / (A_HEADS + B_HEADS))) for i in range(A_HEADS + B_HEADS)]


def _cparams(sem):
    return pltpu.CompilerParams(dimension_semantics=sem, vmem_limit_bytes=VMEM_LIMIT)


def _layer_norm(y, g, b):
    mu = jnp.mean(y, axis=-1, keepdims=True)
    yc = y - mu
    var = jnp.mean(yc * yc, axis=-1, keepdims=True)
    return yc * lax.rsqrt(var + LN_EPS) * g + b


def _inproj_kernel(x_ref, w_ref, o_ref, *, chunk):
    xb = x_ref[...].astype(BF16)
    for c in range(QKV_W // chunk):
        sl = slice(c * chunk, (c + 1) * chunk)
        o_ref[:, sl] = jnp.dot(xb, w_ref[:, sl], preferred_element_type=F32).astype(BF16)


def _inproj(x2d, w_qkv):
    n = x2d.shape[0]
    tm = 1024
    return pl.pallas_call(
        functools.partial(_inproj_kernel, chunk=768),
        grid=(n // tm,),
        in_specs=[pl.BlockSpec((tm, D_MODEL), lambda i: (i, 0)),
                  pl.BlockSpec((D_MODEL, QKV_W), lambda i: (0, 0))],
        out_specs=pl.BlockSpec((tm, QKV_W), lambda i: (i, 0)),
        out_shape=jax.ShapeDtypeStruct((n, QKV_W), BF16),
        compiler_params=_cparams(("arbitrary",)),
        name="inproj",
    )(x2d, w_qkv)


def _attn_a_kernel(sink_ref, q_ref, k_ref, v_ref, o_ref, bias_ref, sink_tab, *, seq):
    nblk = seq // WINDOW
    heads = [j + 4 * hf for j in range(4) for hf in range(2)]
    lane = lax.broadcasted_iota(jnp.int32, (WINDOW, LANES), 1)
    low = lane < HEAD_DIM
    dn = (((1,), (1,)), ((), ()))

    @pl.when(pl.program_id(0) == 0)
    def _():
        kq = (lax.broadcasted_iota(jnp.int32, (WINDOW, A_BAND), 1)
              - lax.broadcasted_iota(jnp.int32, (WINDOW, A_BAND), 0))
        for c in range(3):
            dist = jnp.abs(kq - c * WINDOW).astype(F32)
            for i, h in enumerate(heads):
                bias_ref[c, i * WINDOW:(i + 1) * WINDOW, :] = jnp.where(
                    dist <= float(WINDOW), _ALIBI[h] * dist, -NEG_INF)
        for i, h in enumerate(heads):
            sink_tab[i * WINDOW:(i + 1) * WINDOW, :] = jnp.full((WINDOW, LANES), sink_ref[h], F32)

    sink_col = jnp.max(sink_tab[...], axis=-1, keepdims=True)

    def body(n, carry):
        q0 = pl.multiple_of(n * WINDOW, WINDOW)
        start = pl.multiple_of(jnp.clip((n - 1) * WINDOW, 0, seq - A_BAND), WINDOW)
        cfg = jnp.where(n == 0, 0, jnp.where(n == nblk - 1, 2, 1))
        kb = k_ref[pl.ds(start, A_BAND), :]
        vb = v_ref[pl.ds(start, A_BAND), :]
        pieces = []
        for j in range(4):
            qt = q_ref[pl.ds(q0, WINDOW), j * LANES:(j + 1) * LANES]
            zero = jnp.zeros_like(qt)
            lhs = jnp.concatenate([jnp.where(low, qt, zero), jnp.where(low, zero, qt)], axis=0)
            pieces.append(lax.dot_general(lhs, kb, dn, preferred_element_type=F32))
        t = jnp.concatenate(pieces, axis=0) - bias_ref[cfg]
        m = jnp.maximum(jnp.max(t, axis=-1, keepdims=True), sink_col)
        e = jnp.exp(t - m)
        den = jnp.sum(e, axis=-1, keepdims=True) + jnp.exp(sink_col - m)
        p = (e * (1.0 / den)).astype(BF16)
        pv = jnp.dot(p, vb, preferred_element_type=F32)
        for j in range(4):
            o = jnp.where(low, pv[2 * j * WINDOW:(2 * j + 1) * WINDOW],
                          pv[(2 * j + 1) * WINDOW:(2 * j + 2) * WINDOW])
            o_ref[pl.ds(q0, WINDOW), j * LANES:(j + 1) * LANES] = o.astype(BF16)
        return carry

    lax.fori_loop(0, nblk, body, 0, unroll=8)


def _attn_a(proj, sink, bsz, seq):
    return pl.pallas_call(
        functools.partial(_attn_a_kernel, seq=seq),
        grid=(bsz,),
        in_specs=[pl.BlockSpec(memory_space=pltpu.SMEM),
                  pl.BlockSpec((seq, 4 * LANES), lambda b: (b, A_Q_BLK // 4)),
                  pl.BlockSpec((seq, LANES), lambda b: (b, A_K_BLK)),
                  pl.BlockSpec((seq, LANES), lambda b: (b, A_V_BLK))],
        out_specs=pl.BlockSpec((seq, MIX_W), lambda b: (b, 0)),
        out_shape=jax.ShapeDtypeStruct((bsz * seq, MIX_W), BF16),
        scratch_shapes=[pltpu.VMEM((3, A_HEADS * WINDOW, A_BAND), F32),
                        pltpu.VMEM((A_HEADS * WINDOW, LANES), F32)],
        compiler_params=_cparams(("arbitrary",)),
        name="attn_a",
    )(sink, proj, proj, proj)


def _attn_b_kernel(slope_ref, lamv_ref, gain_ref, q_ref, k_ref, v_ref, o_ref, bias_ref, *,
                   seq, lam_init):
    nblk = seq // B_TQ
    slope = slope_ref[pl.program_id(0)]

    @pl.when(pl.program_id(1) == 0)
    def _():
        ji = (lax.broadcasted_iota(jnp.int32, (B_TQ, B_TQ), 1)
              - lax.broadcasted_iota(jnp.int32, (B_TQ, B_TQ), 0))
        for dd in range(2 * nblk - 1):
            bias_ref[dd] = slope * jnp.abs(ji + (dd - (nblk - 1)) * B_TQ).astype(F32)

    lv = lamv_ref[...]
    lam = (jnp.exp(jnp.sum(lv[0:1] * lv[1:2], axis=-1, keepdims=True))
           - jnp.exp(jnp.sum(lv[2:3] * lv[3:4], axis=-1, keepdims=True)) + lam_init)
    scale = gain_ref[...] * (1.0 - lam_init)
    lane = lax.broadcasted_iota(jnp.int32, (B_TQ, LANES), 1)
    low = lane < HEAD_DIM
    kall = k_ref[...]
    vall = v_ref[...]
    dn = (((1,), (1,)), ((), ()))

    def body(n, carry):
        q0 = pl.multiple_of(n * B_TQ, B_TQ)
        qt = q_ref[pl.ds(q0, B_TQ), :]
        zero = jnp.zeros_like(qt)
        bias = jnp.concatenate([bias_ref[kc - n + (nblk - 1)] for kc in range(nblk)], axis=1)
        es, ls = [], []
        for mp in range(2):
            qm = jnp.where(low, qt, zero) if mp == 0 else jnp.where(low, zero, qt)
            s = lax.dot_general(qm, kall, dn, preferred_element_type=F32) - bias
            m = jnp.max(s, axis=-1, keepdims=True)
            e = jnp.exp(s - m)
            ls.append(jnp.sum(e, axis=-1, keepdims=True))
            es.append(e.astype(BF16))
        a = es[0] * (1.0 / ls[0]).astype(BF16) - es[1] * (lam / ls[1]).astype(BF16)
        o = jnp.dot(a, vall, preferred_element_type=F32)
        o = o * lax.rsqrt(jnp.mean(o * o, axis=-1, keepdims=True) + LN_EPS)
        o_ref[pl.ds(q0, B_TQ), :] = (o * scale).astype(BF16)
        return carry

    lax.fori_loop(0, nblk, body, 0, unroll=True)


def _attn_b(proj, slopes_b, lamv, gain, bsz, seq, lam_init):
    nblk = seq // B_TQ
    return pl.pallas_call(
        functools.partial(_attn_b_kernel, seq=seq, lam_init=lam_init),
        grid=(B_HEADS, bsz),
        in_specs=[pl.BlockSpec(memory_space=pltpu.SMEM),
                  pl.BlockSpec((4, HEAD_DIM), lambda h, b: (0, 0)),
                  pl.BlockSpec((1, 2 * HEAD_DIM), lambda h, b: (0, 0)),
                  pl.BlockSpec((seq, LANES), lambda h, b: (b, B_Q_BLK + h)),
                  pl.BlockSpec((seq, LANES), lambda h, b: (b, B_K_BLK + h)),
                  pl.BlockSpec((seq, LANES), lambda h, b: (b, B_V_BLK + h))],
        out_specs=pl.BlockSpec((seq, LANES), lambda h, b: (b, h)),
        out_shape=jax.ShapeDtypeStruct((bsz * seq, MIX_W), BF16),
        scratch_shapes=[pltpu.VMEM((2 * nblk - 1, B_TQ, B_TQ), F32)],
        compiler_params=_cparams(("arbitrary", "arbitrary")),
        name="attn_b",
    )(slopes_b, lamv, gain, proj, proj, proj)


def _attn_c_kernel(wide_ref, q_ref, k_ref, v_ref, o_ref, bias_ref, *, rows):
    qtok = NA_QROWS * GRID_W
    ktok = NA_SLAB * GRID_W
    ngrp = rows // NA_QROWS
    lane = lax.broadcasted_iota(jnp.int32, (qtok, LANES), 1)
    low = lane < HEAD_DIM
    dn = (((1,), (1,)), ((), ()))

    @pl.when(pl.program_id(1) == 0)
    def _():
        ublk = lax.broadcasted_iota(jnp.int32, (GRID_W, ktok), 1) // GRID_W
        for ci, g in enumerate((0, 1, ngrp - 1)):
            slab0 = min(max(g * NA_QROWS - NA_ROWS // 2, 0), rows - NA_SLAB)
            for a in range(NA_QROWS):
                r = g * NA_QROWS + a
                rs = min(max(r - NA_ROWS // 2, 0), rows - NA_ROWS)
                off = (NA_SLAB + slab0 - r + NA_ROWS - 1) * GRID_W
                inside = jnp.logical_and(ublk >= rs - slab0, ublk < rs - slab0 + NA_ROWS)
                for hh in range(2):
                    win = wide_ref[0, hh, :, off:off + ktok]
                    r0 = hh * qtok + a * GRID_W
                    bias_ref[ci, r0:r0 + GRID_W, :] = jnp.where(inside, win, NEG_INF)

    def body(g, carry):
        slab0 = jnp.clip(g * NA_QROWS - NA_ROWS // 2, 0, rows - NA_SLAB)
        k0 = pl.multiple_of(slab0 * GRID_W, GRID_W)
        q0 = pl.multiple_of(g * qtok, qtok)
        cfg = jnp.where(g == 0, 0, jnp.where(g == ngrp - 1, 2, 1))
        qr = q_ref[pl.ds(q0, qtok), :]
        zero = jnp.zeros_like(qr)
        lhs = jnp.concatenate([jnp.where(low, qr, zero), jnp.where(low, zero, qr)], axis=0)
        ks = k_ref[pl.ds(k0, ktok), :]
        vs = v_ref[pl.ds(k0, ktok), :]
        s = lax.dot_general(lhs, ks, dn, preferred_element_type=F32) + bias_ref[cfg]
        m = jnp.max(s, axis=-1, keepdims=True)
        e = jnp.exp(s - m)
        p = (e * (1.0 / jnp.sum(e, axis=-1, keepdims=True))).astype(BF16)
        pv = jnp.dot(p, vs, preferred_element_type=F32)
        o = jnp.where(low, pv[:qtok], pv[qtok:])
        o_ref[pl.ds(q0, qtok), :] = o.astype(BF16)
        return carry

    lax.fori_loop(0, ngrp, body, 0, unroll=True)


def _attn_c(proj, bias_tab, layer, bsz, seq):
    rows = seq // GRID_W
    npair = C_HEADS // 2
    return pl.pallas_call(
        functools.partial(_attn_c_kernel, rows=rows),
        grid=(npair, bsz),
        in_specs=[pl.BlockSpec((1,) + bias_tab.shape[1:], lambda p, b: (layer * npair + p, 0, 0, 0)),
                  pl.BlockSpec((seq, LANES), lambda p, b: (b, C_Q_BLK + p)),
                  pl.BlockSpec((seq, LANES), lambda p, b: (b, C_K_BLK + p)),
                  pl.BlockSpec((seq, LANES), lambda p, b: (b, C_V_BLK + p))],
        out_specs=pl.BlockSpec((seq, LANES), lambda p, b: (b, p)),
        out_shape=jax.ShapeDtypeStruct((bsz * seq, MIX_W), BF16),
        scratch_shapes=[pltpu.VMEM((3, 2 * NA_QROWS * GRID_W, NA_SLAB * GRID_W), F32)],
        compiler_params=_cparams(("arbitrary", "arbitrary")),
        name="attn_c",
    )(bias_tab, proj, proj, proj)


def _na_bias_strip(rpb, rows):
    assert rows % NA_QROWS == 0 and rows >= NA_SLAB + NA_QROWS and rows >= NA_ROWS
    nh = rpb.shape[0] * C_HEADS
    qc = np.arange(GRID_W)[:, None]
    kc = np.arange(GRID_W)[None, :]
    cs = np.clip(qc - NA_COLS // 2, 0, GRID_W - NA_COLS)
    valid = (kc >= cs) & (kc < cs + NA_COLS)
    ndr = 2 * NA_ROWS - 1
    ncol = 2 * NA_COLS - 1
    colpick = (np.clip(kc - qc + NA_COLS - 1, 0, ncol - 1)[:, :, None] == np.arange(ncol))
    toep = jnp.einsum('hdc,qkc->hqdk', rpb.astype(F32).reshape(nh, ndr, ncol),
                      jnp.asarray(colpick, F32), precision=lax.Precision.HIGHEST)
    toep = jnp.where(valid[None, :, None, :], toep, NEG_INF).reshape(nh, GRID_W, ndr * GRID_W)
    wide = jnp.pad(toep, ((0, 0), (0, 0), (NA_SLAB * GRID_W, NA_SLAB * GRID_W)),
                   constant_values=NEG_INF)
    return wide.reshape(nh // 2, 2, GRID_W, (ndr + 2 * NA_SLAB) * GRID_W)


def _merge_kernel(x_ref, oa_ref, ob_ref, oc_ref, wg_ref, wbr_ref, wout_ref, g_ref, b_ref, o_ref):
    x = x_ref[...]
    xb = x.astype(BF16)
    merged = None
    for i, br_ref in enumerate((oa_ref, ob_ref, oc_ref)):
        gate = jax.nn.sigmoid(jnp.dot(xb, wg_ref[:, i * D_MODEL:(i + 1) * D_MODEL],
                                      preferred_element_type=F32))
        br = jnp.dot(br_ref[...], wbr_ref[i], preferred_element_type=F32)
        merged = gate * br if merged is None else merged + gate * br
    mix = jnp.dot(merged.astype(BF16), wout_ref[...], preferred_element_type=F32)
    o_ref[...] = _layer_norm(DEEPNORM_ALPHA * x + mix, g_ref[...], b_ref[...])


def _merge(x2d, oa, ob, oc, w_gate, w_br, w_out, ln_g, ln_b):
    n = x2d.shape[0]
    tm = 512
    const2 = lambda i: (0, 0)
    return pl.pallas_call(
        _merge_kernel,
        grid=(n // tm,),
        in_specs=[pl.BlockSpec((tm, D_MODEL), lambda i: (i, 0)),
                  pl.BlockSpec((tm, MIX_W), lambda i: (i, 0)),
                  pl.BlockSpec((tm, MIX_W), lambda i: (i, 0)),
                  pl.BlockSpec((tm, MIX_W), lambda i: (i, 0)),
                  pl.BlockSpec((D_MODEL, GATE_W), const2),
                  pl.BlockSpec((N_BRANCH, MIX_W, D_MODEL), lambda i: (0, 0, 0)),
                  pl.BlockSpec((D_MODEL, D_MODEL), const2),
                  pl.BlockSpec((1, D_MODEL), const2),
                  pl.BlockSpec((1, D_MODEL), const2)],
        out_specs=pl.BlockSpec((tm, D_MODEL), lambda i: (i, 0)),
        out_shape=jax.ShapeDtypeStruct((n, D_MODEL), F32),
        compiler_params=_cparams(("arbitrary",)),
        name="merge_ln1",
    )(x2d, oa, ob, oc, w_gate, w_br, w_out, ln_g, ln_b)


def _route_kernel(x_ref, w_ref, b_ref, eidx_ref, gate_ref, rank_ref, cnt_ref, carry_ref):
    t = x_ref.shape[0]

    @pl.when(pl.program_id(0) == 0)
    def _():
        carry_ref[...] = jnp.zeros_like(carry_ref)

    x = x_ref[...]
    w = w_ref[...]
    xh = x.astype(BF16)
    xt = (x - xh.astype(F32)).astype(BF16)
    wh = w.astype(BF16)
    wt = (w - wh.astype(F32)).astype(BF16)
    hh_ht = jnp.dot(xh, jnp.concatenate([wh, wt], axis=1), preferred_element_type=F32)
    th = jnp.dot(xt, wh, preferred_element_type=F32)
    logits = hh_ht[:, :LANES] + hh_ht[:, LANES:] + th + b_ref[...]
    lane = lax.broadcasted_iota(jnp.int32, (t, LANES), 1).astype(F32)
    work = logits
    sels, vals, idxs = [], [], []
    for _ in range(TOP_K):
        m = jnp.max(work, axis=-1, keepdims=True)
        idx = jnp.min(jnp.where(work == m, lane, float(LANES)), axis=-1, keepdims=True)
        sel = lane == idx
        work = jnp.where(sel, -jnp.inf, work)
        sels.append(sel)
        vals.append(m)
        idxs.append(idx)
    ex = [jnp.exp(v - vals[0]) for v in vals]
    den = ex[0] + ex[1] + ex[2] + ex[3]
    onehot = jnp.zeros((t, LANES), F32)
    for sel in sels:
        onehot = jnp.where(sel, 1.0, onehot)
    ri = lax.broadcasted_iota(jnp.int32, (t, t), 0)
    ci = lax.broadcasted_iota(jnp.int32, (t, t), 1)
    tri = jnp.where(ci < ri, 1.0, 0.0).astype(BF16)
    before = jnp.dot(tri, onehot.astype(BF16), preferred_element_type=F32) + carry_ref[...]
    eidx = jnp.zeros((t, LANES), F32)
    gate = jnp.zeros((t, LANES), F32)
    rank = jnp.zeros((t, LANES), F32)
    for k in range(TOP_K):
        rk = jnp.sum(jnp.where(sels[k], before, 0.0), axis=-1, keepdims=True)
        eidx = jnp.where(lane == float(k), idxs[k], eidx)
        gate = jnp.where(lane == float(k), ex[k] / den, gate)
        rank = jnp.where(lane == float(k), rk, rank)
    eidx_ref[...] = eidx.astype(jnp.int32)
    gate_ref[...] = gate
    rank_ref[...] = rank.astype(jnp.int32)
    carry_ref[...] += jnp.sum(onehot, axis=0, keepdims=True)
    cnt_ref[...] = carry_ref[...].astype(jnp.int32)


def _route(x2d, w_router_pad, b_router_pad):
    n = x2d.shape[0]
    t = ROUTER_T
    tile = pl.BlockSpec((t, LANES), lambda i: (i, 0))
    return pl.pallas_call(
        _route_kernel,
        grid=(n // t,),
        in_specs=[pl.BlockSpec((t, D_MODEL), lambda i: (i, 0)),
                  pl.BlockSpec((D_MODEL, LANES), lambda i: (0, 0)),
                  pl.BlockSpec((1, LANES), lambda i: (0, 0))],
        out_specs=[tile, tile, tile, pl.BlockSpec((1, LANES), lambda i: (0, 0))],
        out_shape=[jax.ShapeDtypeStruct((n, LANES), jnp.int32),
                   jax.ShapeDtypeStruct((n, LANES), F32),
                   jax.ShapeDtypeStruct((n, LANES), jnp.int32),
                   jax.ShapeDtypeStruct((1, LANES), jnp.int32)],
        scratch_shapes=[pltpu.VMEM((1, LANES), F32)],
        compiler_params=_cparams(("arbitrary",)),
        name="route",
    )(x2d, w_router_pad, b_router_pad)


ROW_TILE = D_MODEL // LANES


def _to_row_tiled(dst_ref, val):
    rows = val.shape[0]
    for c in range(ROW_TILE):
        dst_ref[pl.ds(c, rows, stride=ROW_TILE), :] = val[:, c * LANES:(c + 1) * LANES]


def _from_row_tiled(src_ref, rows):
    return jnp.concatenate([src_ref[pl.ds(c, rows, stride=ROW_TILE), :] for c in range(ROW_TILE)],
                           axis=1)


def _tile_copy(src_ref, src_row, dst_ref, dst_row, sem):
    return pltpu.make_async_copy(src_ref.at[pl.ds(pl.multiple_of(src_row * ROW_TILE, ROW_TILE), ROW_TILE), :],
                                 dst_ref.at[pl.ds(pl.multiple_of(dst_row * ROW_TILE, ROW_TILE), ROW_TILE), :],
                                 sem)


def _dispatch_kernel(dest_ref, x_ref, xs_ref, xt_ref, sems):
    t = x_ref.shape[0]
    step = pl.program_id(0)
    last = pl.num_programs(0) - 1
    slot = step % 2

    def drain(s):
        for _ in range(TOP_K):
            pltpu.make_async_copy(xt_ref.at[s], xs_ref.at[pl.ds(0, t * ROW_TILE), :], sems.at[s]).wait()

    @pl.when(step >= 2)
    def _():
        drain(slot)

    _to_row_tiled(xt_ref.at[slot], x_ref[...])

    def issue(i, carry):
        for k in range(TOP_K):
            _tile_copy(xt_ref.at[slot], i, xs_ref, dest_ref[0, 0, i * TOP_K + k],
                       sems.at[slot]).start(priority=k % 2)
        return carry

    lax.fori_loop(0, t, issue, 0, unroll=2)

    @pl.when(step == last)
    def _():
        @pl.when(last >= 1)
        def _():
            drain(1 - slot)
        drain(slot)


def _dispatch(x2d, dest3):
    n = x2d.shape[0]
    t = ROUTE_T
    return pl.pallas_call(
        _dispatch_kernel,
        grid=(n // t,),
        in_specs=[pl.BlockSpec((1, 1, t * TOP_K), lambda i: (i, 0, 0), memory_space=pltpu.SMEM),
                  pl.BlockSpec((t, D_MODEL), lambda i: (i, 0))],
        out_specs=pl.BlockSpec(memory_space=pl.ANY),
        out_shape=jax.ShapeDtypeStruct((n * TOP_K * ROW_TILE, LANES), F32),
        scratch_shapes=[pltpu.VMEM((2, t * ROW_TILE, LANES), F32), pltpu.SemaphoreType.DMA((2,))],
        compiler_params=_cparams(("arbitrary",)),
        name="dispatch",
    )(dest3, x2d)


def _expert_weight_copies(wu_hbm, wd_hbm, wu_f32, wd_f32, sems, layer, expert, slot):
    return (pltpu.make_async_copy(wu_hbm.at[layer, expert], wu_f32.at[slot], sems.at[slot, 0]),
            pltpu.make_async_copy(wd_hbm.at[layer, expert], wd_f32.at[slot], sems.at[slot, 1]))


def _expert_kernel(it_e, it_b, it_lo, it_hi, it_first, it_new, it_slot, it_next, n_items,
                   xs_ref, wu_hbm, bu_ref, wd_hbm, bd_ref, ys_ref,
                   wu_f32, wd_f32, wu_bf, wd_bf, sems, *, layer):
    del it_b
    j = pl.program_id(0)
    copies = functools.partial(_expert_weight_copies, wu_hbm, wd_hbm, wu_f32, wd_f32, sems, layer)

    @pl.when(j == 0)
    def _():
        for c in copies(it_e[0], 0):
            c.start()

    @pl.when(jnp.logical_and(j < n_items[0], it_new[j] == 1))
    def _():
        slot = it_slot[j]
        for c in copies(it_e[j], slot):
            c.wait()
        wu_bf[...] = wu_f32[slot].astype(BF16)
        wd_bf[...] = wd_f32[slot].astype(BF16)

        @pl.when(it_next[j] >= 0)
        def _():
            for c in copies(it_next[j], 1 - slot):
                c.start(priority=1)

    def ffn(r0, nr):
        def rows_of(ref, c):
            return ref.at[pl.ds(r0 * ROW_TILE + c, nr, stride=ROW_TILE), :]

        xb = jnp.concatenate([rows_of(xs_ref, c)[...] for c in range(ROW_TILE)], axis=1).astype(BF16)
        bu = bu_ref[0, 0]
        acts = []
        for c0 in range(0, D_EXPERT, EXPERT_CHUNK):
            c1 = c0 + EXPERT_CHUNK
            hg = jnp.dot(xb, wu_bf[:, c0:c1], preferred_element_type=F32) + bu[:, c0:c1]
            hl = (jnp.dot(xb, wu_bf[:, D_EXPERT + c0:D_EXPERT + c1], preferred_element_type=F32)
                  + bu[:, D_EXPERT + c0:D_EXPERT + c1])
            hg = jnp.minimum(hg, SWIGLU_LIMIT)
            hl = jnp.clip(hl, -SWIGLU_LIMIT, SWIGLU_LIMIT)
            acts.append((hg * jax.nn.sigmoid(SWIGLU_ALPHA * hg) * (hl + 1.0)).astype(BF16))
        act = jnp.concatenate(acts, axis=1)
        bd = bd_ref[0, 0]
        row = lax.broadcasted_iota(jnp.int32, (nr, 1), 0) + r0
        mine = jnp.logical_and(row >= it_lo[j], row < it_hi[j])

        def down_proj(first):
            for c0 in range(0, D_MODEL, EXPERT_CHUNK):
                c1 = c0 + EXPERT_CHUNK
                y = jnp.dot(act, wd_bf[:, c0:c1], preferred_element_type=F32) + bd[:, c0:c1]
                for s0 in range(c0, c1, LANES):
                    dst = rows_of(ys_ref, s0 // LANES)
                    other = 0.0 if first else dst[...]
                    dst[...] = jnp.where(mine, y[:, s0 - c0:s0 - c0 + LANES], other)

        @pl.when(it_first[j] == 1)
        def _():
            down_proj(True)
            if nr < MOE_BM:
                other0 = (MOE_BM - nr - r0) * ROW_TILE
                ys_ref[other0:other0 + nr * ROW_TILE, :] = jnp.zeros((nr * ROW_TILE, LANES), F32)

        @pl.when(it_first[j] == 0)
        def _():
            down_proj(False)

    half = MOE_BM // 2
    valid = j < n_items[0]
    low_only = it_hi[j] <= half
    high_only = it_lo[j] >= half

    @pl.when(jnp.logical_and(valid, low_only))
    def _():
        ffn(0, half)

    @pl.when(jnp.logical_and(valid, high_only))
    def _():
        ffn(half, half)

    @pl.when(jnp.logical_and(valid, jnp.logical_not(jnp.logical_or(low_only, high_only))))
    def _():
        ffn(0, MOE_BM)


def _experts(items, xs, w_up, b_up4, w_down, b_down4, layer):
    n_items = items[0].shape[0]
    row_blk = pl.BlockSpec((MOE_BM * ROW_TILE, LANES), lambda j, ie, ib, *rest: (
        ib[jnp.minimum(j, rest[-1][0] - 1)], 0))

    def cur(j, ni):
        return jnp.minimum(j, ni[0] - 1)

    def b_map(j, ie, ib, *rest):
        return (layer, ie[cur(j, rest[-1])], 0, 0)

    return pl.pallas_call(
        functools.partial(_expert_kernel, layer=layer),
        grid_spec=pltpu.PrefetchScalarGridSpec(
            num_scalar_prefetch=len(items),
            grid=(n_items,),
            in_specs=[row_blk,
                      pl.BlockSpec(memory_space=pl.ANY),
                      pl.BlockSpec((1, 1, 1, 2 * D_EXPERT), b_map),
                      pl.BlockSpec(memory_space=pl.ANY),
                      pl.BlockSpec((1, 1, 1, D_MODEL), b_map)],
            out_specs=row_blk,
            scratch_shapes=[pltpu.VMEM((2, D_MODEL, 2 * D_EXPERT), F32),
                            pltpu.VMEM((2, D_EXPERT, D_MODEL), F32),
                            pltpu.VMEM((D_MODEL, 2 * D_EXPERT), BF16),
                            pltpu.VMEM((D_EXPERT, D_MODEL), BF16),
                            pltpu.SemaphoreType.DMA((2, 2))]),
        out_shape=jax.ShapeDtypeStruct(xs.shape, F32),
        compiler_params=_cparams(("arbitrary",)),
        name="experts",
    )(*items, xs, w_up, b_up4, w_down, b_down4)


def _expert_items(counts, n_rows):
    n_items = n_rows // MOE_BM + N_EXPERTS
    end = jnp.cumsum(counts)
    start = end - counts
    first_b = start // MOE_BM
    nb = jnp.where(counts > 0, (end - 1) // MOE_BM - first_b + 1, 0)
    item_end = jnp.cumsum(nb)
    item_start = item_end - nb
    total = item_end[-1]
    jc = jnp.minimum(jnp.arange(n_items, dtype=jnp.int32), total - 1)
    it_e = jnp.sum((item_end[None, :] <= jc[:, None]).astype(jnp.int32), axis=1)
    sel = it_e[:, None] == jnp.arange(N_EXPERTS, dtype=jnp.int32)[None, :]
    pick = lambda v: jnp.sum(jnp.where(sel, v[None, :], 0), axis=1)
    it_b = pick(first_b) + jc - pick(item_start)
    it_lo = jnp.maximum(pick(start), it_b * MOE_BM) - it_b * MOE_BM
    it_hi = jnp.minimum(pick(end), (it_b + 1) * MOE_BM) - it_b * MOE_BM
    prev_b = jnp.concatenate([jnp.full((1,), -1, jnp.int32), it_b[:-1]])
    it_first = (it_b != prev_b).astype(jnp.int32)
    prev_e = jnp.concatenate([jnp.full((1,), -1, jnp.int32), it_e[:-1]])
    it_new = (it_e != prev_e).astype(jnp.int32)
    ar = jnp.arange(N_EXPERTS, dtype=jnp.int32)
    used = counts > 0
    slot_e = (jnp.cumsum(used.astype(jnp.int32)) - 1) % 2
    later = jnp.logical_and(ar[None, :] > ar[:, None], used[None, :])
    next_e = jnp.min(jnp.where(later, ar[None, :], N_EXPERTS), axis=1)
    next_e = jnp.where(next_e == N_EXPERTS, -1, next_e)
    i32 = lambda v: v.astype(jnp.int32)
    return (i32(it_e), i32(it_b), i32(it_lo), i32(it_hi), it_first, it_new, i32(pick(slot_e)),
            i32(pick(next_e)), i32(total).reshape(1)), start


def _combine_kernel(dest_ref, dest_next_ref, gate_ref, x_ref, ys_ref, g_ref, b_ref, o_ref, buf, sems):
    t = x_ref.shape[0]
    step = pl.program_id(0)
    slot = step % 2

    def gather(idx_ref, s):
        def issue(i, carry):
            for k in range(TOP_K):
                _tile_copy(ys_ref, idx_ref[0, 0, i * TOP_K + k], buf.at[s, k], i,
                           sems.at[s]).start(priority=k % 2)
            return carry

        lax.fori_loop(0, t, issue, 0, unroll=2)

    @pl.when(step == 0)
    def _():
        gather(dest_ref, slot)

    @pl.when(step + 1 < pl.num_programs(0))
    def _():
        gather(dest_next_ref, 1 - slot)

    for k in range(TOP_K):
        pltpu.make_async_copy(ys_ref.at[pl.ds(0, t * ROW_TILE), :], buf.at[slot, k], sems.at[slot]).wait()
    gate = gate_ref[...]
    ffn = gate[:, 0:1] * _from_row_tiled(buf.at[slot, 0], t)
    for k in range(1, TOP_K):
        ffn = ffn + gate[:, k:k + 1] * _from_row_tiled(buf.at[slot, k], t)
    o_ref[...] = _layer_norm(DEEPNORM_ALPHA * x_ref[...] + ffn, g_ref[...], b_ref[...])


def _combine(dest3, gate, x2d, ys, ln_g, ln_b):
    n = x2d.shape[0]
    t = ROUTE_T
    const2 = lambda i: (0, 0)
    last = n // t - 1
    return pl.pallas_call(
        _combine_kernel,
        grid=(n // t,),
        in_specs=[pl.BlockSpec((1, 1, t * TOP_K), lambda i: (i, 0, 0), memory_space=pltpu.SMEM),
                  pl.BlockSpec((1, 1, t * TOP_K), lambda i: (jnp.minimum(i + 1, last), 0, 0),
                               memory_space=pltpu.SMEM),
                  pl.BlockSpec((t, LANES), lambda i: (i, 0)),
                  pl.BlockSpec((t, D_MODEL), lambda i: (i, 0)),
                  pl.BlockSpec(memory_space=pl.ANY),
                  pl.BlockSpec((1, D_MODEL), const2),
                  pl.BlockSpec((1, D_MODEL), const2)],
        out_specs=pl.BlockSpec((t, D_MODEL), lambda i: (i, 0)),
        out_shape=jax.ShapeDtypeStruct((n, D_MODEL), F32),
        scratch_shapes=[pltpu.VMEM((2, TOP_K, t * ROW_TILE, LANES), F32),
                        pltpu.SemaphoreType.DMA((2,))],
        compiler_params=_cparams(("arbitrary",)),
        name="combine_ln2",
    )(dest3, dest3, gate, x2d, ys, ln_g, ln_b)


def _a_head_perm():
    grp = A_HEADS // A_KV_HEADS
    order = []
    for j in range(grp):
        order += [j, grp + j]
    return np.concatenate([np.arange(h * HEAD_DIM, (h + 1) * HEAD_DIM) for h in order])


def _moe(x1, layer, w_router, b_router, w_up, b_up, w_down, b_down, ln_g, ln_b):
    n = x1.shape[0]
    wr = jnp.pad(w_router, ((0, 0), (0, LANES - N_EXPERTS)))
    br = jnp.pad(b_router, (0, LANES - N_EXPERTS), constant_values=NEG_INF).reshape(1, LANES)
    eidx, gate, rank, cnt = _route(x1, wr, br)
    items, start = _expert_items(cnt[0, :N_EXPERTS], n * TOP_K)
    e4 = eidx[:, :TOP_K]
    sel = e4[:, :, None] == jnp.arange(N_EXPERTS, dtype=jnp.int32)
    dest = jnp.sum(jnp.where(sel, start, 0), axis=-1) + rank[:, :TOP_K]
    dest3 = dest.reshape(n // ROUTE_T, 1, ROUTE_T * TOP_K).astype(jnp.int32)
    xs = _dispatch(x1, dest3)
    lead = (DEPTH, N_EXPERTS, 1)
    ys = _experts(items, xs, w_up, b_up.reshape(lead + (-1,)), w_down, b_down.reshape(lead + (-1,)), layer)
    return _combine(dest3, gate, x1, ys, ln_g.reshape(1, -1), ln_b.reshape(1, -1))


def kernel(x, w_in, a_sink, lambda_q1, lambda_k1, lambda_q2, lambda_k2, diff_norm_g, na_rpb,
           w_branch, w_out, ln1_g, ln1_b, w_router, b_router, w_up, b_up, w_down, b_down,
           ln2_g, ln2_b):
    bsz, seq, d = x.shape
    n = bsz * seq
    rows = seq // GRID_W
    perm = _a_head_perm()
    slopes_b = jnp.asarray(_ALIBI[A_HEADS:], F32)
    na_bias = _na_bias_strip(na_rpb, rows)
    xcur = x.reshape(n, d)
    for l in range(DEPTH):
        w = w_in[l]
        qscale = HEAD_DIM ** -0.5
        w_qkv = jnp.concatenate([
            w[:, :512][:, perm] * qscale, w[:, 512:768],
            w[:, 768:1280] * qscale, w[:, 1280:2304],
            w[:, 2304:2816] * qscale, w[:, 2816:QKV_W]], axis=1).astype(BF16)
        w_gate = w[:, QKV_W:].astype(BF16)
        w_br = jnp.stack([w_branch[l, 0][perm], w_branch[l, 1], w_branch[l, 2]]).astype(BF16)
        lam_init = 0.8 - 0.6 * math.exp(-0.3 * l)
        lamv = jnp.stack([lambda_q1[l], lambda_k1[l], lambda_q2[l], lambda_k2[l]]).astype(F32)

        proj = _inproj(xcur, w_qkv)
        oa = _attn_a(proj, a_sink[l].astype(F32), bsz, seq)
        ob = _attn_b(proj, slopes_b, lamv, diff_norm_g[l].reshape(1, -1).astype(F32), bsz, seq, lam_init)
        oc = _attn_c(proj, na_bias, l, bsz, seq)
        x1 = _merge(xcur, oa, ob, oc, w_gate, w_br, w_out[l].astype(BF16),
                    ln1_g[l].reshape(1, -1), ln1_b[l].reshape(1, -1))
        xcur = _moe(x1, l, w_router[l], b_router[l], w_up, b_up, w_down, b_down,
                    ln2_g[l], ln2_b[l])
    return xcur.reshape(bsz, seq, d)
```

```python
import functools
import math

import numpy as np
import jax
import jax.numpy as jnp
from jax import lax
from jax.experimental import pallas as pl
from jax.experimental.pallas import tpu as pltpu

F32 = jnp.float32
BF16 = jnp.bfloat16

D_MODEL = 1024
DEPTH = 2
HEAD_DIM = 64
A_HEADS = 8
A_KV_HEADS = 2
WINDOW = 128
B_HEADS = 4
C_HEADS = 8
GRID_W = 64
NA_ROWS = 8
NA_COLS = 16
MIX_W = 512
N_BRANCH = 3
N_EXPERTS = 32
TOP_K = 4
D_EXPERT = 1024
SWIGLU_LIMIT = 7.0
SWIGLU_ALPHA = 1.702
LN_EPS = 1e-5
NEG_INF = -1e30
DEEPNORM_ALPHA = (2 * DEPTH) ** 0.25

LANES = 128
QKV_W = 3840
GATE_W = N_BRANCH * D_MODEL
A_Q_BLK, A_K_BLK, A_V_BLK = 0, 4, 5
B_Q_BLK, B_K_BLK, B_V_BLK = 6, 10, 14
C_Q_BLK, C_K_BLK, C_V_BLK = 18, 22, 26

A_BAND = 3 * WINDOW
B_TQ = 256
NA_QROWS = 4
NA_SLAB = NA_QROWS + NA_ROWS
MOE_BM = 512
EXPERT_CHUNK = 256
ROUTE_T = 512
ROUTER_T = 512
VMEM_LIMIT = 56 * 1024 * 1024

_ALIBI = [float(2.0 ** (-8.0 * (i + 1) / (A_HEADS + B_HEADS))) for i in range(A_HEADS + B_HEADS)]


def _cparams(sem):
    return pltpu.CompilerParams(dimension_semantics=sem, vmem_limit_bytes=VMEM_LIMIT)


def _layer_norm(y, g, b):
    mu = jnp.mean(y, axis=-1, keepdims=True)
    yc = y - mu
    var = jnp.mean(yc * yc, axis=-1, keepdims=True)
    return yc * lax.rsqrt(var + LN_EPS) * g + b


def _inproj_kernel(x_ref, w_ref, o_ref, *, chunk):
    xb = x_ref[...].astype(BF16)
    for c in range(QKV_W // chunk):
        sl = slice(c * chunk, (c + 1) * chunk)
        o_ref[:, sl] = jnp.dot(xb, w_ref[:, sl], preferred_element_type=F32).astype(BF16)


def _inproj(x2d, w_qkv):
    n = x2d.shape[0]
    tm = 1024
    return pl.pallas_call(
        functools.partial(_inproj_kernel, chunk=768),
        grid=(n // tm,),
        in_specs=[pl.BlockSpec((tm, D_MODEL), lambda i: (i, 0)),
                  pl.BlockSpec((D_MODEL, QKV_W), lambda i: (0, 0))],
        out_specs=pl.BlockSpec((tm, QKV_W), lambda i: (i, 0)),
        out_shape=jax.ShapeDtypeStruct((n, QKV_W), BF16),
        compiler_params=_cparams(("arbitrary",)),
        name="inproj",
    )(x2d, w_qkv)


def _attn_a_kernel(sink_ref, q_ref, k_ref, v_ref, o_ref, bias_ref, sink_tab, *, seq):
    nblk = seq // WINDOW
    heads = [j + 4 * hf for j in range(4) for hf in range(2)]
    lane = lax.broadcasted_iota(jnp.int32, (WINDOW, LANES), 1)
    low = lane < HEAD_DIM
    dn = (((1,), (1,)), ((), ()))

    @pl.when(pl.program_id(0) == 0)
    def _():
        kq = (lax.broadcasted_iota(jnp.int32, (WINDOW, A_BAND), 1)
              - lax.broadcasted_iota(jnp.int32, (WINDOW, A_BAND), 0))
        for c in range(3):
            dist = jnp.abs(kq - c * WINDOW).astype(F32)
            for i, h in enumerate(heads):
                bias_ref[c, i * WINDOW:(i + 1) * WINDOW, :] = jnp.where(
                    dist <= float(WINDOW), _ALIBI[h] * dist, -NEG_INF)
        for i, h in enumerate(heads):
            sink_tab[i * WINDOW:(i + 1) * WINDOW, :] = jnp.full((WINDOW, LANES), sink_ref[h], F32)

    sink_col = jnp.max(sink_tab[...], axis=-1, keepdims=True)

    def body(n, carry):
        q0 = pl.multiple_of(n * WINDOW, WINDOW)
        start = pl.multiple_of(jnp.clip((n - 1) * WINDOW, 0, seq - A_BAND), WINDOW)
        cfg = jnp.where(n == 0, 0, jnp.where(n == nblk - 1, 2, 1))
        kb = k_ref[pl.ds(start, A_BAND), :]
        vb = v_ref[pl.ds(start, A_BAND), :]
        pieces = []
        for j in range(4):
            qt = q_ref[pl.ds(q0, WINDOW), j * LANES:(j + 1) * LANES]
            zero = jnp.zeros_like(qt)
            lhs = jnp.concatenate([jnp.where(low, qt, zero), jnp.where(low, zero, qt)], axis=0)
            pieces.append(lax.dot_general(lhs, kb, dn, preferred_element_type=F32))
        t = jnp.concatenate(pieces, axis=0) - bias_ref[cfg]
        m = jnp.maximum(jnp.max(t, axis=-1, keepdims=True), sink_col)
        e = jnp.exp(t - m)
        den = jnp.sum(e, axis=-1, keepdims=True) + jnp.exp(sink_col - m)
        p = (e * (1.0 / den)).astype(BF16)
        pv = jnp.dot(p, vb, preferred_element_type=F32)
        for j in range(4):
            o = jnp.where(low, pv[2 * j * WINDOW:(2 * j + 1) * WINDOW],
                          pv[(2 * j + 1) * WINDOW:(2 * j + 2) * WINDOW])
            o_ref[pl.ds(q0, WINDOW), j * LANES:(j + 1) * LANES] = o.astype(BF16)
        return carry

    lax.fori_loop(0, nblk, body, 0, unroll=8)


def _attn_a(proj, sink, bsz, seq):
    return pl.pallas_call(
        functools.partial(_attn_a_kernel, seq=seq),
        grid=(bsz,),
        in_specs=[pl.BlockSpec(memory_space=pltpu.SMEM),
                  pl.BlockSpec((seq, 4 * LANES), lambda b: (b, A_Q_BLK // 4)),
                  pl.BlockSpec((seq, LANES), lambda b: (b, A_K_BLK)),
                  pl.BlockSpec((seq, LANES), lambda b: (b, A_V_BLK))],
        out_specs=pl.BlockSpec((seq, MIX_W), lambda b: (b, 0)),
        out_shape=jax.ShapeDtypeStruct((bsz * seq, MIX_W), BF16),
        scratch_shapes=[pltpu.VMEM((3, A_HEADS * WINDOW, A_BAND), F32),
                        pltpu.VMEM((A_HEADS * WINDOW, LANES), F32)],
        compiler_params=_cparams(("arbitrary",)),
        name="attn_a",
    )(sink, proj, proj, proj)


def _attn_b_kernel(slope_ref, lamv_ref, gain_ref, q_ref, k_ref, v_ref, o_ref, bias_ref, *,
                   seq, lam_init):
    nblk = seq // B_TQ
    slope = slope_ref[pl.program_id(0)]

    @pl.when(pl.program_id(1) == 0)
    def _():
        ji = (lax.broadcasted_iota(jnp.int32, (B_TQ, B_TQ), 1)
              - lax.broadcasted_iota(jnp.int32, (B_TQ, B_TQ), 0))
        for dd in range(2 * nblk - 1):
            bias_ref[dd] = slope * jnp.abs(ji + (dd - (nblk - 1)) * B_TQ).astype(F32)

    lv = lamv_ref[...]
    lam = (jnp.exp(jnp.sum(lv[0:1] * lv[1:2], axis=-1, keepdims=True))
           - jnp.exp(jnp.sum(lv[2:3] * lv[3:4], axis=-1, keepdims=True)) + lam_init)
    scale = gain_ref[...] * (1.0 - lam_init)
    lane = lax.broadcasted_iota(jnp.int32, (B_TQ, LANES), 1)
    low = lane < HEAD_DIM
    kall = k_ref[...]
    vall = v_ref[...]
    dn = (((1,), (1,)), ((), ()))

    def body(n, carry):
        q0 = pl.multiple_of(n * B_TQ, B_TQ)
        qt = q_ref[pl.ds(q0, B_TQ), :]
        zero = jnp.zeros_like(qt)
        bias = jnp.concatenate([bias_ref[kc - n + (nblk - 1)] for kc in range(nblk)], axis=1)
        es, ls = [], []
        for mp in range(2):
            qm = jnp.where(low, qt, zero) if mp == 0 else jnp.where(low, zero, qt)
            s = lax.dot_general(qm, kall, dn, preferred_element_type=F32) - bias
            m = jnp.max(s, axis=-1, keepdims=True)
            e = jnp.exp(s - m)
            ls.append(jnp.sum(e, axis=-1, keepdims=True))
            es.append(e.astype(BF16))
        a = es[0] * (1.0 / ls[0]).astype(BF16) - es[1] * (lam / ls[1]).astype(BF16)
        o = jnp.dot(a, vall, preferred_element_type=F32)
        o = o * lax.rsqrt(jnp.mean(o * o, axis=-1, keepdims=True) + LN_EPS)
        o_ref[pl.ds(q0, B_TQ), :] = (o * scale).astype(BF16)
        return carry

    lax.fori_loop(0, nblk, body, 0, unroll=True)


def _attn_b(proj, slopes_b, lamv, gain, bsz, seq, lam_init):
    nblk = seq // B_TQ
    return pl.pallas_call(
        functools.partial(_attn_b_kernel, seq=seq, lam_init=lam_init),
        grid=(B_HEADS, bsz),
        in_specs=[pl.BlockSpec(memory_space=pltpu.SMEM),
                  pl.BlockSpec((4, HEAD_DIM), lambda h, b: (0, 0)),
                  pl.BlockSpec((1, 2 * HEAD_DIM), lambda h, b: (0, 0)),
                  pl.BlockSpec((seq, LANES), lambda h, b: (b, B_Q_BLK + h)),
                  pl.BlockSpec((seq, LANES), lambda h, b: (b, B_K_BLK + h)),
                  pl.BlockSpec((seq, LANES), lambda h, b: (b, B_V_BLK + h))],
        out_specs=pl.BlockSpec((seq, LANES), lambda h, b: (b, h)),
        out_shape=jax.ShapeDtypeStruct((bsz * seq, MIX_W), BF16),
        scratch_shapes=[pltpu.VMEM((2 * nblk - 1, B_TQ, B_TQ), F32)],
        compiler_params=_cparams(("arbitrary", "arbitrary")),
        name="attn_b",
    )(slopes_b, lamv, gain, proj, proj, proj)


def _attn_c_kernel(wide_ref, q_ref, k_ref, v_ref, o_ref, bias_ref, *, rows):
    qtok = NA_QROWS * GRID_W
    ktok = NA_SLAB * GRID_W
    ngrp = rows // NA_QROWS
    lane = lax.broadcasted_iota(jnp.int32, (qtok, LANES), 1)
    low = lane < HEAD_DIM
    dn = (((1,), (1,)), ((), ()))

    @pl.when(pl.program_id(1) == 0)
    def _():
        ublk = lax.broadcasted_iota(jnp.int32, (GRID_W, ktok), 1) // GRID_W
        for ci, g in enumerate((0, 1, ngrp - 1)):
            slab0 = min(max(g * NA_QROWS - NA_ROWS // 2, 0), rows - NA_SLAB)
            for a in range(NA_QROWS):
                r = g * NA_QROWS + a
                rs = min(max(r - NA_ROWS // 2, 0), rows - NA_ROWS)
                off = (NA_SLAB + slab0 - r + NA_ROWS - 1) * GRID_W
                inside = jnp.logical_and(ublk >= rs - slab0, ublk < rs - slab0 + NA_ROWS)
                for hh in range(2):
                    win = wide_ref[0, hh, :, off:off + ktok]
                    r0 = hh * qtok + a * GRID_W
                    bias_ref[ci, r0:r0 + GRID_W, :] = jnp.where(inside, win, NEG_INF)

    def body(g, carry):
        slab0 = jnp.clip(g * NA_QROWS - NA_ROWS // 2, 0, rows - NA_SLAB)
        k0 = pl.multiple_of(slab0 * GRID_W, GRID_W)
        q0 = pl.multiple_of(g * qtok, qtok)
        cfg = jnp.where(g == 0, 0, jnp.where(g == ngrp - 1, 2, 1))
        qr = q_ref[pl.ds(q0, qtok), :]
        zero = jnp.zeros_like(qr)
        lhs = jnp.concatenate([jnp.where(low, qr, zero), jnp.where(low, zero, qr)], axis=0)
        ks = k_ref[pl.ds(k0, ktok), :]
        vs = v_ref[pl.ds(k0, ktok), :]
        s = lax.dot_general(lhs, ks, dn, preferred_element_type=F32) + bias_ref[cfg]
        m = jnp.max(s, axis=-1, keepdims=True)
        e = jnp.exp(s - m)
        p = (e * (1.0 / jnp.sum(e, axis=-1, keepdims=True))).astype(BF16)
        pv = jnp.dot(p, vs, preferred_element_type=F32)
        o = jnp.where(low, pv[:qtok], pv[qtok:])
        o_ref[pl.ds(q0, qtok), :] = o.astype(BF16)
        return carry

    lax.fori_loop(0, ngrp, body, 0, unroll=True)


def _attn_c(proj, bias_tab, layer, bsz, seq):
    rows = seq // GRID_W
    npair = C_HEADS // 2
    return pl.pallas_call(
        functools.partial(_attn_c_kernel, rows=rows),
        grid=(npair, bsz),
        in_specs=[pl.BlockSpec((1,) + bias_tab.shape[1:], lambda p, b: (layer * npair + p, 0, 0, 0)),
                  pl.BlockSpec((seq, LANES), lambda p, b: (b, C_Q_BLK + p)),
                  pl.BlockSpec((seq, LANES), lambda p, b: (b, C_K_BLK + p)),
                  pl.BlockSpec((seq, LANES), lambda p, b: (b, C_V_BLK + p))],
        out_specs=pl.BlockSpec((seq, LANES), lambda p, b: (b, p)),
        out_shape=jax.ShapeDtypeStruct((bsz * seq, MIX_W), BF16),
        scratch_shapes=[pltpu.VMEM((3, 2 * NA_QROWS * GRID_W, NA_SLAB * GRID_W), F32)],
        compiler_params=_cparams(("arbitrary", "arbitrary")),
        name="attn_c",
    )(bias_tab, proj, proj, proj)


def _na_bias_strip(rpb, rows):
    assert rows % NA_QROWS == 0 and rows >= NA_SLAB + NA_QROWS and rows >= NA_ROWS
    nh = rpb.shape[0] * C_HEADS
    qc = np.arange(GRID_W)[:, None]
    kc = np.arange(GRID_W)[None, :]
    cs = np.clip(qc - NA_COLS // 2, 0, GRID_W - NA_COLS)
    valid = (kc >= cs) & (kc < cs + NA_COLS)
    ndr = 2 * NA_ROWS - 1
    ncol = 2 * NA_COLS - 1
    colpick = (np.clip(kc - qc + NA_COLS - 1, 0, ncol - 1)[:, :, None] == np.arange(ncol))
    toep = jnp.einsum('hdc,qkc->hqdk', rpb.astype(F32).reshape(nh, ndr, ncol),
                      jnp.asarray(colpick, F32), precision=lax.Precision.HIGHEST)
    toep = jnp.where(valid[None, :, None, :], toep, NEG_INF).reshape(nh, GRID_W, ndr * GRID_W)
    wide = jnp.pad(toep, ((0, 0), (0, 0), (NA_SLAB * GRID_W, NA_SLAB * GRID_W)),
                   constant_values=NEG_INF)
    return wide.reshape(nh // 2, 2, GRID_W, (ndr + 2 * NA_SLAB) * GRID_W)


def _merge_kernel(x_ref, oa_ref, ob_ref, oc_ref, wg_ref, wbr_ref, wout_ref, g_ref, b_ref, o_ref):
    half = x_ref.shape[0] // 2
    for r0 in (0, half):
        rows = slice(r0, r0 + half)
        x = x_ref[rows, :]
        xb = x.astype(BF16)
        merged = None
        for i, br_ref in enumerate((oa_ref, ob_ref, oc_ref)):
            gate = jax.nn.sigmoid(jnp.dot(xb, wg_ref[:, i * D_MODEL:(i + 1) * D_MODEL],
                                          preferred_element_type=F32))
            br = jnp.dot(br_ref[rows, :], wbr_ref[i], preferred_element_type=F32)
            merged = gate * br if merged is None else merged + gate * br
        mix = jnp.dot(merged.astype(BF16), wout_ref[...], preferred_element_type=F32)
        o_ref[rows, :] = _layer_norm(DEEPNORM_ALPHA * x + mix, g_ref[...], b_ref[...])


def _merge(x2d, oa, ob, oc, w_gate, w_br, w_out, ln_g, ln_b):
    n = x2d.shape[0]
    tm = 1024
    const2 = lambda i: (0, 0)
    return pl.pallas_call(
        _merge_kernel,
        grid=(n // tm,),
        in_specs=[pl.BlockSpec((tm, D_MODEL), lambda i: (i, 0)),
                  pl.BlockSpec((tm, MIX_W), lambda i: (i, 0)),
                  pl.BlockSpec((tm, MIX_W), lambda i: (i, 0)),
                  pl.BlockSpec((tm, MIX_W), lambda i: (i, 0)),
                  pl.BlockSpec((D_MODEL, GATE_W), const2),
                  pl.BlockSpec((N_BRANCH, MIX_W, D_MODEL), lambda i: (0, 0, 0)),
                  pl.BlockSpec((D_MODEL, D_MODEL), const2),
                  pl.BlockSpec((1, D_MODEL), const2),
                  pl.BlockSpec((1, D_MODEL), const2)],
        out_specs=pl.BlockSpec((tm, D_MODEL), lambda i: (i, 0)),
        out_shape=jax.ShapeDtypeStruct((n, D_MODEL), F32),
        compiler_params=_cparams(("arbitrary",)),
        name="merge_ln1",
    )(x2d, oa, ob, oc, w_gate, w_br, w_out, ln_g, ln_b)


def _route_kernel(x_ref, w_ref, b_ref, eidx_ref, gate_ref, rank_ref, cnt_ref, carry_ref):
    t = x_ref.shape[0]

    @pl.when(pl.program_id(0) == 0)
    def _():
        carry_ref[...] = jnp.zeros_like(carry_ref)

    x = x_ref[...]
    w = w_ref[...]
    xh = x.astype(BF16)
    xt = (x - xh.astype(F32)).astype(BF16)
    wh = w.astype(BF16)
    wt = (w - wh.astype(F32)).astype(BF16)
    hh_ht = jnp.dot(xh, jnp.concatenate([wh, wt], axis=1), preferred_element_type=F32)
    th = jnp.dot(xt, wh, preferred_element_type=F32)
    logits = hh_ht[:, :LANES] + hh_ht[:, LANES:] + th + b_ref[...]
    lane = lax.broadcasted_iota(jnp.int32, (t, LANES), 1).astype(F32)
    work = logits
    sels, vals, idxs = [], [], []
    for _ in range(TOP_K):
        m = jnp.max(work, axis=-1, keepdims=True)
        idx = jnp.min(jnp.where(work == m, lane, float(LANES)), axis=-1, keepdims=True)
        sel = lane == idx
        work = jnp.where(sel, -jnp.inf, work)
        sels.append(sel)
        vals.append(m)
        idxs.append(idx)
    ex = [jnp.exp(v - vals[0]) for v in vals]
    den = ex[0] + ex[1] + ex[2] + ex[3]
    onehot = jnp.zeros((t, LANES), F32)
    for sel in sels:
        onehot = jnp.where(sel, 1.0, onehot)
    ri = lax.broadcasted_iota(jnp.int32, (t, t), 0)
    ci = lax.broadcasted_iota(jnp.int32, (t, t), 1)
    tri = jnp.where(ci < ri, 1.0, 0.0).astype(BF16)
    before = jnp.dot(tri, onehot.astype(BF16), preferred_element_type=F32) + carry_ref[...]
    eidx = jnp.zeros((t, LANES), F32)
    gate = jnp.zeros((t, LANES), F32)
    rank = jnp.zeros((t, LANES), F32)
    for k in range(TOP_K):
        rk = jnp.sum(jnp.where(sels[k], before, 0.0), axis=-1, keepdims=True)
        eidx = jnp.where(lane == float(k), idxs[k], eidx)
        gate = jnp.where(lane == float(k), ex[k] / den, gate)
        rank = jnp.where(lane == float(k), rk, rank)
    eidx_ref[...] = eidx.astype(jnp.int32)
    gate_ref[...] = gate
    rank_ref[...] = rank.astype(jnp.int32)
    carry_ref[...] += jnp.sum(onehot, axis=0, keepdims=True)
    cnt_ref[...] = carry_ref[...].astype(jnp.int32)


def _route(x2d, w_router_pad, b_router_pad):
    n = x2d.shape[0]
    t = ROUTER_T
    tile = pl.BlockSpec((t, LANES), lambda i: (i, 0))
    return pl.pallas_call(
        _route_kernel,
        grid=(n // t,),
        in_specs=[pl.BlockSpec((t, D_MODEL), lambda i: (i, 0)),
                  pl.BlockSpec((D_MODEL, LANES), lambda i: (0, 0)),
                  pl.BlockSpec((1, LANES), lambda i: (0, 0))],
        out_specs=[tile, tile, tile, pl.BlockSpec((1, LANES), lambda i: (0, 0))],
        out_shape=[jax.ShapeDtypeStruct((n, LANES), jnp.int32),
                   jax.ShapeDtypeStruct((n, LANES), F32),
                   jax.ShapeDtypeStruct((n, LANES), jnp.int32),
                   jax.ShapeDtypeStruct((1, LANES), jnp.int32)],
        scratch_shapes=[pltpu.VMEM((1, LANES), F32)],
        compiler_params=_cparams(("arbitrary",)),
        name="route",
    )(x2d, w_router_pad, b_router_pad)


ROW_TILE = D_MODEL // LANES


def _to_row_tiled(dst_ref, val):
    rows = val.shape[0]
    for c in range(ROW_TILE):
        dst_ref[pl.ds(c, rows, stride=ROW_TILE), :] = val[:, c * LANES:(c + 1) * LANES]


def _from_row_tiled(src_ref, rows):
    return jnp.concatenate([src_ref[pl.ds(c, rows, stride=ROW_TILE), :] for c in range(ROW_TILE)],
                           axis=1)


def _tile_copy(src_ref, src_row, dst_ref, dst_row, sem):
    return pltpu.make_async_copy(src_ref.at[pl.ds(pl.multiple_of(src_row * ROW_TILE, ROW_TILE), ROW_TILE), :],
                                 dst_ref.at[pl.ds(pl.multiple_of(dst_row * ROW_TILE, ROW_TILE), ROW_TILE), :],
                                 sem)


def _dispatch_kernel(dest_ref, x_ref, xs_ref, xt_ref, sems):
    t = x_ref.shape[0]
    step = pl.program_id(0)
    last = pl.num_programs(0) - 1
    slot = step % 2

    def drain(s):
        for _ in range(TOP_K):
            pltpu.make_async_copy(xt_ref.at[s], xs_ref.at[pl.ds(0, t * ROW_TILE), :], sems.at[s]).wait()

    @pl.when(step >= 2)
    def _():
        drain(slot)

    _to_row_tiled(xt_ref.at[slot], x_ref[...])

    def issue(i, carry):
        for k in range(TOP_K):
            _tile_copy(xt_ref.at[slot], i, xs_ref, dest_ref[0, 0, i * TOP_K + k],
                       sems.at[slot]).start(priority=k % 2)
        return carry

    lax.fori_loop(0, t, issue, 0, unroll=2)

    @pl.when(step == last)
    def _():
        @pl.when(last >= 1)
        def _():
            drain(1 - slot)
        drain(slot)


def _dispatch(x2d, dest3):
    n = x2d.shape[0]
    t = ROUTE_T
    return pl.pallas_call(
        _dispatch_kernel,
        grid=(n // t,),
        in_specs=[pl.BlockSpec((1, 1, t * TOP_K), lambda i: (i, 0, 0), memory_space=pltpu.SMEM),
                  pl.BlockSpec((t, D_MODEL), lambda i: (i, 0))],
        out_specs=pl.BlockSpec(memory_space=pl.ANY),
        out_shape=jax.ShapeDtypeStruct((n * TOP_K * ROW_TILE, LANES), F32),
        scratch_shapes=[pltpu.VMEM((2, t * ROW_TILE, LANES), F32), pltpu.SemaphoreType.DMA((2,))],
        compiler_params=_cparams(("arbitrary",)),
        name="dispatch",
    )(dest3, x2d)


def _expert_weight_copies(wu_hbm, wd_hbm, wu_f32, wd_f32, sems, layer, expert, slot):
    return (pltpu.make_async_copy(wu_hbm.at[layer, expert], wu_f32.at[slot], sems.at[slot, 0]),
            pltpu.make_async_copy(wd_hbm.at[layer, expert], wd_f32.at[slot], sems.at[slot, 1]))


def _expert_kernel(it_e, it_b, it_lo, it_hi, it_first, it_new, it_slot, it_next, n_items,
                   xs_ref, wu_hbm, bu_ref, wd_hbm, bd_ref, ys_ref,
                   wu_f32, wd_f32, wu_bf, wd_bf, sems, *, layer):
    del it_b
    j = pl.program_id(0)
    copies = functools.partial(_expert_weight_copies, wu_hbm, wd_hbm, wu_f32, wd_f32, sems, layer)

    @pl.when(j == 0)
    def _():
        for c in copies(it_e[0], 0):
            c.start()

    @pl.when(jnp.logical_and(j < n_items[0], it_new[j] == 1))
    def _():
        slot = it_slot[j]
        for c in copies(it_e[j], slot):
            c.wait()
        wu_bf[...] = wu_f32[slot].astype(BF16)
        wd_bf[...] = wd_f32[slot].astype(BF16)

        @pl.when(it_next[j] >= 0)
        def _():
            for c in copies(it_next[j], 1 - slot):
                c.start(priority=1)

    def ffn(r0, nr):
        def rows_of(ref, c):
            return ref.at[pl.ds(r0 * ROW_TILE + c, nr, stride=ROW_TILE), :]

        xb = jnp.concatenate([rows_of(xs_ref, c)[...] for c in range(ROW_TILE)], axis=1).astype(BF16)
        bu = bu_ref[0, 0]
        acts = []
        for c0 in range(0, D_EXPERT, EXPERT_CHUNK):
            c1 = c0 + EXPERT_CHUNK
            hg = jnp.dot(xb, wu_bf[:, c0:c1], preferred_element_type=F32) + bu[:, c0:c1]
            hl = (jnp.dot(xb, wu_bf[:, D_EXPERT + c0:D_EXPERT + c1], preferred_element_type=F32)
                  + bu[:, D_EXPERT + c0:D_EXPERT + c1])
            hg = jnp.minimum(hg, SWIGLU_LIMIT)
            hl = jnp.clip(hl, -SWIGLU_LIMIT, SWIGLU_LIMIT)
            acts.append((hg * jax.nn.sigmoid(SWIGLU_ALPHA * hg) * (hl + 1.0)).astype(BF16))
        act = jnp.concatenate(acts, axis=1)
        bd = bd_ref[0, 0]
        row = lax.broadcasted_iota(jnp.int32, (nr, 1), 0) + r0
        mine = jnp.logical_and(row >= it_lo[j], row < it_hi[j])

        def down_proj(first):
            for c0 in range(0, D_MODEL, EXPERT_CHUNK):
                c1 = c0 + EXPERT_CHUNK
                y = jnp.dot(act, wd_bf[:, c0:c1], preferred_element_type=F32) + bd[:, c0:c1]
                for s0 in range(c0, c1, LANES):
                    dst = rows_of(ys_ref, s0 // LANES)
                    other = 0.0 if first else dst[...]
                    dst[...] = jnp.where(mine, y[:, s0 - c0:s0 - c0 + LANES], other)

        @pl.when(it_first[j] == 1)
        def _():
            down_proj(True)
            if nr < MOE_BM:
                other0 = (MOE_BM - nr - r0) * ROW_TILE
                ys_ref[other0:other0 + nr * ROW_TILE, :] = jnp.zeros((nr * ROW_TILE, LANES), F32)

        @pl.when(it_first[j] == 0)
        def _():
            down_proj(False)

    half = MOE_BM // 2
    valid = j < n_items[0]
    low_only = it_hi[j] <= half
    high_only = it_lo[j] >= half

    @pl.when(jnp.logical_and(valid, low_only))
    def _():
        ffn(0, half)

    @pl.when(jnp.logical_and(valid, high_only))
    def _():
        ffn(half, half)

    @pl.when(jnp.logical_and(valid, jnp.logical_not(jnp.logical_or(low_only, high_only))))
    def _():
        ffn(0, MOE_BM)


def _experts(items, xs, w_up, b_up4, w_down, b_down4, layer):
    n_items = items[0].shape[0]
    row_blk = pl.BlockSpec((MOE_BM * ROW_TILE, LANES), lambda j, ie, ib, *rest: (
        ib[jnp.minimum(j, rest[-1][0] - 1)], 0))

    def cur(j, ni):
        return jnp.minimum(j, ni[0] - 1)

    def b_map(j, ie, ib, *rest):
        return (layer, ie[cur(j, rest[-1])], 0, 0)

    return pl.pallas_call(
        functools.partial(_expert_kernel, layer=layer),
        grid_spec=pltpu.PrefetchScalarGridSpec(
            num_scalar_prefetch=len(items),
            grid=(n_items,),
            in_specs=[row_blk,
                      pl.BlockSpec(memory_space=pl.ANY),
                      pl.BlockSpec((1, 1, 1, 2 * D_EXPERT), b_map),
                      pl.BlockSpec(memory_space=pl.ANY),
                      pl.BlockSpec((1, 1, 1, D_MODEL), b_map)],
            out_specs=row_blk,
            scratch_shapes=[pltpu.VMEM((2, D_MODEL, 2 * D_EXPERT), F32),
                            pltpu.VMEM((2, D_EXPERT, D_MODEL), F32),
                            pltpu.VMEM((D_MODEL, 2 * D_EXPERT), BF16),
                            pltpu.VMEM((D_EXPERT, D_MODEL), BF16),
                            pltpu.SemaphoreType.DMA((2, 2))]),
        out_shape=jax.ShapeDtypeStruct(xs.shape, F32),
        compiler_params=_cparams(("arbitrary",)),
        name="experts",
    )(*items, xs, w_up, b_up4, w_down, b_down4)


def _expert_items(counts, n_rows):
    n_items = n_rows // MOE_BM + N_EXPERTS
    end = jnp.cumsum(counts)
    start = end - counts
    first_b = start // MOE_BM
    nb = jnp.where(counts > 0, (end - 1) // MOE_BM - first_b + 1, 0)
    item_end = jnp.cumsum(nb)
    item_start = item_end - nb
    total = item_end[-1]
    jc = jnp.minimum(jnp.arange(n_items, dtype=jnp.int32), total - 1)
    it_e = jnp.sum((item_end[None, :] <= jc[:, None]).astype(jnp.int32), axis=1)
    sel = it_e[:, None] == jnp.arange(N_EXPERTS, dtype=jnp.int32)[None, :]
    pick = lambda v: jnp.sum(jnp.where(sel, v[None, :], 0), axis=1)
    it_b = pick(first_b) + jc - pick(item_start)
    it_lo = jnp.maximum(pick(start), it_b * MOE_BM) - it_b * MOE_BM
    it_hi = jnp.minimum(pick(end), (it_b + 1) * MOE_BM) - it_b * MOE_BM
    prev_b = jnp.concatenate([jnp.full((1,), -1, jnp.int32), it_b[:-1]])
    it_first = (it_b != prev_b).astype(jnp.int32)
    prev_e = jnp.concatenate([jnp.full((1,), -1, jnp.int32), it_e[:-1]])
    it_new = (it_e != prev_e).astype(jnp.int32)
    ar = jnp.arange(N_EXPERTS, dtype=jnp.int32)
    used = counts > 0
    slot_e = (jnp.cumsum(used.astype(jnp.int32)) - 1) % 2
    later = jnp.logical_and(ar[None, :] > ar[:, None], used[None, :])
    next_e = jnp.min(jnp.where(later, ar[None, :], N_EXPERTS), axis=1)
    next_e = jnp.where(next_e == N_EXPERTS, -1, next_e)
    i32 = lambda v: v.astype(jnp.int32)
    return (i32(it_e), i32(it_b), i32(it_lo), i32(it_hi), it_first, it_new, i32(pick(slot_e)),
            i32(pick(next_e)), i32(total).reshape(1)), start


def _combine_kernel(dest_ref, dest_next_ref, gate_ref, x_ref, ys_ref, g_ref, b_ref, o_ref, buf, sems):
    t = x_ref.shape[0]
    step = pl.program_id(0)
    slot = step % 2

    def gather(idx_ref, s):
        def issue(i, carry):
            for k in range(TOP_K):
                _tile_copy(ys_ref, idx_ref[0, 0, i * TOP_K + k], buf.at[s, k], i,
                           sems.at[s]).start(priority=k % 2)
            return carry

        lax.fori_loop(0, t, issue, 0, unroll=2)

    @pl.when(step == 0)
    def _():
        gather(dest_ref, slot)

    @pl.when(step + 1 < pl.num_programs(0))
    def _():
        gather(dest_next_ref, 1 - slot)

    for k in range(TOP_K):
        pltpu.make_async_copy(ys_ref.at[pl.ds(0, t * ROW_TILE), :], buf.at[slot, k], sems.at[slot]).wait()
    gate = gate_ref[...]
    ffn = gate[:, 0:1] * _from_row_tiled(buf.at[slot, 0], t)
    for k in range(1, TOP_K):
        ffn = ffn + gate[:, k:k + 1] * _from_row_tiled(buf.at[slot, k], t)
    o_ref[...] = _layer_norm(DEEPNORM_ALPHA * x_ref[...] + ffn, g_ref[...], b_ref[...])


def _combine(dest3, gate, x2d, ys, ln_g, ln_b):
    n = x2d.shape[0]
    t = ROUTE_T
    const2 = lambda i: (0, 0)
    last = n // t - 1
    return pl.pallas_call(
        _combine_kernel,
        grid=(n // t,),
        in_specs=[pl.BlockSpec((1, 1, t * TOP_K), lambda i: (i, 0, 0), memory_space=pltpu.SMEM),
                  pl.BlockSpec((1, 1, t * TOP_K), lambda i: (jnp.minimum(i + 1, last), 0, 0),
                               memory_space=pltpu.SMEM),
                  pl.BlockSpec((t, LANES), lambda i: (i, 0)),
                  pl.BlockSpec((t, D_MODEL), lambda i: (i, 0)),
                  pl.BlockSpec(memory_space=pl.ANY),
                  pl.BlockSpec((1, D_MODEL), const2),
                  pl.BlockSpec((1, D_MODEL), const2)],
        out_specs=pl.BlockSpec((t, D_MODEL), lambda i: (i, 0)),
        out_shape=jax.ShapeDtypeStruct((n, D_MODEL), F32),
        scratch_shapes=[pltpu.VMEM((2, TOP_K, t * ROW_TILE, LANES), F32),
                        pltpu.SemaphoreType.DMA((2,))],
        compiler_params=_cparams(("arbitrary",)),
        name="combine_ln2",
    )(dest3, dest3, gate, x2d, ys, ln_g, ln_b)


def _a_head_perm():
    grp = A_HEADS // A_KV_HEADS
    order = []
    for j in range(grp):
        order += [j, grp + j]
    return np.concatenate([np.arange(h * HEAD_DIM, (h + 1) * HEAD_DIM) for h in order])


def _moe(x1, layer, w_router, b_router, w_up, b_up, w_down, b_down, ln_g, ln_b):
    n = x1.shape[0]
    wr = jnp.pad(w_router, ((0, 0), (0, LANES - N_EXPERTS)))
    br = jnp.pad(b_router, (0, LANES - N_EXPERTS), constant_values=NEG_INF).reshape(1, LANES)
    eidx, gate, rank, cnt = _route(x1, wr, br)
    items, start = _expert_items(cnt[0, :N_EXPERTS], n * TOP_K)
    e4 = eidx[:, :TOP_K]
    sel = e4[:, :, None] == jnp.arange(N_EXPERTS, dtype=jnp.int32)
    dest = jnp.sum(jnp.where(sel, start, 0), axis=-1) + rank[:, :TOP_K]
    dest3 = dest.reshape(n // ROUTE_T, 1, ROUTE_T * TOP_K).astype(jnp.int32)
    xs = _dispatch(x1, dest3)
    lead = (DEPTH, N_EXPERTS, 1)
    ys = _experts(items, xs, w_up, b_up.reshape(lead + (-1,)), w_down, b_down.reshape(lead + (-1,)), layer)
    return _combine(dest3, gate, x1, ys, ln_g.reshape(1, -1), ln_b.reshape(1, -1))


def kernel(x, w_in, a_sink, lambda_q1, lambda_k1, lambda_q2, lambda_k2, diff_norm_g, na_rpb,
           w_branch, w_out, ln1_g, ln1_b, w_router, b_router, w_up, b_up, w_down, b_down,
           ln2_g, ln2_b):
    bsz, seq, d = x.shape
    n = bsz * seq
    rows = seq // GRID_W
    perm = _a_head_perm()
    slopes_b = jnp.asarray(_ALIBI[A_HEADS:], F32)
    na_bias = _na_bias_strip(na_rpb, rows)
    xcur = x.reshape(n, d)
    for l in range(DEPTH):
        w = w_in[l]
        qscale = HEAD_DIM ** -0.5
        w_qkv = jnp.concatenate([
            w[:, :512][:, perm] * qscale, w[:, 512:768],
            w[:, 768:1280] * qscale, w[:, 1280:2304],
            w[:, 2304:2816] * qscale, w[:, 2816:QKV_W]], axis=1).astype(BF16)
        w_gate = w[:, QKV_W:].astype(BF16)
        w_br = jnp.stack([w_branch[l, 0][perm], w_branch[l, 1], w_branch[l, 2]]).astype(BF16)
        lam_init = 0.8 - 0.6 * math.exp(-0.3 * l)
        lamv = jnp.stack([lambda_q1[l], lambda_k1[l], lambda_q2[l], lambda_k2[l]]).astype(F32)

        proj = _inproj(xcur, w_qkv)
        oa = _attn_a(proj, a_sink[l].astype(F32), bsz, seq)
        ob = _attn_b(proj, slopes_b, lamv, diff_norm_g[l].reshape(1, -1).astype(F32), bsz, seq, lam_init)
        oc = _attn_c(proj, na_bias, l, bsz, seq)
        x1 = _merge(xcur, oa, ob, oc, w_gate, w_br, w_out[l].astype(BF16),
                    ln1_g[l].reshape(1, -1), ln1_b[l].reshape(1, -1))
        xcur = _moe(x1, l, w_router[l], b_router[l], w_up, b_up, w_down, b_down,
                    ln2_g[l], ln2_b[l])
    return xcur.reshape(bsz, seq, d)
```
